```python
import jax, jax.numpy as jnp
from jax import lax
import numpy as np

D_MODEL = 1024
BATCH = 4
SEQ = 4096
DEPTH = 4
DEC_BATCH = 32
DEC_SEQ = 4
PAST_LEN = 8192
PAGE_SIZE = 128

N_A_LAYERS = DEPTH // 2
N_B_LAYERS = DEPTH - N_A_LAYERS
CHUNK = 128
A_WIDTH = 3 * D_MODEL // 4
A_GROUPS = 4
A_GROUP_DIM = A_WIDTH // A_GROUPS
MEM_TOKENS = 256
MEM_HEADS = 4
MEM_HEAD_DIM = 64
MEM_WIDTH = MEM_HEADS * MEM_HEAD_DIM
HEAD_DIM = 64
NSA_HEADS = A_WIDTH // HEAD_DIM
NSA_KV_HEADS = 4
GQA_GROUP = NSA_HEADS // NSA_KV_HEADS
NSA_WIDTH = NSA_HEADS * HEAD_DIM
KV_WIDTH = NSA_KV_HEADS * HEAD_DIM
N_BRANCH = 3
CMP_BLOCK = 32
CMP_STRIDE = 16
CMP_RATIO = CMP_BLOCK // CMP_STRIDE
CMP_HIDDEN = 2 * HEAD_DIM
SEL_BLOCK = 64
SEL_TOPK = 16
WINDOW = 512
SEL_Q_BLOCK = 32
ROPE_THETA = 500000.0
ROPE_DIM = HEAD_DIM // 4
D_FF = 4 * D_MODEL
DEEPNORM_ALPHA = (2.0 * DEPTH) ** 0.25
DEEPNORM_BETA = (8.0 * DEPTH) ** -0.25
LN_EPS = 1e-5
NEG = -1e30
FORCE_BONUS = 1e4

kernel_name = "yoco_gmlp_nsa_memory_decoder_step"


def layer_norm(x, g, b):
    xf = x.astype(jnp.float32)
    mu = xf.mean(-1, keepdims=True)
    var = jnp.square(xf - mu).mean(-1, keepdims=True)
    return ((xf - mu) * lax.rsqrt(var + LN_EPS) * g.astype(jnp.float32) + b.astype(jnp.float32)).astype(x.dtype)


def rope(x, pos):
    half = ROPE_DIM // 2
    inv_freq = ROPE_THETA ** (-jnp.arange(half, dtype=jnp.float32) / half)
    ang = pos.astype(jnp.float32)[:, None] * inv_freq[None, :]
    ang = ang.reshape((pos.shape[0],) + (1,) * (x.ndim - 3) + (half,))
    cos, sin = jnp.cos(ang), jnp.sin(ang)
    xr = x[..., :ROPE_DIM].astype(jnp.float32)
    x1, x2 = xr[..., :half], xr[..., half:]
    rot = jnp.concatenate([x1 * cos - x2 * sin, x2 * cos + x1 * sin], axis=-1)
    return jnp.concatenate([rot.astype(x.dtype), x[..., ROPE_DIM:]], axis=-1)


def masked_softmax(s, mask):
    s = jnp.where(mask, s, NEG)
    return jnp.where(mask, jax.nn.softmax(s, axis=-1), 0.0)


def spatial_gate(u, v, w_s, b_s):
    B, T = u.shape[:2]
    n_chunk = -(-T // CHUNK)
    pad = n_chunk * CHUNK - T
    vp = jnp.pad(v, ((0, 0), (0, pad), (0, 0), (0, 0))).reshape(B, n_chunk, CHUNK, A_GROUPS, A_GROUP_DIM)
    causal = jnp.tril(jnp.ones((CHUNK, CHUNK), dtype=bool))
    w = jnp.where(causal[None], w_s, 0.0).astype(v.dtype)
    mixed = jnp.einsum('gts,bcsgd->bctgd', w, vp) + b_s.T[None, None, :, :, None]
    mixed = mixed.reshape(B, n_chunk * CHUNK, A_GROUPS, A_GROUP_DIM)[:, :T]
    return u * mixed


def mem_attend(q, mem_kv):
    s = jnp.einsum('bthd,bmhd->bhtm', q, mem_kv[:, :, 0]).astype(jnp.float32) * (MEM_HEAD_DIM ** -0.5)
    p = jax.nn.softmax(s, axis=-1).astype(q.dtype)
    return jnp.einsum('bhtm,bmhd->bthd', p, mem_kv[:, :, 1])


def sq_relu_mlp(x, w_up, w_down):
    return jnp.square(jax.nn.relu(x @ w_up)) @ w_down


def mixer_a(x, mem_kv, w_in, ln_g, ln_b, w_s, b_s, w_out):
    B, T = x.shape[:2]
    proj = x @ w_in
    uv = jax.nn.gelu(proj[..., :2 * A_WIDTH])
    u = uv[..., :A_WIDTH]
    v = layer_norm(uv[..., A_WIDTH:], ln_g, ln_b)
    mixed = spatial_gate(u.reshape(B, T, A_GROUPS, A_GROUP_DIM), v.reshape(B, T, A_GROUPS, A_GROUP_DIM), w_s, b_s)
    q_mem = proj[..., 2 * A_WIDTH:].reshape(B, T, MEM_HEADS, MEM_HEAD_DIM)
    o_mem = mem_attend(q_mem, mem_kv).reshape(B, T, MEM_WIDTH)
    return jnp.concatenate([mixed.reshape(B, T, A_WIDTH), o_mem], axis=-1) @ w_out, v


def mixer_b(x, pos, mem_kv, attend, w_in, w_out):
    B, T = x.shape[:2]
    proj = x @ w_in
    q = rope(proj[..., :NSA_WIDTH].reshape(B, T, NSA_HEADS, HEAD_DIM), pos)
    gates = jax.nn.sigmoid(proj[..., NSA_WIDTH:NSA_WIDTH + N_BRANCH * NSA_HEADS].reshape(B, T, NSA_HEADS, N_BRANCH))
    q_mem = proj[..., NSA_WIDTH + N_BRANCH * NSA_HEADS:].reshape(B, T, MEM_HEADS, MEM_HEAD_DIM)
    o_nsa = attend(q, gates)
    o_mem = mem_attend(q_mem, mem_kv).reshape(B, T, MEM_WIDTH)
    return jnp.concatenate([o_nsa, o_mem], axis=-1) @ w_out


def shared_rows(h, pos, w_kv_shared):
    B, T = h.shape[:2]
    kv = (h @ w_kv_shared).reshape(B, T, N_BRANCH, 2, NSA_KV_HEADS, HEAD_DIM)
    k = rope(kv[:, :, :, 0], pos)
    return jnp.stack([k, kv[:, :, :, 1]], axis=3)


def compress(rows, cmp_pe, w_phi1, w_phi2):
    B, L = rows.shape[:2]
    n_str = -(-L // CMP_STRIDE)
    x = jnp.pad(rows, ((0, 0), (0, n_str * CMP_STRIDE - L), (0, 0), (0, 0), (0, 0)))
    x = x.reshape(B, n_str, CMP_STRIDE, 2, NSA_KV_HEADS, HEAD_DIM)
    n_cmp = n_str - CMP_RATIO + 1
    w1 = w_phi1.reshape(CMP_RATIO, CMP_STRIDE, 2, HEAD_DIM, CMP_HIDDEN)
    h = jnp.einsum('lcd,lcde->ce', cmp_pe, w_phi1)[:, None, :]
    for j in range(CMP_RATIO):
        h = h + jnp.einsum('bnschd,scde->bnche', x, w1[j])[:, j:j + n_cmp]
    return jnp.einsum('bnche,ced->bnchd', jax.nn.gelu(h), w_phi2)


def to_sel_blocks(rows):
    B, L = rows.shape[:2]
    n_slc = -(-L // SEL_BLOCK)
    x = jnp.pad(rows, ((0, 0), (0, n_slc * SEL_BLOCK - L), (0, 0), (0, 0), (0, 0)))
    return x.reshape(B, n_slc, SEL_BLOCK, 2, NSA_KV_HEADS, HEAD_DIM)


def nsa_query_block(q, gates, t_pos, kv_cmp, slc_blocks, win_kv, win_pos):
    B, Q = q.shape[:2]
    dt = q.dtype
    scale = HEAD_DIM ** -0.5
    qg = q.reshape(B, Q, NSA_KV_HEADS, GQA_GROUP, HEAD_DIM)
    n_cmp = kv_cmp.shape[1]
    cmp_start = jnp.arange(n_cmp) * CMP_STRIDE
    m_cmp = (cmp_start + CMP_BLOCK - 1)[None, :] <= t_pos[:, None]
    s_cmp = jnp.einsum('bqgrd,bngd->bgrqn', qg, kv_cmp[:, :, 0]).astype(jnp.float32) * scale
    p_cmp = masked_softmax(s_cmp, m_cmp)
    o_cmp = jnp.einsum('bgrqn,bngd->bqgrd', p_cmp.astype(dt), kv_cmp[:, :, 1])
    n_slc = slc_blocks.shape[1]
    sel_start = jnp.arange(n_slc) * SEL_BLOCK
    overlap = ((cmp_start[:, None] < sel_start[None, :] + SEL_BLOCK)
               & (cmp_start[:, None] + CMP_BLOCK > sel_start[None, :])).astype(jnp.float32)
    imp = jnp.einsum('bgrqn,nj->bgqj', p_cmp, overlap)
    blk = jnp.arange(n_slc)[None, :]
    cur = (t_pos // SEL_BLOCK)[:, None]
    forced = (blk == 0) | (blk == cur) | (blk == cur - 1)
    causal = sel_start[None, :] <= t_pos[:, None]
    score = jnp.where(causal, imp + FORCE_BONUS * forced.astype(jnp.float32), NEG)
    n_top = min(SEL_TOPK, n_slc)
    top_val, top_idx = lax.top_k(score, n_top)
    bi = jnp.arange(B)[:, None, None, None]
    gi = jnp.arange(NSA_KV_HEADS)[None, :, None, None]
    sel = slc_blocks[bi, top_idx, :, :, gi, :]
    sel = sel.reshape(B, NSA_KV_HEADS, Q, n_top * SEL_BLOCK, 2, HEAD_DIM)
    k_pos = top_idx[..., None] * SEL_BLOCK + jnp.arange(SEL_BLOCK)
    m_sel = ((top_val > NEG / 2)[..., None] & (k_pos <= t_pos[None, None, :, None, None]))
    m_sel = m_sel.reshape(B, NSA_KV_HEADS, 1, Q, n_top * SEL_BLOCK)
    s_sel = jnp.einsum('bqgrd,bgqkd->bgrqk', qg, sel[..., 0, :]).astype(jnp.float32) * scale
    p_sel = masked_softmax(s_sel, m_sel)
    o_sel = jnp.einsum('bgrqk,bgqkd->bqgrd', p_sel.astype(dt), sel[..., 1, :])
    diff = t_pos[:, None] - win_pos[None, :]
    m_win = (diff >= 0) & (diff < WINDOW) & (win_pos[None, :] >= 0)
    s_win = jnp.einsum('bqgrd,bkgd->bgrqk', qg, win_kv[:, :, 0]).astype(jnp.float32) * scale
    p_win = masked_softmax(s_win, m_win)
    o_win = jnp.einsum('bgrqk,bkgd->bqgrd', p_win.astype(dt), win_kv[:, :, 1])
    g = gates.reshape(B, Q, NSA_KV_HEADS, GQA_GROUP, N_BRANCH, 1)
    o = g[..., 0, :] * o_cmp + g[..., 1, :] * o_sel + g[..., 2, :] * o_win
    return o.reshape(B, Q, NSA_WIDTH)


def run_trunk(x, pos, mem_kv, nsa_context, p):
    v_rows = []
    attend, nsa_state = None, None
    for l in range(DEPTH):
        if l < N_A_LAYERS:
            mix, v = mixer_a(x, mem_kv[l], p["w_in_a"][l], p["ln_v_g"][l], p["ln_v_b"][l],
                             p["w_spatial"][l], p["b_spatial"][l], p["w_out_a"][l])
            v_rows.append(v)
        else:
            if l == N_A_LAYERS:
                attend, nsa_state = nsa_context(x)
            i = l - N_A_LAYERS
            mix = mixer_b(x, pos, mem_kv[l], attend, p["w_in_b"][i], p["w_out_b"][i])
        x = layer_norm(DEEPNORM_ALPHA * x + mix, p["ln1_g"][l], p["ln1_b"][l])
        x = layer_norm(DEEPNORM_ALPHA * x + sq_relu_mlp(x, p["w_up"][l], p["w_down"][l]), p["ln2_g"][l], p["ln2_b"][l])
    return x, jnp.stack(v_rows), nsa_state


def setup_inputs(seed: int = 0) -> dict:
    key = jax.random.key(seed)
    ks = iter(jax.random.split(key, 40))

    def nrm(shape, scale):
        return jax.random.normal(next(ks), shape, jnp.float32) * scale

    n_pages = PAST_LEN // PAGE_SIZE
    n_used = DEC_BATCH * n_pages
    n_pool = n_used + max(1, n_used // 4)
    page_table = jax.random.permutation(next(ks), n_pool)[:n_used].reshape(DEC_BATCH, n_pages).astype(jnp.int32)
    win_buf = min(WINDOW, PAST_LEN)
    a_in = 2 * A_WIDTH + MEM_WIDTH
    b_in = NSA_WIDTH + N_BRANCH * NSA_HEADS + MEM_WIDTH
    return {
        "x_prompt": nrm((BATCH, SEQ, D_MODEL), 1.0),
        "x_sample": nrm((DEC_BATCH, DEC_SEQ, D_MODEL), 1.0),
        "cache_cmp_kv": nrm((n_pool, PAGE_SIZE, 2, NSA_KV_HEADS, HEAD_DIM), 1.0),
        "cache_slc_kv": nrm((n_pool, PAGE_SIZE, 2, NSA_KV_HEADS, HEAD_DIM), 1.0),
        "cache_win_kv": nrm((DEC_BATCH, win_buf, 2, NSA_KV_HEADS, HEAD_DIM), 1.0),
        "cache_mem_kv": nrm((DEPTH, DEC_BATCH, MEM_TOKENS, 2, MEM_HEADS, MEM_HEAD_DIM), 1.0),
        "page_table": page_table,
        "mem_prompt": nrm((BATCH, MEM_TOKENS, D_MODEL), 1.0),
        "w_in_a": nrm((N_A_LAYERS, D_MODEL, a_in), D_MODEL ** -0.5),
        "ln_v_g": 1.0 + nrm((N_A_LAYERS, A_WIDTH), 0.05),
        "ln_v_b": nrm((N_A_LAYERS, A_WIDTH), 0.02),
        "w_spatial": nrm((N_A_LAYERS, A_GROUPS, CHUNK, CHUNK), CHUNK ** -0.5),
        "b_spatial": 1.0 + nrm((N_A_LAYERS, A_GROUPS, CHUNK), 0.1),
        "w_out_a": nrm((N_A_LAYERS, A_WIDTH + MEM_WIDTH, D_MODEL), (A_WIDTH + MEM_WIDTH) ** -0.5 * DEEPNORM_BETA),
        "w_in_b": nrm((N_B_LAYERS, D_MODEL, b_in), D_MODEL ** -0.5),
        "w_out_b": nrm((N_B_LAYERS, NSA_WIDTH + MEM_WIDTH, D_MODEL), (NSA_WIDTH + MEM_WIDTH) ** -0.5 * DEEPNORM_BETA),
        "w_kv_shared": nrm((D_MODEL, N_BRANCH * 2 * KV_WIDTH), D_MODEL ** -0.5),
        "cmp_pe": nrm((CMP_BLOCK, 2, HEAD_DIM), 0.1),
        "w_phi1": nrm((CMP_BLOCK, 2, HEAD_DIM, CMP_HIDDEN), (CMP_BLOCK * HEAD_DIM) ** -0.5),
        "w_phi2": nrm((2, CMP_HIDDEN, HEAD_DIM), CMP_HIDDEN ** -0.5),
        "w_mem_kv": nrm((DEPTH, D_MODEL, 2 * MEM_WIDTH), D_MODEL ** -0.5),
        "ln1_g": 1.0 + nrm((DEPTH, D_MODEL), 0.05),
        "ln1_b": nrm((DEPTH, D_MODEL), 0.02),
        "ln2_g": 1.0 + nrm((DEPTH, D_MODEL), 0.05),
        "ln2_b": nrm((DEPTH, D_MODEL), 0.02),
        "w_up": nrm((DEPTH, D_MODEL, D_FF), D_MODEL ** -0.5),
        "w_down": nrm((DEPTH, D_FF, D_MODEL), D_FF ** -0.5 * DEEPNORM_BETA),
    }


def reference(x_prompt, x_sample, cache_cmp_kv, cache_slc_kv, cache_win_kv, cache_mem_kv, page_table, mem_prompt,
              w_in_a, ln_v_g, ln_v_b, w_spatial, b_spatial, w_out_a, w_in_b, w_out_b, w_kv_shared, cmp_pe,
              w_phi1, w_phi2, w_mem_kv, ln1_g, ln1_b, ln2_g, ln2_b, w_up, w_down):
    p = {"w_in_a": w_in_a, "ln_v_g": ln_v_g, "ln_v_b": ln_v_b, "w_spatial": w_spatial, "b_spatial": b_spatial,
         "w_out_a": w_out_a, "w_in_b": w_in_b, "w_out_b": w_out_b, "ln1_g": ln1_g, "ln1_b": ln1_b,
         "ln2_g": ln2_g, "ln2_b": ln2_b, "w_up": w_up, "w_down": w_down}

    Bp, S = x_prompt.shape[:2]
    pos_p = jnp.arange(S, dtype=jnp.int32)
    mem_kv_prompt = jnp.einsum('bmd,lde->lbme', mem_prompt, w_mem_kv).reshape(
        DEPTH, Bp, MEM_TOKENS, 2, MEM_HEADS, MEM_HEAD_DIM)

    def prompt_ctx(h):
        rows = shared_rows(h, pos_p, w_kv_shared)
        cmp_rows, slc_rows, win_rows = rows[:, :, 0], rows[:, :, 1], rows[:, :, 2]
        kv_cmp = compress(cmp_rows, cmp_pe, w_phi1, w_phi2)
        slc_blocks = to_sel_blocks(slc_rows)
        win_pad = jnp.pad(win_rows, ((0, 0), (WINDOW, 0), (0, 0), (0, 0), (0, 0)))

        def attend(q, gates):
            n_blk = S // SEL_Q_BLOCK
            qb = q.reshape(Bp, n_blk, SEL_Q_BLOCK, NSA_HEADS, HEAD_DIM).swapaxes(0, 1)
            gb = gates.reshape(Bp, n_blk, SEL_Q_BLOCK, NSA_HEADS, N_BRANCH).swapaxes(0, 1)

            def body(args):
                c, qc, gc = args
                t0 = c * SEL_Q_BLOCK
                t_pos = t0 + jnp.arange(SEL_Q_BLOCK, dtype=jnp.int32)
                win = lax.dynamic_slice_in_dim(win_pad, t0, WINDOW + SEL_Q_BLOCK, axis=1)
                win_pos = t0 - WINDOW + jnp.arange(WINDOW + SEL_Q_BLOCK, dtype=jnp.int32)
                return nsa_query_block(qc, gc, t_pos, kv_cmp, slc_blocks, win, win_pos)

            out = lax.map(body, (jnp.arange(n_blk, dtype=jnp.int32), qb, gb))
            return out.swapaxes(0, 1).reshape(Bp, S, NSA_WIDTH)

        return attend, (cmp_rows, slc_rows, win_rows[:, S - min(WINDOW, S):])

    y_prompt, _, (cmp_kv_prompt, slc_kv_prompt, win_kv_prompt) = run_trunk(
        x_prompt, pos_p, mem_kv_prompt, prompt_ctx, p)

    Bs, T = x_sample.shape[:2]
    past_len = page_table.shape[1] * PAGE_SIZE
    pos_s = past_len + jnp.arange(T, dtype=jnp.int32)
    n_buf = cache_win_kv.shape[1]

    def sample_ctx(h):
        rows = shared_rows(h, pos_s, w_kv_shared)
        new_cmp, new_slc, new_win = rows[:, :, 0], rows[:, :, 1], rows[:, :, 2]
        past_cmp = cache_cmp_kv[page_table].reshape(Bs, past_len, 2, NSA_KV_HEADS, HEAD_DIM)
        past_slc = cache_slc_kv[page_table].reshape(Bs, past_len, 2, NSA_KV_HEADS, HEAD_DIM)
        kv_cmp = compress(jnp.concatenate([past_cmp, new_cmp], axis=1), cmp_pe, w_phi1, w_phi2)
        slc_blocks = to_sel_blocks(jnp.concatenate([past_slc, new_slc], axis=1))
        win_all = jnp.concatenate([cache_win_kv, new_win], axis=1)
        win_pos = past_len - n_buf + jnp.arange(n_buf + T, dtype=jnp.int32)

        def attend(q, gates):
            return nsa_query_block(q, gates, pos_s, kv_cmp, slc_blocks, win_all, win_pos)

        return attend, (new_cmp, new_slc, win_all[:, T:])

    y_sample, gmlp_v_sample, (cmp_kv_sample, slc_kv_sample, win_kv_sample) = run_trunk(
        x_sample, pos_s, cache_mem_kv, sample_ctx, p)

    return (y_prompt, y_sample, cmp_kv_prompt, slc_kv_prompt, win_kv_prompt, mem_kv_prompt,
            cmp_kv_sample, slc_kv_sample, win_kv_sample, gmlp_v_sample)
```

```python
import functools
import math

import jax
import jax.numpy as jnp
from jax import lax
from jax.experimental import pallas as pl
from jax.experimental.pallas import tpu as pltpu

BF = jnp.bfloat16
F32 = jnp.float32

DEPTH = 4
N_A = 2
A_WIDTH = 768
A_GROUPS = 4
A_GROUP_DIM = 192
MEM_TOKENS = 256
MEM_HEADS = 4
MEM_WIDTH = 256
HEAD_DIM = 64
NSA_HEADS = 12
KV_HEADS = 4
GQA = 3
NSA_WIDTH = 768
KV_WIDTH = 256
ROW_WIDTH = 2 * KV_WIDTH
CMP_BLOCK = 32
CMP_STRIDE = 16
CMP_HIDDEN = 128
SEL_BLOCK = 64
SEL_TOPK = 16
WINDOW = 512
ROPE_THETA = 500000.0
ROPE_DIM = 16
D_FF_CHUNK = 1024
ALPHA = (2.0 * DEPTH) ** 0.25
LN_EPS = 1e-5
NEG = -1e30
FORCE_BONUS = 1e4
PAGE = 128
LANES = 128
GATE_PAD = 128
VMEM_LIMIT = 56 * 1024 * 1024


def _cparams(sem, vmem=None):
    return pltpu.CompilerParams(dimension_semantics=sem, vmem_limit_bytes=vmem)


def _const_spec(shape):
    return pl.BlockSpec(shape, lambda *_: (0,) * len(shape), pipeline_mode=pl.Buffered(1))


def _ln(x, g, b):
    mu = jnp.mean(x, axis=-1, keepdims=True)
    xc = x - mu
    var = jnp.mean(xc * xc, axis=-1, keepdims=True)
    return xc * lax.rsqrt(var + LN_EPS) * g + b


def _dot(a, b):
    return jnp.dot(a, b, preferred_element_type=F32)


def _dot_nt(a, b):
    return lax.dot_general(a, b, (((1,), (1,)), ((), ())), preferred_element_type=F32)


def _rope_cols(x, c, a, b):
    cols = []
    for j in range(x.shape[1] // LANES):
        xc = x[:, j * LANES:(j + 1) * LANES]
        up = pltpu.roll(xc, LANES - ROPE_DIM // 2, axis=1)
        dn = pltpu.roll(xc, ROPE_DIM // 2, axis=1)
        cols.append(xc * c + up * a + dn * b)
    return cols[0] if len(cols) == 1 else jnp.concatenate(cols, axis=1)


def _mem_kv_kernel(x_ref, w_ref, o_ref):
    o_ref[0] = _dot(x_ref[...].astype(BF), w_ref[0].astype(BF))


def _mem_kv_proj(mem_prompt, w_mem_kv):
    b, m, d = mem_prompt.shape
    x = mem_prompt.reshape(b * m, d)
    depth, _, n = w_mem_kv.shape
    tm = 512
    out = pl.pallas_call(
        _mem_kv_kernel,
        grid=(depth, (b * m) // tm),
        in_specs=[pl.BlockSpec((tm, d), lambda l, i: (i, 0)),
                  pl.BlockSpec((1, d, n), lambda l, i: (l, 0, 0))],
        out_specs=pl.BlockSpec((1, tm, n), lambda l, i: (l, i, 0)),
        out_shape=jax.ShapeDtypeStruct((depth, b * m, n), F32),
        compiler_params=_cparams(("parallel", "parallel")),
        name="mem_kv_proj",
    )(x, w_mem_kv)
    return out.reshape(depth, b, m, n)


def _a_in_kernel(x_ref, w_ref, g_ref, b_ref, ws_ref, bs_ref, mixed_ref, qm_ref, *v_refs, chunk_shift):
    x = x_ref[0].astype(BF)
    proj = _dot(x, w_ref[...])
    uv = jax.nn.gelu(proj[:, :2 * A_WIDTH])
    u = uv[:, :A_WIDTH]
    v = _ln(uv[:, A_WIDTH:], g_ref[...], b_ref[...])
    if v_refs:
        v_refs[0][0] = v
    qm_ref[0] = proj[:, 2 * A_WIDTH:]
    tm = x.shape[0]
    r = ws_ref.shape[1]
    row = lax.broadcasted_iota(jnp.int32, (r, r), 0)
    col = lax.broadcasted_iota(jnp.int32, (r, r), 1)
    allowed = (col <= row) & ((row >> chunk_shift) == (col >> chunk_shift))
    lane = lax.broadcasted_iota(jnp.int32, (1, A_WIDTH), 1)
    wgs = [jnp.where(allowed, ws_ref[g], 0.0).astype(BF) for g in range(A_GROUPS)]
    gmask = [(lane >= g * A_GROUP_DIM) & (lane < (g + 1) * A_GROUP_DIM) for g in range(A_GROUPS)]
    vb = v.astype(BF)
    for ci in range(tm // r):
        vc = vb[ci * r:(ci + 1) * r]
        acc = bs_ref[...]
        for g in range(A_GROUPS):
            acc = acc + jnp.where(gmask[g], _dot(wgs[g], vc), 0.0)
        mixed_ref[0, ci * r:(ci + 1) * r, :] = (u[ci * r:(ci + 1) * r] * acc).astype(BF)


def _a_in(x, w_in, ln_g, ln_b, ws, bs, *, chunk, tm, emit_v):
    b, s, d = x.shape
    n = w_in.shape[1]
    r = ws.shape[1]
    outs = [jax.ShapeDtypeStruct((b, s, A_WIDTH), BF), jax.ShapeDtypeStruct((b, s, MEM_WIDTH), F32)]
    ospecs = [pl.BlockSpec((1, tm, A_WIDTH), lambda i, j: (i, j, 0)),
              pl.BlockSpec((1, tm, MEM_WIDTH), lambda i, j: (i, j, 0))]
    if emit_v:
        outs.append(jax.ShapeDtypeStruct((b, s, A_WIDTH), F32))
        ospecs.append(pl.BlockSpec((1, tm, A_WIDTH), lambda i, j: (i, j, 0)))
    return pl.pallas_call(
        functools.partial(_a_in_kernel, chunk_shift=int(math.log2(chunk))),
        grid=(b, s // tm),
        in_specs=[pl.BlockSpec((1, tm, d), lambda i, j: (i, j, 0)),
                  _const_spec((d, n)), _const_spec((1, A_WIDTH)), _const_spec((1, A_WIDTH)),
                  _const_spec((A_GROUPS, r, r)), _const_spec((r, A_WIDTH))],
        out_specs=ospecs,
        out_shape=outs,
        compiler_params=_cparams(("parallel", "parallel"), VMEM_LIMIT),
        name="a_in_gate",
    )(x, w_in, ln_g, ln_b, ws, bs)


def _mem_attn_kernel(q_ref, kv_ref, o_ref):
    q = q_ref[0]
    kv = kv_ref[0]
    k = kv[:, :MEM_WIDTH].astype(BF)
    v = kv[:, MEM_WIDTH:]
    lane = lax.broadcasted_iota(jnp.int32, (1, MEM_WIDTH), 1)
    out = jnp.zeros(q.shape, F32)
    for h in range(MEM_HEADS):
        hm = (lane >= h * HEAD_DIM) & (lane < (h + 1) * HEAD_DIM)
        s = _dot_nt(jnp.where(hm, q, 0.0).astype(BF), k) * (HEAD_DIM ** -0.5)
        m = jnp.max(s, axis=-1, keepdims=True)
        p = jnp.exp(s - m)
        p = p / jnp.sum(p, axis=-1, keepdims=True)
        out = out + _dot(p.astype(BF), jnp.where(hm, v, 0.0).astype(BF))
    o_ref[0] = out


def _mem_attn(q, mem_kv, *, tq):
    b, s, w = q.shape
    return pl.pallas_call(
        _mem_attn_kernel,
        grid=(b, s // tq),
        in_specs=[pl.BlockSpec((1, tq, w), lambda i, j: (i, j, 0)),
                  pl.BlockSpec((1, MEM_TOKENS, 2 * w), lambda i, j: (i, 0, 0))],
        out_specs=pl.BlockSpec((1, tq, w), lambda i, j: (i, j, 0)),
        out_shape=jax.ShapeDtypeStruct((b, s, w), F32),
        compiler_params=_cparams(("parallel", "parallel")),
        name="mem_attn",
    )(q, mem_kv)


def _out_mlp_kernel(x_ref, a_ref, om_ref, wo_ref, g1_ref, b1_ref, wu_ref, wd_ref, g2_ref, b2_ref, o_ref,
                    *, a_transposed):
    x = x_ref[0]
    a = a_ref[0]
    a = a.T.astype(BF) if a_transposed else a.astype(BF)
    na = a.shape[1]
    mix = _dot(a, wo_ref[:na, :]) + _dot(om_ref[0].astype(BF), wo_ref[na:, :])
    y = _ln(ALPHA * x + mix, g1_ref[...], b1_ref[...])
    yb = y.astype(BF)
    acc = jnp.zeros(x.shape, F32)
    for c in range(wu_ref.shape[1] // D_FF_CHUNK):
        h = _dot(yb, wu_ref[:, c * D_FF_CHUNK:(c + 1) * D_FF_CHUNK])
        h = jnp.square(jnp.maximum(h, 0.0)).astype(BF)
        acc = acc + _dot(h, wd_ref[c * D_FF_CHUNK:(c + 1) * D_FF_CHUNK, :])
    o_ref[0] = _ln(ALPHA * y + acc, g2_ref[...], b2_ref[...])


def _out_mlp(x, a, om, wo, g1, b1, wu, wd, g2, b2, *, tm, a_transposed):
    b, s, d = x.shape
    na = wo.shape[0] - om.shape[2]
    dff = wu.shape[1]
    if a_transposed:
        a_spec = pl.BlockSpec((1, na, tm), lambda i, j: (i, 0, j))
    else:
        a_spec = pl.BlockSpec((1, tm, na), lambda i, j: (i, j, 0))
    return pl.pallas_call(
        functools.partial(_out_mlp_kernel, a_transposed=a_transposed),
        grid=(b, s // tm),
        in_specs=[pl.BlockSpec((1, tm, d), lambda i, j: (i, j, 0)),
                  a_spec,
                  pl.BlockSpec((1, tm, om.shape[2]), lambda i, j: (i, j, 0)),
                  _const_spec(wo.shape), _const_spec((1, d)), _const_spec((1, d)),
                  _const_spec((d, dff)), _const_spec((dff, d)), _const_spec((1, d)), _const_spec((1, d))],
        out_specs=pl.BlockSpec((1, tm, d), lambda i, j: (i, j, 0)),
        out_shape=jax.ShapeDtypeStruct((b, s, d), F32),
        compiler_params=_cparams(("parallel", "parallel"), VMEM_LIMIT),
        name="out_mlp",
    )(x, a, om, wo, g1, b1, wu, wd, g2, b2)


def _kv_shared_kernel(h_ref, w_ref, c_ref, a_ref, b_ref, cmp_ref, slc_ref, win_ref):
    kv = _dot(h_ref[0].astype(BF), w_ref[...])
    c, a, b = c_ref[...], a_ref[...], b_ref[...]
    for br, ref in enumerate((cmp_ref, slc_ref, win_ref)):
        k = kv[:, br * ROW_WIDTH:br * ROW_WIDTH + KV_WIDTH]
        ref[0, :, :KV_WIDTH] = _rope_cols(k, c, a, b)
        ref[0, :, KV_WIDTH:] = kv[:, br * ROW_WIDTH + KV_WIDTH:(br + 1) * ROW_WIDTH]


def _kv_shared(h, w_kv, tabs, *, tm):
    b, s, d = h.shape
    row_spec = pl.BlockSpec((1, tm, ROW_WIDTH), lambda i, j: (i, j, 0))
    tab_spec = pl.BlockSpec((tm, LANES), lambda i, j: (j, 0))
    return pl.pallas_call(
        _kv_shared_kernel,
        grid=(b, s // tm),
        in_specs=[pl.BlockSpec((1, tm, d), lambda i, j: (i, j, 0)), _const_spec(w_kv.shape),
                  tab_spec, tab_spec, tab_spec],
        out_specs=[row_spec, row_spec, row_spec],
        out_shape=[jax.ShapeDtypeStruct((b, s, ROW_WIDTH), F32)] * 3,
        compiler_params=_cparams(("parallel", "parallel")),
        name="kv_shared",
    )(h, w_kv, *tabs)


def _b_in_kernel(x_ref, w_ref, c_ref, a_ref, b_ref, q_ref, qm_ref, gt_ref):
    proj = _dot(x_ref[0].astype(BF), w_ref[...])
    q = _rope_cols(proj[:, :NSA_WIDTH], c_ref[...], a_ref[...], b_ref[...]) * (HEAD_DIM ** -0.5)
    for h in range(NSA_HEADS):
        q_ref[0, h] = q[:, h * HEAD_DIM:(h + 1) * HEAD_DIM].astype(BF)
    qm_ref[0] = proj[:, NSA_WIDTH:NSA_WIDTH + MEM_WIDTH]
    gt_ref[0] = jax.nn.sigmoid(proj[:, NSA_WIDTH + MEM_WIDTH:]).T


def _b_in(x, w_in, tabs, *, tm):
    b, s, d = x.shape
    tab_spec = pl.BlockSpec((tm, LANES), lambda i, j: (j, 0))
    return pl.pallas_call(
        _b_in_kernel,
        grid=(b, s // tm),
        in_specs=[pl.BlockSpec((1, tm, d), lambda i, j: (i, j, 0)), _const_spec(w_in.shape),
                  tab_spec, tab_spec, tab_spec],
        out_specs=[pl.BlockSpec((1, NSA_HEADS, tm, HEAD_DIM), lambda i, j: (i, 0, j, 0)),
                   pl.BlockSpec((1, tm, MEM_WIDTH), lambda i, j: (i, j, 0)),
                   pl.BlockSpec((1, GATE_PAD, tm), lambda i, j: (i, 0, j))],
        out_shape=[jax.ShapeDtypeStruct((b, NSA_HEADS, s, HEAD_DIM), BF),
                   jax.ShapeDtypeStruct((b, s, MEM_WIDTH), F32),
                   jax.ShapeDtypeStruct((b, GATE_PAD, s), F32)],
        compiler_params=_cparams(("parallel", "parallel")),
        name="b_in",
    )(x, w_in, *tabs)


def _kv_layout_kernel(pt_ref, page_ref, *refs, tail_page):
    if tail_page is None:
        k_ref, vt_ref = refs
        x = page_ref[0]
    else:
        tail_ref, k_ref, vt_ref = refs
        x = jnp.where(pl.program_id(1) == tail_page, tail_ref[0], page_ref[0])
    vt = x[:, KV_WIDTH:].T
    for g in range(KV_HEADS):
        k_ref[0, g] = x[:, g * HEAD_DIM:(g + 1) * HEAD_DIM].astype(BF)
        vt_ref[0, g, 0] = vt[g * HEAD_DIM:(g + 1) * HEAD_DIM].astype(BF)


def _kv_layout(pool, table, tail=None):
    b, n_real = table.shape
    n_pages = n_real + (0 if tail is None else 1)
    in_specs = [pl.BlockSpec((1, PAGE, ROW_WIDTH),
                             lambda i, p, pt: (pt[i * n_real + jnp.minimum(p, n_real - 1)], 0, 0))]
    args = [pool]
    if tail is not None:
        in_specs.append(pl.BlockSpec((1, PAGE, ROW_WIDTH), lambda i, p, pt: (i, 0, 0)))
        args.append(tail)
    return pl.pallas_call(
        functools.partial(_kv_layout_kernel, tail_page=None if tail is None else n_real),
        grid_spec=pltpu.PrefetchScalarGridSpec(
            num_scalar_prefetch=1, grid=(b, n_pages), in_specs=in_specs,
            out_specs=[pl.BlockSpec((1, KV_HEADS, PAGE, HEAD_DIM), lambda i, p, pt: (i, 0, p, 0)),
                       pl.BlockSpec((1, KV_HEADS, 1, HEAD_DIM, PAGE), lambda i, p, pt: (i, 0, p, 0, 0))]),
        out_shape=[jax.ShapeDtypeStruct((b, KV_HEADS, n_pages * PAGE, HEAD_DIM), BF),
                   jax.ShapeDtypeStruct((b, KV_HEADS, n_pages, HEAD_DIM, PAGE), BF)],
        compiler_params=_cparams(("parallel", "parallel")),
        name="kv_layout",
    )(table.reshape(-1), *args)


def _rows_layout(rows):
    b, l, _ = rows.shape
    n = l // PAGE
    table = jnp.arange(b * n, dtype=jnp.int32).reshape(b, n)
    return _kv_layout(rows.reshape(b * n, PAGE, ROW_WIDTH), table)


def _compress_kernel(pt_ref, *refs, n_slots, tail_group):
    page_refs = refs[:n_slots]
    rest = refs[n_slots:]
    if tail_group is not None:
        tail_ref, rest = rest[0], rest[1:]
    w1_ref, pe_ref, w2_ref, o_ref, xs_ref = rest
    k = pl.program_id(1)
    m = n_slots * (PAGE // CMP_STRIDE)

    @pl.when(k == 0)
    def _():
        xs_ref[:, :CMP_STRIDE, :] = jnp.zeros((4, CMP_STRIDE, LANES), F32)

    for i, pr in enumerate(page_refs):
        x = pr[0]
        if tail_group is not None and i == n_slots - 1:
            x = jnp.where(k == tail_group, tail_ref[0], x)
        for vcol in range(4):
            xs_ref[vcol, CMP_STRIDE + i * PAGE:CMP_STRIDE + (i + 1) * PAGE, :] = x[:, vcol * LANES:(vcol + 1) * LANES]

    out_cols = []
    for c in range(2):
        hpe = (_dot(pe_ref[c, 0], w1_ref[c, 0])[0:1] + _dot(pe_ref[c, 1], w1_ref[c, 1])[0:1])
        acc = jnp.zeros((m, KV_WIDTH), F32)
        for hp in range(2):
            vcol = 2 * c + hp
            prev = jnp.concatenate(
                [xs_ref[vcol, pl.ds(s, m, stride=CMP_STRIDE), :] for s in range(CMP_STRIDE)], axis=1).astype(BF)
            cur = jnp.concatenate(
                [xs_ref[vcol, pl.ds(CMP_STRIDE + s, m, stride=CMP_STRIDE), :] for s in range(CMP_STRIDE)],
                axis=1).astype(BF)
            h = _dot(prev, w1_ref[c, 0]) + _dot(cur, w1_ref[c, 1]) + hpe
            acc = acc + _dot(jax.nn.gelu(h).astype(BF), w2_ref[c, hp])
        out_cols.append(acc)
    o_ref[0] = jnp.concatenate(out_cols, axis=1)

    for vcol in range(4):
        xs_ref[vcol, :CMP_STRIDE, :] = xs_ref[vcol, n_slots * PAGE:n_slots * PAGE + CMP_STRIDE, :]


def _compress(pool, table, tail, w1, pe, w2):
    b, n_real = table.shape
    n_pages = n_real + (0 if tail is None else 1)
    n_slots = max(p for p in range(1, 17) if n_pages % p == 0)
    n_groups = n_pages // n_slots
    m = n_slots * (PAGE // CMP_STRIDE)

    def page_spec(slot):
        return pl.BlockSpec(
            (1, PAGE, ROW_WIDTH),
            lambda i, k, pt: (pt[i * n_real + jnp.minimum(k * n_slots + slot, n_real - 1)], 0, 0))

    in_specs = [page_spec(slot) for slot in range(n_slots)]
    args = [pool] * n_slots
    if tail is not None:
        in_specs.append(pl.BlockSpec((1, PAGE, ROW_WIDTH), lambda i, k, pt: (i, 0, 0)))
        args.append(tail)
    in_specs += [pl.BlockSpec(w1.shape, lambda i, k, pt: (0,) * 4),
                 pl.BlockSpec(pe.shape, lambda i, k, pt: (0,) * 4),
                 pl.BlockSpec(w2.shape, lambda i, k, pt: (0,) * 4)]
    return pl.pallas_call(
        functools.partial(_compress_kernel, n_slots=n_slots,
                          tail_group=None if tail is None else n_groups - 1),
        grid_spec=pltpu.PrefetchScalarGridSpec(
            num_scalar_prefetch=1, grid=(b, n_groups), in_specs=in_specs,
            out_specs=pl.BlockSpec((1, m, ROW_WIDTH), lambda i, k, pt: (i, k, 0)),
            scratch_shapes=[pltpu.VMEM((4, CMP_STRIDE + n_slots * PAGE, LANES), F32)]),
        out_shape=jax.ShapeDtypeStruct((b, n_groups * m, ROW_WIDTH), F32),
        compiler_params=_cparams(("parallel", "arbitrary"), VMEM_LIMIT),
        name="compress",
    )(table.reshape(-1), *args, w1, pe, w2)


def _nsa_kernel(q_ref, gt_ref, kc_ref, vct_ref, ks_ref, vst_ref, kw_ref, vwt_ref, o_ref, psum_ref, bias_ref,
                *, t_base, wpos0, n_slc):
    g = pl.program_id(1)
    i = pl.program_id(2)
    nq = q_ref.shape[2]
    nl = GQA * nq
    ncp = kc_ref.shape[2]
    nsp = bias_ref.shape[0]
    nkt_s = vst_ref.shape[2]
    nkt_w = vwt_ref.shape[2]
    q = q_ref[0].reshape(nl, HEAD_DIM)
    lane = lax.broadcasted_iota(jnp.int32, (1, nl), 1)
    q0 = t_base + i * nq
    t = q0 + (lane & (nq - 1))

    s = _dot_nt(kc_ref[0, 0], q)
    n_idx = lax.broadcasted_iota(jnp.int32, (ncp, 1), 0)
    cmask = n_idx * CMP_STRIDE + (CMP_BLOCK - 1) <= t
    s = jnp.where(cmask, s, NEG)
    p = jnp.where(cmask, jnp.exp(s - jnp.max(s, axis=0, keepdims=True)), 0.0)
    l = jnp.sum(p, axis=0, keepdims=True)
    p = p * jnp.where(l > 0.0, 1.0 / l, 0.0)
    o_cmp = jnp.zeros((HEAD_DIM, nl), F32)
    for kt in range(ncp // PAGE):
        o_cmp = o_cmp + _dot(vct_ref[0, 0, kt], p[kt * PAGE:(kt + 1) * PAGE].astype(BF))

    psum = p[:, :nq]
    for r in range(1, GQA):
        psum = psum + p[:, r * nq:(r + 1) * nq]
    front = 8
    psum_ref[:front, :] = jnp.zeros((front, nq), F32)
    psum_ref[front:front + ncp, :] = psum
    psum_ref[front + ncp:, :] = jnp.zeros((psum_ref.shape[0] - front - ncp, nq), F32)
    ratio = SEL_BLOCK // CMP_STRIDE
    imp = psum_ref[pl.ds(front - 1, nsp, stride=ratio), :]
    for d in range(1, ratio + 1):
        imp = imp + psum_ref[pl.ds(front - 1 + d, nsp, stride=ratio), :]
    tq = t[:, :nq]
    j = lax.broadcasted_iota(jnp.int32, (nsp, 1), 0)
    cur = tq >> int(math.log2(SEL_BLOCK))
    forced = (j == 0) | (j == cur) | (j == cur - 1)
    score = jnp.where(j * SEL_BLOCK <= tq, imp + FORCE_BONUS * forced.astype(F32), NEG)
    sub = 8
    chunks = [score[c * sub:(c + 1) * sub] for c in range(nsp // sub)]
    jc = j[:sub]
    cnts = [jnp.zeros((sub, nq), F32) for _ in chunks]
    for ii in range(n_slc):
        ri = score[ii:ii + 1, :]
        for c, sc_c in enumerate(chunks):
            if c * sub > ii:
                one = jnp.where(ri >= sc_c, 1.0, 0.0)
            elif c * sub + sub - 1 <= ii:
                one = jnp.where(ri > sc_c, 1.0, 0.0)
            else:
                one = jnp.where(jc + c * sub > ii, jnp.where(ri >= sc_c, 1.0, 0.0), jnp.where(ri > sc_c, 1.0, 0.0))
            cnts[c] = cnts[c] + one
    cnt = jnp.concatenate(cnts, axis=0)
    bias_ref[...] = jnp.where(cnt < float(SEL_TOPK), jnp.where(score > NEG / 2, 0.0, NEG), NEG)

    key_row = lax.broadcasted_iota(jnp.int32, (PAGE, 1), 0)

    def scores(k_ref, kt):
        kb = k_ref[0, 0, pl.ds(pl.multiple_of(kt * PAGE, PAGE), PAGE), :]
        return _dot_nt(kb, q)

    def online(carry, sc, vt):
        m_old, l_old, acc = carry
        m_new = jnp.maximum(m_old, jnp.max(sc, axis=0, keepdims=True))
        alpha = jnp.exp(m_old - m_new)
        pe = jnp.exp(sc - m_new)
        return (m_new, l_old * alpha + jnp.sum(pe, axis=0, keepdims=True),
                acc * alpha + _dot(vt, pe.astype(BF)))

    def sel_bias(kt):
        halves = []
        for hb in range(PAGE // SEL_BLOCK):
            row = bias_ref[pl.ds(kt * (PAGE // SEL_BLOCK) + hb, 1), :]
            row = jnp.concatenate([row] * GQA, axis=1)
            halves.append(jnp.broadcast_to(row, (SEL_BLOCK, nl)))
        return jnp.concatenate(halves, axis=0)

    init = (jnp.full((1, nl), NEG, F32), jnp.zeros((1, nl), F32), jnp.zeros((HEAD_DIM, nl), F32))

    last = jnp.minimum(nkt_s - 1, (q0 + nq - 1) // PAGE)

    def sel_body(kt, carry):
        return online(carry, scores(ks_ref, kt) + sel_bias(kt), vst_ref[0, 0, kt])

    carry = lax.fori_loop(0, last, sel_body, init)
    sc = scores(ks_ref, last) + sel_bias(last)
    sc = jnp.where(last * PAGE + key_row <= t, sc, NEG)
    _, l_s, acc_s = online(carry, sc, vst_ref[0, 0, last])

    rel = (q0 - wpos0) // PAGE
    lo = jnp.maximum(0, rel - WINDOW // PAGE)
    hi = jnp.minimum(nkt_w - 1, rel + (nq - 1) // PAGE)

    def win_body(kt, carry):
        d = t - (wpos0 + kt * PAGE + key_row)
        sc = jnp.where((d >= 0) & (d < WINDOW), scores(kw_ref, kt), NEG)
        return online(carry, sc, vwt_ref[0, 0, kt])

    _, l_w, acc_w = lax.fori_loop(lo, hi + 1, win_body, init)

    o_sel = acc_s * jnp.where(l_s > 0.0, 1.0 / l_s, 0.0)
    o_win = acc_w * jnp.where(l_w > 0.0, 1.0 / l_w, 0.0)
    for r in range(GQA):
        sl = slice(r * nq, (r + 1) * nq)
        grow = (GQA * g + r) * 3
        o = (gt_ref[0, pl.ds(grow, 1), :] * o_cmp[:, sl]
             + gt_ref[0, pl.ds(grow + 1, 1), :] * o_sel[:, sl]
             + gt_ref[0, pl.ds(grow + 2, 1), :] * o_win[:, sl])
        o_ref[0, r * HEAD_DIM:(r + 1) * HEAD_DIM, :] = o


def _nsa(q, gt, kc, vct, ks, vst, kw, vwt, *, t_base, wpos0, n_slc):
    b, _, sq, _ = q.shape
    nq = LANES
    assert t_base % PAGE == 0 and (t_base - wpos0) % PAGE == 0 and t_base >= wpos0
    nsp = -(-n_slc // 8) * 8
    ncp = kc.shape[2]
    psum_rows = 8 + max(ncp + 8, (SEL_BLOCK // CMP_STRIDE) * nsp + 8)
    psum_rows = -(-psum_rows // 8) * 8

    def kv_specs(k, vt):
        return [pl.BlockSpec((1, 1) + k.shape[2:], lambda bi, gi, ii: (bi, gi, 0, 0)),
                pl.BlockSpec((1, 1) + vt.shape[2:], lambda bi, gi, ii: (bi, gi, 0, 0, 0))]

    return pl.pallas_call(
        functools.partial(_nsa_kernel, t_base=t_base, wpos0=wpos0, n_slc=n_slc),
        grid=(b, KV_HEADS, sq // nq),
        in_specs=[pl.BlockSpec((1, GQA, nq, HEAD_DIM), lambda bi, gi, ii: (bi, gi, ii, 0)),
                  pl.BlockSpec((1, GATE_PAD, nq), lambda bi, gi, ii: (bi, 0, ii))]
        + kv_specs(kc, vct) + kv_specs(ks, vst) + kv_specs(kw, vwt),
        out_specs=pl.BlockSpec((1, GQA * HEAD_DIM, nq), lambda bi, gi, ii: (bi, gi, ii)),
        out_shape=jax.ShapeDtypeStruct((b, NSA_WIDTH, sq), F32),
        scratch_shapes=[pltpu.VMEM((psum_rows, nq), F32), pltpu.VMEM((nsp, nq), F32)],
        compiler_params=_cparams(("parallel", "parallel", "parallel"), VMEM_LIMIT),
        name="nsa_attend",
    )(q, gt, kc, vct, ks, vst, kw, vwt)


def _rope_tables(pos):
    half = ROPE_DIM // 2
    inv_freq = ROPE_THETA ** (-jnp.arange(half, dtype=F32) / half)
    ang = pos.astype(F32)[:, None] * inv_freq[None, :]
    cos, sin = jnp.cos(ang), jnp.sin(ang)
    n = pos.shape[0]
    rest = HEAD_DIM - ROPE_DIM
    c = jnp.concatenate([cos, cos, jnp.ones((n, rest), F32)], axis=1)
    a = jnp.concatenate([-sin, jnp.zeros((n, half + rest), F32)], axis=1)
    b = jnp.concatenate([jnp.zeros((n, half), F32), sin, jnp.zeros((n, rest), F32)], axis=1)
    reps = LANES // HEAD_DIM
    return tuple(jnp.tile(x, (1, reps)) for x in (c, a, b))


def _compress_weights(cmp_pe, w_phi1, w_phi2):
    ratio = CMP_BLOCK // CMP_STRIDE
    w1 = w_phi1.reshape(ratio, CMP_STRIDE, 2, HEAD_DIM, CMP_HIDDEN)
    eye2 = jnp.eye(2, dtype=w_phi1.dtype)
    big = jnp.einsum('jscde,hg->cjshdge', w1, eye2)
    big = big.reshape(2, ratio, CMP_STRIDE * 2 * HEAD_DIM, 2 * CMP_HIDDEN).astype(BF)
    pe = cmp_pe.reshape(ratio, CMP_STRIDE, 2, HEAD_DIM)
    pe = jnp.broadcast_to(pe.transpose(2, 0, 1, 3)[:, :, :, None, :], (2, ratio, CMP_STRIDE, 2, HEAD_DIM))
    pe = pe.reshape(2, ratio, 1, CMP_STRIDE * 2 * HEAD_DIM)
    pe = jnp.broadcast_to(pe, (2, ratio, 8, pe.shape[-1])).astype(BF)
    eye4 = jnp.eye(KV_HEADS, dtype=w_phi2.dtype).reshape(2, 2, KV_HEADS)
    w2 = jnp.einsum('ced,pgh->cpgehd', w_phi2, eye4)
    w2 = w2.reshape(2, 2, 2 * CMP_HIDDEN, KV_WIDTH).astype(BF)
    return big, pe, w2


def _b_in_weight(w):
    n_gate = w.shape[1] - NSA_WIDTH - MEM_WIDTH
    gates = jnp.pad(w[:, NSA_WIDTH:NSA_WIDTH + n_gate], ((0, 0), (0, GATE_PAD - n_gate)))
    return jnp.concatenate([w[:, :NSA_WIDTH], w[:, NSA_WIDTH + n_gate:], gates], axis=1).astype(BF)


def _pad_rows(x, n):
    return jnp.pad(x, ((0, 0), (0, n - x.shape[1]), (0, 0)))


def _trunk_a(x, mem_kv, p, l, *, chunk, tm, tq, emit_v, mem_batch):
    b, s, _ = x.shape
    if chunk == PAGE:
        ws, bs = p["w_spatial"][l], jnp.repeat(p["b_spatial"][l].T, A_GROUP_DIM, axis=1)
    else:
        reps = PAGE // chunk
        ws = jnp.tile(p["w_spatial"][l][:, :chunk, :chunk], (1, reps, reps))
        bs = jnp.tile(jnp.repeat(p["b_spatial"][l].T[:chunk], A_GROUP_DIM, axis=1), (reps, 1))
    res = _a_in(x, p["w_in_a"][l], p["ln_v_g"][l][None], p["ln_v_b"][l][None], ws, bs,
                chunk=chunk, tm=tm, emit_v=emit_v)
    mixed, qm = res[0], res[1]
    om = _mem_attn(qm.reshape(mem_batch, -1, MEM_WIDTH), mem_kv, tq=tq).reshape(b, s, MEM_WIDTH)
    x = _mlp_block(x, mixed, om, p["w_out_a"][l], p, l, tm=tm, a_transposed=False)
    return x, (res[2] if emit_v else None)


def _mlp_block(x, a, om, wo, p, l, *, tm, a_transposed):
    return _out_mlp(x, a, om, wo, p["ln1_g"][l][None], p["ln1_b"][l][None], p["w_up"][l], p["w_down"][l],
                    p["ln2_g"][l][None], p["ln2_b"][l][None], tm=tm, a_transposed=a_transposed)


def kernel(x_prompt, x_sample, cache_cmp_kv, cache_slc_kv, cache_win_kv, cache_mem_kv, page_table, mem_prompt,
           w_in_a, ln_v_g, ln_v_b, w_spatial, b_spatial, w_out_a, w_in_b, w_out_b, w_kv_shared, cmp_pe,
           w_phi1, w_phi2, w_mem_kv, ln1_g, ln1_b, ln2_g, ln2_b, w_up, w_down):
    bp, s, d = x_prompt.shape
    bs, t, _ = x_sample.shape
    n_pages = page_table.shape[1]
    past = n_pages * PAGE
    n_buf = cache_win_kv.shape[1]
    assert bs * t == PAGE and PAGE % t == 0 and s % PAGE == 0 and n_buf % PAGE == 0

    p = {"w_in_a": w_in_a.astype(BF), "ln_v_g": ln_v_g, "ln_v_b": ln_v_b, "w_spatial": w_spatial,
         "b_spatial": b_spatial, "w_out_a": w_out_a.astype(BF), "w_out_b": w_out_b.astype(BF),
         "ln1_g": ln1_g, "ln1_b": ln1_b, "ln2_g": ln2_g, "ln2_b": ln2_b,
         "w_up": w_up.astype(BF), "w_down": w_down.astype(BF)}
    w_in_b2 = [_b_in_weight(w_in_b[i]) for i in range(w_in_b.shape[0])]
    w_kv = w_kv_shared.astype(BF)
    cw1, cpe, cw2 = _compress_weights(cmp_pe, w_phi1, w_phi2)

    tm_p = 256
    mem_kv_prompt = _mem_kv_proj(mem_prompt, w_mem_kv)
    x = x_prompt
    for l in range(N_A):
        x, _ = _trunk_a(x, mem_kv_prompt[l], p, l, chunk=PAGE, tm=tm_p, tq=tm_p, emit_v=False, mem_batch=bp)
    tabs_p = _rope_tables(jnp.arange(s, dtype=jnp.int32))
    cmp_rows, slc_rows, win_rows = _kv_shared(x, w_kv, tabs_p, tm=tm_p)
    n_pg = s // PAGE
    ident = jnp.arange(bp * n_pg, dtype=jnp.int32).reshape(bp, n_pg)
    cmp_blk = _compress(cmp_rows.reshape(bp * n_pg, PAGE, ROW_WIDTH), ident, None, cw1, cpe, cw2)
    n_cmp = s // CMP_STRIDE - 1
    ncp = -(-n_cmp // PAGE) * PAGE
    kc, vct = _rows_layout(_pad_rows(cmp_blk[:, 1:1 + n_cmp], ncp))
    ks, vst = _rows_layout(slc_rows)
    kw, vwt = _rows_layout(win_rows)
    for i in range(DEPTH - N_A):
        l = N_A + i
        q, qm, gt = _b_in(x, w_in_b2[i], tabs_p, tm=tm_p)
        om = _mem_attn(qm, mem_kv_prompt[l], tq=tm_p)
        ot = _nsa(q, gt, kc, vct, ks, vst, kw, vwt, t_base=0, wpos0=0, n_slc=s // SEL_BLOCK)
        x = _mlp_block(x, ot, om, p["w_out_b"][i], p, l, tm=tm_p, a_transposed=True)
    y_prompt = x
    shape5 = (2, KV_HEADS, HEAD_DIM)
    cmp_kv_prompt = cmp_rows.reshape(bp, s, *shape5)
    slc_kv_prompt = slc_rows.reshape(bp, s, *shape5)
    win_kv_prompt = win_rows[:, s - min(WINDOW, s):].reshape(bp, min(WINDOW, s), *shape5)
    mem_kv_prompt_out = mem_kv_prompt.reshape(DEPTH, bp, MEM_TOKENS, 2, MEM_HEADS, HEAD_DIM)

    rows = bs * t
    x = x_sample.reshape(1, rows, d)
    mem_s = cache_mem_kv.reshape(DEPTH, bs, MEM_TOKENS, 2 * MEM_WIDTH)
    v_rows = []
    for l in range(N_A):
        x, v = _trunk_a(x, mem_s[l], p, l, chunk=t, tm=rows, tq=t, emit_v=True, mem_batch=bs)
        v_rows.append(v.reshape(bs, t, A_WIDTH))
    pos_s = past + (jnp.arange(rows, dtype=jnp.int32) % t)
    tabs_s = _rope_tables(pos_s)
    new_cmp, new_slc, new_win = (r.reshape(bs, t, ROW_WIDTH) for r in _kv_shared(x, w_kv, tabs_s, tm=rows))
    cmp_blk = _compress(cache_cmp_kv.reshape(-1, PAGE, ROW_WIDTH), page_table, _pad_rows(new_cmp, PAGE),
                        cw1, cpe, cw2)
    n_cmp = -(-(past + t) // CMP_STRIDE) - 1
    ncp = -(-n_cmp // PAGE) * PAGE
    kc, vct = _rows_layout(_pad_rows(cmp_blk[:, 1:1 + n_cmp], ncp))
    ks, vst = _kv_layout(cache_slc_kv.reshape(-1, PAGE, ROW_WIDTH), page_table, _pad_rows(new_slc, PAGE))
    win_all = jnp.concatenate([cache_win_kv.reshape(bs, n_buf, ROW_WIDTH), new_win], axis=1)
    kw, vwt = _rows_layout(_pad_rows(win_all, n_buf + PAGE))
    n_slc = -(-(past + t) // SEL_BLOCK)
    for i in range(DEPTH - N_A):
        l = N_A + i
        q, qm, gt = _b_in(x, w_in_b2[i], tabs_s, tm=rows)
        om = _mem_attn(qm.reshape(bs, t, MEM_WIDTH), mem_s[l], tq=t).reshape(1, rows, MEM_WIDTH)
        q = q.reshape(NSA_HEADS, bs, t, HEAD_DIM).transpose(1, 0, 2, 3)
        q = jnp.pad(q, ((0, 0), (0, 0), (0, LANES - t), (0, 0)))
        gt = gt.reshape(GATE_PAD, bs, t).transpose(1, 0, 2)
        gt = jnp.pad(gt, ((0, 0), (0, 0), (0, LANES - t)))
        ot = _nsa(q, gt, kc, vct, ks, vst, kw, vwt, t_base=past, wpos0=past - n_buf, n_slc=n_slc)
        o = ot[:, :, :t].transpose(0, 2, 1).reshape(1, rows, NSA_WIDTH)
        x = _mlp_block(x, o, om, p["w_out_b"][i], p, l, tm=rows, a_transposed=False)
    y_sample = x.reshape(bs, t, d)
    cmp_kv_sample = new_cmp.reshape(bs, t, *shape5)
    slc_kv_sample = new_slc.reshape(bs, t, *shape5)
    win_kv_sample = win_all[:, t:].reshape(bs, n_buf, *shape5)
    gmlp_v_sample = jnp.stack(v_rows)

    return (y_prompt, y_sample, cmp_kv_prompt, slc_kv_prompt, win_kv_prompt, mem_kv_prompt_out,
            cmp_kv_sample, slc_kv_sample, win_kv_sample, gmlp_v_sample)
```

```python
import functools
import math

import jax
import jax.numpy as jnp
from jax import lax
from jax.experimental import pallas as pl
from jax.experimental.pallas import tpu as pltpu

BF = jnp.bfloat16
F32 = jnp.float32

DEPTH = 4
N_A = 2
A_WIDTH = 768
A_GROUPS = 4
A_GROUP_DIM = 192
MEM_TOKENS = 256
MEM_HEADS = 4
MEM_WIDTH = 256
HEAD_DIM = 64
NSA_HEADS = 12
KV_HEADS = 4
GQA = 3
NSA_WIDTH = 768
KV_WIDTH = 256
ROW_WIDTH = 2 * KV_WIDTH
CMP_BLOCK = 32
CMP_STRIDE = 16
CMP_HIDDEN = 128
SEL_BLOCK = 64
SEL_TOPK = 16
WINDOW = 512
ROPE_THETA = 500000.0
ROPE_DIM = 16
D_FF_CHUNK = 1024
ALPHA = (2.0 * DEPTH) ** 0.25
LN_EPS = 1e-5
NEG = -1e30
FORCE_BONUS = 1e4
PAGE = 128
LANES = 128
GATE_PAD = 128
SEL_GROUP = 4
VMEM_LIMIT = 56 * 1024 * 1024


def _cparams(sem, vmem=None):
    return pltpu.CompilerParams(dimension_semantics=sem, vmem_limit_bytes=vmem)


def _const_spec(shape):
    return pl.BlockSpec(shape, lambda *_: (0,) * len(shape), pipeline_mode=pl.Buffered(1))


def _ln(x, g, b):
    mu = jnp.mean(x, axis=-1, keepdims=True)
    xc = x - mu
    var = jnp.mean(xc * xc, axis=-1, keepdims=True)
    return xc * lax.rsqrt(var + LN_EPS) * g + b


def _dot(a, b):
    return jnp.dot(a, b, preferred_element_type=F32)


def _dot_nt(a, b):
    return lax.dot_general(a, b, (((1,), (1,)), ((), ())), preferred_element_type=F32)


def _rope_cols(x, c, a, b):
    cols = []
    for j in range(x.shape[1] // LANES):
        xc = x[:, j * LANES:(j + 1) * LANES]
        up = pltpu.roll(xc, LANES - ROPE_DIM // 2, axis=1)
        dn = pltpu.roll(xc, ROPE_DIM // 2, axis=1)
        cols.append(xc * c + up * a + dn * b)
    return cols[0] if len(cols) == 1 else jnp.concatenate(cols, axis=1)


def _mem_kv_kernel(x_ref, w_ref, o_ref):
    o_ref[0] = _dot(x_ref[...].astype(BF), w_ref[0].astype(BF))


def _mem_kv_proj(mem_prompt, w_mem_kv):
    b, m, d = mem_prompt.shape
    x = mem_prompt.reshape(b * m, d)
    depth, _, n = w_mem_kv.shape
    tm = 512
    out = pl.pallas_call(
        _mem_kv_kernel,
        grid=(depth, (b * m) // tm),
        in_specs=[pl.BlockSpec((tm, d), lambda l, i: (i, 0)),
                  pl.BlockSpec((1, d, n), lambda l, i: (l, 0, 0))],
        out_specs=pl.BlockSpec((1, tm, n), lambda l, i: (l, i, 0)),
        out_shape=jax.ShapeDtypeStruct((depth, b * m, n), F32),
        compiler_params=_cparams(("parallel", "parallel")),
        name="mem_kv_proj",
    )(x, w_mem_kv)
    return out.reshape(depth, b, m, n)


def _a_in_kernel(x_ref, w_ref, g_ref, b_ref, ws_ref, bs_ref, mixed_ref, qm_ref, *v_refs, chunk_shift):
    x = x_ref[0].astype(BF)
    proj = _dot(x, w_ref[...])
    uv = jax.nn.gelu(proj[:, :2 * A_WIDTH])
    u = uv[:, :A_WIDTH]
    v = _ln(uv[:, A_WIDTH:], g_ref[...], b_ref[...])
    if v_refs:
        v_refs[0][0] = v
    qm_ref[0] = proj[:, 2 * A_WIDTH:]
    tm = x.shape[0]
    r = ws_ref.shape[1]
    row = lax.broadcasted_iota(jnp.int32, (r, r), 0)
    col = lax.broadcasted_iota(jnp.int32, (r, r), 1)
    allowed = (col <= row) & ((row >> chunk_shift) == (col >> chunk_shift))
    lane = lax.broadcasted_iota(jnp.int32, (1, A_WIDTH), 1)
    wgs = [jnp.where(allowed, ws_ref[g], 0.0).astype(BF) for g in range(A_GROUPS)]
    gmask = [(lane >= g * A_GROUP_DIM) & (lane < (g + 1) * A_GROUP_DIM) for g in range(A_GROUPS)]
    vb = v.astype(BF)
    for ci in range(tm // r):
        vc = vb[ci * r:(ci + 1) * r]
        acc = bs_ref[...]
        for g in range(A_GROUPS):
            acc = acc + jnp.where(gmask[g], _dot(wgs[g], vc), 0.0)
        mixed_ref[0, ci * r:(ci + 1) * r, :] = (u[ci * r:(ci + 1) * r] * acc).astype(BF)


def _a_in(x, w_in, ln_g, ln_b, ws, bs, *, chunk, tm, emit_v):
    b, s, d = x.shape
    n = w_in.shape[1]
    r = ws.shape[1]
    outs = [jax.ShapeDtypeStruct((b, s, A_WIDTH), BF), jax.ShapeDtypeStruct((b, s, MEM_WIDTH), F32)]
    ospecs = [pl.BlockSpec((1, tm, A_WIDTH), lambda i, j: (i, j, 0)),
              pl.BlockSpec((1, tm, MEM_WIDTH), lambda i, j: (i, j, 0))]
    if emit_v:
        outs.append(jax.ShapeDtypeStruct((b, s, A_WIDTH), F32))
        ospecs.append(pl.BlockSpec((1, tm, A_WIDTH), lambda i, j: (i, j, 0)))
    return pl.pallas_call(
        functools.partial(_a_in_kernel, chunk_shift=int(math.log2(chunk))),
        grid=(b, s // tm),
        in_specs=[pl.BlockSpec((1, tm, d), lambda i, j: (i, j, 0)),
                  _const_spec((d, n)), _const_spec((1, A_WIDTH)), _const_spec((1, A_WIDTH)),
                  _const_spec((A_GROUPS, r, r)), _const_spec((r, A_WIDTH))],
        out_specs=ospecs,
        out_shape=outs,
        compiler_params=_cparams(("parallel", "parallel"), VMEM_LIMIT),
        name="a_in_gate",
    )(x, w_in, ln_g, ln_b, ws, bs)


def _mem_attn_kernel(q_ref, kv_ref, o_ref):
    q = q_ref[0]
    kv = kv_ref[0]
    k = kv[:, :MEM_WIDTH].astype(BF)
    v = kv[:, MEM_WIDTH:]
    lane = lax.broadcasted_iota(jnp.int32, (1, MEM_WIDTH), 1)
    out = jnp.zeros(q.shape, F32)
    for h in range(MEM_HEADS):
        hm = (lane >= h * HEAD_DIM) & (lane < (h + 1) * HEAD_DIM)
        s = _dot_nt(jnp.where(hm, q, 0.0).astype(BF), k) * (HEAD_DIM ** -0.5)
        m = jnp.max(s, axis=-1, keepdims=True)
        p = jnp.exp(s - m)
        p = p / jnp.sum(p, axis=-1, keepdims=True)
        out = out + _dot(p.astype(BF), jnp.where(hm, v, 0.0).astype(BF))
    o_ref[0] = out


def _mem_attn(q, mem_kv, *, tq):
    b, s, w = q.shape
    return pl.pallas_call(
        _mem_attn_kernel,
        grid=(b, s // tq),
        in_specs=[pl.BlockSpec((1, tq, w), lambda i, j: (i, j, 0)),
                  pl.BlockSpec((1, MEM_TOKENS, 2 * w), lambda i, j: (i, 0, 0))],
        out_specs=pl.BlockSpec((1, tq, w), lambda i, j: (i, j, 0)),
        out_shape=jax.ShapeDtypeStruct((b, s, w), F32),
        compiler_params=_cparams(("parallel", "parallel")),
        name="mem_attn",
    )(q, mem_kv)


def _out_mlp_kernel(x_ref, a_ref, om_ref, wo_ref, g1_ref, b1_ref, wu_ref, wd_ref, g2_ref, b2_ref, o_ref,
                    *, a_transposed):
    x = x_ref[0]
    a = a_ref[0]
    a = a.T.astype(BF) if a_transposed else a.astype(BF)
    na = a.shape[1]
    mix = _dot(a, wo_ref[:na, :]) + _dot(om_ref[0].astype(BF), wo_ref[na:, :])
    y = _ln(ALPHA * x + mix, g1_ref[...], b1_ref[...])
    yb = y.astype(BF)
    acc = jnp.zeros(x.shape, F32)
    for c in range(wu_ref.shape[1] // D_FF_CHUNK):
        h = _dot(yb, wu_ref[:, c * D_FF_CHUNK:(c + 1) * D_FF_CHUNK])
        h = jnp.square(jnp.maximum(h, 0.0)).astype(BF)
        acc = acc + _dot(h, wd_ref[c * D_FF_CHUNK:(c + 1) * D_FF_CHUNK, :])
    o_ref[0] = _ln(ALPHA * y + acc, g2_ref[...], b2_ref[...])


def _out_mlp(x, a, om, wo, g1, b1, wu, wd, g2, b2, *, tm, a_transposed):
    b, s, d = x.shape
    na = wo.shape[0] - om.shape[2]
    dff = wu.shape[1]
    if a_transposed:
        a_spec = pl.BlockSpec((1, na, tm), lambda i, j: (i, 0, j))
    else:
        a_spec = pl.BlockSpec((1, tm, na), lambda i, j: (i, j, 0))
    return pl.pallas_call(
        functools.partial(_out_mlp_kernel, a_transposed=a_transposed),
        grid=(b, s // tm),
        in_specs=[pl.BlockSpec((1, tm, d), lambda i, j: (i, j, 0)),
                  a_spec,
                  pl.BlockSpec((1, tm, om.shape[2]), lambda i, j: (i, j, 0)),
                  _const_spec(wo.shape), _const_spec((1, d)), _const_spec((1, d)),
                  _const_spec((d, dff)), _const_spec((dff, d)), _const_spec((1, d)), _const_spec((1, d))],
        out_specs=pl.BlockSpec((1, tm, d), lambda i, j: (i, j, 0)),
        out_shape=jax.ShapeDtypeStruct((b, s, d), F32),
        compiler_params=_cparams(("parallel", "parallel"), VMEM_LIMIT),
        name="out_mlp",
    )(x, a, om, wo, g1, b1, wu, wd, g2, b2)


def _kv_shared_kernel(h_ref, w_ref, c_ref, a_ref, b_ref, cmp_ref, slc_ref, win_ref):
    kv = _dot(h_ref[0].astype(BF), w_ref[...])
    c, a, b = c_ref[...], a_ref[...], b_ref[...]
    for br, ref in enumerate((cmp_ref, slc_ref, win_ref)):
        k = kv[:, br * ROW_WIDTH:br * ROW_WIDTH + KV_WIDTH]
        ref[0, :, :KV_WIDTH] = _rope_cols(k, c, a, b)
        ref[0, :, KV_WIDTH:] = kv[:, br * ROW_WIDTH + KV_WIDTH:(br + 1) * ROW_WIDTH]


def _kv_shared(h, w_kv, tabs, *, tm):
    b, s, d = h.shape
    row_spec = pl.BlockSpec((1, tm, ROW_WIDTH), lambda i, j: (i, j, 0))
    tab_spec = pl.BlockSpec((tm, LANES), lambda i, j: (j, 0))
    return pl.pallas_call(
        _kv_shared_kernel,
        grid=(b, s // tm),
        in_specs=[pl.BlockSpec((1, tm, d), lambda i, j: (i, j, 0)), _const_spec(w_kv.shape),
                  tab_spec, tab_spec, tab_spec],
        out_specs=[row_spec, row_spec, row_spec],
        out_shape=[jax.ShapeDtypeStruct((b, s, ROW_WIDTH), F32)] * 3,
        compiler_params=_cparams(("parallel", "parallel")),
        name="kv_shared",
    )(h, w_kv, *tabs)


def _b_in_kernel(x_ref, w_ref, c_ref, a_ref, b_ref, q_ref, qm_ref, gt_ref):
    proj = _dot(x_ref[0].astype(BF), w_ref[...])
    q = _rope_cols(proj[:, :NSA_WIDTH], c_ref[...], a_ref[...], b_ref[...]) * (HEAD_DIM ** -0.5)
    for h in range(NSA_HEADS):
        q_ref[0, h] = q[:, h * HEAD_DIM:(h + 1) * HEAD_DIM].astype(BF)
    qm_ref[0] = proj[:, NSA_WIDTH:NSA_WIDTH + MEM_WIDTH]
    gt_ref[0] = jax.nn.sigmoid(proj[:, NSA_WIDTH + MEM_WIDTH:]).T


def _b_in(x, w_in, tabs, *, tm):
    b, s, d = x.shape
    tab_spec = pl.BlockSpec((tm, LANES), lambda i, j: (j, 0))
    return pl.pallas_call(
        _b_in_kernel,
        grid=(b, s // tm),
        in_specs=[pl.BlockSpec((1, tm, d), lambda i, j: (i, j, 0)), _const_spec(w_in.shape),
                  tab_spec, tab_spec, tab_spec],
        out_specs=[pl.BlockSpec((1, NSA_HEADS, tm, HEAD_DIM), lambda i, j: (i, 0, j, 0)),
                   pl.BlockSpec((1, tm, MEM_WIDTH), lambda i, j: (i, j, 0)),
                   pl.BlockSpec((1, GATE_PAD, tm), lambda i, j: (i, 0, j))],
        out_shape=[jax.ShapeDtypeStruct((b, NSA_HEADS, s, HEAD_DIM), BF),
                   jax.ShapeDtypeStruct((b, s, MEM_WIDTH), F32),
                   jax.ShapeDtypeStruct((b, GATE_PAD, s), F32)],
        compiler_params=_cparams(("parallel", "parallel")),
        name="b_in",
    )(x, w_in, *tabs)


def _kv_layout_kernel(pt_ref, page_ref, *refs, tail_page):
    if tail_page is None:
        k_ref, vt_ref = refs
        x = page_ref[0]
    else:
        tail_ref, k_ref, vt_ref = refs
        x = jnp.where(pl.program_id(1) == tail_page, tail_ref[0], page_ref[0])
    vt = x[:, KV_WIDTH:].T
    for g in range(KV_HEADS):
        k_ref[0, g] = x[:, g * HEAD_DIM:(g + 1) * HEAD_DIM].astype(BF)
        vt_ref[0, g, 0] = vt[g * HEAD_DIM:(g + 1) * HEAD_DIM].astype(BF)


def _kv_layout(pool, table, tail=None):
    b, n_real = table.shape
    n_pages = n_real + (0 if tail is None else 1)
    in_specs = [pl.BlockSpec((1, PAGE, ROW_WIDTH),
                             lambda i, p, pt: (pt[i * n_real + jnp.minimum(p, n_real - 1)], 0, 0))]
    args = [pool]
    if tail is not None:
        in_specs.append(pl.BlockSpec((1, PAGE, ROW_WIDTH), lambda i, p, pt: (i, 0, 0)))
        args.append(tail)
    return pl.pallas_call(
        functools.partial(_kv_layout_kernel, tail_page=None if tail is None else n_real),
        grid_spec=pltpu.PrefetchScalarGridSpec(
            num_scalar_prefetch=1, grid=(b, n_pages), in_specs=in_specs,
            out_specs=[pl.BlockSpec((1, KV_HEADS, PAGE, HEAD_DIM), lambda i, p, pt: (i, 0, p, 0)),
                       pl.BlockSpec((1, KV_HEADS, 1, HEAD_DIM, PAGE), lambda i, p, pt: (i, 0, p, 0, 0))]),
        out_shape=[jax.ShapeDtypeStruct((b, KV_HEADS, n_pages * PAGE, HEAD_DIM), BF),
                   jax.ShapeDtypeStruct((b, KV_HEADS, n_pages, HEAD_DIM, PAGE), BF)],
        compiler_params=_cparams(("parallel", "parallel")),
        name="kv_layout",
    )(table.reshape(-1), *args)


def _rows_layout(rows):
    b, l, _ = rows.shape
    n = l // PAGE
    table = jnp.arange(b * n, dtype=jnp.int32).reshape(b, n)
    return _kv_layout(rows.reshape(b * n, PAGE, ROW_WIDTH), table)


def _compress_kernel(pt_ref, *refs, n_slots, tail_group):
    page_refs = refs[:n_slots]
    rest = refs[n_slots:]
    if tail_group is not None:
        tail_ref, rest = rest[0], rest[1:]
    w1_ref, pe_ref, w2_ref, o_ref, xs_ref = rest
    k = pl.program_id(1)
    m = n_slots * (PAGE // CMP_STRIDE)

    @pl.when(k == 0)
    def _():
        xs_ref[:, :CMP_STRIDE, :] = jnp.zeros((4, CMP_STRIDE, LANES), F32)

    for i, pr in enumerate(page_refs):
        x = pr[0]
        if tail_group is not None and i == n_slots - 1:
            x = jnp.where(k == tail_group, tail_ref[0], x)
        for vcol in range(4):
            xs_ref[vcol, CMP_STRIDE + i * PAGE:CMP_STRIDE + (i + 1) * PAGE, :] = x[:, vcol * LANES:(vcol + 1) * LANES]

    out_cols = []
    for c in range(2):
        hpe = (_dot(pe_ref[c, 0], w1_ref[c, 0])[0:1] + _dot(pe_ref[c, 1], w1_ref[c, 1])[0:1])
        acc = jnp.zeros((m, KV_WIDTH), F32)
        for hp in range(2):
            vcol = 2 * c + hp
            prev = jnp.concatenate(
                [xs_ref[vcol, pl.ds(s, m, stride=CMP_STRIDE), :] for s in range(CMP_STRIDE)], axis=1).astype(BF)
            cur = jnp.concatenate(
                [xs_ref[vcol, pl.ds(CMP_STRIDE + s, m, stride=CMP_STRIDE), :] for s in range(CMP_STRIDE)],
                axis=1).astype(BF)
            h = _dot(prev, w1_ref[c, 0]) + _dot(cur, w1_ref[c, 1]) + hpe
            acc = acc + _dot(jax.nn.gelu(h).astype(BF), w2_ref[c, hp])
        out_cols.append(acc)
    o_ref[0] = jnp.concatenate(out_cols, axis=1)

    for vcol in range(4):
        xs_ref[vcol, :CMP_STRIDE, :] = xs_ref[vcol, n_slots * PAGE:n_slots * PAGE + CMP_STRIDE, :]


def _compress(pool, table, tail, w1, pe, w2):
    b, n_real = table.shape
    n_pages = n_real + (0 if tail is None else 1)
    n_slots = max(p for p in range(1, 17) if n_pages % p == 0)
    n_groups = n_pages // n_slots
    m = n_slots * (PAGE // CMP_STRIDE)

    def page_spec(slot):
        return pl.BlockSpec(
            (1, PAGE, ROW_WIDTH),
            lambda i, k, pt: (pt[i * n_real + jnp.minimum(k * n_slots + slot, n_real - 1)], 0, 0))

    in_specs = [page_spec(slot) for slot in range(n_slots)]
    args = [pool] * n_slots
    if tail is not None:
        in_specs.append(pl.BlockSpec((1, PAGE, ROW_WIDTH), lambda i, k, pt: (i, 0, 0)))
        args.append(tail)
    in_specs += [pl.BlockSpec(w1.shape, lambda i, k, pt: (0,) * 4),
                 pl.BlockSpec(pe.shape, lambda i, k, pt: (0,) * 4),
                 pl.BlockSpec(w2.shape, lambda i, k, pt: (0,) * 4)]
    return pl.pallas_call(
        functools.partial(_compress_kernel, n_slots=n_slots,
                          tail_group=None if tail is None else n_groups - 1),
        grid_spec=pltpu.PrefetchScalarGridSpec(
            num_scalar_prefetch=1, grid=(b, n_groups), in_specs=in_specs,
            out_specs=pl.BlockSpec((1, m, ROW_WIDTH), lambda i, k, pt: (i, k, 0)),
            scratch_shapes=[pltpu.VMEM((4, CMP_STRIDE + n_slots * PAGE, LANES), F32)]),
        out_shape=jax.ShapeDtypeStruct((b, n_groups * m, ROW_WIDTH), F32),
        compiler_params=_cparams(("parallel", "arbitrary"), VMEM_LIMIT),
        name="compress",
    )(table.reshape(-1), *args, w1, pe, w2)


def _nsa_kernel(q_ref, gt_ref, kc_ref, vct_ref, ks_ref, vst_ref, kw_ref, vwt_ref, o_ref, psum_ref, bias_ref,
                sbuf0_ref, sbuf1_ref, *, t_base, wpos0, n_slc):
    g = pl.program_id(1)
    i = pl.program_id(2)
    nq = q_ref.shape[2]
    nl = GQA * nq
    ncp = kc_ref.shape[2]
    nsp = bias_ref.shape[0]
    nkt_s = vst_ref.shape[2]
    q = q_ref[0].reshape(nl, HEAD_DIM)
    lane = lax.broadcasted_iota(jnp.int32, (1, nl), 1)
    q0 = t_base + i * nq
    t = q0 + (lane & (nq - 1))

    s = _dot_nt(kc_ref[0, 0], q)
    n_idx = lax.broadcasted_iota(jnp.int32, (ncp, 1), 0)
    cmask = n_idx * CMP_STRIDE + (CMP_BLOCK - 1) <= t
    s = jnp.where(cmask, s, NEG)
    p = jnp.where(cmask, jnp.exp(s - jnp.max(s, axis=0, keepdims=True)), 0.0)
    l = jnp.sum(p, axis=0, keepdims=True)
    p = p * jnp.where(l > 0.0, 1.0 / l, 0.0)
    o_cmp = jnp.zeros((HEAD_DIM, nl), F32)
    for kt in range(ncp // PAGE):
        o_cmp = o_cmp + _dot(vct_ref[0, 0, kt], p[kt * PAGE:(kt + 1) * PAGE].astype(BF))

    psum = p[:, :nq]
    for r in range(1, GQA):
        psum = psum + p[:, r * nq:(r + 1) * nq]
    front = 8
    psum_ref[:front, :] = jnp.zeros((front, nq), F32)
    psum_ref[front:front + ncp, :] = psum
    psum_ref[front + ncp:, :] = jnp.zeros((psum_ref.shape[0] - front - ncp, nq), F32)
    ratio = SEL_BLOCK // CMP_STRIDE
    imp = psum_ref[pl.ds(front - 1, nsp, stride=ratio), :]
    for d in range(1, ratio + 1):
        imp = imp + psum_ref[pl.ds(front - 1 + d, nsp, stride=ratio), :]
    tq = t[:, :nq]
    j = lax.broadcasted_iota(jnp.int32, (nsp, 1), 0)
    cur = tq >> int(math.log2(SEL_BLOCK))
    forced = (j == 0) | (j == cur) | (j == cur - 1)
    score = jnp.where(j * SEL_BLOCK <= tq, imp + FORCE_BONUS * forced.astype(F32), NEG)
    sub = 8
    chunks = [score[c * sub:(c + 1) * sub] for c in range(nsp // sub)]
    jc = j[:sub]
    cnts = [jnp.zeros((sub, nq), F32) for _ in chunks]
    for ii in range(n_slc):
        ri = score[ii:ii + 1, :]
        for c, sc_c in enumerate(chunks):
            if c * sub > ii:
                one = jnp.where(ri >= sc_c, 1.0, 0.0)
            elif c * sub + sub - 1 <= ii:
                one = jnp.where(ri > sc_c, 1.0, 0.0)
            else:
                one = jnp.where(jc + c * sub > ii, jnp.where(ri >= sc_c, 1.0, 0.0), jnp.where(ri > sc_c, 1.0, 0.0))
            cnts[c] = cnts[c] + one
    cnt = jnp.concatenate(cnts, axis=0)
    bias_ref[...] = jnp.where(cnt < float(SEL_TOPK), jnp.where(score > NEG / 2, 0.0, NEG), NEG)

    gk = SEL_GROUP * PAGE
    blocks_per_group = gk // SEL_BLOCK
    key_row = lax.broadcasted_iota(jnp.int32, (gk, 1), 0)

    def fill(buf, gi):
        kb = ks_ref[0, 0, pl.ds(pl.multiple_of(gi * gk, gk), gk), :]
        rows = bias_ref[pl.ds(pl.multiple_of(gi * blocks_per_group, blocks_per_group), blocks_per_group), :]
        rows = jnp.concatenate([rows] * GQA, axis=1)
        bias = jnp.concatenate([jnp.broadcast_to(rows[b:b + 1], (SEL_BLOCK, nl)) for b in range(blocks_per_group)],
                               axis=0)
        buf[...] = _dot_nt(kb, q) + bias

    def consume(buf, gi, carry, causal):
        m_old, l_old, acc = carry
        sc = buf[...]
        if causal:
            sc = jnp.where(gi * gk + key_row <= t, sc, NEG)
        m_new = jnp.maximum(m_old, jnp.max(sc, axis=0, keepdims=True))
        alpha = jnp.exp(m_old - m_new)
        pe = jnp.exp(sc - m_new)
        acc = acc * alpha
        for u in range(SEL_GROUP):
            acc = acc + _dot(vst_ref[0, 0, gi * SEL_GROUP + u], pe[u * PAGE:(u + 1) * PAGE].astype(BF))
        return m_new, l_old * alpha + jnp.sum(pe, axis=0, keepdims=True), acc

    init = (jnp.full((1, nl), NEG, F32), jnp.zeros((1, nl), F32), jnp.zeros((HEAD_DIM, nl), F32))

    n_groups = jnp.minimum(nkt_s - 1, (q0 + nq - 1) // PAGE) // SEL_GROUP + 1
    fill(sbuf0_ref, 0)

    def pair_body(kk, carry):
        gi = 2 * kk
        fill(sbuf1_ref, gi + 1)
        carry = consume(sbuf0_ref, gi, carry, False)
        fill(sbuf0_ref, gi + 2)
        return consume(sbuf1_ref, gi + 1, carry, False)

    n_pairs = (n_groups - 1) // 2
    carry = lax.fori_loop(0, n_pairs, pair_body, init)
    gi = 2 * n_pairs

    def odd_tail(carry):
        fill(sbuf1_ref, gi + 1)
        carry = consume(sbuf0_ref, gi, carry, False)
        return consume(sbuf1_ref, gi + 1, carry, True)

    def even_tail(carry):
        return consume(sbuf0_ref, gi, carry, True)

    _, l_s, acc_s = lax.cond(n_groups - 1 - gi == 1, odd_tail, even_tail, carry)

    rel = (q0 - wpos0) // PAGE
    n_wt = WINDOW // PAGE + 1
    tile_row = lax.broadcasted_iota(jnp.int32, (PAGE, 1), 0)
    parts = []
    for u in range(n_wt):
        kt = rel - (n_wt - 1) + u
        ktc = jnp.maximum(kt, 0)
        kb = kw_ref[0, 0, pl.ds(pl.multiple_of(ktc * PAGE, PAGE), PAGE), :]
        krel = kt * PAGE + tile_row
        kpos = jnp.where(krel >= 0, wpos0 + krel, -(1 << 30))
        d = lax.bitcast_convert_type(t - kpos, jnp.uint32)
        parts.append(jnp.where(d < WINDOW, _dot_nt(kb, q), NEG))
    sw = jnp.concatenate(parts, axis=0)
    pw = jnp.exp(sw - jnp.max(sw, axis=0, keepdims=True))
    l_w = jnp.sum(pw, axis=0, keepdims=True)
    acc_w = jnp.zeros((HEAD_DIM, nl), F32)
    for u in range(n_wt):
        ktc = jnp.maximum(rel - (n_wt - 1) + u, 0)
        acc_w = acc_w + _dot(vwt_ref[0, 0, ktc], pw[u * PAGE:(u + 1) * PAGE].astype(BF))

    o_sel = acc_s * jnp.where(l_s > 0.0, 1.0 / l_s, 0.0)
    o_win = acc_w * jnp.where(l_w > 0.0, 1.0 / l_w, 0.0)
    for r in range(GQA):
        sl = slice(r * nq, (r + 1) * nq)
        grow = (GQA * g + r) * 3
        o = (gt_ref[0, pl.ds(grow, 1), :] * o_cmp[:, sl]
             + gt_ref[0, pl.ds(grow + 1, 1), :] * o_sel[:, sl]
             + gt_ref[0, pl.ds(grow + 2, 1), :] * o_win[:, sl])
        o_ref[0, r * HEAD_DIM:(r + 1) * HEAD_DIM, :] = o


def _nsa(q, gt, kc, vct, ks, vst, kw, vwt, *, t_base, wpos0, n_slc):
    b, _, sq, _ = q.shape
    nq = LANES
    assert t_base % PAGE == 0 and (t_base - wpos0) % PAGE == 0 and t_base >= wpos0
    assert vst.shape[2] % SEL_GROUP == 0
    nsp = max(-(-n_slc // 8) * 8, vst.shape[2] * (PAGE // SEL_BLOCK))
    ncp = kc.shape[2]
    psum_rows = 8 + max(ncp + 8, (SEL_BLOCK // CMP_STRIDE) * nsp + 8)
    psum_rows = -(-psum_rows // 8) * 8

    def kv_specs(k, vt):
        return [pl.BlockSpec((1, 1) + k.shape[2:], lambda bi, gi, ii: (bi, gi, 0, 0)),
                pl.BlockSpec((1, 1) + vt.shape[2:], lambda bi, gi, ii: (bi, gi, 0, 0, 0))]

    return pl.pallas_call(
        functools.partial(_nsa_kernel, t_base=t_base, wpos0=wpos0, n_slc=n_slc),
        grid=(b, KV_HEADS, sq // nq),
        in_specs=[pl.BlockSpec((1, GQA, nq, HEAD_DIM), lambda bi, gi, ii: (bi, gi, ii, 0)),
                  pl.BlockSpec((1, GATE_PAD, nq), lambda bi, gi, ii: (bi, 0, ii))]
        + kv_specs(kc, vct) + kv_specs(ks, vst) + kv_specs(kw, vwt),
        out_specs=pl.BlockSpec((1, GQA * HEAD_DIM, nq), lambda bi, gi, ii: (bi, gi, ii)),
        out_shape=jax.ShapeDtypeStruct((b, NSA_WIDTH, sq), F32),
        scratch_shapes=[pltpu.VMEM((psum_rows, nq), F32), pltpu.VMEM((nsp, nq), F32),
                        pltpu.VMEM((SEL_GROUP * PAGE, GQA * nq), F32), pltpu.VMEM((SEL_GROUP * PAGE, GQA * nq), F32)],
        compiler_params=_cparams(("parallel", "parallel", "parallel"), VMEM_LIMIT),
        name="nsa_attend",
    )(q, gt, kc, vct, ks, vst, kw, vwt)


def _rope_tables(pos):
    half = ROPE_DIM // 2
    inv_freq = ROPE_THETA ** (-jnp.arange(half, dtype=F32) / half)
    ang = pos.astype(F32)[:, None] * inv_freq[None, :]
    cos, sin = jnp.cos(ang), jnp.sin(ang)
    n = pos.shape[0]
    rest = HEAD_DIM - ROPE_DIM
    c = jnp.concatenate([cos, cos, jnp.ones((n, rest), F32)], axis=1)
    a = jnp.concatenate([-sin, jnp.zeros((n, half + rest), F32)], axis=1)
    b = jnp.concatenate([jnp.zeros((n, half), F32), sin, jnp.zeros((n, rest), F32)], axis=1)
    reps = LANES // HEAD_DIM
    return tuple(jnp.tile(x, (1, reps)) for x in (c, a, b))


def _compress_weights(cmp_pe, w_phi1, w_phi2):
    ratio = CMP_BLOCK // CMP_STRIDE
    w1 = w_phi1.reshape(ratio, CMP_STRIDE, 2, HEAD_DIM, CMP_HIDDEN)
    eye2 = jnp.eye(2, dtype=w_phi1.dtype)
    big = jnp.einsum('jscde,hg->cjshdge', w1, eye2)
    big = big.reshape(2, ratio, CMP_STRIDE * 2 * HEAD_DIM, 2 * CMP_HIDDEN).astype(BF)
    pe = cmp_pe.reshape(ratio, CMP_STRIDE, 2, HEAD_DIM)
    pe = jnp.broadcast_to(pe.transpose(2, 0, 1, 3)[:, :, :, None, :], (2, ratio, CMP_STRIDE, 2, HEAD_DIM))
    pe = pe.reshape(2, ratio, 1, CMP_STRIDE * 2 * HEAD_DIM)
    pe = jnp.broadcast_to(pe, (2, ratio, 8, pe.shape[-1])).astype(BF)
    eye4 = jnp.eye(KV_HEADS, dtype=w_phi2.dtype).reshape(2, 2, KV_HEADS)
    w2 = jnp.einsum('ced,pgh->cpgehd', w_phi2, eye4)
    w2 = w2.reshape(2, 2, 2 * CMP_HIDDEN, KV_WIDTH).astype(BF)
    return big, pe, w2


def _b_in_weight(w):
    n_gate = w.shape[1] - NSA_WIDTH - MEM_WIDTH
    gates = jnp.pad(w[:, NSA_WIDTH:NSA_WIDTH + n_gate], ((0, 0), (0, GATE_PAD - n_gate)))
    return jnp.concatenate([w[:, :NSA_WIDTH], w[:, NSA_WIDTH + n_gate:], gates], axis=1).astype(BF)


def _pad_rows(x, n):
    return jnp.pad(x, ((0, 0), (0, n - x.shape[1]), (0, 0)))


def _trunk_a(x, mem_kv, p, l, *, chunk, tm, tq, emit_v, mem_batch):
    b, s, _ = x.shape
    if chunk == PAGE:
        ws, bs = p["w_spatial"][l], jnp.repeat(p["b_spatial"][l].T, A_GROUP_DIM, axis=1)
    else:
        reps = PAGE // chunk
        ws = jnp.tile(p["w_spatial"][l][:, :chunk, :chunk], (1, reps, reps))
        bs = jnp.tile(jnp.repeat(p["b_spatial"][l].T[:chunk], A_GROUP_DIM, axis=1), (reps, 1))
    res = _a_in(x, p["w_in_a"][l], p["ln_v_g"][l][None], p["ln_v_b"][l][None], ws, bs,
                chunk=chunk, tm=tm, emit_v=emit_v)
    mixed, qm = res[0], res[1]
    om = _mem_attn(qm.reshape(mem_batch, -1, MEM_WIDTH), mem_kv, tq=tq).reshape(b, s, MEM_WIDTH)
    x = _mlp_block(x, mixed, om, p["w_out_a"][l], p, l, tm=tm, a_transposed=False)
    return x, (res[2] if emit_v else None)


def _mlp_block(x, a, om, wo, p, l, *, tm, a_transposed):
    return _out_mlp(x, a, om, wo, p["ln1_g"][l][None], p["ln1_b"][l][None], p["w_up"][l], p["w_down"][l],
                    p["ln2_g"][l][None], p["ln2_b"][l][None], tm=tm, a_transposed=a_transposed)


def kernel(x_prompt, x_sample, cache_cmp_kv, cache_slc_kv, cache_win_kv, cache_mem_kv, page_table, mem_prompt,
           w_in_a, ln_v_g, ln_v_b, w_spatial, b_spatial, w_out_a, w_in_b, w_out_b, w_kv_shared, cmp_pe,
           w_phi1, w_phi2, w_mem_kv, ln1_g, ln1_b, ln2_g, ln2_b, w_up, w_down):
    bp, s, d = x_prompt.shape
    bs, t, _ = x_sample.shape
    n_pages = page_table.shape[1]
    past = n_pages * PAGE
    n_buf = cache_win_kv.shape[1]
    assert bs * t == PAGE and PAGE % t == 0 and s % PAGE == 0 and n_buf % PAGE == 0

    p = {"w_in_a": w_in_a.astype(BF), "ln_v_g": ln_v_g, "ln_v_b": ln_v_b, "w_spatial": w_spatial,
         "b_spatial": b_spatial, "w_out_a": w_out_a.astype(BF), "w_out_b": w_out_b.astype(BF),
         "ln1_g": ln1_g, "ln1_b": ln1_b, "ln2_g": ln2_g, "ln2_b": ln2_b,
         "w_up": w_up.astype(BF), "w_down": w_down.astype(BF)}
    w_in_b2 = [_b_in_weight(w_in_b[i]) for i in range(w_in_b.shape[0])]
    w_kv = w_kv_shared.astype(BF)
    cw1, cpe, cw2 = _compress_weights(cmp_pe, w_phi1, w_phi2)

    tm_p = 256
    mem_kv_prompt = _mem_kv_proj(mem_prompt, w_mem_kv)
    x = x_prompt
    for l in range(N_A):
        x, _ = _trunk_a(x, mem_kv_prompt[l], p, l, chunk=PAGE, tm=tm_p, tq=tm_p, emit_v=False, mem_batch=bp)
    tabs_p = _rope_tables(jnp.arange(s, dtype=jnp.int32))
    cmp_rows, slc_rows, win_rows = _kv_shared(x, w_kv, tabs_p, tm=tm_p)
    n_pg = s // PAGE
    ident = jnp.arange(bp * n_pg, dtype=jnp.int32).reshape(bp, n_pg)
    cmp_blk = _compress(cmp_rows.reshape(bp * n_pg, PAGE, ROW_WIDTH), ident, None, cw1, cpe, cw2)
    n_cmp = s // CMP_STRIDE - 1
    ncp = -(-n_cmp // PAGE) * PAGE
    kc, vct = _rows_layout(_pad_rows(cmp_blk[:, 1:1 + n_cmp], ncp))
    ks, vst = _rows_layout(slc_rows)
    kw, vwt = _rows_layout(win_rows)
    for i in range(DEPTH - N_A):
        l = N_A + i
        q, qm, gt = _b_in(x, w_in_b2[i], tabs_p, tm=tm_p)
        om = _mem_attn(qm, mem_kv_prompt[l], tq=tm_p)
        ot = _nsa(q, gt, kc, vct, ks, vst, kw, vwt, t_base=0, wpos0=0, n_slc=s // SEL_BLOCK)
        x = _mlp_block(x, ot, om, p["w_out_b"][i], p, l, tm=tm_p, a_transposed=True)
    y_prompt = x
    shape5 = (2, KV_HEADS, HEAD_DIM)
    cmp_kv_prompt = cmp_rows.reshape(bp, s, *shape5)
    slc_kv_prompt = slc_rows.reshape(bp, s, *shape5)
    win_kv_prompt = win_rows[:, s - min(WINDOW, s):].reshape(bp, min(WINDOW, s), *shape5)
    mem_kv_prompt_out = mem_kv_prompt.reshape(DEPTH, bp, MEM_TOKENS, 2, MEM_HEADS, HEAD_DIM)

    rows = bs * t
    x = x_sample.reshape(1, rows, d)
    mem_s = cache_mem_kv.reshape(DEPTH, bs, MEM_TOKENS, 2 * MEM_WIDTH)
    v_rows = []
    for l in range(N_A):
        x, v = _trunk_a(x, mem_s[l], p, l, chunk=t, tm=rows, tq=t, emit_v=True, mem_batch=bs)
        v_rows.append(v.reshape(bs, t, A_WIDTH))
    pos_s = past + (jnp.arange(rows, dtype=jnp.int32) % t)
    tabs_s = _rope_tables(pos_s)
    new_cmp, new_slc, new_win = (r.reshape(bs, t, ROW_WIDTH) for r in _kv_shared(x, w_kv, tabs_s, tm=rows))
    cmp_blk = _compress(cache_cmp_kv.reshape(-1, PAGE, ROW_WIDTH), page_table, _pad_rows(new_cmp, PAGE),
                        cw1, cpe, cw2)
    n_cmp = -(-(past + t) // CMP_STRIDE) - 1
    ncp = -(-n_cmp // PAGE) * PAGE
    kc, vct = _rows_layout(_pad_rows(cmp_blk[:, 1:1 + n_cmp], ncp))
    ks, vst = _kv_layout(cache_slc_kv.reshape(-1, PAGE, ROW_WIDTH), page_table, _pad_rows(new_slc, PAGE))
    extra = -vst.shape[2] % SEL_GROUP
    ks = jnp.pad(ks, ((0, 0), (0, 0), (0, extra * PAGE), (0, 0)))
    vst = jnp.pad(vst, ((0, 0), (0, 0), (0, extra), (0, 0), (0, 0)))
    win_all = jnp.concatenate([cache_win_kv.reshape(bs, n_buf, ROW_WIDTH), new_win], axis=1)
    kw, vwt = _rows_layout(_pad_rows(win_all, n_buf + PAGE))
    n_slc = -(-(past + t) // SEL_BLOCK)
    for i in range(DEPTH - N_A):
        l = N_A + i
        q, qm, gt = _b_in(x, w_in_b2[i], tabs_s, tm=rows)
        om = _mem_attn(qm.reshape(bs, t, MEM_WIDTH), mem_s[l], tq=t).reshape(1, rows, MEM_WIDTH)
        q = q.reshape(NSA_HEADS, bs, t, HEAD_DIM).transpose(1, 0, 2, 3)
        q = jnp.pad(q, ((0, 0), (0, 0), (0, LANES - t), (0, 0)))
        gt = gt.reshape(GATE_PAD, bs, t).transpose(1, 0, 2)
        gt = jnp.pad(gt, ((0, 0), (0, 0), (0, LANES - t)))
        ot = _nsa(q, gt, kc, vct, ks, vst, kw, vwt, t_base=past, wpos0=past - n_buf, n_slc=n_slc)
        o = ot[:, :, :t].transpose(0, 2, 1).reshape(1, rows, NSA_WIDTH)
        x = _mlp_block(x, o, om, p["w_out_b"][i], p, l, tm=rows, a_transposed=False)
    y_sample = x.reshape(bs, t, d)
    cmp_kv_sample = new_cmp.reshape(bs, t, *shape5)
    slc_kv_sample = new_slc.reshape(bs, t, *shape5)
    win_kv_sample = win_all[:, t:].reshape(bs, n_buf, *shape5)
    gmlp_v_sample = jnp.stack(v_rows)

    return (y_prompt, y_sample, cmp_kv_prompt, slc_kv_prompt, win_kv_prompt, mem_kv_prompt_out,
            cmp_kv_sample, slc_kv_sample, win_kv_sample, gmlp_v_sample)
```

```python
import functools
import math

import jax
import jax.numpy as jnp
from jax import lax
from jax.experimental import pallas as pl
from jax.experimental.pallas import tpu as pltpu

BF = jnp.bfloat16
F32 = jnp.float32

DEPTH = 4
N_A = 2
A_WIDTH = 768
A_GROUPS = 4
A_GROUP_DIM = 192
MEM_TOKENS = 256
MEM_HEADS = 4
MEM_WIDTH = 256
HEAD_DIM = 64
NSA_HEADS = 12
KV_HEADS = 4
GQA = 3
NSA_WIDTH = 768
KV_WIDTH = 256
ROW_WIDTH = 2 * KV_WIDTH
CMP_BLOCK = 32
CMP_STRIDE = 16
CMP_HIDDEN = 128
SEL_BLOCK = 64
SEL_TOPK = 16
WINDOW = 512
ROPE_THETA = 500000.0
ROPE_DIM = 16
D_FF_CHUNK = 1024
ALPHA = (2.0 * DEPTH) ** 0.25
LN_EPS = 1e-5
NEG = -1e30
FORCE_BONUS = 1e4
PAGE = 128
LANES = 128
GATE_PAD = 128
SEL_GROUP = 4
VMEM_LIMIT = 56 * 1024 * 1024


def _cparams(sem, vmem=None):
    return pltpu.CompilerParams(dimension_semantics=sem, vmem_limit_bytes=vmem)


def _const_spec(shape):
    return pl.BlockSpec(shape, lambda *_: (0,) * len(shape), pipeline_mode=pl.Buffered(1))


def _ln(x, g, b):
    mu = jnp.mean(x, axis=-1, keepdims=True)
    xc = x - mu
    var = jnp.mean(xc * xc, axis=-1, keepdims=True)
    return xc * lax.rsqrt(var + LN_EPS) * g + b


def _dot(a, b):
    return jnp.dot(a, b, preferred_element_type=F32)


def _dot_nt(a, b):
    return lax.dot_general(a, b, (((1,), (1,)), ((), ())), preferred_element_type=F32)


def _rope_cols(x, c, a, b):
    cols = []
    for j in range(x.shape[1] // LANES):
        xc = x[:, j * LANES:(j + 1) * LANES]
        up = pltpu.roll(xc, LANES - ROPE_DIM // 2, axis=1)
        dn = pltpu.roll(xc, ROPE_DIM // 2, axis=1)
        cols.append(xc * c + up * a + dn * b)
    return cols[0] if len(cols) == 1 else jnp.concatenate(cols, axis=1)


def _mem_kv_kernel(x_ref, w_ref, o_ref):
    o_ref[0] = _dot(x_ref[...].astype(BF), w_ref[0].astype(BF))


def _mem_kv_proj(mem_prompt, w_mem_kv):
    b, m, d = mem_prompt.shape
    x = mem_prompt.reshape(b * m, d)
    depth, _, n = w_mem_kv.shape
    tm = 512
    out = pl.pallas_call(
        _mem_kv_kernel,
        grid=(depth, (b * m) // tm),
        in_specs=[pl.BlockSpec((tm, d), lambda l, i: (i, 0)),
                  pl.BlockSpec((1, d, n), lambda l, i: (l, 0, 0))],
        out_specs=pl.BlockSpec((1, tm, n), lambda l, i: (l, i, 0)),
        out_shape=jax.ShapeDtypeStruct((depth, b * m, n), F32),
        compiler_params=_cparams(("parallel", "parallel")),
        name="mem_kv_proj",
    )(x, w_mem_kv)
    return out.reshape(depth, b, m, n)


def _a_in_kernel(x_ref, w_ref, g_ref, b_ref, ws_ref, bs_ref, mixed_ref, qm_ref, *v_refs, chunk_shift):
    x = x_ref[0].astype(BF)
    proj = _dot(x, w_ref[...])
    uv = jax.nn.gelu(proj[:, :2 * A_WIDTH])
    u = uv[:, :A_WIDTH]
    v = _ln(uv[:, A_WIDTH:], g_ref[...], b_ref[...])
    if v_refs:
        v_refs[0][0] = v
    qm_ref[0] = proj[:, 2 * A_WIDTH:]
    tm = x.shape[0]
    r = ws_ref.shape[1]
    row = lax.broadcasted_iota(jnp.int32, (r, r), 0)
    col = lax.broadcasted_iota(jnp.int32, (r, r), 1)
    allowed = (col <= row) & ((row >> chunk_shift) == (col >> chunk_shift))
    lane = lax.broadcasted_iota(jnp.int32, (1, A_WIDTH), 1)
    wgs = [jnp.where(allowed, ws_ref[g], 0.0).astype(BF) for g in range(A_GROUPS)]
    gmask = [(lane >= g * A_GROUP_DIM) & (lane < (g + 1) * A_GROUP_DIM) for g in range(A_GROUPS)]
    vb = v.astype(BF)
    for ci in range(tm // r):
        vc = vb[ci * r:(ci + 1) * r]
        acc = bs_ref[...]
        for g in range(A_GROUPS):
            acc = acc + jnp.where(gmask[g], _dot(wgs[g], vc), 0.0)
        mixed_ref[0, ci * r:(ci + 1) * r, :] = (u[ci * r:(ci + 1) * r] * acc).astype(BF)


def _a_in(x, w_in, ln_g, ln_b, ws, bs, *, chunk, tm, emit_v):
    b, s, d = x.shape
    n = w_in.shape[1]
    r = ws.shape[1]
    outs = [jax.ShapeDtypeStruct((b, s, A_WIDTH), BF), jax.ShapeDtypeStruct((b, s, MEM_WIDTH), F32)]
    ospecs = [pl.BlockSpec((1, tm, A_WIDTH), lambda i, j: (i, j, 0)),
              pl.BlockSpec((1, tm, MEM_WIDTH), lambda i, j: (i, j, 0))]
    if emit_v:
        outs.append(jax.ShapeDtypeStruct((b, s, A_WIDTH), F32))
        ospecs.append(pl.BlockSpec((1, tm, A_WIDTH), lambda i, j: (i, j, 0)))
    return pl.pallas_call(
        functools.partial(_a_in_kernel, chunk_shift=int(math.log2(chunk))),
        grid=(b, s // tm),
        in_specs=[pl.BlockSpec((1, tm, d), lambda i, j: (i, j, 0)),
                  _const_spec((d, n)), _const_spec((1, A_WIDTH)), _const_spec((1, A_WIDTH)),
                  _const_spec((A_GROUPS, r, r)), _const_spec((r, A_WIDTH))],
        out_specs=ospecs,
        out_shape=outs,
        compiler_params=_cparams(("parallel", "parallel"), VMEM_LIMIT),
        name="a_in_gate",
    )(x, w_in, ln_g, ln_b, ws, bs)


def _mem_attn_kernel(q_ref, kv_ref, o_ref):
    q = q_ref[0]
    kv = kv_ref[0]
    k = kv[:, :MEM_WIDTH].astype(BF)
    v = kv[:, MEM_WIDTH:]
    lane = lax.broadcasted_iota(jnp.int32, (1, MEM_WIDTH), 1)
    out = jnp.zeros(q.shape, F32)
    for h in range(MEM_HEADS):
        hm = (lane >= h * HEAD_DIM) & (lane < (h + 1) * HEAD_DIM)
        s = _dot_nt(jnp.where(hm, q, 0.0).astype(BF), k) * (HEAD_DIM ** -0.5)
        m = jnp.max(s, axis=-1, keepdims=True)
        p = jnp.exp(s - m)
        p = p / jnp.sum(p, axis=-1, keepdims=True)
        out = out + _dot(p.astype(BF), jnp.where(hm, v, 0.0).astype(BF))
    o_ref[0] = out


def _mem_attn(q, mem_kv, *, tq):
    b, s, w = q.shape
    return pl.pallas_call(
        _mem_attn_kernel,
        grid=(b, s // tq),
        in_specs=[pl.BlockSpec((1, tq, w), lambda i, j: (i, j, 0)),
                  pl.BlockSpec((1, MEM_TOKENS, 2 * w), lambda i, j: (i, 0, 0))],
        out_specs=pl.BlockSpec((1, tq, w), lambda i, j: (i, j, 0)),
        out_shape=jax.ShapeDtypeStruct((b, s, w), F32),
        compiler_params=_cparams(("parallel", "parallel")),
        name="mem_attn",
    )(q, mem_kv)


def _out_mlp_kernel(x_ref, a_ref, om_ref, wo_ref, g1_ref, b1_ref, wu_ref, wd_ref, g2_ref, b2_ref, o_ref,
                    *, a_transposed):
    x = x_ref[0]
    a = a_ref[0]
    a = a.T.astype(BF) if a_transposed else a.astype(BF)
    na = a.shape[1]
    mix = _dot(a, wo_ref[:na, :]) + _dot(om_ref[0].astype(BF), wo_ref[na:, :])
    y = _ln(ALPHA * x + mix, g1_ref[...], b1_ref[...])
    yb = y.astype(BF)
    acc = jnp.zeros(x.shape, F32)
    for c in range(wu_ref.shape[1] // D_FF_CHUNK):
        h = _dot(yb, wu_ref[:, c * D_FF_CHUNK:(c + 1) * D_FF_CHUNK])
        h = jnp.square(jnp.maximum(h, 0.0)).astype(BF)
        acc = acc + _dot(h, wd_ref[c * D_FF_CHUNK:(c + 1) * D_FF_CHUNK, :])
    o_ref[0] = _ln(ALPHA * y + acc, g2_ref[...], b2_ref[...])


def _out_mlp(x, a, om, wo, g1, b1, wu, wd, g2, b2, *, tm, a_transposed):
    b, s, d = x.shape
    na = wo.shape[0] - om.shape[2]
    dff = wu.shape[1]
    if a_transposed:
        a_spec = pl.BlockSpec((1, na, tm), lambda i, j: (i, 0, j))
    else:
        a_spec = pl.BlockSpec((1, tm, na), lambda i, j: (i, j, 0))
    return pl.pallas_call(
        functools.partial(_out_mlp_kernel, a_transposed=a_transposed),
        grid=(b, s // tm),
        in_specs=[pl.BlockSpec((1, tm, d), lambda i, j: (i, j, 0)),
                  a_spec,
                  pl.BlockSpec((1, tm, om.shape[2]), lambda i, j: (i, j, 0)),
                  _const_spec(wo.shape), _const_spec((1, d)), _const_spec((1, d)),
                  _const_spec((d, dff)), _const_spec((dff, d)), _const_spec((1, d)), _const_spec((1, d))],
        out_specs=pl.BlockSpec((1, tm, d), lambda i, j: (i, j, 0)),
        out_shape=jax.ShapeDtypeStruct((b, s, d), F32),
        compiler_params=_cparams(("parallel", "parallel"), VMEM_LIMIT),
        name="out_mlp",
    )(x, a, om, wo, g1, b1, wu, wd, g2, b2)


def _kv_shared_kernel(h_ref, w_ref, c_ref, a_ref, b_ref, cmp_ref, slc_ref, win_ref, *attn_refs):
    kv = _dot(h_ref[0].astype(BF), w_ref[...])
    c, a, b = c_ref[...], a_ref[...], b_ref[...]
    tm = kv.shape[0]
    for br, ref in enumerate((cmp_ref, slc_ref, win_ref)):
        k = _rope_cols(kv[:, br * ROW_WIDTH:br * ROW_WIDTH + KV_WIDTH], c, a, b)
        v = kv[:, br * ROW_WIDTH + KV_WIDTH:(br + 1) * ROW_WIDTH]
        ref[0, :, :KV_WIDTH] = k
        ref[0, :, KV_WIDTH:] = v
        if attn_refs and br > 0:
            k_ref, vt_ref = attn_refs[2 * (br - 1):2 * br]
            vt = v.T
            for g in range(KV_HEADS):
                k_ref[0, g] = k[:, g * HEAD_DIM:(g + 1) * HEAD_DIM].astype(BF)
                for u in range(tm // PAGE):
                    vt_ref[0, g, u] = vt[g * HEAD_DIM:(g + 1) * HEAD_DIM, u * PAGE:(u + 1) * PAGE].astype(BF)


def _kv_shared(h, w_kv, tabs, *, tm, attn_layouts):
    b, s, d = h.shape
    row_spec = pl.BlockSpec((1, tm, ROW_WIDTH), lambda i, j: (i, j, 0))
    tab_spec = pl.BlockSpec((tm, LANES), lambda i, j: (j, 0))
    out_specs = [row_spec, row_spec, row_spec]
    out_shape = [jax.ShapeDtypeStruct((b, s, ROW_WIDTH), F32)] * 3
    if attn_layouts:
        out_specs += [pl.BlockSpec((1, KV_HEADS, tm, HEAD_DIM), lambda i, j: (i, 0, j, 0)),
                      pl.BlockSpec((1, KV_HEADS, tm // PAGE, HEAD_DIM, PAGE), lambda i, j: (i, 0, j, 0, 0))] * 2
        out_shape += [jax.ShapeDtypeStruct((b, KV_HEADS, s, HEAD_DIM), BF),
                      jax.ShapeDtypeStruct((b, KV_HEADS, s // PAGE, HEAD_DIM, PAGE), BF)] * 2
    return pl.pallas_call(
        _kv_shared_kernel,
        grid=(b, s // tm),
        in_specs=[pl.BlockSpec((1, tm, d), lambda i, j: (i, j, 0)), _const_spec(w_kv.shape),
                  tab_spec, tab_spec, tab_spec],
        out_specs=out_specs,
        out_shape=out_shape,
        compiler_params=_cparams(("parallel", "parallel")),
        name="kv_shared",
    )(h, w_kv, *tabs)


def _b_in_kernel(x_ref, w_ref, c_ref, a_ref, b_ref, q_ref, qm_ref, gt_ref):
    proj = _dot(x_ref[0].astype(BF), w_ref[...])
    q = _rope_cols(proj[:, :NSA_WIDTH], c_ref[...], a_ref[...], b_ref[...]) * (HEAD_DIM ** -0.5)
    for h in range(NSA_HEADS):
        q_ref[0, h] = q[:, h * HEAD_DIM:(h + 1) * HEAD_DIM].astype(BF)
    qm_ref[0] = proj[:, NSA_WIDTH:NSA_WIDTH + MEM_WIDTH]
    gt_ref[0] = jax.nn.sigmoid(proj[:, NSA_WIDTH + MEM_WIDTH:]).T


def _b_in(x, w_in, tabs, *, tm):
    b, s, d = x.shape
    tab_spec = pl.BlockSpec((tm, LANES), lambda i, j: (j, 0))
    return pl.pallas_call(
        _b_in_kernel,
        grid=(b, s // tm),
        in_specs=[pl.BlockSpec((1, tm, d), lambda i, j: (i, j, 0)), _const_spec(w_in.shape),
                  tab_spec, tab_spec, tab_spec],
        out_specs=[pl.BlockSpec((1, NSA_HEADS, tm, HEAD_DIM), lambda i, j: (i, 0, j, 0)),
                   pl.BlockSpec((1, tm, MEM_WIDTH), lambda i, j: (i, j, 0)),
                   pl.BlockSpec((1, GATE_PAD, tm), lambda i, j: (i, 0, j))],
        out_shape=[jax.ShapeDtypeStruct((b, NSA_HEADS, s, HEAD_DIM), BF),
                   jax.ShapeDtypeStruct((b, s, MEM_WIDTH), F32),
                   jax.ShapeDtypeStruct((b, GATE_PAD, s), F32)],
        compiler_params=_cparams(("parallel", "parallel")),
        name="b_in",
    )(x, w_in, *tabs)


def _kv_layout_kernel(x_ref, k_ref, vt_ref):
    x = x_ref[0]
    vt = x[:, KV_WIDTH:].T
    for g in range(KV_HEADS):
        k_ref[0, g] = x[:, g * HEAD_DIM:(g + 1) * HEAD_DIM].astype(BF)
        vt_ref[0, g, 0] = vt[g * HEAD_DIM:(g + 1) * HEAD_DIM].astype(BF)


def _rows_layout(rows):
    b, l, _ = rows.shape
    n = l // PAGE
    return pl.pallas_call(
        _kv_layout_kernel,
        grid=(b, n),
        in_specs=[pl.BlockSpec((1, PAGE, ROW_WIDTH), lambda i, p: (i, p, 0))],
        out_specs=[pl.BlockSpec((1, KV_HEADS, PAGE, HEAD_DIM), lambda i, p: (i, 0, p, 0)),
                   pl.BlockSpec((1, KV_HEADS, 1, HEAD_DIM, PAGE), lambda i, p: (i, 0, p, 0, 0))],
        out_shape=[jax.ShapeDtypeStruct((b, KV_HEADS, l, HEAD_DIM), BF),
                   jax.ShapeDtypeStruct((b, KV_HEADS, n, HEAD_DIM, PAGE), BF)],
        compiler_params=_cparams(("parallel", "parallel")),
        name="kv_layout",
    )(rows)


def _compress_kernel(pt_ref, *refs, n_slots, tail_group):
    page_refs = refs[:n_slots]
    rest = refs[n_slots:]
    if tail_group is not None:
        tail_ref, rest = rest[0], rest[1:]
    w1_ref, pe_ref, w2_ref, o_ref, xs_ref = rest
    k = pl.program_id(1)
    m = n_slots * (PAGE // CMP_STRIDE)

    @pl.when(k == 0)
    def _():
        xs_ref[:, :CMP_STRIDE, :] = jnp.zeros((4, CMP_STRIDE, LANES), F32)

    for i, pr in enumerate(page_refs):
        x = pr[0]
        if tail_group is not None and i == n_slots - 1:
            x = jnp.where(k == tail_group, tail_ref[0], x)
        rows = slice(CMP_STRIDE + i * PAGE, CMP_STRIDE + (i + 1) * PAGE)
        if x.ndim == 2:
            for vcol in range(4):
                xs_ref[vcol, rows, :] = x[:, vcol * LANES:(vcol + 1) * LANES]
        else:
            for c in range(2):
                xc = x[c].reshape(KV_WIDTH, PAGE).T
                for hp in range(2):
                    xs_ref[2 * c + hp, rows, :] = xc[:, hp * LANES:(hp + 1) * LANES]

    def strided(vcol, first):
        return jnp.concatenate(
            [xs_ref[vcol, pl.ds(first + s, m, stride=CMP_STRIDE), :] for s in range(CMP_STRIDE)], axis=1)

    out_cols = []
    for c in range(2):
        hpe = (_dot(pe_ref[c, 0], w1_ref[c, 0])[0:1] + _dot(pe_ref[c, 1], w1_ref[c, 1])[0:1])
        prev = jnp.concatenate([strided(2 * c + hp, 0) for hp in range(2)], axis=0).astype(BF)
        cur = jnp.concatenate([strided(2 * c + hp, CMP_STRIDE) for hp in range(2)], axis=0).astype(BF)
        h = _dot(prev, w1_ref[c, 0]) + _dot(cur, w1_ref[c, 1]) + hpe
        gl = jax.nn.gelu(h).astype(BF)
        out_cols.append(_dot(gl[:m], w2_ref[c, 0]) + _dot(gl[m:], w2_ref[c, 1]))
    o_ref[0] = jnp.concatenate(out_cols, axis=1)

    for vcol in range(4):
        xs_ref[vcol, :CMP_STRIDE, :] = xs_ref[vcol, n_slots * PAGE:n_slots * PAGE + CMP_STRIDE, :]


def _compress(pool, table, tail, w1, pe, w2):
    b, n_real = table.shape
    n_pages = n_real + (0 if tail is None else 1)
    n_slots = max(p for p in range(1, 17) if n_pages % p == 0)
    n_groups = n_pages // n_slots
    m = n_slots * (PAGE // CMP_STRIDE)
    page_shape = pool.shape[1:]
    zeros = (0,) * len(page_shape)

    def page_spec(slot):
        return pl.BlockSpec(
            (1,) + page_shape,
            lambda i, k, pt: (pt[i * n_real + jnp.minimum(k * n_slots + slot, n_real - 1)],) + zeros)

    in_specs = [page_spec(slot) for slot in range(n_slots)]
    args = [pool] * n_slots
    if tail is not None:
        in_specs.append(pl.BlockSpec((1,) + page_shape, lambda i, k, pt: (i,) + zeros))
        args.append(tail)
    in_specs += [pl.BlockSpec(w1.shape, lambda i, k, pt: (0,) * 4),
                 pl.BlockSpec(pe.shape, lambda i, k, pt: (0,) * 4),
                 pl.BlockSpec(w2.shape, lambda i, k, pt: (0,) * 4)]
    return pl.pallas_call(
        functools.partial(_compress_kernel, n_slots=n_slots,
                          tail_group=None if tail is None else n_groups - 1),
        grid_spec=pltpu.PrefetchScalarGridSpec(
            num_scalar_prefetch=1, grid=(b, n_groups), in_specs=in_specs,
            out_specs=pl.BlockSpec((1, m, ROW_WIDTH), lambda i, k, pt: (i, k, 0)),
            scratch_shapes=[pltpu.VMEM((4, CMP_STRIDE + n_slots * PAGE, LANES), F32)]),
        out_shape=jax.ShapeDtypeStruct((b, n_groups * m, ROW_WIDTH), F32),
        compiler_params=_cparams(("parallel", "arbitrary"), VMEM_LIMIT),
        name="compress",
    )(table.reshape(-1), *args, w1, pe, w2)


def _nsa_kernel(q_ref, gt_ref, kc_ref, vct_ref, ks_ref, vst_ref, kw_ref, vwt_ref, o_ref, psum_ref, bias_ref,
                sbuf0_ref, sbuf1_ref, *, t_base, wpos0, n_slc):
    g = pl.program_id(1)
    i = pl.program_id(2)
    nq = q_ref.shape[2]
    nl = GQA * nq
    ncp = kc_ref.shape[2]
    nsp = bias_ref.shape[0]
    nkt_s = vst_ref.shape[2]
    q = q_ref[0].reshape(nl, HEAD_DIM)
    lane = lax.broadcasted_iota(jnp.int32, (1, nl), 1)
    q0 = t_base + i * nq
    t = q0 + (lane & (nq - 1))

    s = _dot_nt(kc_ref[0, 0], q)
    n_idx = lax.broadcasted_iota(jnp.int32, (ncp, 1), 0)
    cmask = n_idx * CMP_STRIDE + (CMP_BLOCK - 1) <= t
    s = jnp.where(cmask, s, NEG)
    p = jnp.where(cmask, jnp.exp(s - jnp.max(s, axis=0, keepdims=True)), 0.0)
    l = jnp.sum(p, axis=0, keepdims=True)
    p = p * jnp.where(l > 0.0, 1.0 / l, 0.0)
    o_cmp = jnp.zeros((HEAD_DIM, nl), F32)
    for kt in range(ncp // PAGE):
        o_cmp = o_cmp + _dot(vct_ref[0, 0, kt], p[kt * PAGE:(kt + 1) * PAGE].astype(BF))

    psum = p[:, :nq]
    for r in range(1, GQA):
        psum = psum + p[:, r * nq:(r + 1) * nq]
    front = 8
    psum_ref[:front, :] = jnp.zeros((front, nq), F32)
    psum_ref[front:front + ncp, :] = psum
    psum_ref[front + ncp:, :] = jnp.zeros((psum_ref.shape[0] - front - ncp, nq), F32)
    ratio = SEL_BLOCK // CMP_STRIDE
    imp = psum_ref[pl.ds(front - 1, nsp, stride=ratio), :]
    for d in range(1, ratio + 1):
        imp = imp + psum_ref[pl.ds(front - 1 + d, nsp, stride=ratio), :]
    tq = t[:, :nq]
    j = lax.broadcasted_iota(jnp.int32, (nsp, 1), 0)
    cur = tq >> int(math.log2(SEL_BLOCK))
    forced = (j == 0) | (j == cur) | (j == cur - 1)
    score = jnp.where(j * SEL_BLOCK <= tq, imp + FORCE_BONUS * forced.astype(F32), NEG)
    sub = 8
    chunks = [score[c * sub:(c + 1) * sub] for c in range(nsp // sub)]
    jc = j[:sub]
    cnts = [jnp.zeros((sub, nq), F32) for _ in chunks]
    for ii in range(n_slc):
        ri = score[ii:ii + 1, :]
        for c, sc_c in enumerate(chunks):
            if c * sub > ii:
                one = jnp.where(ri >= sc_c, 1.0, 0.0)
            elif c * sub + sub - 1 <= ii:
                one = jnp.where(ri > sc_c, 1.0, 0.0)
            else:
                one = jnp.where(jc + c * sub > ii, jnp.where(ri >= sc_c, 1.0, 0.0), jnp.where(ri > sc_c, 1.0, 0.0))
            cnts[c] = cnts[c] + one
    cnt = jnp.concatenate(cnts, axis=0)
    bias_ref[...] = jnp.where(cnt < float(SEL_TOPK), jnp.where(score > NEG / 2, 0.0, NEG), NEG)

    gk = SEL_GROUP * PAGE
    blocks_per_group = gk // SEL_BLOCK
    key_row = lax.broadcasted_iota(jnp.int32, (gk, 1), 0)

    def fill(buf, gi):
        kb = ks_ref[0, 0, pl.ds(pl.multiple_of(gi * gk, gk), gk), :]
        rows = bias_ref[pl.ds(pl.multiple_of(gi * blocks_per_group, blocks_per_group), blocks_per_group), :]
        rows = jnp.concatenate([rows] * GQA, axis=1)
        bias = jnp.concatenate([jnp.broadcast_to(rows[b:b + 1], (SEL_BLOCK, nl)) for b in range(blocks_per_group)],
                               axis=0)
        buf[...] = _dot_nt(kb, q) + bias

    def consume(buf, gi, carry, causal):
        m_old, l_old, acc = carry
        sc = buf[...]
        if causal:
            sc = jnp.where(gi * gk + key_row <= t, sc, NEG)
        m_new = jnp.maximum(m_old, jnp.max(sc, axis=0, keepdims=True))
        alpha = jnp.exp(m_old - m_new)
        pe = jnp.exp(sc - m_new)
        acc = acc * alpha
        for u in range(SEL_GROUP):
            acc = acc + _dot(vst_ref[0, 0, gi * SEL_GROUP + u], pe[u * PAGE:(u + 1) * PAGE].astype(BF))
        return m_new, l_old * alpha + jnp.sum(pe, axis=0, keepdims=True), acc

    init = (jnp.full((1, nl), NEG, F32), jnp.zeros((1, nl), F32), jnp.zeros((HEAD_DIM, nl), F32))

    n_groups = jnp.minimum(nkt_s - 1, (q0 + nq - 1) // PAGE) // SEL_GROUP + 1
    fill(sbuf0_ref, 0)

    def pair_body(kk, carry):
        gi = 2 * kk
        fill(sbuf1_ref, gi + 1)
        carry = consume(sbuf0_ref, gi, carry, False)
        fill(sbuf0_ref, gi + 2)
        return consume(sbuf1_ref, gi + 1, carry, False)

    n_pairs = (n_groups - 1) // 2
    carry = lax.fori_loop(0, n_pairs, pair_body, init)
    gi = 2 * n_pairs

    def odd_tail(carry):
        fill(sbuf1_ref, gi + 1)
        carry = consume(sbuf0_ref, gi, carry, False)
        return consume(sbuf1_ref, gi + 1, carry, True)

    def even_tail(carry):
        return consume(sbuf0_ref, gi, carry, True)

    _, l_s, acc_s = lax.cond(n_groups - 1 - gi == 1, odd_tail, even_tail, carry)

    rel = (q0 - wpos0) // PAGE
    n_wt = WINDOW // PAGE + 1
    tile_row = lax.broadcasted_iota(jnp.int32, (PAGE, 1), 0)
    parts = []
    for u in range(n_wt):
        kt = rel - (n_wt - 1) + u
        ktc = jnp.maximum(kt, 0)
        kb = kw_ref[0, 0, pl.ds(pl.multiple_of(ktc * PAGE, PAGE), PAGE), :]
        krel = kt * PAGE + tile_row
        kpos = jnp.where(krel >= 0, wpos0 + krel, -(1 << 30))
        d = lax.bitcast_convert_type(t - kpos, jnp.uint32)
        parts.append(jnp.where(d < WINDOW, _dot_nt(kb, q), NEG))
    sw = jnp.concatenate(parts, axis=0)
    pw = jnp.exp(sw - jnp.max(sw, axis=0, keepdims=True))
    l_w = jnp.sum(pw, axis=0, keepdims=True)
    acc_w = jnp.zeros((HEAD_DIM, nl), F32)
    for u in range(n_wt):
        ktc = jnp.maximum(rel - (n_wt - 1) + u, 0)
        acc_w = acc_w + _dot(vwt_ref[0, 0, ktc], pw[u * PAGE:(u + 1) * PAGE].astype(BF))

    o_sel = acc_s * jnp.where(l_s > 0.0, 1.0 / l_s, 0.0)
    o_win = acc_w * jnp.where(l_w > 0.0, 1.0 / l_w, 0.0)
    for r in range(GQA):
        sl = slice(r * nq, (r + 1) * nq)
        grow = (GQA * g + r) * 3
        o = (gt_ref[0, pl.ds(grow, 1), :] * o_cmp[:, sl]
             + gt_ref[0, pl.ds(grow + 1, 1), :] * o_sel[:, sl]
             + gt_ref[0, pl.ds(grow + 2, 1), :] * o_win[:, sl])
        o_ref[0, r * HEAD_DIM:(r + 1) * HEAD_DIM, :] = o


def _nsa(q, gt, kc, vct, ks, vst, kw, vwt, *, t_base, wpos0, n_slc):
    b, _, sq, _ = q.shape
    nq = LANES
    assert t_base % PAGE == 0 and (t_base - wpos0) % PAGE == 0 and t_base >= wpos0
    assert vst.shape[2] % SEL_GROUP == 0
    nsp = max(-(-n_slc // 8) * 8, vst.shape[2] * (PAGE // SEL_BLOCK))
    ncp = kc.shape[2]
    psum_rows = 8 + max(ncp + 8, (SEL_BLOCK // CMP_STRIDE) * nsp + 8)
    psum_rows = -(-psum_rows // 8) * 8

    def kv_specs(k, vt):
        return [pl.BlockSpec((1, 1) + k.shape[2:], lambda bi, gi, ii: (bi, gi, 0, 0)),
                pl.BlockSpec((1, 1) + vt.shape[2:], lambda bi, gi, ii: (bi, gi, 0, 0, 0))]

    return pl.pallas_call(
        functools.partial(_nsa_kernel, t_base=t_base, wpos0=wpos0, n_slc=n_slc),
        grid=(b, KV_HEADS, sq // nq),
        in_specs=[pl.BlockSpec((1, GQA, nq, HEAD_DIM), lambda bi, gi, ii: (bi, gi, ii, 0)),
                  pl.BlockSpec((1, GATE_PAD, nq), lambda bi, gi, ii: (bi, 0, ii))]
        + kv_specs(kc, vct) + kv_specs(ks, vst) + kv_specs(kw, vwt),
        out_specs=pl.BlockSpec((1, GQA * HEAD_DIM, nq), lambda bi, gi, ii: (bi, gi, ii)),
        out_shape=jax.ShapeDtypeStruct((b, NSA_WIDTH, sq), F32),
        scratch_shapes=[pltpu.VMEM((psum_rows, nq), F32), pltpu.VMEM((nsp, nq), F32),
                        pltpu.VMEM((SEL_GROUP * PAGE, GQA * nq), F32), pltpu.VMEM((SEL_GROUP * PAGE, GQA * nq), F32)],
        compiler_params=_cparams(("parallel", "parallel", "parallel"), VMEM_LIMIT),
        name="nsa_attend",
    )(q, gt, kc, vct, ks, vst, kw, vwt)


SAMPLE_ROWS = 64
ROWS_PER_GROUP = SAMPLE_ROWS // KV_HEADS


def _nsa_sample_kernel(pt_ref, *refs, n_slots, t_base, wpos0, n_slc, t_q):
    page_refs = refs[:n_slots]
    (tail_ref, q_ref, gt_ref, kct_ref, vct_ref, kwt_ref, vwt_ref, o_ref,
     ind_ref, m_ref, l_ref, acc_ref, oc_ref, ow_ref) = refs[n_slots:]
    k = pl.program_id(1)
    n_steps = pl.num_programs(1)
    q = q_ref[0]
    row = lax.broadcasted_iota(jnp.int32, (SAMPLE_ROWS, 1), 0)
    t = t_base + (row & (t_q - 1))

    @pl.when(k == 0)
    def _():
        ncp = kct_ref.shape[2]
        s = _dot(q, kct_ref[0])
        n_idx = lax.broadcasted_iota(jnp.int32, (1, ncp), 1)
        cmask = n_idx * CMP_STRIDE + (CMP_BLOCK - 1) <= t
        s = jnp.where(cmask, s, NEG)
        p = jnp.where(cmask, jnp.exp(s - jnp.max(s, axis=1, keepdims=True)), 0.0)
        l = jnp.sum(p, axis=1, keepdims=True)
        p = p * jnp.where(l > 0.0, 1.0 / l, 0.0)
        oc_ref[...] = _dot_nt(p.astype(BF), vct_ref[0])

        parts = []
        for g in range(KV_HEADS):
            base = g * ROWS_PER_GROUP
            acc = p[base:base + t_q]
            for r in range(1, GQA):
                acc = acc + p[base + r * t_q:base + (r + 1) * t_q]
            parts.append(acc)
        psum = jnp.concatenate(parts, axis=0)
        nsl = ind_ref.shape[1]
        n_col = lax.broadcasted_iota(jnp.int32, (ncp, 1), 0)
        j_row = lax.broadcasted_iota(jnp.int32, (1, nsl), 1)
        ratio = SEL_BLOCK // CMP_STRIDE
        overlap = jnp.where((n_col >= ratio * j_row - 1) & (n_col <= ratio * j_row + ratio - 1), 1.0, 0.0).astype(BF)
        hi = psum.astype(BF)
        mid = (psum - hi.astype(F32)).astype(BF)
        lo = (psum - hi.astype(F32) - mid.astype(F32)).astype(BF)
        imp = _dot(hi, overlap) + _dot(mid, overlap) + _dot(lo, overlap)
        nsel_rows = KV_HEADS * t_q
        tq = t_base + (lax.broadcasted_iota(jnp.int32, (nsel_rows, 1), 0) & (t_q - 1))
        cur = tq >> int(math.log2(SEL_BLOCK))
        forced = (j_row == 0) | (j_row == cur) | (j_row == cur - 1)
        score = jnp.where(j_row * SEL_BLOCK <= tq, imp + FORCE_BONUS * forced.astype(F32), NEG)
        score_t = jnp.concatenate([score, jnp.full((LANES - nsel_rows, nsl), NEG, F32)], axis=0).T
        sub = 8
        nsp = -(-n_slc // sub) * sub
        jc = lax.broadcasted_iota(jnp.int32, (sub, 1), 0)
        chunks = [score_t[c * sub:(c + 1) * sub] for c in range(nsp // sub)]
        cnts = [jnp.zeros((sub, LANES), F32) for _ in chunks]
        for ii in range(n_slc):
            ri = score_t[ii:ii + 1, :]
            for c, sc_c in enumerate(chunks):
                if c * sub > ii:
                    one = jnp.where(ri >= sc_c, 1.0, 0.0)
                elif c * sub + sub - 1 <= ii:
                    one = jnp.where(ri > sc_c, 1.0, 0.0)
                else:
                    one = jnp.where(jc + c * sub > ii, jnp.where(ri >= sc_c, 1.0, 0.0), jnp.where(ri > sc_c, 1.0, 0.0))
                cnts[c] = cnts[c] + one
        chosen = [jnp.where(cn < float(SEL_TOPK), jnp.where(sc_c > NEG / 2, 1.0, 0.0), 0.0)
                  for cn, sc_c in zip(cnts, chunks)]
        chosen_t = jnp.concatenate(chosen + [jnp.zeros((nsl - nsp, LANES), F32)], axis=0)
        chosen_rows = chosen_t.T.astype(BF)
        col = lax.broadcasted_iota(jnp.int32, (1, LANES), 1)
        src = (row >> int(math.log2(ROWS_PER_GROUP))) * t_q + (row & (t_q - 1))
        expand = jnp.where(col == src, 1.0, 0.0).astype(BF)
        ind_ref[...] = _dot(expand, chosen_rows).astype(BF)

        nw = kwt_ref.shape[2]
        sw = _dot(q, kwt_ref[0])
        kpos = wpos0 + lax.broadcasted_iota(jnp.int32, (1, nw), 1)
        d = lax.bitcast_convert_type(t - kpos, jnp.uint32)
        sw = jnp.where(d < WINDOW, sw, NEG)
        pw = jnp.exp(sw - jnp.max(sw, axis=1, keepdims=True))
        lw = jnp.sum(pw, axis=1, keepdims=True)
        ow_ref[...] = _dot_nt(pw.astype(BF), vwt_ref[0]) * (1.0 / lw)

        m_ref[...] = jnp.full(m_ref.shape, NEG, F32)
        l_ref[...] = jnp.zeros(l_ref.shape, F32)
        acc_ref[...] = jnp.zeros(acc_ref.shape, F32)

    vts = []
    scs = []
    for i, pr in enumerate(page_refs):
        x = pr[0]
        if i == n_slots - 1:
            x = jnp.where(k == n_steps - 1, tail_ref[0], x)
        scs.append(_dot(q, x[0].reshape(KV_WIDTH, PAGE).astype(BF)))
        vts.append(x[1].reshape(KV_WIDTH, PAGE).astype(BF))
    sc = jnp.concatenate(scs, axis=1)
    nk = n_slots * PAGE
    kidx = k * nk + lax.broadcasted_iota(jnp.int32, (1, nk), 1)
    nsl = ind_ref.shape[1]
    j_col = lax.broadcasted_iota(jnp.int32, (nsl, 1), 0)
    block_of_key = jnp.where(j_col == (kidx >> int(math.log2(SEL_BLOCK))), 1.0, 0.0).astype(BF)
    picked = _dot(ind_ref[...], block_of_key)
    sc = jnp.where(picked > 0.5, jnp.where(kidx <= t, sc, NEG), NEG)
    m_old = m_ref[:, :1]
    m_new = jnp.maximum(m_old, jnp.max(sc, axis=1, keepdims=True))
    alpha = jnp.exp(m_old - m_new)
    pe = jnp.exp(sc - m_new)
    acc = acc_ref[...] * alpha
    for i in range(n_slots):
        acc = acc + _dot_nt(pe[:, i * PAGE:(i + 1) * PAGE].astype(BF), vts[i])
    l_new = l_ref[:, :1] * alpha + jnp.sum(pe, axis=1, keepdims=True)
    m_ref[...] = jnp.broadcast_to(m_new, m_ref.shape)
    l_ref[...] = jnp.broadcast_to(l_new, l_ref.shape)
    acc_ref[...] = acc

    @pl.when(k == n_steps - 1)
    def _():
        o_sel = acc * jnp.where(l_new > 0.0, 1.0 / l_new, 0.0)
        gts = gt_ref[0]
        o_ref[0] = gts[:, 0:1] * oc_ref[...] + gts[:, 1:2] * o_sel + gts[:, 2:3] * ow_ref[...]


def _nsa_sample(q_bd, gates, pool_t, table, tail_t, kct, vct, kwt, vwt, *, t_base, wpos0, n_slc, t_q):
    b, n_real = table.shape
    n_pages = n_real + 1
    n_slots = max(p for p in range(1, 17) if n_pages % p == 0)
    n_steps = n_pages // n_slots
    nsl = -(-n_slc // LANES) * LANES
    page_shape = pool_t.shape[1:]
    zeros = (0,) * len(page_shape)

    def page_spec(slot):
        return pl.BlockSpec(
            (1,) + page_shape,
            lambda i, k, pt: (pt[i * n_real + jnp.minimum(k * n_slots + slot, n_real - 1)],) + zeros)

    def per_batch(x):
        return pl.BlockSpec((1,) + x.shape[1:], lambda i, k, pt: (i,) + (0,) * (x.ndim - 1))

    acc_shape = pltpu.VMEM((SAMPLE_ROWS, KV_WIDTH), F32)
    stat_shape = pltpu.VMEM((SAMPLE_ROWS, LANES), F32)
    return pl.pallas_call(
        functools.partial(_nsa_sample_kernel, n_slots=n_slots, t_base=t_base, wpos0=wpos0, n_slc=n_slc, t_q=t_q),
        grid_spec=pltpu.PrefetchScalarGridSpec(
            num_scalar_prefetch=1, grid=(b, n_steps),
            in_specs=[page_spec(slot) for slot in range(n_slots)]
            + [per_batch(x) for x in (tail_t, q_bd, gates, kct, vct, kwt, vwt)],
            out_specs=pl.BlockSpec((1, SAMPLE_ROWS, KV_WIDTH), lambda i, k, pt: (i, 0, 0)),
            scratch_shapes=[pltpu.VMEM((SAMPLE_ROWS, nsl), BF), stat_shape, stat_shape, acc_shape, acc_shape,
                            acc_shape]),
        out_shape=jax.ShapeDtypeStruct((b, SAMPLE_ROWS, KV_WIDTH), F32),
        compiler_params=_cparams(("parallel", "arbitrary"), VMEM_LIMIT),
        name="nsa_sample",
    )(table.reshape(-1), *([pool_t] * n_slots), tail_t, q_bd, gates, kct, vct, kwt, vwt)


def _rope_tables(pos):
    half = ROPE_DIM // 2
    inv_freq = ROPE_THETA ** (-jnp.arange(half, dtype=F32) / half)
    ang = pos.astype(F32)[:, None] * inv_freq[None, :]
    cos, sin = jnp.cos(ang), jnp.sin(ang)
    n = pos.shape[0]
    rest = HEAD_DIM - ROPE_DIM
    c = jnp.concatenate([cos, cos, jnp.ones((n, rest), F32)], axis=1)
    a = jnp.concatenate([-sin, jnp.zeros((n, half + rest), F32)], axis=1)
    b = jnp.concatenate([jnp.zeros((n, half), F32), sin, jnp.zeros((n, rest), F32)], axis=1)
    reps = LANES // HEAD_DIM
    return tuple(jnp.tile(x, (1, reps)) for x in (c, a, b))


def _compress_weights(cmp_pe, w_phi1, w_phi2):
    ratio = CMP_BLOCK // CMP_STRIDE
    w1 = w_phi1.reshape(ratio, CMP_STRIDE, 2, HEAD_DIM, CMP_HIDDEN)
    eye2 = jnp.eye(2, dtype=w_phi1.dtype)
    big = jnp.einsum('jscde,hg->cjshdge', w1, eye2)
    big = big.reshape(2, ratio, CMP_STRIDE * 2 * HEAD_DIM, 2 * CMP_HIDDEN).astype(BF)
    pe = cmp_pe.reshape(ratio, CMP_STRIDE, 2, HEAD_DIM)
    pe = jnp.broadcast_to(pe.transpose(2, 0, 1, 3)[:, :, :, None, :], (2, ratio, CMP_STRIDE, 2, HEAD_DIM))
    pe = pe.reshape(2, ratio, 1, CMP_STRIDE * 2 * HEAD_DIM)
    pe = jnp.broadcast_to(pe, (2, ratio, 8, pe.shape[-1])).astype(BF)
    eye4 = jnp.eye(KV_HEADS, dtype=w_phi2.dtype).reshape(2, 2, KV_HEADS)
    w2 = jnp.einsum('ced,pgh->cpgehd', w_phi2, eye4)
    w2 = w2.reshape(2, 2, 2 * CMP_HIDDEN, KV_WIDTH).astype(BF)
    return big, pe, w2


def _b_in_weight(w):
    n_gate = w.shape[1] - NSA_WIDTH - MEM_WIDTH
    gates = jnp.pad(w[:, NSA_WIDTH:NSA_WIDTH + n_gate], ((0, 0), (0, GATE_PAD - n_gate)))
    return jnp.concatenate([w[:, :NSA_WIDTH], w[:, NSA_WIDTH + n_gate:], gates], axis=1).astype(BF)


def _pad_rows(x, n):
    return jnp.pad(x, ((0, 0), (0, n - x.shape[1]), (0, 0)))


def _tail_page_t(new_rows):
    b = new_rows.shape[0]
    return _pad_rows(new_rows, PAGE).reshape(b, PAGE, 2, KV_HEADS, HEAD_DIM).transpose(0, 2, 3, 4, 1)


def _rows_t(rows):
    b, n, _ = rows.shape
    kv = rows.reshape(b, n, 2, KV_WIDTH).transpose(0, 2, 3, 1).astype(BF)
    return kv[:, 0], kv[:, 1]


def _trunk_a(x, mem_kv, p, l, *, chunk, tm, tq, emit_v, mem_batch):
    b, s, _ = x.shape
    if chunk == PAGE:
        ws, bs = p["w_spatial"][l], jnp.repeat(p["b_spatial"][l].T, A_GROUP_DIM, axis=1)
    else:
        reps = PAGE // chunk
        ws = jnp.tile(p["w_spatial"][l][:, :chunk, :chunk], (1, reps, reps))
        bs = jnp.tile(jnp.repeat(p["b_spatial"][l].T[:chunk], A_GROUP_DIM, axis=1), (reps, 1))
    res = _a_in(x, p["w_in_a"][l], p["ln_v_g"][l][None], p["ln_v_b"][l][None], ws, bs,
                chunk=chunk, tm=tm, emit_v=emit_v)
    mixed, qm = res[0], res[1]
    om = _mem_attn(qm.reshape(mem_batch, -1, MEM_WIDTH), mem_kv, tq=tq).reshape(b, s, MEM_WIDTH)
    x = _mlp_block(x, mixed, om, p["w_out_a"][l], p, l, tm=tm, a_transposed=False)
    return x, (res[2] if emit_v else None)


def _mlp_block(x, a, om, wo, p, l, *, tm, a_transposed):
    return _out_mlp(x, a, om, wo, p["ln1_g"][l][None], p["ln1_b"][l][None], p["w_up"][l], p["w_down"][l],
                    p["ln2_g"][l][None], p["ln2_b"][l][None], tm=tm, a_transposed=a_transposed)


def kernel(x_prompt, x_sample, cache_cmp_kv, cache_slc_kv, cache_win_kv, cache_mem_kv, page_table, mem_prompt,
           w_in_a, ln_v_g, ln_v_b, w_spatial, b_spatial, w_out_a, w_in_b, w_out_b, w_kv_shared, cmp_pe,
           w_phi1, w_phi2, w_mem_kv, ln1_g, ln1_b, ln2_g, ln2_b, w_up, w_down):
    bp, s, d = x_prompt.shape
    bs, t, _ = x_sample.shape
    n_pages = page_table.shape[1]
    past = n_pages * PAGE
    n_buf = cache_win_kv.shape[1]
    assert bs * t == PAGE and PAGE % t == 0 and s % PAGE == 0 and n_buf % PAGE == 0

    p = {"w_in_a": w_in_a.astype(BF), "ln_v_g": ln_v_g, "ln_v_b": ln_v_b, "w_spatial": w_spatial,
         "b_spatial": b_spatial, "w_out_a": w_out_a.astype(BF), "w_out_b": w_out_b.astype(BF),
         "ln1_g": ln1_g, "ln1_b": ln1_b, "ln2_g": ln2_g, "ln2_b": ln2_b,
         "w_up": w_up.astype(BF), "w_down": w_down.astype(BF)}
    w_in_b2 = [_b_in_weight(w_in_b[i]) for i in range(w_in_b.shape[0])]
    w_kv = w_kv_shared.astype(BF)
    cw1, cpe, cw2 = _compress_weights(cmp_pe, w_phi1, w_phi2)

    tm_p = 256
    mem_kv_prompt = _mem_kv_proj(mem_prompt, w_mem_kv)
    x = x_prompt
    for l in range(N_A):
        x, _ = _trunk_a(x, mem_kv_prompt[l], p, l, chunk=PAGE, tm=tm_p, tq=tm_p, emit_v=False, mem_batch=bp)
    tabs_p = _rope_tables(jnp.arange(s, dtype=jnp.int32))
    cmp_rows, slc_rows, win_rows, ks, vst, kw, vwt = _kv_shared(x, w_kv, tabs_p, tm=tm_p, attn_layouts=True)
    n_pg = s // PAGE
    ident = jnp.arange(bp * n_pg, dtype=jnp.int32).reshape(bp, n_pg)
    cmp_blk = _compress(cmp_rows.reshape(bp * n_pg, PAGE, ROW_WIDTH), ident, None, cw1, cpe, cw2)
    n_cmp = s // CMP_STRIDE - 1
    ncp = -(-n_cmp // PAGE) * PAGE
    kc, vct = _rows_layout(_pad_rows(cmp_blk[:, 1:1 + n_cmp], ncp))
    for i in range(DEPTH - N_A):
        l = N_A + i
        q, qm, gt = _b_in(x, w_in_b2[i], tabs_p, tm=tm_p)
        om = _mem_attn(qm, mem_kv_prompt[l], tq=tm_p)
        ot = _nsa(q, gt, kc, vct, ks, vst, kw, vwt, t_base=0, wpos0=0, n_slc=s // SEL_BLOCK)
        x = _mlp_block(x, ot, om, p["w_out_b"][i], p, l, tm=tm_p, a_transposed=True)
    y_prompt = x
    shape5 = (2, KV_HEADS, HEAD_DIM)
    cmp_kv_prompt = cmp_rows.reshape(bp, s, *shape5)
    slc_kv_prompt = slc_rows.reshape(bp, s, *shape5)
    win_kv_prompt = win_rows[:, s - min(WINDOW, s):].reshape(bp, min(WINDOW, s), *shape5)
    mem_kv_prompt_out = mem_kv_prompt.reshape(DEPTH, bp, MEM_TOKENS, 2, MEM_HEADS, HEAD_DIM)

    rows = bs * t
    x = x_sample.reshape(1, rows, d)
    mem_s = cache_mem_kv.reshape(DEPTH, bs, MEM_TOKENS, 2 * MEM_WIDTH)
    v_rows = []
    for l in range(N_A):
        x, v = _trunk_a(x, mem_s[l], p, l, chunk=t, tm=rows, tq=t, emit_v=True, mem_batch=bs)
        v_rows.append(v.reshape(bs, t, A_WIDTH))
    pos_s = past + (jnp.arange(rows, dtype=jnp.int32) % t)
    tabs_s = _rope_tables(pos_s)
    new_cmp, new_slc, new_win = (r.reshape(bs, t, ROW_WIDTH)
                                 for r in _kv_shared(x, w_kv, tabs_s, tm=rows, attn_layouts=False))
    cmp_pool_t = cache_cmp_kv.transpose(0, 2, 3, 4, 1)
    slc_pool_t = cache_slc_kv.transpose(0, 2, 3, 4, 1)
    cmp_blk = _compress(cmp_pool_t, page_table, _tail_page_t(new_cmp), cw1, cpe, cw2)
    n_cmp = -(-(past + t) // CMP_STRIDE) - 1
    ncp = -(-n_cmp // PAGE) * PAGE
    kct, vct = _rows_t(_pad_rows(cmp_blk[:, 1:1 + n_cmp], ncp))
    win_all = jnp.concatenate([cache_win_kv.reshape(bs, n_buf, ROW_WIDTH), new_win], axis=1)
    kwt, vwt = _rows_t(_pad_rows(win_all, n_buf + PAGE))
    slc_tail_t = _tail_page_t(new_slc)
    n_slc = -(-(past + t) // SEL_BLOCK)
    per_group = GQA * t
    eye = jnp.eye(KV_HEADS, dtype=BF)
    for i in range(DEPTH - N_A):
        l = N_A + i
        q, qm, gt = _b_in(x, w_in_b2[i], tabs_s, tm=rows)
        om = _mem_attn(qm.reshape(bs, t, MEM_WIDTH), mem_s[l], tq=t).reshape(1, rows, MEM_WIDTH)
        q = q.reshape(KV_HEADS, GQA, bs, t, HEAD_DIM).transpose(2, 0, 1, 3, 4).reshape(bs, KV_HEADS, per_group, HEAD_DIM)
        q = jnp.pad(q, ((0, 0), (0, 0), (0, ROWS_PER_GROUP - per_group), (0, 0)))
        q_bd = jnp.einsum('bgxd,gh->bgxhd', q, eye).reshape(bs, SAMPLE_ROWS, KV_WIDTH)
        gs = gt[0, :NSA_HEADS * 3].reshape(KV_HEADS, GQA, 3, bs, t).transpose(3, 0, 1, 4, 2)
        gs = gs.reshape(bs, KV_HEADS, per_group, 3)
        gs = jnp.pad(gs, ((0, 0), (0, 0), (0, ROWS_PER_GROUP - per_group), (0, LANES - 3)))
        o = _nsa_sample(q_bd, gs.reshape(bs, SAMPLE_ROWS, LANES), slc_pool_t, page_table, slc_tail_t,
                        kct, vct, kwt, vwt, t_base=past, wpos0=past - n_buf, n_slc=n_slc, t_q=t)
        o = o.reshape(bs, KV_HEADS, ROWS_PER_GROUP, KV_HEADS, HEAD_DIM)
        o = jnp.stack([o[:, g, :per_group, g] for g in range(KV_HEADS)], axis=1)
        o = o.reshape(bs, KV_HEADS, GQA, t, HEAD_DIM).transpose(0, 3, 1, 2, 4).reshape(1, rows, NSA_WIDTH)
        x = _mlp_block(x, o, om, p["w_out_b"][i], p, l, tm=rows, a_transposed=False)
    y_sample = x.reshape(bs, t, d)
    cmp_kv_sample = new_cmp.reshape(bs, t, *shape5)
    slc_kv_sample = new_slc.reshape(bs, t, *shape5)
    win_kv_sample = win_all[:, t:].reshape(bs, n_buf, *shape5)
    gmlp_v_sample = jnp.stack(v_rows)

    return (y_prompt, y_sample, cmp_kv_prompt, slc_kv_prompt, win_kv_prompt, mem_kv_prompt_out,
            cmp_kv_sample, slc_kv_sample, win_kv_sample, gmlp_v_sample)
```

```python
import functools
import math

import jax
import jax.numpy as jnp
from jax import lax
from jax.experimental import pallas as pl
from jax.experimental.pallas import tpu as pltpu

BF = jnp.bfloat16
F32 = jnp.float32

DEPTH = 4
N_A = 2
A_WIDTH = 768
A_GROUPS = 4
A_GROUP_DIM = 192
MEM_TOKENS = 256
MEM_HEADS = 4
MEM_WIDTH = 256
HEAD_DIM = 64
NSA_HEADS = 12
KV_HEADS = 4
GQA = 3
NSA_WIDTH = 768
KV_WIDTH = 256
ROW_WIDTH = 2 * KV_WIDTH
CMP_BLOCK = 32
CMP_STRIDE = 16
CMP_HIDDEN = 128
SEL_BLOCK = 64
SEL_TOPK = 16
WINDOW = 512
ROPE_THETA = 500000.0
ROPE_DIM = 16
D_FF_CHUNK = 1024
ALPHA = (2.0 * DEPTH) ** 0.25
LN_EPS = 1e-5
NEG = -1e30
FORCE_BONUS = 1e4
PAGE = 128
LANES = 128
GATE_PAD = 128
SEL_GROUP = 4
VMEM_LIMIT = 56 * 1024 * 1024


def _cparams(sem, vmem=None):
    return pltpu.CompilerParams(dimension_semantics=sem, vmem_limit_bytes=vmem)


def _const_spec(shape):
    return pl.BlockSpec(shape, lambda *_: (0,) * len(shape), pipeline_mode=pl.Buffered(1))


def _ln(x, g, b):
    mu = jnp.mean(x, axis=-1, keepdims=True)
    xc = x - mu
    var = jnp.mean(xc * xc, axis=-1, keepdims=True)
    return xc * lax.rsqrt(var + LN_EPS) * g + b


def _dot(a, b):
    return jnp.dot(a, b, preferred_element_type=F32)


def _dot_nt(a, b):
    return lax.dot_general(a, b, (((1,), (1,)), ((), ())), preferred_element_type=F32)


def _rope_cols(x, c, a, b):
    cols = []
    for j in range(x.shape[1] // LANES):
        xc = x[:, j * LANES:(j + 1) * LANES]
        up = pltpu.roll(xc, LANES - ROPE_DIM // 2, axis=1)
        dn = pltpu.roll(xc, ROPE_DIM // 2, axis=1)
        cols.append(xc * c + up * a + dn * b)
    return cols[0] if len(cols) == 1 else jnp.concatenate(cols, axis=1)


def _mem_kv_kernel(x_ref, w_ref, o_ref):
    o_ref[0] = _dot(x_ref[...].astype(BF), w_ref[0].astype(BF))


def _mem_kv_proj(mem_prompt, w_mem_kv):
    b, m, d = mem_prompt.shape
    x = mem_prompt.reshape(b * m, d)
    depth, _, n = w_mem_kv.shape
    tm = 512
    out = pl.pallas_call(
        _mem_kv_kernel,
        grid=(depth, (b * m) // tm),
        in_specs=[pl.BlockSpec((tm, d), lambda l, i: (i, 0)),
                  pl.BlockSpec((1, d, n), lambda l, i: (l, 0, 0))],
        out_specs=pl.BlockSpec((1, tm, n), lambda l, i: (l, i, 0)),
        out_shape=jax.ShapeDtypeStruct((depth, b * m, n), F32),
        compiler_params=_cparams(("parallel", "parallel")),
        name="mem_kv_proj",
    )(x, w_mem_kv)
    return out.reshape(depth, b, m, n)


def _a_in_kernel(x_ref, w_ref, g_ref, b_ref, ws_ref, bs_ref, mixed_ref, qm_ref, *v_refs, chunk_shift):
    x = x_ref[0].astype(BF)
    proj = _dot(x, w_ref[...])
    uv = jax.nn.gelu(proj[:, :2 * A_WIDTH])
    u = uv[:, :A_WIDTH]
    v = _ln(uv[:, A_WIDTH:], g_ref[...], b_ref[...])
    if v_refs:
        v_refs[0][0] = v
    qm_ref[0] = proj[:, 2 * A_WIDTH:]
    tm = x.shape[0]
    r = ws_ref.shape[1]
    row = lax.broadcasted_iota(jnp.int32, (r, r), 0)
    col = lax.broadcasted_iota(jnp.int32, (r, r), 1)
    allowed = (col <= row) & ((row >> chunk_shift) == (col >> chunk_shift))
    lane = lax.broadcasted_iota(jnp.int32, (1, A_WIDTH), 1)
    wgs = [jnp.where(allowed, ws_ref[g], 0.0).astype(BF) for g in range(A_GROUPS)]
    gmask = [(lane >= g * A_GROUP_DIM) & (lane < (g + 1) * A_GROUP_DIM) for g in range(A_GROUPS)]
    vb = v.astype(BF)
    for ci in range(tm // r):
        vc = vb[ci * r:(ci + 1) * r]
        acc = bs_ref[...]
        for g in range(A_GROUPS):
            acc = acc + jnp.where(gmask[g], _dot(wgs[g], vc), 0.0)
        mixed_ref[0, ci * r:(ci + 1) * r, :] = (u[ci * r:(ci + 1) * r] * acc).astype(BF)


def _a_in(x, w_in, ln_g, ln_b, ws, bs, *, chunk, tm, emit_v):
    b, s, d = x.shape
    n = w_in.shape[1]
    r = ws.shape[1]
    outs = [jax.ShapeDtypeStruct((b, s, A_WIDTH), BF), jax.ShapeDtypeStruct((b, s, MEM_WIDTH), F32)]
    ospecs = [pl.BlockSpec((1, tm, A_WIDTH), lambda i, j: (i, j, 0)),
              pl.BlockSpec((1, tm, MEM_WIDTH), lambda i, j: (i, j, 0))]
    if emit_v:
        outs.append(jax.ShapeDtypeStruct((b, s, A_WIDTH), F32))
        ospecs.append(pl.BlockSpec((1, tm, A_WIDTH), lambda i, j: (i, j, 0)))
    return pl.pallas_call(
        functools.partial(_a_in_kernel, chunk_shift=int(math.log2(chunk))),
        grid=(b, s // tm),
        in_specs=[pl.BlockSpec((1, tm, d), lambda i, j: (i, j, 0)),
                  _const_spec((d, n)), _const_spec((1, A_WIDTH)), _const_spec((1, A_WIDTH)),
                  _const_spec((A_GROUPS, r, r)), _const_spec((r, A_WIDTH))],
        out_specs=ospecs,
        out_shape=outs,
        compiler_params=_cparams(("parallel", "parallel"), VMEM_LIMIT),
        name="a_in_gate",
    )(x, w_in, ln_g, ln_b, ws, bs)


def _mem_attn_kernel(q_ref, kv_ref, o_ref):
    q = q_ref[0]
    kv = kv_ref[0]
    k = kv[:, :MEM_WIDTH].astype(BF)
    v = kv[:, MEM_WIDTH:]
    lane = lax.broadcasted_iota(jnp.int32, (1, MEM_WIDTH), 1)
    out = jnp.zeros(q.shape, F32)
    for h in range(MEM_HEADS):
        hm = (lane >= h * HEAD_DIM) & (lane < (h + 1) * HEAD_DIM)
        s = _dot_nt(jnp.where(hm, q, 0.0).astype(BF), k) * (HEAD_DIM ** -0.5)
        m = jnp.max(s, axis=-1, keepdims=True)
        p = jnp.exp(s - m)
        p = p / jnp.sum(p, axis=-1, keepdims=True)
        out = out + _dot(p.astype(BF), jnp.where(hm, v, 0.0).astype(BF))
    o_ref[0] = out


def _mem_attn(q, mem_kv, *, tq):
    b, s, w = q.shape
    return pl.pallas_call(
        _mem_attn_kernel,
        grid=(b, s // tq),
        in_specs=[pl.BlockSpec((1, tq, w), lambda i, j: (i, j, 0)),
                  pl.BlockSpec((1, MEM_TOKENS, 2 * w), lambda i, j: (i, 0, 0))],
        out_specs=pl.BlockSpec((1, tq, w), lambda i, j: (i, j, 0)),
        out_shape=jax.ShapeDtypeStruct((b, s, w), F32),
        compiler_params=_cparams(("parallel", "parallel")),
        name="mem_attn",
    )(q, mem_kv)


def _out_mlp_kernel(x_ref, a_ref, om_ref, wo_ref, g1_ref, b1_ref, wu_ref, wd_ref, g2_ref, b2_ref, o_ref,
                    *, a_transposed):
    x = x_ref[0]
    a = a_ref[0]
    a = a.T.astype(BF) if a_transposed else a.astype(BF)
    na = a.shape[1]
    mix = _dot(a, wo_ref[:na, :]) + _dot(om_ref[0].astype(BF), wo_ref[na:, :])
    y = _ln(ALPHA * x + mix, g1_ref[...], b1_ref[...])
    yb = y.astype(BF)
    acc = jnp.zeros(x.shape, F32)
    for c in range(wu_ref.shape[1] // D_FF_CHUNK):
        h = _dot(yb, wu_ref[:, c * D_FF_CHUNK:(c + 1) * D_FF_CHUNK])
        h = jnp.square(jnp.maximum(h, 0.0)).astype(BF)
        acc = acc + _dot(h, wd_ref[c * D_FF_CHUNK:(c + 1) * D_FF_CHUNK, :])
    o_ref[0] = _ln(ALPHA * y + acc, g2_ref[...], b2_ref[...])


def _out_mlp(x, a, om, wo, g1, b1, wu, wd, g2, b2, *, tm, a_transposed):
    b, s, d = x.shape
    na = wo.shape[0] - om.shape[2]
    dff = wu.shape[1]
    if a_transposed:
        a_spec = pl.BlockSpec((1, na, tm), lambda i, j: (i, 0, j))
    else:
        a_spec = pl.BlockSpec((1, tm, na), lambda i, j: (i, j, 0))
    return pl.pallas_call(
        functools.partial(_out_mlp_kernel, a_transposed=a_transposed),
        grid=(b, s // tm),
        in_specs=[pl.BlockSpec((1, tm, d), lambda i, j: (i, j, 0)),
                  a_spec,
                  pl.BlockSpec((1, tm, om.shape[2]), lambda i, j: (i, j, 0)),
                  _const_spec(wo.shape), _const_spec((1, d)), _const_spec((1, d)),
                  _const_spec((d, dff)), _const_spec((dff, d)), _const_spec((1, d)), _const_spec((1, d))],
        out_specs=pl.BlockSpec((1, tm, d), lambda i, j: (i, j, 0)),
        out_shape=jax.ShapeDtypeStruct((b, s, d), F32),
        compiler_params=_cparams(("parallel", "parallel"), VMEM_LIMIT),
        name="out_mlp",
    )(x, a, om, wo, g1, b1, wu, wd, g2, b2)


def _kv_shared_kernel(h_ref, w_ref, c_ref, a_ref, b_ref, cmp_ref, slc_ref, win_ref, *attn_refs):
    kv = _dot(h_ref[0].astype(BF), w_ref[...])
    c, a, b = c_ref[...], a_ref[...], b_ref[...]
    tm = kv.shape[0]
    for br, ref in enumerate((cmp_ref, slc_ref, win_ref)):
        k = _rope_cols(kv[:, br * ROW_WIDTH:br * ROW_WIDTH + KV_WIDTH], c, a, b)
        v = kv[:, br * ROW_WIDTH + KV_WIDTH:(br + 1) * ROW_WIDTH]
        ref[0, :, :KV_WIDTH] = k
        ref[0, :, KV_WIDTH:] = v
        if attn_refs and br > 0:
            k_ref, vt_ref = attn_refs[2 * (br - 1):2 * br]
            vt = v.T
            for g in range(KV_HEADS):
                k_ref[0, g] = k[:, g * HEAD_DIM:(g + 1) * HEAD_DIM].astype(BF)
                for u in range(tm // PAGE):
                    vt_ref[0, g, u] = vt[g * HEAD_DIM:(g + 1) * HEAD_DIM, u * PAGE:(u + 1) * PAGE].astype(BF)


def _kv_shared(h, w_kv, tabs, *, tm, attn_layouts):
    b, s, d = h.shape
    row_spec = pl.BlockSpec((1, tm, ROW_WIDTH), lambda i, j: (i, j, 0))
    tab_spec = pl.BlockSpec((tm, LANES), lambda i, j: (j, 0))
    out_specs = [row_spec, row_spec, row_spec]
    out_shape = [jax.ShapeDtypeStruct((b, s, ROW_WIDTH), F32)] * 3
    if attn_layouts:
        out_specs += [pl.BlockSpec((1, KV_HEADS, tm, HEAD_DIM), lambda i, j: (i, 0, j, 0)),
                      pl.BlockSpec((1, KV_HEADS, tm // PAGE, HEAD_DIM, PAGE), lambda i, j: (i, 0, j, 0, 0))] * 2
        out_shape += [jax.ShapeDtypeStruct((b, KV_HEADS, s, HEAD_DIM), BF),
                      jax.ShapeDtypeStruct((b, KV_HEADS, s // PAGE, HEAD_DIM, PAGE), BF)] * 2
    return pl.pallas_call(
        _kv_shared_kernel,
        grid=(b, s // tm),
        in_specs=[pl.BlockSpec((1, tm, d), lambda i, j: (i, j, 0)), _const_spec(w_kv.shape),
                  tab_spec, tab_spec, tab_spec],
        out_specs=out_specs,
        out_shape=out_shape,
        compiler_params=_cparams(("parallel", "parallel")),
        name="kv_shared",
    )(h, w_kv, *tabs)


def _b_in_kernel(x_ref, w_ref, c_ref, a_ref, b_ref, q_ref, qm_ref, gt_ref):
    proj = _dot(x_ref[0].astype(BF), w_ref[...])
    q = _rope_cols(proj[:, :NSA_WIDTH], c_ref[...], a_ref[...], b_ref[...]) * (HEAD_DIM ** -0.5)
    for h in range(NSA_HEADS):
        q_ref[0, h] = q[:, h * HEAD_DIM:(h + 1) * HEAD_DIM].astype(BF)
    qm_ref[0] = proj[:, NSA_WIDTH:NSA_WIDTH + MEM_WIDTH]
    gt_ref[0] = jax.nn.sigmoid(proj[:, NSA_WIDTH + MEM_WIDTH:]).T


def _b_in(x, w_in, tabs, *, tm):
    b, s, d = x.shape
    tab_spec = pl.BlockSpec((tm, LANES), lambda i, j: (j, 0))
    return pl.pallas_call(
        _b_in_kernel,
        grid=(b, s // tm),
        in_specs=[pl.BlockSpec((1, tm, d), lambda i, j: (i, j, 0)), _const_spec(w_in.shape),
                  tab_spec, tab_spec, tab_spec],
        out_specs=[pl.BlockSpec((1, NSA_HEADS, tm, HEAD_DIM), lambda i, j: (i, 0, j, 0)),
                   pl.BlockSpec((1, tm, MEM_WIDTH), lambda i, j: (i, j, 0)),
                   pl.BlockSpec((1, GATE_PAD, tm), lambda i, j: (i, 0, j))],
        out_shape=[jax.ShapeDtypeStruct((b, NSA_HEADS, s, HEAD_DIM), BF),
                   jax.ShapeDtypeStruct((b, s, MEM_WIDTH), F32),
                   jax.ShapeDtypeStruct((b, GATE_PAD, s), F32)],
        compiler_params=_cparams(("parallel", "parallel")),
        name="b_in",
    )(x, w_in, *tabs)


def _kv_layout_kernel(x_ref, k_ref, vt_ref):
    x = x_ref[0]
    vt = x[:, KV_WIDTH:].T
    for g in range(KV_HEADS):
        k_ref[0, g] = x[:, g * HEAD_DIM:(g + 1) * HEAD_DIM].astype(BF)
        vt_ref[0, g, 0] = vt[g * HEAD_DIM:(g + 1) * HEAD_DIM].astype(BF)


def _rows_layout(rows):
    b, l, _ = rows.shape
    n = l // PAGE
    return pl.pallas_call(
        _kv_layout_kernel,
        grid=(b, n),
        in_specs=[pl.BlockSpec((1, PAGE, ROW_WIDTH), lambda i, p: (i, p, 0))],
        out_specs=[pl.BlockSpec((1, KV_HEADS, PAGE, HEAD_DIM), lambda i, p: (i, 0, p, 0)),
                   pl.BlockSpec((1, KV_HEADS, 1, HEAD_DIM, PAGE), lambda i, p: (i, 0, p, 0, 0))],
        out_shape=[jax.ShapeDtypeStruct((b, KV_HEADS, l, HEAD_DIM), BF),
                   jax.ShapeDtypeStruct((b, KV_HEADS, n, HEAD_DIM, PAGE), BF)],
        compiler_params=_cparams(("parallel", "parallel")),
        name="kv_layout",
    )(rows)


def _compress_kernel(pt_ref, *refs, n_slots, tail_group):
    page_refs = refs[:n_slots]
    rest = refs[n_slots:]
    if tail_group is not None:
        tail_ref, rest = rest[0], rest[1:]
    w1_ref, pe_ref, w2_ref, o_ref, xs_ref, carry_ref = rest
    k = pl.program_id(1)
    per_page = PAGE // CMP_STRIDE
    m = n_slots * per_page
    hid2 = 2 * CMP_HIDDEN

    @pl.when(k == 0)
    def _():
        carry_ref[...] = jnp.zeros(carry_ref.shape, F32)

    for i, pr in enumerate(page_refs):
        x = pr[0]
        if tail_group is not None and i == n_slots - 1:
            x = jnp.where(k == tail_group, tail_ref[0], x)
        rows = slice(i * PAGE, (i + 1) * PAGE)
        if x.ndim == 2:
            for vcol in range(4):
                xs_ref[vcol, rows, :] = x[:, vcol * LANES:(vcol + 1) * LANES]
        else:
            for c in range(2):
                xc = x[c].reshape(KV_WIDTH, PAGE).T
                for hp in range(2):
                    xs_ref[2 * c + hp, rows, :] = xc[:, hp * LANES:(hp + 1) * LANES]

    def strides(vcol):
        return jnp.concatenate(
            [xs_ref[vcol, pl.ds(s, m, stride=CMP_STRIDE), :] for s in range(CMP_STRIDE)], axis=1)

    first_row = lax.broadcasted_iota(jnp.int32, (m, 1), 0) == 0
    out_cols = []
    for c in range(2):
        w1 = w1_ref[c]
        hpe = _dot(pe_ref[c, 0], w1)[0:1, :hid2] + _dot(pe_ref[c, 1], w1)[0:1, hid2:]
        lhs = jnp.concatenate([strides(2 * c + hp) for hp in range(2)], axis=0).astype(BF)
        a = _dot(lhs, w1)
        acc = jnp.zeros((m, KV_WIDTH), F32)
        for hp in range(2):
            a0 = a[hp * m:(hp + 1) * m, :hid2]
            a1 = a[hp * m:(hp + 1) * m, hid2:]
            prev = jnp.where(first_row, carry_ref[c, hp, 0:1, :], pltpu.roll(a0, 1, axis=0))
            carry_ref[c, hp, 0:1, :] = a0[m - 1:m]
            acc = acc + _dot(jax.nn.gelu(prev + a1 + hpe).astype(BF), w2_ref[c, hp])
        out_cols.append(acc)
    o_ref[0] = jnp.concatenate(out_cols, axis=1)


def _compress(pool, table, tail, w1, pe, w2):
    b, n_real = table.shape
    n_pages = n_real + (0 if tail is None else 1)
    n_slots = max(p for p in range(1, 17) if n_pages % p == 0)
    n_groups = n_pages // n_slots
    m = n_slots * (PAGE // CMP_STRIDE)
    page_shape = pool.shape[1:]
    zeros = (0,) * len(page_shape)

    def page_spec(slot):
        return pl.BlockSpec(
            (1,) + page_shape,
            lambda i, k, pt: (pt[i * n_real + jnp.minimum(k * n_slots + slot, n_real - 1)],) + zeros)

    in_specs = [page_spec(slot) for slot in range(n_slots)]
    args = [pool] * n_slots
    if tail is not None:
        in_specs.append(pl.BlockSpec((1,) + page_shape, lambda i, k, pt: (i,) + zeros))
        args.append(tail)
    in_specs += [pl.BlockSpec(x.shape, lambda i, k, pt, nd=x.ndim: (0,) * nd) for x in (w1, pe, w2)]
    return pl.pallas_call(
        functools.partial(_compress_kernel, n_slots=n_slots,
                          tail_group=None if tail is None else n_groups - 1),
        grid_spec=pltpu.PrefetchScalarGridSpec(
            num_scalar_prefetch=1, grid=(b, n_groups), in_specs=in_specs,
            out_specs=pl.BlockSpec((1, m, ROW_WIDTH), lambda i, k, pt: (i, k, 0)),
            scratch_shapes=[pltpu.VMEM((4, n_slots * PAGE, LANES), F32),
                            pltpu.VMEM((2, 2, 8, 2 * CMP_HIDDEN), F32)]),
        out_shape=jax.ShapeDtypeStruct((b, n_groups * m, ROW_WIDTH), F32),
        compiler_params=_cparams(("parallel", "arbitrary"), VMEM_LIMIT),
        name="compress",
    )(table.reshape(-1), *args, w1, pe, w2)


def _nsa_kernel(q_ref, gt_ref, kc_ref, vct_ref, ks_ref, vst_ref, kw_ref, vwt_ref, o_ref, psum_ref, bias_ref,
                sbuf0_ref, sbuf1_ref, *, t_base, wpos0, n_slc):
    g = pl.program_id(1)
    i = pl.program_id(2)
    nq = q_ref.shape[2]
    nl = GQA * nq
    ncp = kc_ref.shape[2]
    gk = SEL_GROUP * PAGE
    blocks_per_group = gk // SEL_BLOCK
    nsp = bias_ref.shape[0]
    nkt_s = vst_ref.shape[2]
    q = q_ref[0].reshape(nl, HEAD_DIM)
    lane = lax.broadcasted_iota(jnp.int32, (1, nl), 1)
    q0 = t_base + i * nq
    t = q0 + (lane & (nq - 1))

    s = _dot_nt(kc_ref[0, 0], q)
    n_idx = lax.broadcasted_iota(jnp.int32, (ncp, 1), 0)
    cmask = n_idx * CMP_STRIDE + (CMP_BLOCK - 1) <= t
    s = jnp.where(cmask, s, NEG)
    p = jnp.where(cmask, jnp.exp(s - jnp.max(s, axis=0, keepdims=True)), 0.0)
    l = jnp.sum(p, axis=0, keepdims=True)
    p = p * jnp.where(l > 0.0, 1.0 / l, 0.0)
    o_cmp = jnp.zeros((HEAD_DIM, nl), F32)
    for kt in range(ncp // PAGE):
        o_cmp = o_cmp + _dot(vct_ref[0, 0, kt], p[kt * PAGE:(kt + 1) * PAGE].astype(BF))

    rel = (q0 - wpos0) // PAGE
    n_wt = WINDOW // PAGE + 1
    tile_row = lax.broadcasted_iota(jnp.int32, (PAGE, 1), 0)
    parts = []
    for u in range(n_wt):
        kt = rel - (n_wt - 1) + u
        ktc = jnp.maximum(kt, 0)
        kb = kw_ref[0, 0, pl.ds(pl.multiple_of(ktc * PAGE, PAGE), PAGE), :]
        krel = kt * PAGE + tile_row
        kpos = jnp.where(krel >= 0, wpos0 + krel, -(1 << 30))
        d = lax.bitcast_convert_type(t - kpos, jnp.uint32)
        parts.append(jnp.where(d < WINDOW, _dot_nt(kb, q), NEG))
    sw = jnp.concatenate(parts, axis=0)
    pw = jnp.exp(sw - jnp.max(sw, axis=0, keepdims=True))
    l_w = jnp.sum(pw, axis=0, keepdims=True)
    acc_w = jnp.zeros((HEAD_DIM, nl), F32)
    for u in range(n_wt):
        ktc = jnp.maximum(rel - (n_wt - 1) + u, 0)
        acc_w = acc_w + _dot(vwt_ref[0, 0, ktc], pw[u * PAGE:(u + 1) * PAGE].astype(BF))
    o_win = acc_w * (1.0 / l_w)

    psum = p[:, :nq]
    for r in range(1, GQA):
        psum = psum + p[:, r * nq:(r + 1) * nq]
    front = 8
    psum_ref[:front, :] = jnp.zeros((front, nq), F32)
    psum_ref[front:front + ncp, :] = psum
    psum_ref[front + ncp:, :] = jnp.zeros((psum_ref.shape[0] - front - ncp, nq), F32)
    ratio = SEL_BLOCK // CMP_STRIDE
    imp = psum_ref[pl.ds(front - 1, nsp, stride=ratio), :]
    for d in range(1, ratio + 1):
        imp = imp + psum_ref[pl.ds(front - 1 + d, nsp, stride=ratio), :]
    tq = t[:, :nq]
    j = lax.broadcasted_iota(jnp.int32, (nsp, 1), 0)
    cur = tq >> int(math.log2(SEL_BLOCK))
    forced = (j == 0) | (j == cur) | (j == cur - 1)
    score = jnp.where(j * SEL_BLOCK <= tq, imp + FORCE_BONUS * forced.astype(F32), NEG)
    sub = 8
    chunks = [score[c * sub:(c + 1) * sub] for c in range(nsp // sub)]
    jc = j[:sub]
    cnts = [jnp.zeros((sub, nq), F32) for _ in chunks]
    for ii in range(n_slc):
        ri = score[ii:ii + 1, :]
        for c, sc_c in enumerate(chunks):
            if c * sub > ii:
                one = jnp.where(ri >= sc_c, 1.0, 0.0)
            elif c * sub + sub - 1 <= ii:
                one = jnp.where(ri > sc_c, 1.0, 0.0)
            else:
                one = jnp.where(jc + c * sub > ii, jnp.where(ri >= sc_c, 1.0, 0.0), jnp.where(ri > sc_c, 1.0, 0.0))
            cnts[c] = cnts[c] + one
    cnt = jnp.concatenate(cnts, axis=0)
    bias_ref[...] = jnp.where(cnt < float(SEL_TOPK), jnp.where(score > NEG / 2, 0.0, NEG), NEG)

    key_row = lax.broadcasted_iota(jnp.int32, (gk, 1), 0)

    def fill(buf, gi):
        kb = ks_ref[0, 0, pl.ds(pl.multiple_of(gi * gk, gk), gk), :]
        rows = bias_ref[pl.ds(pl.multiple_of(gi * blocks_per_group, blocks_per_group), blocks_per_group), :]
        rows = jnp.concatenate([rows] * GQA, axis=1)
        bias = jnp.concatenate([jnp.broadcast_to(rows[b:b + 1], (SEL_BLOCK, nl)) for b in range(blocks_per_group)],
                               axis=0)
        buf[...] = _dot_nt(kb, q) + bias

    def consume(buf, gi, carry, causal):
        m_old, l_old, acc = carry
        sc = buf[...]
        if causal:
            sc = jnp.where(gi * gk + key_row <= t, sc, NEG)
        m_new = jnp.maximum(m_old, jnp.max(sc, axis=0, keepdims=True))
        alpha = jnp.exp(m_old - m_new)
        pe = jnp.exp(sc - m_new)
        acc = acc * alpha
        for u in range(SEL_GROUP):
            acc = acc + _dot(vst_ref[0, 0, gi * SEL_GROUP + u], pe[u * PAGE:(u + 1) * PAGE].astype(BF))
        return m_new, l_old * alpha + jnp.sum(pe, axis=0, keepdims=True), acc

    init = (jnp.full((1, nl), NEG, F32), jnp.zeros((1, nl), F32), jnp.zeros((HEAD_DIM, nl), F32))

    n_groups = jnp.minimum(nkt_s - 1, (q0 + nq - 1) // PAGE) // SEL_GROUP + 1
    fill(sbuf0_ref, 0)

    def pair_body(kk, carry):
        gi = 2 * kk
        fill(sbuf1_ref, gi + 1)
        carry = consume(sbuf0_ref, gi, carry, False)
        fill(sbuf0_ref, gi + 2)
        return consume(sbuf1_ref, gi + 1, carry, False)

    n_pairs = (n_groups - 1) // 2
    carry = lax.fori_loop(0, n_pairs, pair_body, init)
    gi = 2 * n_pairs

    def odd_tail(carry):
        fill(sbuf1_ref, gi + 1)
        carry = consume(sbuf0_ref, gi, carry, False)
        return consume(sbuf1_ref, gi + 1, carry, True)

    def even_tail(carry):
        return consume(sbuf0_ref, gi, carry, True)

    _, l_s, acc_s = lax.cond(n_groups - 1 - gi == 1, odd_tail, even_tail, carry)

    o_sel = acc_s * jnp.where(l_s > 0.0, 1.0 / l_s, 0.0)
    for r in range(GQA):
        sl = slice(r * nq, (r + 1) * nq)
        grow = (GQA * g + r) * 3
        o = (gt_ref[0, pl.ds(grow, 1), :] * o_cmp[:, sl]
             + gt_ref[0, pl.ds(grow + 1, 1), :] * o_sel[:, sl]
             + gt_ref[0, pl.ds(grow + 2, 1), :] * o_win[:, sl])
        o_ref[0, r * HEAD_DIM:(r + 1) * HEAD_DIM, :] = o


def _nsa(q, gt, kc, vct, ks, vst, kw, vwt, *, t_base, wpos0, n_slc):
    b, _, sq, _ = q.shape
    nq = LANES
    assert t_base % PAGE == 0 and (t_base - wpos0) % PAGE == 0 and t_base >= wpos0
    assert vst.shape[2] % SEL_GROUP == 0
    nsp = max(-(-n_slc // 8) * 8, vst.shape[2] * (PAGE // SEL_BLOCK))
    ncp = kc.shape[2]
    psum_rows = 8 + max(ncp + 8, (SEL_BLOCK // CMP_STRIDE) * nsp + 8)
    psum_rows = -(-psum_rows // 8) * 8

    def kv_specs(k, vt):
        return [pl.BlockSpec((1, 1) + k.shape[2:], lambda bi, gi, ii: (bi, gi, 0, 0)),
                pl.BlockSpec((1, 1) + vt.shape[2:], lambda bi, gi, ii: (bi, gi, 0, 0, 0))]

    return pl.pallas_call(
        functools.partial(_nsa_kernel, t_base=t_base, wpos0=wpos0, n_slc=n_slc),
        grid=(b, KV_HEADS, sq // nq),
        in_specs=[pl.BlockSpec((1, GQA, nq, HEAD_DIM), lambda bi, gi, ii: (bi, gi, ii, 0)),
                  pl.BlockSpec((1, GATE_PAD, nq), lambda bi, gi, ii: (bi, 0, ii))]
        + kv_specs(kc, vct) + kv_specs(ks, vst) + kv_specs(kw, vwt),
        out_specs=pl.BlockSpec((1, GQA * HEAD_DIM, nq), lambda bi, gi, ii: (bi, gi, ii)),
        out_shape=jax.ShapeDtypeStruct((b, NSA_WIDTH, sq), F32),
        scratch_shapes=[pltpu.VMEM((psum_rows, nq), F32), pltpu.VMEM((nsp, nq), F32),
                        pltpu.VMEM((SEL_GROUP * PAGE, GQA * nq), F32), pltpu.VMEM((SEL_GROUP * PAGE, GQA * nq), F32)],
        compiler_params=_cparams(("parallel", "parallel", "parallel"), VMEM_LIMIT),
        name="nsa_attend",
    )(q, gt, kc, vct, ks, vst, kw, vwt)


SAMPLE_ROWS = 64
ROWS_PER_GROUP = SAMPLE_ROWS // KV_HEADS


def _nsa_sample_kernel(pt_ref, *refs, n_slots, t_base, wpos0, n_slc, t_q):
    page_refs = refs[:n_slots]
    (tail_ref, q_ref, gt_ref, kct_ref, vct_ref, kwt_ref, vwt_ref, o_ref,
     ind_ref, m_ref, l_ref, acc_ref, oc_ref, ow_ref) = refs[n_slots:]
    k = pl.program_id(1)
    n_steps = pl.num_programs(1)
    q = q_ref[0]
    row = lax.broadcasted_iota(jnp.int32, (SAMPLE_ROWS, 1), 0)
    t = t_base + (row & (t_q - 1))

    @pl.when(k == 0)
    def _():
        ncp = kct_ref.shape[2]
        s = _dot(q, kct_ref[0])
        n_idx = lax.broadcasted_iota(jnp.int32, (1, ncp), 1)
        cmask = n_idx * CMP_STRIDE + (CMP_BLOCK - 1) <= t
        s = jnp.where(cmask, s, NEG)
        p = jnp.where(cmask, jnp.exp(s - jnp.max(s, axis=1, keepdims=True)), 0.0)
        l = jnp.sum(p, axis=1, keepdims=True)
        p = p * jnp.where(l > 0.0, 1.0 / l, 0.0)
        oc_ref[...] = _dot_nt(p.astype(BF), vct_ref[0])

        parts = []
        for g in range(KV_HEADS):
            base = g * ROWS_PER_GROUP
            acc = p[base:base + t_q]
            for r in range(1, GQA):
                acc = acc + p[base + r * t_q:base + (r + 1) * t_q]
            parts.append(acc)
        psum = jnp.concatenate(parts, axis=0)
        nsl = ind_ref.shape[1]
        n_col = lax.broadcasted_iota(jnp.int32, (ncp, 1), 0)
        j_row = lax.broadcasted_iota(jnp.int32, (1, nsl), 1)
        ratio = SEL_BLOCK // CMP_STRIDE
        overlap = jnp.where((n_col >= ratio * j_row - 1) & (n_col <= ratio * j_row + ratio - 1), 1.0, 0.0).astype(BF)
        hi = psum.astype(BF)
        mid = (psum - hi.astype(F32)).astype(BF)
        lo = (psum - hi.astype(F32) - mid.astype(F32)).astype(BF)
        imp = _dot(hi, overlap) + _dot(mid, overlap) + _dot(lo, overlap)
        nsel_rows = KV_HEADS * t_q
        tq = t_base + (lax.broadcasted_iota(jnp.int32, (nsel_rows, 1), 0) & (t_q - 1))
        cur = tq >> int(math.log2(SEL_BLOCK))
        forced = (j_row == 0) | (j_row == cur) | (j_row == cur - 1)
        score = jnp.where(j_row * SEL_BLOCK <= tq, imp + FORCE_BONUS * forced.astype(F32), NEG)
        score_t = jnp.concatenate([score, jnp.full((LANES - nsel_rows, nsl), NEG, F32)], axis=0).T
        sub = 8
        nsp = -(-n_slc // sub) * sub
        jc = lax.broadcasted_iota(jnp.int32, (sub, 1), 0)
        chunks = [score_t[c * sub:(c + 1) * sub] for c in range(nsp // sub)]
        cnts = [jnp.zeros((sub, LANES), F32) for _ in chunks]
        for ii in range(n_slc):
            ri = score_t[ii:ii + 1, :]
            for c, sc_c in enumerate(chunks):
                if c * sub > ii:
                    one = jnp.where(ri >= sc_c, 1.0, 0.0)
                elif c * sub + sub - 1 <= ii:
                    one = jnp.where(ri > sc_c, 1.0, 0.0)
                else:
                    one = jnp.where(jc + c * sub > ii, jnp.where(ri >= sc_c, 1.0, 0.0), jnp.where(ri > sc_c, 1.0, 0.0))
                cnts[c] = cnts[c] + one
        chosen = [jnp.where(cn < float(SEL_TOPK), jnp.where(sc_c > NEG / 2, 1.0, 0.0), 0.0)
                  for cn, sc_c in zip(cnts, chunks)]
        chosen_t = jnp.concatenate(chosen + [jnp.zeros((nsl - nsp, LANES), F32)], axis=0)
        chosen_rows = chosen_t.T.astype(BF)
        col = lax.broadcasted_iota(jnp.int32, (1, LANES), 1)
        src = (row >> int(math.log2(ROWS_PER_GROUP))) * t_q + (row & (t_q - 1))
        expand = jnp.where(col == src, 1.0, 0.0).astype(BF)
        ind_ref[...] = _dot(expand, chosen_rows).astype(BF)

        nw = kwt_ref.shape[2]
        sw = _dot(q, kwt_ref[0])
        kpos = wpos0 + lax.broadcasted_iota(jnp.int32, (1, nw), 1)
        d = lax.bitcast_convert_type(t - kpos, jnp.uint32)
        sw = jnp.where(d < WINDOW, sw, NEG)
        pw = jnp.exp(sw - jnp.max(sw, axis=1, keepdims=True))
        lw = jnp.sum(pw, axis=1, keepdims=True)
        ow_ref[...] = _dot_nt(pw.astype(BF), vwt_ref[0]) * (1.0 / lw)

        m_ref[...] = jnp.full(m_ref.shape, NEG, F32)
        l_ref[...] = jnp.zeros(l_ref.shape, F32)
        acc_ref[...] = jnp.zeros(acc_ref.shape, F32)

    vts = []
    scs = []
    for i, pr in enumerate(page_refs):
        x = pr[0]
        if i == n_slots - 1:
            x = jnp.where(k == n_steps - 1, tail_ref[0], x)
        scs.append(_dot(q, x[0].reshape(KV_WIDTH, PAGE).astype(BF)))
        vts.append(x[1].reshape(KV_WIDTH, PAGE).astype(BF))
    sc = jnp.concatenate(scs, axis=1)
    nk = n_slots * PAGE
    kidx = k * nk + lax.broadcasted_iota(jnp.int32, (1, nk), 1)
    nsl = ind_ref.shape[1]
    j_col = lax.broadcasted_iota(jnp.int32, (nsl, 1), 0)
    block_of_key = jnp.where(j_col == (kidx >> int(math.log2(SEL_BLOCK))), 1.0, 0.0).astype(BF)
    picked = _dot(ind_ref[...], block_of_key)
    sc = jnp.where(picked > 0.5, jnp.where(kidx <= t, sc, NEG), NEG)
    m_old = m_ref[:, :1]
    m_new = jnp.maximum(m_old, jnp.max(sc, axis=1, keepdims=True))
    alpha = jnp.exp(m_old - m_new)
    pe = jnp.exp(sc - m_new)
    acc = acc_ref[...] * alpha
    for i in range(n_slots):
        acc = acc + _dot_nt(pe[:, i * PAGE:(i + 1) * PAGE].astype(BF), vts[i])
    l_new = l_ref[:, :1] * alpha + jnp.sum(pe, axis=1, keepdims=True)
    m_ref[...] = jnp.broadcast_to(m_new, m_ref.shape)
    l_ref[...] = jnp.broadcast_to(l_new, l_ref.shape)
    acc_ref[...] = acc

    @pl.when(k == n_steps - 1)
    def _():
        o_sel = acc * jnp.where(l_new > 0.0, 1.0 / l_new, 0.0)
        gts = gt_ref[0]
        o_ref[0] = gts[:, 0:1] * oc_ref[...] + gts[:, 1:2] * o_sel + gts[:, 2:3] * ow_ref[...]


def _nsa_sample(q_bd, gates, pool_t, table, tail_t, kct, vct, kwt, vwt, *, t_base, wpos0, n_slc, t_q):
    b, n_real = table.shape
    n_pages = n_real + 1
    n_slots = max(p for p in range(1, 17) if n_pages % p == 0)
    n_steps = n_pages // n_slots
    nsl = -(-n_slc // LANES) * LANES
    page_shape = pool_t.shape[1:]
    zeros = (0,) * len(page_shape)

    def page_spec(slot):
        return pl.BlockSpec(
            (1,) + page_shape,
            lambda i, k, pt: (pt[i * n_real + jnp.minimum(k * n_slots + slot, n_real - 1)],) + zeros)

    def per_batch(x):
        return pl.BlockSpec((1,) + x.shape[1:], lambda i, k, pt: (i,) + (0,) * (x.ndim - 1))

    acc_shape = pltpu.VMEM((SAMPLE_ROWS, KV_WIDTH), F32)
    stat_shape = pltpu.VMEM((SAMPLE_ROWS, LANES), F32)
    return pl.pallas_call(
        functools.partial(_nsa_sample_kernel, n_slots=n_slots, t_base=t_base, wpos0=wpos0, n_slc=n_slc, t_q=t_q),
        grid_spec=pltpu.PrefetchScalarGridSpec(
            num_scalar_prefetch=1, grid=(b, n_steps),
            in_specs=[page_spec(slot) for slot in range(n_slots)]
            + [per_batch(x) for x in (tail_t, q_bd, gates, kct, vct, kwt, vwt)],
            out_specs=pl.BlockSpec((1, SAMPLE_ROWS, KV_WIDTH), lambda i, k, pt: (i, 0, 0)),
            scratch_shapes=[pltpu.VMEM((SAMPLE_ROWS, nsl), BF), stat_shape, stat_shape, acc_shape, acc_shape,
                            acc_shape]),
        out_shape=jax.ShapeDtypeStruct((b, SAMPLE_ROWS, KV_WIDTH), F32),
        compiler_params=_cparams(("parallel", "arbitrary"), VMEM_LIMIT),
        name="nsa_sample",
    )(table.reshape(-1), *([pool_t] * n_slots), tail_t, q_bd, gates, kct, vct, kwt, vwt)


def _rope_tables(pos):
    half = ROPE_DIM // 2
    inv_freq = ROPE_THETA ** (-jnp.arange(half, dtype=F32) / half)
    ang = pos.astype(F32)[:, None] * inv_freq[None, :]
    cos, sin = jnp.cos(ang), jnp.sin(ang)
    n = pos.shape[0]
    rest = HEAD_DIM - ROPE_DIM
    c = jnp.concatenate([cos, cos, jnp.ones((n, rest), F32)], axis=1)
    a = jnp.concatenate([-sin, jnp.zeros((n, half + rest), F32)], axis=1)
    b = jnp.concatenate([jnp.zeros((n, half), F32), sin, jnp.zeros((n, rest), F32)], axis=1)
    reps = LANES // HEAD_DIM
    return tuple(jnp.tile(x, (1, reps)) for x in (c, a, b))


def _compress_weights(cmp_pe, w_phi1, w_phi2):
    ratio = CMP_BLOCK // CMP_STRIDE
    w1 = w_phi1.reshape(ratio, CMP_STRIDE, 2, HEAD_DIM, CMP_HIDDEN)
    eye2 = jnp.eye(2, dtype=w_phi1.dtype)
    big = jnp.einsum('jscde,hg->cjshdge', w1, eye2)
    big = big.reshape(2, ratio, CMP_STRIDE * 2 * HEAD_DIM, 2 * CMP_HIDDEN).astype(BF)
    big = jnp.concatenate([big[:, jj] for jj in range(ratio)], axis=-1)
    pe = cmp_pe.reshape(ratio, CMP_STRIDE, 2, HEAD_DIM)
    pe = jnp.broadcast_to(pe.transpose(2, 0, 1, 3)[:, :, :, None, :], (2, ratio, CMP_STRIDE, 2, HEAD_DIM))
    pe = pe.reshape(2, ratio, 1, CMP_STRIDE * 2 * HEAD_DIM)
    pe = jnp.broadcast_to(pe, (2, ratio, 8, pe.shape[-1])).astype(BF)
    eye4 = jnp.eye(KV_HEADS, dtype=w_phi2.dtype).reshape(2, 2, KV_HEADS)
    w2 = jnp.einsum('ced,pgh->cpgehd', w_phi2, eye4)
    w2 = w2.reshape(2, 2, 2 * CMP_HIDDEN, KV_WIDTH).astype(BF)
    return big, pe, w2


def _b_in_weight(w):
    n_gate = w.shape[1] - NSA_WIDTH - MEM_WIDTH
    gates = jnp.pad(w[:, NSA_WIDTH:NSA_WIDTH + n_gate], ((0, 0), (0, GATE_PAD - n_gate)))
    return jnp.concatenate([w[:, :NSA_WIDTH], w[:, NSA_WIDTH + n_gate:], gates], axis=1).astype(BF)


def _pad_rows(x, n):
    return jnp.pad(x, ((0, 0), (0, n - x.shape[1]), (0, 0)))


def _tail_page_t(new_rows):
    b = new_rows.shape[0]
    return _pad_rows(new_rows, PAGE).reshape(b, PAGE, 2, KV_HEADS, HEAD_DIM).transpose(0, 2, 3, 4, 1)


def _rows_t(rows):
    b, n, _ = rows.shape
    kv = rows.reshape(b, n, 2, KV_WIDTH).transpose(0, 2, 3, 1).astype(BF)
    return kv[:, 0], kv[:, 1]


def _row_tiles(s):
    return {"proj": min(s, 256), "gate": min(s, 512), "mlp": min(s, 512), "mem": min(s, 512)}


def _trunk_a(x, mem_kv, p, l, *, chunk, tiles, tq, emit_v, mem_batch):
    b, s, _ = x.shape
    if chunk == PAGE:
        ws, bs = p["w_spatial"][l], jnp.repeat(p["b_spatial"][l].T, A_GROUP_DIM, axis=1)
    else:
        reps = PAGE // chunk
        ws = jnp.tile(p["w_spatial"][l][:, :chunk, :chunk], (1, reps, reps))
        bs = jnp.tile(jnp.repeat(p["b_spatial"][l].T[:chunk], A_GROUP_DIM, axis=1), (reps, 1))
    res = _a_in(x, p["w_in_a"][l], p["ln_v_g"][l][None], p["ln_v_b"][l][None], ws, bs,
                chunk=chunk, tm=tiles["gate"], emit_v=emit_v)
    mixed, qm = res[0], res[1]
    om = _mem_attn(qm.reshape(mem_batch, -1, MEM_WIDTH), mem_kv, tq=tq).reshape(b, s, MEM_WIDTH)
    x = _mlp_block(x, mixed, om, p["w_out_a"][l], p, l, tm=tiles["mlp"], a_transposed=False)
    return x, (res[2] if emit_v else None)


def _mlp_block(x, a, om, wo, p, l, *, tm, a_transposed):
    return _out_mlp(x, a, om, wo, p["ln1_g"][l][None], p["ln1_b"][l][None], p["w_up"][l], p["w_down"][l],
                    p["ln2_g"][l][None], p["ln2_b"][l][None], tm=tm, a_transposed=a_transposed)


def kernel(x_prompt, x_sample, cache_cmp_kv, cache_slc_kv, cache_win_kv, cache_mem_kv, page_table, mem_prompt,
           w_in_a, ln_v_g, ln_v_b, w_spatial, b_spatial, w_out_a, w_in_b, w_out_b, w_kv_shared, cmp_pe,
           w_phi1, w_phi2, w_mem_kv, ln1_g, ln1_b, ln2_g, ln2_b, w_up, w_down):
    bp, s, d = x_prompt.shape
    bs, t, _ = x_sample.shape
    n_pages = page_table.shape[1]
    past = n_pages * PAGE
    n_buf = cache_win_kv.shape[1]
    assert bs * t == PAGE and PAGE % t == 0 and s % PAGE == 0 and n_buf % PAGE == 0

    p = {"w_in_a": w_in_a.astype(BF), "ln_v_g": ln_v_g, "ln_v_b": ln_v_b, "w_spatial": w_spatial,
         "b_spatial": b_spatial, "w_out_a": w_out_a.astype(BF), "w_out_b": w_out_b.astype(BF),
         "ln1_g": ln1_g, "ln1_b": ln1_b, "ln2_g": ln2_g, "ln2_b": ln2_b,
         "w_up": w_up.astype(BF), "w_down": w_down.astype(BF)}
    w_in_b2 = [_b_in_weight(w_in_b[i]) for i in range(w_in_b.shape[0])]
    w_kv = w_kv_shared.astype(BF)
    cw1, cpe, cw2 = _compress_weights(cmp_pe, w_phi1, w_phi2)

    tl = _row_tiles(s)
    mem_kv_prompt = _mem_kv_proj(mem_prompt, w_mem_kv)
    x = x_prompt
    for l in range(N_A):
        x, _ = _trunk_a(x, mem_kv_prompt[l], p, l, chunk=PAGE, tiles=tl, tq=tl["mem"], emit_v=False, mem_batch=bp)
    tabs_p = _rope_tables(jnp.arange(s, dtype=jnp.int32))
    cmp_rows, slc_rows, win_rows, ks, vst, kw, vwt = _kv_shared(x, w_kv, tabs_p, tm=tl["proj"], attn_layouts=True)
    n_pg = s // PAGE
    ident = jnp.arange(bp * n_pg, dtype=jnp.int32).reshape(bp, n_pg)
    cmp_blk = _compress(cmp_rows.reshape(bp * n_pg, PAGE, ROW_WIDTH), ident, None, cw1, cpe, cw2)
    n_cmp = s // CMP_STRIDE - 1
    ncp = -(-n_cmp // PAGE) * PAGE
    kc, vct = _rows_layout(_pad_rows(cmp_blk[:, 1:1 + n_cmp], ncp))
    for i in range(DEPTH - N_A):
        l = N_A + i
        q, qm, gt = _b_in(x, w_in_b2[i], tabs_p, tm=tl["proj"])
        om = _mem_attn(qm, mem_kv_prompt[l], tq=tl["mem"])
        ot = _nsa(q, gt, kc, vct, ks, vst, kw, vwt, t_base=0, wpos0=0, n_slc=s // SEL_BLOCK)
        x = _mlp_block(x, ot, om, p["w_out_b"][i], p, l, tm=tl["mlp"], a_transposed=True)
    y_prompt = x
    shape5 = (2, KV_HEADS, HEAD_DIM)
    cmp_kv_prompt = cmp_rows.reshape(bp, s, *shape5)
    slc_kv_prompt = slc_rows.reshape(bp, s, *shape5)
    win_kv_prompt = win_rows[:, s - min(WINDOW, s):].reshape(bp, min(WINDOW, s), *shape5)
    mem_kv_prompt_out = mem_kv_prompt.reshape(DEPTH, bp, MEM_TOKENS, 2, MEM_HEADS, HEAD_DIM)

    rows = bs * t
    x = x_sample.reshape(1, rows, d)
    mem_s = cache_mem_kv.reshape(DEPTH, bs, MEM_TOKENS, 2 * MEM_WIDTH)
    v_rows = []
    for l in range(N_A):
        x, v = _trunk_a(x, mem_s[l], p, l, chunk=t, tiles=_row_tiles(rows), tq=t, emit_v=True, mem_batch=bs)
        v_rows.append(v.reshape(bs, t, A_WIDTH))
    pos_s = past + (jnp.arange(rows, dtype=jnp.int32) % t)
    tabs_s = _rope_tables(pos_s)
    new_cmp, new_slc, new_win = (r.reshape(bs, t, ROW_WIDTH)
                                 for r in _kv_shared(x, w_kv, tabs_s, tm=rows, attn_layouts=False))
    cmp_pool_t = cache_cmp_kv.transpose(0, 2, 3, 4, 1)
    slc_pool_t = cache_slc_kv.transpose(0, 2, 3, 4, 1)
    cmp_blk = _compress(cmp_pool_t, page_table, _tail_page_t(new_cmp), cw1, cpe, cw2)
    n_cmp = -(-(past + t) // CMP_STRIDE) - 1
    ncp = -(-n_cmp // PAGE) * PAGE
    kct, vct = _rows_t(_pad_rows(cmp_blk[:, 1:1 + n_cmp], ncp))
    win_all = jnp.concatenate([cache_win_kv.reshape(bs, n_buf, ROW_WIDTH), new_win], axis=1)
    kwt, vwt = _rows_t(_pad_rows(win_all, n_buf + PAGE))
    slc_tail_t = _tail_page_t(new_slc)
    n_slc = -(-(past + t) // SEL_BLOCK)
    per_group = GQA * t
    eye = jnp.eye(KV_HEADS, dtype=BF)
    for i in range(DEPTH - N_A):
        l = N_A + i
        q, qm, gt = _b_in(x, w_in_b2[i], tabs_s, tm=rows)
        om = _mem_attn(qm.reshape(bs, t, MEM_WIDTH), mem_s[l], tq=t).reshape(1, rows, MEM_WIDTH)
        q = q.reshape(KV_HEADS, GQA, bs, t, HEAD_DIM).transpose(2, 0, 1, 3, 4).reshape(bs, KV_HEADS, per_group, HEAD_DIM)
        q = jnp.pad(q, ((0, 0), (0, 0), (0, ROWS_PER_GROUP - per_group), (0, 0)))
        q_bd = jnp.einsum('bgxd,gh->bgxhd', q, eye).reshape(bs, SAMPLE_ROWS, KV_WIDTH)
        gs = gt[0, :NSA_HEADS * 3].reshape(KV_HEADS, GQA, 3, bs, t).transpose(3, 0, 1, 4, 2)
        gs = gs.reshape(bs, KV_HEADS, per_group, 3)
        gs = jnp.pad(gs, ((0, 0), (0, 0), (0, ROWS_PER_GROUP - per_group), (0, LANES - 3)))
        o = _nsa_sample(q_bd, gs.reshape(bs, SAMPLE_ROWS, LANES), slc_pool_t, page_table, slc_tail_t,
                        kct, vct, kwt, vwt, t_base=past, wpos0=past - n_buf, n_slc=n_slc, t_q=t)
        o = o.reshape(bs, KV_HEADS, ROWS_PER_GROUP, KV_HEADS, HEAD_DIM)
        o = jnp.stack([o[:, g, :per_group, g] for g in range(KV_HEADS)], axis=1)
        o = o.reshape(bs, KV_HEADS, GQA, t, HEAD_DIM).transpose(0, 3, 1, 2, 4).reshape(1, rows, NSA_WIDTH)
        x = _mlp_block(x, o, om, p["w_out_b"][i], p, l, tm=rows, a_transposed=False)
    y_sample = x.reshape(bs, t, d)
    cmp_kv_sample = new_cmp.reshape(bs, t, *shape5)
    slc_kv_sample = new_slc.reshape(bs, t, *shape5)
    win_kv_sample = win_all[:, t:].reshape(bs, n_buf, *shape5)
    gmlp_v_sample = jnp.stack(v_rows)

    return (y_prompt, y_sample, cmp_kv_prompt, slc_kv_prompt, win_kv_prompt, mem_kv_prompt_out,
            cmp_kv_sample, slc_kv_sample, win_kv_sample, gmlp_v_sample)
```

```python
import functools
import math

import jax
import jax.numpy as jnp
from jax import lax
from jax.experimental import pallas as pl
from jax.experimental.pallas import tpu as pltpu

BF = jnp.bfloat16
F32 = jnp.float32

DEPTH = 4
N_A = 2
A_WIDTH = 768
A_GROUPS = 4
A_GROUP_DIM = 192
MEM_TOKENS = 256
MEM_HEADS = 4
MEM_WIDTH = 256
HEAD_DIM = 64
NSA_HEADS = 12
KV_HEADS = 4
GQA = 3
NSA_WIDTH = 768
KV_WIDTH = 256
ROW_WIDTH = 2 * KV_WIDTH
CMP_BLOCK = 32
CMP_STRIDE = 16
CMP_HIDDEN = 128
SEL_BLOCK = 64
SEL_TOPK = 16
WINDOW = 512
ROPE_THETA = 500000.0
ROPE_DIM = 16
D_FF_CHUNK = 1024
ALPHA = (2.0 * DEPTH) ** 0.25
LN_EPS = 1e-5
NEG = -1e30
FORCE_BONUS = 1e4
PAGE = 128
LANES = 128
GATE_PAD = 128
SEL_GROUP = 4
VMEM_LIMIT = 56 * 1024 * 1024


def _cparams(sem, vmem=None):
    return pltpu.CompilerParams(dimension_semantics=sem, vmem_limit_bytes=vmem)


def _const_spec(shape):
    return pl.BlockSpec(shape, lambda *_: (0,) * len(shape), pipeline_mode=pl.Buffered(1))


def _ln(x, g, b):
    mu = jnp.mean(x, axis=-1, keepdims=True)
    xc = x - mu
    var = jnp.mean(xc * xc, axis=-1, keepdims=True)
    return xc * lax.rsqrt(var + LN_EPS) * g + b


def _dot(a, b):
    return jnp.dot(a, b, preferred_element_type=F32)


def _dot_nt(a, b):
    return lax.dot_general(a, b, (((1,), (1,)), ((), ())), preferred_element_type=F32)


def _rope_cols(x, c, a, b):
    cols = []
    for j in range(x.shape[1] // LANES):
        xc = x[:, j * LANES:(j + 1) * LANES]
        up = pltpu.roll(xc, LANES - ROPE_DIM // 2, axis=1)
        dn = pltpu.roll(xc, ROPE_DIM // 2, axis=1)
        cols.append(xc * c + up * a + dn * b)
    return cols[0] if len(cols) == 1 else jnp.concatenate(cols, axis=1)


def _mem_kv_kernel(x_ref, w_ref, o_ref):
    o_ref[0] = _dot(x_ref[...].astype(BF), w_ref[0].astype(BF))


def _mem_kv_proj(mem_prompt, w_mem_kv):
    b, m, d = mem_prompt.shape
    x = mem_prompt.reshape(b * m, d)
    depth, _, n = w_mem_kv.shape
    tm = 512
    out = pl.pallas_call(
        _mem_kv_kernel,
        grid=(depth, (b * m) // tm),
        in_specs=[pl.BlockSpec((tm, d), lambda l, i: (i, 0)),
                  pl.BlockSpec((1, d, n), lambda l, i: (l, 0, 0))],
        out_specs=pl.BlockSpec((1, tm, n), lambda l, i: (l, i, 0)),
        out_shape=jax.ShapeDtypeStruct((depth, b * m, n), F32),
        compiler_params=_cparams(("parallel", "parallel")),
        name="mem_kv_proj",
    )(x, w_mem_kv)
    return out.reshape(depth, b, m, n)


def _a_in_kernel(x_ref, w_ref, g_ref, b_ref, ws_ref, bs_ref, mixed_ref, qm_ref, *v_refs, chunk_shift):
    x = x_ref[0].astype(BF)
    proj = _dot(x, w_ref[...])
    uv = jax.nn.gelu(proj[:, :2 * A_WIDTH])
    u = uv[:, :A_WIDTH]
    v = _ln(uv[:, A_WIDTH:], g_ref[...], b_ref[...])
    if v_refs:
        v_refs[0][0] = v
    qm_ref[0] = proj[:, 2 * A_WIDTH:]
    tm = x.shape[0]
    r = ws_ref.shape[1]
    row = lax.broadcasted_iota(jnp.int32, (r, r), 0)
    col = lax.broadcasted_iota(jnp.int32, (r, r), 1)
    allowed = (col <= row) & ((row >> chunk_shift) == (col >> chunk_shift))
    lane = lax.broadcasted_iota(jnp.int32, (1, A_WIDTH), 1)
    wgs = [jnp.where(allowed, ws_ref[g], 0.0).astype(BF) for g in range(A_GROUPS)]
    gmask = [(lane >= g * A_GROUP_DIM) & (lane < (g + 1) * A_GROUP_DIM) for g in range(A_GROUPS)]
    vb = v.astype(BF)
    for ci in range(tm // r):
        vc = vb[ci * r:(ci + 1) * r]
        acc = bs_ref[...]
        for g in range(A_GROUPS):
            acc = acc + jnp.where(gmask[g], _dot(wgs[g], vc), 0.0)
        mixed_ref[0, ci * r:(ci + 1) * r, :] = (u[ci * r:(ci + 1) * r] * acc).astype(BF)


def _a_in(x, w_in, ln_g, ln_b, ws, bs, *, chunk, tm, emit_v):
    b, s, d = x.shape
    n = w_in.shape[1]
    r = ws.shape[1]
    outs = [jax.ShapeDtypeStruct((b, s, A_WIDTH), BF), jax.ShapeDtypeStruct((b, s, MEM_WIDTH), F32)]
    ospecs = [pl.BlockSpec((1, tm, A_WIDTH), lambda i, j: (i, j, 0)),
              pl.BlockSpec((1, tm, MEM_WIDTH), lambda i, j: (i, j, 0))]
    if emit_v:
        outs.append(jax.ShapeDtypeStruct((b, s, A_WIDTH), F32))
        ospecs.append(pl.BlockSpec((1, tm, A_WIDTH), lambda i, j: (i, j, 0)))
    return pl.pallas_call(
        functools.partial(_a_in_kernel, chunk_shift=int(math.log2(chunk))),
        grid=(b, s // tm),
        in_specs=[pl.BlockSpec((1, tm, d), lambda i, j: (i, j, 0)),
                  _const_spec((d, n)), _const_spec((1, A_WIDTH)), _const_spec((1, A_WIDTH)),
                  _const_spec((A_GROUPS, r, r)), _const_spec((r, A_WIDTH))],
        out_specs=ospecs,
        out_shape=outs,
        compiler_params=_cparams(("parallel", "parallel"), VMEM_LIMIT),
        name="a_in_gate",
    )(x, w_in, ln_g, ln_b, ws, bs)


def _mem_attn_kernel(q_ref, kv_ref, o_ref, *, rows_per_set):
    q = q_ref[0]
    nb = kv_ref.shape[0]
    kv = kv_ref[...].reshape(nb * MEM_TOKENS, 2 * MEM_WIDTH)
    k = kv[:, :MEM_WIDTH].astype(BF)
    v = kv[:, MEM_WIDTH:]
    lane = lax.broadcasted_iota(jnp.int32, (1, MEM_WIDTH), 1)
    if nb > 1:
        row_set = lax.broadcasted_iota(jnp.int32, (q.shape[0], 1), 0) >> int(math.log2(rows_per_set))
        col_set = lax.broadcasted_iota(jnp.int32, (1, nb * MEM_TOKENS), 1) >> int(math.log2(MEM_TOKENS))
        own = row_set == col_set
    out = jnp.zeros(q.shape, F32)
    for h in range(MEM_HEADS):
        hm = (lane >= h * HEAD_DIM) & (lane < (h + 1) * HEAD_DIM)
        s = _dot_nt(jnp.where(hm, q, 0.0).astype(BF), k) * (HEAD_DIM ** -0.5)
        if nb > 1:
            s = jnp.where(own, s, NEG)
        m = jnp.max(s, axis=-1, keepdims=True)
        p = jnp.exp(s - m)
        p = p / jnp.sum(p, axis=-1, keepdims=True)
        out = out + _dot(p.astype(BF), jnp.where(hm, v, 0.0).astype(BF))
    o_ref[0] = out


def _mem_attn(q, mem_kv, *, tq, sets_per_step=1):
    b, s, w = q.shape
    if sets_per_step == 1:
        kv_spec = pl.BlockSpec((1, MEM_TOKENS, 2 * w), lambda i, j: (i, 0, 0))
    else:
        assert b == 1
        kv_spec = pl.BlockSpec((sets_per_step, MEM_TOKENS, 2 * w), lambda i, j: (j, 0, 0))
    return pl.pallas_call(
        functools.partial(_mem_attn_kernel, rows_per_set=tq // sets_per_step),
        grid=(b, s // tq),
        in_specs=[pl.BlockSpec((1, tq, w), lambda i, j: (i, j, 0)), kv_spec],
        out_specs=pl.BlockSpec((1, tq, w), lambda i, j: (i, j, 0)),
        out_shape=jax.ShapeDtypeStruct((b, s, w), F32),
        compiler_params=_cparams(("parallel", "parallel")),
        name="mem_attn",
    )(q, mem_kv)


def _out_mlp_kernel(x_ref, a_ref, om_ref, wo_ref, g1_ref, b1_ref, wu_ref, wd_ref, g2_ref, b2_ref, o_ref,
                    *, a_transposed):
    x = x_ref[0]
    a = a_ref[0]
    a = a.T.astype(BF) if a_transposed else a.astype(BF)
    na = a.shape[1]
    mix = _dot(a, wo_ref[:na, :]) + _dot(om_ref[0].astype(BF), wo_ref[na:, :])
    y = _ln(ALPHA * x + mix, g1_ref[...], b1_ref[...])
    yb = y.astype(BF)
    acc = jnp.zeros(x.shape, F32)
    for c in range(wu_ref.shape[1] // D_FF_CHUNK):
        h = _dot(yb, wu_ref[:, c * D_FF_CHUNK:(c + 1) * D_FF_CHUNK])
        h = jnp.square(jnp.maximum(h, 0.0)).astype(BF)
        acc = acc + _dot(h, wd_ref[c * D_FF_CHUNK:(c + 1) * D_FF_CHUNK, :])
    o_ref[0] = _ln(ALPHA * y + acc, g2_ref[...], b2_ref[...])


def _out_mlp(x, a, om, wo, g1, b1, wu, wd, g2, b2, *, tm, a_transposed):
    b, s, d = x.shape
    na = wo.shape[0] - om.shape[2]
    dff = wu.shape[1]
    if a_transposed:
        a_spec = pl.BlockSpec((1, na, tm), lambda i, j: (i, 0, j))
    else:
        a_spec = pl.BlockSpec((1, tm, na), lambda i, j: (i, j, 0))
    return pl.pallas_call(
        functools.partial(_out_mlp_kernel, a_transposed=a_transposed),
        grid=(b, s // tm),
        in_specs=[pl.BlockSpec((1, tm, d), lambda i, j: (i, j, 0)),
                  a_spec,
                  pl.BlockSpec((1, tm, om.shape[2]), lambda i, j: (i, j, 0)),
                  _const_spec(wo.shape), _const_spec((1, d)), _const_spec((1, d)),
                  _const_spec((d, dff)), _const_spec((dff, d)), _const_spec((1, d)), _const_spec((1, d))],
        out_specs=pl.BlockSpec((1, tm, d), lambda i, j: (i, j, 0)),
        out_shape=jax.ShapeDtypeStruct((b, s, d), F32),
        compiler_params=_cparams(("parallel", "parallel"), VMEM_LIMIT),
        name="out_mlp",
    )(x, a, om, wo, g1, b1, wu, wd, g2, b2)


def _kv_shared_kernel(h_ref, w_ref, c_ref, a_ref, b_ref, cmp_ref, slc_ref, win_ref, *attn_refs):
    kv = _dot(h_ref[0].astype(BF), w_ref[...])
    c, a, b = c_ref[...], a_ref[...], b_ref[...]
    tm = kv.shape[0]
    for br, ref in enumerate((cmp_ref, slc_ref, win_ref)):
        k = _rope_cols(kv[:, br * ROW_WIDTH:br * ROW_WIDTH + KV_WIDTH], c, a, b)
        v = kv[:, br * ROW_WIDTH + KV_WIDTH:(br + 1) * ROW_WIDTH]
        ref[0, :, :KV_WIDTH] = k
        ref[0, :, KV_WIDTH:] = v
        if attn_refs and br > 0:
            k_ref, vt_ref = attn_refs[2 * (br - 1):2 * br]
            vt = v.T
            for g in range(KV_HEADS):
                k_ref[0, g] = k[:, g * HEAD_DIM:(g + 1) * HEAD_DIM].astype(BF)
                for u in range(tm // PAGE):
                    vt_ref[0, g, u] = vt[g * HEAD_DIM:(g + 1) * HEAD_DIM, u * PAGE:(u + 1) * PAGE].astype(BF)


def _kv_shared(h, w_kv, tabs, *, tm, attn_layouts):
    b, s, d = h.shape
    row_spec = pl.BlockSpec((1, tm, ROW_WIDTH), lambda i, j: (i, j, 0))
    tab_spec = pl.BlockSpec((tm, LANES), lambda i, j: (j, 0))
    out_specs = [row_spec, row_spec, row_spec]
    out_shape = [jax.ShapeDtypeStruct((b, s, ROW_WIDTH), F32)] * 3
    if attn_layouts:
        out_specs += [pl.BlockSpec((1, KV_HEADS, tm, HEAD_DIM), lambda i, j: (i, 0, j, 0)),
                      pl.BlockSpec((1, KV_HEADS, tm // PAGE, HEAD_DIM, PAGE), lambda i, j: (i, 0, j, 0, 0))] * 2
        out_shape += [jax.ShapeDtypeStruct((b, KV_HEADS, s, HEAD_DIM), BF),
                      jax.ShapeDtypeStruct((b, KV_HEADS, s // PAGE, HEAD_DIM, PAGE), BF)] * 2
    return pl.pallas_call(
        _kv_shared_kernel,
        grid=(b, s // tm),
        in_specs=[pl.BlockSpec((1, tm, d), lambda i, j: (i, j, 0)), _const_spec(w_kv.shape),
                  tab_spec, tab_spec, tab_spec],
        out_specs=out_specs,
        out_shape=out_shape,
        compiler_params=_cparams(("parallel", "parallel")),
        name="kv_shared",
    )(h, w_kv, *tabs)


def _b_in_kernel(x_ref, w_ref, c_ref, a_ref, b_ref, q_ref, qm_ref, gt_ref):
    proj = _dot(x_ref[0].astype(BF), w_ref[...])
    q = _rope_cols(proj[:, :NSA_WIDTH], c_ref[...], a_ref[...], b_ref[...]) * (HEAD_DIM ** -0.5)
    for h in range(NSA_HEADS):
        q_ref[0, h] = q[:, h * HEAD_DIM:(h + 1) * HEAD_DIM].astype(BF)
    qm_ref[0] = proj[:, NSA_WIDTH:NSA_WIDTH + MEM_WIDTH]
    gt_ref[0] = jax.nn.sigmoid(proj[:, NSA_WIDTH + MEM_WIDTH:]).T


def _b_in(x, w_in, tabs, *, tm):
    b, s, d = x.shape
    tab_spec = pl.BlockSpec((tm, LANES), lambda i, j: (j, 0))
    return pl.pallas_call(
        _b_in_kernel,
        grid=(b, s // tm),
        in_specs=[pl.BlockSpec((1, tm, d), lambda i, j: (i, j, 0)), _const_spec(w_in.shape),
                  tab_spec, tab_spec, tab_spec],
        out_specs=[pl.BlockSpec((1, NSA_HEADS, tm, HEAD_DIM), lambda i, j: (i, 0, j, 0)),
                   pl.BlockSpec((1, tm, MEM_WIDTH), lambda i, j: (i, j, 0)),
                   pl.BlockSpec((1, GATE_PAD, tm), lambda i, j: (i, 0, j))],
        out_shape=[jax.ShapeDtypeStruct((b, NSA_HEADS, s, HEAD_DIM), BF),
                   jax.ShapeDtypeStruct((b, s, MEM_WIDTH), F32),
                   jax.ShapeDtypeStruct((b, GATE_PAD, s), F32)],
        compiler_params=_cparams(("parallel", "parallel")),
        name="b_in",
    )(x, w_in, *tabs)


def _kv_layout_kernel(x_ref, k_ref, vt_ref):
    x = x_ref[0]
    vt = x[:, KV_WIDTH:].T
    for g in range(KV_HEADS):
        k_ref[0, g] = x[:, g * HEAD_DIM:(g + 1) * HEAD_DIM].astype(BF)
        vt_ref[0, g, 0] = vt[g * HEAD_DIM:(g + 1) * HEAD_DIM].astype(BF)


def _rows_layout(rows):
    b, l, _ = rows.shape
    n = l // PAGE
    return pl.pallas_call(
        _kv_layout_kernel,
        grid=(b, n),
        in_specs=[pl.BlockSpec((1, PAGE, ROW_WIDTH), lambda i, p: (i, p, 0))],
        out_specs=[pl.BlockSpec((1, KV_HEADS, PAGE, HEAD_DIM), lambda i, p: (i, 0, p, 0)),
                   pl.BlockSpec((1, KV_HEADS, 1, HEAD_DIM, PAGE), lambda i, p: (i, 0, p, 0, 0))],
        out_shape=[jax.ShapeDtypeStruct((b, KV_HEADS, l, HEAD_DIM), BF),
                   jax.ShapeDtypeStruct((b, KV_HEADS, n, HEAD_DIM, PAGE), BF)],
        compiler_params=_cparams(("parallel", "parallel")),
        name="kv_layout",
    )(rows)


def _compress_kernel(pt_ref, *refs, n_slots, tail_group):
    page_refs = refs[:n_slots]
    rest = refs[n_slots:]
    if tail_group is not None:
        tail_ref, rest = rest[0], rest[1:]
    w1_ref, pe_ref, w2_ref, o_ref, xs_ref, carry_ref = rest
    k = pl.program_id(1)
    per_page = PAGE // CMP_STRIDE
    m = n_slots * per_page
    hid2 = 2 * CMP_HIDDEN

    @pl.when(k == 0)
    def _():
        carry_ref[...] = jnp.zeros(carry_ref.shape, F32)

    for i, pr in enumerate(page_refs):
        x = pr[0]
        if tail_group is not None and i == n_slots - 1:
            x = jnp.where(k == tail_group, tail_ref[0], x)
        rows = slice(i * PAGE, (i + 1) * PAGE)
        if x.ndim == 2:
            for vcol in range(4):
                xs_ref[vcol, rows, :] = x[:, vcol * LANES:(vcol + 1) * LANES]
        else:
            for c in range(2):
                xc = x[c].reshape(KV_WIDTH, PAGE).T
                for hp in range(2):
                    xs_ref[2 * c + hp, rows, :] = xc[:, hp * LANES:(hp + 1) * LANES]

    def strides(vcol):
        return jnp.concatenate(
            [xs_ref[vcol, pl.ds(s, m, stride=CMP_STRIDE), :] for s in range(CMP_STRIDE)], axis=1)

    first_row = lax.broadcasted_iota(jnp.int32, (m, 1), 0) == 0
    out_cols = []
    for c in range(2):
        w1 = w1_ref[c]
        lhs = jnp.concatenate([strides(2 * c + hp) for hp in range(2)] + [pe_ref[c, 0], pe_ref[c, 1]],
                              axis=0).astype(BF)
        a = _dot(lhs, w1)
        hpe = a[2 * m:2 * m + 1, :hid2] + a[2 * m + 8:2 * m + 9, hid2:]
        acc = jnp.zeros((m, KV_WIDTH), F32)
        for hp in range(2):
            a0 = a[hp * m:(hp + 1) * m, :hid2]
            a1 = a[hp * m:(hp + 1) * m, hid2:]
            prev = jnp.where(first_row, carry_ref[c, hp, 0:1, :], pltpu.roll(a0, 1, axis=0))
            carry_ref[c, hp, 0:1, :] = a0[m - 1:m]
            acc = acc + _dot(jax.nn.gelu(prev + a1 + hpe).astype(BF), w2_ref[c, hp])
        out_cols.append(acc)
    o_ref[0] = jnp.concatenate(out_cols, axis=1)


def _compress(pool, table, tail, w1, pe, w2):
    b, n_real = table.shape
    n_pages = n_real + (0 if tail is None else 1)
    n_slots = max(p for p in range(1, 17) if n_pages % p == 0)
    n_groups = n_pages // n_slots
    m = n_slots * (PAGE // CMP_STRIDE)
    page_shape = pool.shape[1:]
    zeros = (0,) * len(page_shape)

    def page_spec(slot):
        return pl.BlockSpec(
            (1,) + page_shape,
            lambda i, k, pt: (pt[i * n_real + jnp.minimum(k * n_slots + slot, n_real - 1)],) + zeros)

    in_specs = [page_spec(slot) for slot in range(n_slots)]
    args = [pool] * n_slots
    if tail is not None:
        in_specs.append(pl.BlockSpec((1,) + page_shape, lambda i, k, pt: (i,) + zeros))
        args.append(tail)
    in_specs += [pl.BlockSpec(x.shape, lambda i, k, pt, nd=x.ndim: (0,) * nd) for x in (w1, pe, w2)]
    return pl.pallas_call(
        functools.partial(_compress_kernel, n_slots=n_slots,
                          tail_group=None if tail is None else n_groups - 1),
        grid_spec=pltpu.PrefetchScalarGridSpec(
            num_scalar_prefetch=1, grid=(b, n_groups), in_specs=in_specs,
            out_specs=pl.BlockSpec((1, m, ROW_WIDTH), lambda i, k, pt: (i, k, 0)),
            scratch_shapes=[pltpu.VMEM((4, n_slots * PAGE, LANES), F32),
                            pltpu.VMEM((2, 2, 8, 2 * CMP_HIDDEN), F32)]),
        out_shape=jax.ShapeDtypeStruct((b, n_groups * m, ROW_WIDTH), F32),
        compiler_params=_cparams(("parallel", "arbitrary"), VMEM_LIMIT),
        name="compress",
    )(table.reshape(-1), *args, w1, pe, w2)


def _nsa_kernel(q_ref, gt_ref, kc_ref, vct_ref, ks_ref, vst_ref, kw_ref, vwt_ref, o_ref, psum_ref, bias_ref,
                sbuf0_ref, sbuf1_ref, *, t_base, wpos0, n_slc):
    g = pl.program_id(1)
    i = pl.program_id(2)
    nq = q_ref.shape[2]
    nl = GQA * nq
    ncp = kc_ref.shape[2]
    gk = SEL_GROUP * PAGE
    blocks_per_group = gk // SEL_BLOCK
    nsp = bias_ref.shape[0]
    nkt_s = vst_ref.shape[2]
    q = q_ref[0].reshape(nl, HEAD_DIM)
    lane = lax.broadcasted_iota(jnp.int32, (1, nl), 1)
    q0 = t_base + i * nq
    t = q0 + (lane & (nq - 1))

    s = _dot_nt(kc_ref[0, 0], q)
    n_idx = lax.broadcasted_iota(jnp.int32, (ncp, 1), 0)
    cmask = n_idx * CMP_STRIDE + (CMP_BLOCK - 1) <= t
    s = jnp.where(cmask, s, NEG)
    p = jnp.where(cmask, jnp.exp(s - jnp.max(s, axis=0, keepdims=True)), 0.0)
    l = jnp.sum(p, axis=0, keepdims=True)
    p = p * jnp.where(l > 0.0, 1.0 / l, 0.0)
    o_cmp = jnp.zeros((HEAD_DIM, nl), F32)
    for kt in range(ncp // PAGE):
        o_cmp = o_cmp + _dot(vct_ref[0, 0, kt], p[kt * PAGE:(kt + 1) * PAGE].astype(BF))

    rel = (q0 - wpos0) // PAGE
    n_wt = WINDOW // PAGE + 1
    tile_row = lax.broadcasted_iota(jnp.int32, (PAGE, 1), 0)
    parts = []
    for u in range(n_wt):
        kt = rel - (n_wt - 1) + u
        ktc = jnp.maximum(kt, 0)
        kb = kw_ref[0, 0, pl.ds(pl.multiple_of(ktc * PAGE, PAGE), PAGE), :]
        krel = kt * PAGE + tile_row
        kpos = jnp.where(krel >= 0, wpos0 + krel, -(1 << 30))
        d = lax.bitcast_convert_type(t - kpos, jnp.uint32)
        parts.append(jnp.where(d < WINDOW, _dot_nt(kb, q), NEG))
    sw = jnp.concatenate(parts, axis=0)
    pw = jnp.exp(sw - jnp.max(sw, axis=0, keepdims=True))
    l_w = jnp.sum(pw, axis=0, keepdims=True)
    acc_w = jnp.zeros((HEAD_DIM, nl), F32)
    for u in range(n_wt):
        ktc = jnp.maximum(rel - (n_wt - 1) + u, 0)
        acc_w = acc_w + _dot(vwt_ref[0, 0, ktc], pw[u * PAGE:(u + 1) * PAGE].astype(BF))
    o_win = acc_w * (1.0 / l_w)

    psum = p[:, :nq]
    for r in range(1, GQA):
        psum = psum + p[:, r * nq:(r + 1) * nq]
    front = 8
    psum_ref[:front, :] = jnp.zeros((front, nq), F32)
    psum_ref[front:front + ncp, :] = psum
    psum_ref[front + ncp:, :] = jnp.zeros((psum_ref.shape[0] - front - ncp, nq), F32)
    ratio = SEL_BLOCK // CMP_STRIDE
    imp = psum_ref[pl.ds(front - 1, nsp, stride=ratio), :]
    for d in range(1, ratio + 1):
        imp = imp + psum_ref[pl.ds(front - 1 + d, nsp, stride=ratio), :]
    tq = t[:, :nq]
    j = lax.broadcasted_iota(jnp.int32, (nsp, 1), 0)
    cur = tq >> int(math.log2(SEL_BLOCK))
    forced = (j == 0) | (j == cur) | (j == cur - 1)
    score = jnp.where(j * SEL_BLOCK <= tq, imp + FORCE_BONUS * forced.astype(F32), NEG)
    sub = 8
    chunks = [score[c * sub:(c + 1) * sub] for c in range(nsp // sub)]
    jc = j[:sub]
    cnts = [jnp.zeros((sub, nq), F32) for _ in chunks]
    for ii in range(n_slc):
        ri = score[ii:ii + 1, :]
        for c, sc_c in enumerate(chunks):
            if c * sub > ii:
                one = jnp.where(ri >= sc_c, 1.0, 0.0)
            elif c * sub + sub - 1 <= ii:
                one = jnp.where(ri > sc_c, 1.0, 0.0)
            else:
                one = jnp.where(jc + c * sub > ii, jnp.where(ri >= sc_c, 1.0, 0.0), jnp.where(ri > sc_c, 1.0, 0.0))
            cnts[c] = cnts[c] + one
    cnt = jnp.concatenate(cnts, axis=0)
    bias_ref[...] = jnp.where(cnt < float(SEL_TOPK), jnp.where(score > NEG / 2, 0.0, NEG), NEG)

    key_row = lax.broadcasted_iota(jnp.int32, (gk, 1), 0)

    def fill(buf, gi):
        kb = ks_ref[0, 0, pl.ds(pl.multiple_of(gi * gk, gk), gk), :]
        rows = bias_ref[pl.ds(pl.multiple_of(gi * blocks_per_group, blocks_per_group), blocks_per_group), :]
        rows = jnp.concatenate([rows] * GQA, axis=1)
        bias = jnp.concatenate([jnp.broadcast_to(rows[b:b + 1], (SEL_BLOCK, nl)) for b in range(blocks_per_group)],
                               axis=0)
        buf[...] = _dot_nt(kb, q) + bias

    def consume(buf, gi, carry, causal):
        m_old, l_old, acc = carry
        sc = buf[...]
        if causal:
            sc = jnp.where(gi * gk + key_row <= t, sc, NEG)
        m_new = jnp.maximum(m_old, jnp.max(sc, axis=0, keepdims=True))
        alpha = jnp.exp(m_old - m_new)
        pe = jnp.exp(sc - m_new)
        acc = acc * alpha
        for u in range(SEL_GROUP):
            acc = acc + _dot(vst_ref[0, 0, gi * SEL_GROUP + u], pe[u * PAGE:(u + 1) * PAGE].astype(BF))
        return m_new, l_old * alpha + jnp.sum(pe, axis=0, keepdims=True), acc

    init = (jnp.full((1, nl), NEG, F32), jnp.zeros((1, nl), F32), jnp.zeros((HEAD_DIM, nl), F32))

    n_groups = jnp.minimum(nkt_s - 1, (q0 + nq - 1) // PAGE) // SEL_GROUP + 1
    fill(sbuf0_ref, 0)

    def pair_body(kk, carry):
        gi = 2 * kk
        fill(sbuf1_ref, gi + 1)
        carry = consume(sbuf0_ref, gi, carry, False)
        fill(sbuf0_ref, gi + 2)
        return consume(sbuf1_ref, gi + 1, carry, False)

    n_pairs = (n_groups - 1) // 2
    carry = lax.fori_loop(0, n_pairs, pair_body, init)
    gi = 2 * n_pairs

    def odd_tail(carry):
        fill(sbuf1_ref, gi + 1)
        carry = consume(sbuf0_ref, gi, carry, False)
        return consume(sbuf1_ref, gi + 1, carry, True)

    def even_tail(carry):
        return consume(sbuf0_ref, gi, carry, True)

    _, l_s, acc_s = lax.cond(n_groups - 1 - gi == 1, odd_tail, even_tail, carry)

    o_sel = acc_s * jnp.where(l_s > 0.0, 1.0 / l_s, 0.0)
    for r in range(GQA):
        sl = slice(r * nq, (r + 1) * nq)
        grow = (GQA * g + r) * 3
        o = (gt_ref[0, pl.ds(grow, 1), :] * o_cmp[:, sl]
             + gt_ref[0, pl.ds(grow + 1, 1), :] * o_sel[:, sl]
             + gt_ref[0, pl.ds(grow + 2, 1), :] * o_win[:, sl])
        o_ref[0, r * HEAD_DIM:(r + 1) * HEAD_DIM, :] = o


def _nsa(q, gt, kc, vct, ks, vst, kw, vwt, *, t_base, wpos0, n_slc):
    b, _, sq, _ = q.shape
    nq = LANES
    assert t_base % PAGE == 0 and (t_base - wpos0) % PAGE == 0 and t_base >= wpos0
    assert vst.shape[2] % SEL_GROUP == 0
    nsp = max(-(-n_slc // 8) * 8, vst.shape[2] * (PAGE // SEL_BLOCK))
    ncp = kc.shape[2]
    psum_rows = 8 + max(ncp + 8, (SEL_BLOCK // CMP_STRIDE) * nsp + 8)
    psum_rows = -(-psum_rows // 8) * 8

    def kv_specs(k, vt):
        return [pl.BlockSpec((1, 1) + k.shape[2:], lambda bi, gi, ii: (bi, gi, 0, 0)),
                pl.BlockSpec((1, 1) + vt.shape[2:], lambda bi, gi, ii: (bi, gi, 0, 0, 0))]

    return pl.pallas_call(
        functools.partial(_nsa_kernel, t_base=t_base, wpos0=wpos0, n_slc=n_slc),
        grid=(b, KV_HEADS, sq // nq),
        in_specs=[pl.BlockSpec((1, GQA, nq, HEAD_DIM), lambda bi, gi, ii: (bi, gi, ii, 0)),
                  pl.BlockSpec((1, GATE_PAD, nq), lambda bi, gi, ii: (bi, 0, ii))]
        + kv_specs(kc, vct) + kv_specs(ks, vst) + kv_specs(kw, vwt),
        out_specs=pl.BlockSpec((1, GQA * HEAD_DIM, nq), lambda bi, gi, ii: (bi, gi, ii)),
        out_shape=jax.ShapeDtypeStruct((b, NSA_WIDTH, sq), F32),
        scratch_shapes=[pltpu.VMEM((psum_rows, nq), F32), pltpu.VMEM((nsp, nq), F32),
                        pltpu.VMEM((SEL_GROUP * PAGE, GQA * nq), F32), pltpu.VMEM((SEL_GROUP * PAGE, GQA * nq), F32)],
        compiler_params=_cparams(("parallel", "parallel", "parallel"), VMEM_LIMIT),
        name="nsa_attend",
    )(q, gt, kc, vct, ks, vst, kw, vwt)


SAMPLE_ROWS = 64
ROWS_PER_GROUP = SAMPLE_ROWS // KV_HEADS


def _nsa_sample_kernel(pt_ref, *refs, n_slots, t_base, wpos0, n_slc, t_q):
    page_refs = refs[:n_slots]
    (tail_ref, q_ref, gt_ref, kct_ref, vct_ref, kwt_ref, vwt_ref, o_ref,
     ind_ref, expand_ref, m_ref, l_ref, acc_ref, oc_ref, ow_ref) = refs[n_slots:]
    k = pl.program_id(1)
    n_steps = pl.num_programs(1)
    q = q_ref[0]
    row = lax.broadcasted_iota(jnp.int32, (SAMPLE_ROWS, 1), 0)
    t = t_base + (row & (t_q - 1))

    @pl.when(k == 0)
    def _():
        ncp = kct_ref.shape[2]
        s = _dot(q, kct_ref[0])
        n_idx = lax.broadcasted_iota(jnp.int32, (1, ncp), 1)
        cmask = n_idx * CMP_STRIDE + (CMP_BLOCK - 1) <= t
        s = jnp.where(cmask, s, NEG)
        p = jnp.where(cmask, jnp.exp(s - jnp.max(s, axis=1, keepdims=True)), 0.0)
        l = jnp.sum(p, axis=1, keepdims=True)
        p = p * jnp.where(l > 0.0, 1.0 / l, 0.0)
        oc_ref[...] = _dot_nt(p.astype(BF), vct_ref[0])

        parts = []
        for g in range(KV_HEADS):
            base = g * ROWS_PER_GROUP
            acc = p[base:base + t_q]
            for r in range(1, GQA):
                acc = acc + p[base + r * t_q:base + (r + 1) * t_q]
            parts.append(acc)
        psum = jnp.concatenate(parts, axis=0)
        nsl = -(-n_slc // LANES) * LANES
        n_col = lax.broadcasted_iota(jnp.int32, (ncp, 1), 0)
        j_row = lax.broadcasted_iota(jnp.int32, (1, nsl), 1)
        ratio = SEL_BLOCK // CMP_STRIDE
        overlap = jnp.where((n_col >= ratio * j_row - 1) & (n_col <= ratio * j_row + ratio - 1), 1.0, 0.0).astype(BF)
        hi = psum.astype(BF)
        mid = (psum - hi.astype(F32)).astype(BF)
        lo = (psum - hi.astype(F32) - mid.astype(F32)).astype(BF)
        imp = _dot(hi, overlap) + _dot(mid, overlap) + _dot(lo, overlap)
        nsel_rows = KV_HEADS * t_q
        tq = t_base + (lax.broadcasted_iota(jnp.int32, (nsel_rows, 1), 0) & (t_q - 1))
        cur = tq >> int(math.log2(SEL_BLOCK))
        forced = (j_row == 0) | (j_row == cur) | (j_row == cur - 1)
        score = jnp.where(j_row * SEL_BLOCK <= tq, imp + FORCE_BONUS * forced.astype(F32), NEG)
        score_t = jnp.concatenate([score, jnp.full((LANES - nsel_rows, nsl), NEG, F32)], axis=0).T
        sub = 8
        nsp = -(-n_slc // sub) * sub
        jc = lax.broadcasted_iota(jnp.int32, (sub, 1), 0)
        chunks = [score_t[c * sub:(c + 1) * sub] for c in range(nsp // sub)]
        cnts = [jnp.zeros((sub, LANES), F32) for _ in chunks]
        for ii in range(n_slc):
            ri = score_t[ii:ii + 1, :]
            for c, sc_c in enumerate(chunks):
                if c * sub > ii:
                    one = jnp.where(ri >= sc_c, 1.0, 0.0)
                elif c * sub + sub - 1 <= ii:
                    one = jnp.where(ri > sc_c, 1.0, 0.0)
                else:
                    one = jnp.where(jc + c * sub > ii, jnp.where(ri >= sc_c, 1.0, 0.0), jnp.where(ri > sc_c, 1.0, 0.0))
                cnts[c] = cnts[c] + one
        chosen = [jnp.where(cn < float(SEL_TOPK), jnp.where(sc_c > NEG / 2, 1.0, 0.0), 0.0)
                  for cn, sc_c in zip(cnts, chunks)]
        chosen_t = jnp.concatenate(chosen + [jnp.zeros((nsl - nsp, LANES), F32)], axis=0)
        chosen_rows = chosen_t.T.astype(BF)
        col = lax.broadcasted_iota(jnp.int32, (1, LANES), 1)
        src = (row >> int(math.log2(ROWS_PER_GROUP))) * t_q + (row & (t_q - 1))
        expand = jnp.where(col == src, 1.0, 0.0).astype(BF)
        ind = _dot(expand, chosen_rows).astype(BF)
        blocks_per_step = n_slots * (PAGE // SEL_BLOCK)
        j_all = lax.broadcasted_iota(jnp.int32, (nsl, 1), 0)
        for st in range(ind_ref.shape[0]):
            pick = jnp.where((j_all == st * blocks_per_step + col) & (col < blocks_per_step), 1.0, 0.0).astype(BF)
            ind_ref[st] = _dot(ind, pick).astype(BF)
        key_lane = lax.broadcasted_iota(jnp.int32, (1, n_slots * PAGE), 1)
        local_block = lax.broadcasted_iota(jnp.int32, (LANES, 1), 0)
        expand_ref[...] = jnp.where(local_block == (key_lane >> int(math.log2(SEL_BLOCK))), 1.0, 0.0).astype(BF)

        nw = kwt_ref.shape[2]
        sw = _dot(q, kwt_ref[0])
        kpos = wpos0 + lax.broadcasted_iota(jnp.int32, (1, nw), 1)
        d = lax.bitcast_convert_type(t - kpos, jnp.uint32)
        sw = jnp.where(d < WINDOW, sw, NEG)
        pw = jnp.exp(sw - jnp.max(sw, axis=1, keepdims=True))
        lw = jnp.sum(pw, axis=1, keepdims=True)
        ow_ref[...] = _dot_nt(pw.astype(BF), vwt_ref[0]) * (1.0 / lw)

        m_ref[...] = jnp.full(m_ref.shape, NEG, F32)
        l_ref[...] = jnp.zeros(l_ref.shape, F32)
        acc_ref[...] = jnp.zeros(acc_ref.shape, F32)

    vts = []
    scs = []
    for i, pr in enumerate(page_refs):
        x = pr[0]
        if i == n_slots - 1:
            x = jnp.where(k == n_steps - 1, tail_ref[0], x)
        scs.append(_dot(q, x[0].reshape(KV_WIDTH, PAGE).astype(BF)))
        vts.append(x[1].reshape(KV_WIDTH, PAGE).astype(BF))
    sc = jnp.concatenate(scs, axis=1)
    nk = n_slots * PAGE
    kidx = k * nk + lax.broadcasted_iota(jnp.int32, (1, nk), 1)
    picked = _dot(ind_ref[k], expand_ref[...])
    sc = jnp.where(picked > 0.5, jnp.where(kidx <= t, sc, NEG), NEG)
    m_old = m_ref[:, :1]
    m_new = jnp.maximum(m_old, jnp.max(sc, axis=1, keepdims=True))
    alpha = jnp.exp(m_old - m_new)
    pe = jnp.exp(sc - m_new)
    acc = acc_ref[...] * alpha
    for i in range(n_slots):
        acc = acc + _dot_nt(pe[:, i * PAGE:(i + 1) * PAGE].astype(BF), vts[i])
    l_new = l_ref[:, :1] * alpha + jnp.sum(pe, axis=1, keepdims=True)
    m_ref[...] = jnp.broadcast_to(m_new, m_ref.shape)
    l_ref[...] = jnp.broadcast_to(l_new, l_ref.shape)
    acc_ref[...] = acc

    @pl.when(k == n_steps - 1)
    def _():
        o_sel = acc * jnp.where(l_new > 0.0, 1.0 / l_new, 0.0)
        gts = gt_ref[0]
        o_ref[0] = gts[:, 0:1] * oc_ref[...] + gts[:, 1:2] * o_sel + gts[:, 2:3] * ow_ref[...]


def _nsa_sample(q_bd, gates, pool_t, table, tail_t, kct, vct, kwt, vwt, *, t_base, wpos0, n_slc, t_q):
    b, n_real = table.shape
    n_pages = n_real + 1
    n_slots = max(p for p in range(1, 17) if n_pages % p == 0)
    n_steps = n_pages // n_slots
    nsl = -(-n_slc // LANES) * LANES
    page_shape = pool_t.shape[1:]
    zeros = (0,) * len(page_shape)

    def page_spec(slot):
        return pl.BlockSpec(
            (1,) + page_shape,
            lambda i, k, pt: (pt[i * n_real + jnp.minimum(k * n_slots + slot, n_real - 1)],) + zeros)

    def per_batch(x):
        return pl.BlockSpec((1,) + x.shape[1:], lambda i, k, pt: (i,) + (0,) * (x.ndim - 1))

    acc_shape = pltpu.VMEM((SAMPLE_ROWS, KV_WIDTH), F32)
    stat_shape = pltpu.VMEM((SAMPLE_ROWS, LANES), F32)
    return pl.pallas_call(
        functools.partial(_nsa_sample_kernel, n_slots=n_slots, t_base=t_base, wpos0=wpos0, n_slc=n_slc, t_q=t_q),
        grid_spec=pltpu.PrefetchScalarGridSpec(
            num_scalar_prefetch=1, grid=(b, n_steps),
            in_specs=[page_spec(slot) for slot in range(n_slots)]
            + [per_batch(x) for x in (tail_t, q_bd, gates, kct, vct, kwt, vwt)],
            out_specs=pl.BlockSpec((1, SAMPLE_ROWS, KV_WIDTH), lambda i, k, pt: (i, 0, 0)),
            scratch_shapes=[pltpu.VMEM((n_steps, SAMPLE_ROWS, LANES), BF), pltpu.VMEM((LANES, n_slots * PAGE), BF),
                            stat_shape, stat_shape, acc_shape, acc_shape, acc_shape]),
        out_shape=jax.ShapeDtypeStruct((b, SAMPLE_ROWS, KV_WIDTH), F32),
        compiler_params=_cparams(("parallel", "arbitrary"), VMEM_LIMIT),
        name="nsa_sample",
    )(table.reshape(-1), *([pool_t] * n_slots), tail_t, q_bd, gates, kct, vct, kwt, vwt)


def _rope_tables(pos):
    half = ROPE_DIM // 2
    inv_freq = ROPE_THETA ** (-jnp.arange(half, dtype=F32) / half)
    ang = pos.astype(F32)[:, None] * inv_freq[None, :]
    cos, sin = jnp.cos(ang), jnp.sin(ang)
    n = pos.shape[0]
    rest = HEAD_DIM - ROPE_DIM
    c = jnp.concatenate([cos, cos, jnp.ones((n, rest), F32)], axis=1)
    a = jnp.concatenate([-sin, jnp.zeros((n, half + rest), F32)], axis=1)
    b = jnp.concatenate([jnp.zeros((n, half), F32), sin, jnp.zeros((n, rest), F32)], axis=1)
    reps = LANES // HEAD_DIM
    return tuple(jnp.tile(x, (1, reps)) for x in (c, a, b))


def _compress_weights(cmp_pe, w_phi1, w_phi2):
    ratio = CMP_BLOCK // CMP_STRIDE
    w1 = w_phi1.reshape(ratio, CMP_STRIDE, 2, HEAD_DIM, CMP_HIDDEN)
    eye2 = jnp.eye(2, dtype=w_phi1.dtype)
    big = jnp.einsum('jscde,hg->cjshdge', w1, eye2)
    big = big.reshape(2, ratio, CMP_STRIDE * 2 * HEAD_DIM, 2 * CMP_HIDDEN).astype(BF)
    big = jnp.concatenate([big[:, jj] for jj in range(ratio)], axis=-1)
    pe = cmp_pe.reshape(ratio, CMP_STRIDE, 2, HEAD_DIM)
    pe = jnp.broadcast_to(pe.transpose(2, 0, 1, 3)[:, :, :, None, :], (2, ratio, CMP_STRIDE, 2, HEAD_DIM))
    pe = pe.reshape(2, ratio, 1, CMP_STRIDE * 2 * HEAD_DIM)
    pe = jnp.broadcast_to(pe, (2, ratio, 8, pe.shape[-1]))
    eye4 = jnp.eye(KV_HEADS, dtype=w_phi2.dtype).reshape(2, 2, KV_HEADS)
    w2 = jnp.einsum('ced,pgh->cpgehd', w_phi2, eye4)
    w2 = w2.reshape(2, 2, 2 * CMP_HIDDEN, KV_WIDTH).astype(BF)
    return big, pe, w2


def _b_in_weight(w):
    n_gate = w.shape[1] - NSA_WIDTH - MEM_WIDTH
    gates = jnp.pad(w[:, NSA_WIDTH:NSA_WIDTH + n_gate], ((0, 0), (0, GATE_PAD - n_gate)))
    return jnp.concatenate([w[:, :NSA_WIDTH], w[:, NSA_WIDTH + n_gate:], gates], axis=1).astype(BF)


def _pad_rows(x, n):
    return jnp.pad(x, ((0, 0), (0, n - x.shape[1]), (0, 0)))


def _tail_page_t(new_rows):
    b = new_rows.shape[0]
    return _pad_rows(new_rows, PAGE).reshape(b, PAGE, 2, KV_HEADS, HEAD_DIM).transpose(0, 2, 3, 4, 1)


def _rows_t(rows):
    b, n, _ = rows.shape
    kv = rows.reshape(b, n, 2, KV_WIDTH).transpose(0, 2, 3, 1).astype(BF)
    return kv[:, 0], kv[:, 1]


def _row_tiles(s):
    return {"proj": min(s, 256), "gate": min(s, 512), "mlp": min(s, 512), "mem": min(s, 512)}


SAMPLE_MEM_SETS = 8


def _mem_attn_rows(qm, mem_kv, rows_per_set):
    return _mem_attn(qm, mem_kv, tq=SAMPLE_MEM_SETS * rows_per_set, sets_per_step=SAMPLE_MEM_SETS)


def _trunk_a(x, mem_kv, p, l, *, chunk, tiles, emit_v, per_row_sets):
    b, s, _ = x.shape
    if chunk == PAGE:
        ws, bs = p["w_spatial"][l], jnp.repeat(p["b_spatial"][l].T, A_GROUP_DIM, axis=1)
    else:
        reps = PAGE // chunk
        ws = jnp.tile(p["w_spatial"][l][:, :chunk, :chunk], (1, reps, reps))
        bs = jnp.tile(jnp.repeat(p["b_spatial"][l].T[:chunk], A_GROUP_DIM, axis=1), (reps, 1))
    res = _a_in(x, p["w_in_a"][l], p["ln_v_g"][l][None], p["ln_v_b"][l][None], ws, bs,
                chunk=chunk, tm=tiles["gate"], emit_v=emit_v)
    mixed, qm = res[0], res[1]
    om = _mem_attn_rows(qm, mem_kv, chunk) if per_row_sets else _mem_attn(qm, mem_kv, tq=tiles["mem"])
    x = _mlp_block(x, mixed, om, p["w_out_a"][l], p, l, tm=tiles["mlp"], a_transposed=False)
    return x, (res[2] if emit_v else None)


def _mlp_block(x, a, om, wo, p, l, *, tm, a_transposed):
    return _out_mlp(x, a, om, wo, p["ln1_g"][l][None], p["ln1_b"][l][None], p["w_up"][l], p["w_down"][l],
                    p["ln2_g"][l][None], p["ln2_b"][l][None], tm=tm, a_transposed=a_transposed)


def kernel(x_prompt, x_sample, cache_cmp_kv, cache_slc_kv, cache_win_kv, cache_mem_kv, page_table, mem_prompt,
           w_in_a, ln_v_g, ln_v_b, w_spatial, b_spatial, w_out_a, w_in_b, w_out_b, w_kv_shared, cmp_pe,
           w_phi1, w_phi2, w_mem_kv, ln1_g, ln1_b, ln2_g, ln2_b, w_up, w_down):
    bp, s, d = x_prompt.shape
    bs, t, _ = x_sample.shape
    n_pages = page_table.shape[1]
    past = n_pages * PAGE
    n_buf = cache_win_kv.shape[1]
    assert bs * t == PAGE and PAGE % t == 0 and s % PAGE == 0 and n_buf % PAGE == 0

    p = {"w_in_a": w_in_a.astype(BF), "ln_v_g": ln_v_g, "ln_v_b": ln_v_b, "w_spatial": w_spatial,
         "b_spatial": b_spatial, "w_out_a": w_out_a.astype(BF), "w_out_b": w_out_b.astype(BF),
         "ln1_g": ln1_g, "ln1_b": ln1_b, "ln2_g": ln2_g, "ln2_b": ln2_b,
         "w_up": w_up.astype(BF), "w_down": w_down.astype(BF)}
    w_in_b2 = [_b_in_weight(w_in_b[i]) for i in range(w_in_b.shape[0])]
    w_kv = w_kv_shared.astype(BF)
    cw1, cpe, cw2 = _compress_weights(cmp_pe, w_phi1, w_phi2)

    tl = _row_tiles(s)
    mem_kv_prompt = _mem_kv_proj(mem_prompt, w_mem_kv)
    x = x_prompt
    for l in range(N_A):
        x, _ = _trunk_a(x, mem_kv_prompt[l], p, l, chunk=PAGE, tiles=tl, emit_v=False, per_row_sets=False)
    tabs_p = _rope_tables(jnp.arange(s, dtype=jnp.int32))
    cmp_rows, slc_rows, win_rows, ks, vst, kw, vwt = _kv_shared(x, w_kv, tabs_p, tm=tl["proj"], attn_layouts=True)
    n_pg = s // PAGE
    ident = jnp.arange(bp * n_pg, dtype=jnp.int32).reshape(bp, n_pg)
    cmp_blk = _compress(cmp_rows.reshape(bp * n_pg, PAGE, ROW_WIDTH), ident, None, cw1, cpe, cw2)
    n_cmp = s // CMP_STRIDE - 1
    ncp = -(-n_cmp // PAGE) * PAGE
    kc, vct = _rows_layout(_pad_rows(cmp_blk[:, 1:1 + n_cmp], ncp))
    for i in range(DEPTH - N_A):
        l = N_A + i
        q, qm, gt = _b_in(x, w_in_b2[i], tabs_p, tm=tl["proj"])
        om = _mem_attn(qm, mem_kv_prompt[l], tq=tl["mem"])
        ot = _nsa(q, gt, kc, vct, ks, vst, kw, vwt, t_base=0, wpos0=0, n_slc=s // SEL_BLOCK)
        x = _mlp_block(x, ot, om, p["w_out_b"][i], p, l, tm=tl["mlp"], a_transposed=True)
    y_prompt = x
    shape5 = (2, KV_HEADS, HEAD_DIM)
    cmp_kv_prompt = cmp_rows.reshape(bp, s, *shape5)
    slc_kv_prompt = slc_rows.reshape(bp, s, *shape5)
    win_kv_prompt = win_rows[:, s - min(WINDOW, s):].reshape(bp, min(WINDOW, s), *shape5)
    mem_kv_prompt_out = mem_kv_prompt.reshape(DEPTH, bp, MEM_TOKENS, 2, MEM_HEADS, HEAD_DIM)

    rows = bs * t
    x = x_sample.reshape(1, rows, d)
    mem_s = cache_mem_kv.reshape(DEPTH, bs, MEM_TOKENS, 2 * MEM_WIDTH)
    v_rows = []
    for l in range(N_A):
        x, v = _trunk_a(x, mem_s[l], p, l, chunk=t, tiles=_row_tiles(rows), emit_v=True, per_row_sets=True)
        v_rows.append(v.reshape(bs, t, A_WIDTH))
    pos_s = past + (jnp.arange(rows, dtype=jnp.int32) % t)
    tabs_s = _rope_tables(pos_s)
    new_cmp, new_slc, new_win = (r.reshape(bs, t, ROW_WIDTH)
                                 for r in _kv_shared(x, w_kv, tabs_s, tm=rows, attn_layouts=False))
    cmp_pool_t = cache_cmp_kv.transpose(0, 2, 3, 4, 1)
    slc_pool_t = cache_slc_kv.transpose(0, 2, 3, 4, 1)
    cmp_blk = _compress(cmp_pool_t, page_table, _tail_page_t(new_cmp), cw1, cpe, cw2)
    n_cmp = -(-(past + t) // CMP_STRIDE) - 1
    ncp = -(-n_cmp // PAGE) * PAGE
    kct, vct = _rows_t(_pad_rows(cmp_blk[:, 1:1 + n_cmp], ncp))
    win_all = jnp.concatenate([cache_win_kv.reshape(bs, n_buf, ROW_WIDTH), new_win], axis=1)
    kwt, vwt = _rows_t(_pad_rows(win_all, n_buf + PAGE))
    slc_tail_t = _tail_page_t(new_slc)
    n_slc = -(-(past + t) // SEL_BLOCK)
    per_group = GQA * t
    eye = jnp.eye(KV_HEADS, dtype=BF)
    for i in range(DEPTH - N_A):
        l = N_A + i
        q, qm, gt = _b_in(x, w_in_b2[i], tabs_s, tm=rows)
        om = _mem_attn_rows(qm, mem_s[l], t)
        q = q.reshape(KV_HEADS, GQA, bs, t, HEAD_DIM).transpose(2, 0, 1, 3, 4).reshape(bs, KV_HEADS, per_group, HEAD_DIM)
        q = jnp.pad(q, ((0, 0), (0, 0), (0, ROWS_PER_GROUP - per_group), (0, 0)))
        q_bd = jnp.einsum('bgxd,gh->bgxhd', q, eye).reshape(bs, SAMPLE_ROWS, KV_WIDTH)
        gs = gt[0, :NSA_HEADS * 3].reshape(KV_HEADS, GQA, 3, bs, t).transpose(3, 0, 1, 4, 2)
        gs = gs.reshape(bs, KV_HEADS, per_group, 3)
        gs = jnp.pad(gs, ((0, 0), (0, 0), (0, ROWS_PER_GROUP - per_group), (0, LANES - 3)))
        o = _nsa_sample(q_bd, gs.reshape(bs, SAMPLE_ROWS, LANES), slc_pool_t, page_table, slc_tail_t,
                        kct, vct, kwt, vwt, t_base=past, wpos0=past - n_buf, n_slc=n_slc, t_q=t)
        o = o.reshape(bs, KV_HEADS, ROWS_PER_GROUP, KV_HEADS, HEAD_DIM)
        o = jnp.stack([o[:, g, :per_group, g] for g in range(KV_HEADS)], axis=1)
        o = o.reshape(bs, KV_HEADS, GQA, t, HEAD_DIM).transpose(0, 3, 1, 2, 4).reshape(1, rows, NSA_WIDTH)
        x = _mlp_block(x, o, om, p["w_out_b"][i], p, l, tm=rows, a_transposed=False)
    y_sample = x.reshape(bs, t, d)
    cmp_kv_sample = new_cmp.reshape(bs, t, *shape5)
    slc_kv_sample = new_slc.reshape(bs, t, *shape5)
    win_kv_sample = win_all[:, t:].reshape(bs, n_buf, *shape5)
    gmlp_v_sample = jnp.stack(v_rows)

    return (y_prompt, y_sample, cmp_kv_prompt, slc_kv_prompt, win_kv_prompt, mem_kv_prompt_out,
            cmp_kv_sample, slc_kv_sample, win_kv_sample, gmlp_v_sample)
```

```python
import functools
import math

import jax
import jax.numpy as jnp
from jax import lax
from jax.experimental import pallas as pl
from jax.experimental.pallas import tpu as pltpu

BF = jnp.bfloat16
F32 = jnp.float32

DEPTH = 4
N_A = 2
A_WIDTH = 768
A_GROUPS = 4
A_GROUP_DIM = 192
MEM_TOKENS = 256
MEM_HEADS = 4
MEM_WIDTH = 256
HEAD_DIM = 64
NSA_HEADS = 12
KV_HEADS = 4
GQA = 3
NSA_WIDTH = 768
KV_WIDTH = 256
ROW_WIDTH = 2 * KV_WIDTH
CMP_BLOCK = 32
CMP_STRIDE = 16
CMP_HIDDEN = 128
SEL_BLOCK = 64
SEL_TOPK = 16
WINDOW = 512
ROPE_THETA = 500000.0
ROPE_DIM = 16
D_FF_CHUNK = 1024
ALPHA = (2.0 * DEPTH) ** 0.25
LN_EPS = 1e-5
NEG = -1e30
FORCE_BONUS = 1e4
PAGE = 128
LANES = 128
GATE_PAD = 128
SEL_GROUP = 4
NSA_QUERY_TILE = 256
Q_SCALE = HEAD_DIM ** -0.5 * math.log2(math.e)
VMEM_LIMIT = 56 * 1024 * 1024


def _cparams(sem, vmem=None):
    return pltpu.CompilerParams(dimension_semantics=sem, vmem_limit_bytes=vmem)


def _const_spec(shape):
    return pl.BlockSpec(shape, lambda *_: (0,) * len(shape), pipeline_mode=pl.Buffered(1))


def _ln(x, g, b):
    mu = jnp.mean(x, axis=-1, keepdims=True)
    xc = x - mu
    var = jnp.mean(xc * xc, axis=-1, keepdims=True)
    return xc * lax.rsqrt(var + LN_EPS) * g + b


def _dot(a, b):
    return jnp.dot(a, b, preferred_element_type=F32)


def _dot_nt(a, b):
    return lax.dot_general(a, b, (((1,), (1,)), ((), ())), preferred_element_type=F32)


def _rope_cols(x, c, a, b):
    cols = []
    for j in range(x.shape[1] // LANES):
        xc = x[:, j * LANES:(j + 1) * LANES]
        up = pltpu.roll(xc, LANES - ROPE_DIM // 2, axis=1)
        dn = pltpu.roll(xc, ROPE_DIM // 2, axis=1)
        cols.append(xc * c + up * a + dn * b)
    return cols[0] if len(cols) == 1 else jnp.concatenate(cols, axis=1)


def _mem_kv_kernel(x_ref, w_ref, o_ref):
    o_ref[0] = _dot(x_ref[...].astype(BF), w_ref[0].astype(BF))


def _mem_kv_proj(mem_prompt, w_mem_kv):
    b, m, d = mem_prompt.shape
    x = mem_prompt.reshape(b * m, d)
    depth, _, n = w_mem_kv.shape
    tm = 512
    out = pl.pallas_call(
        _mem_kv_kernel,
        grid=(depth, (b * m) // tm),
        in_specs=[pl.BlockSpec((tm, d), lambda l, i: (i, 0)),
                  pl.BlockSpec((1, d, n), lambda l, i: (l, 0, 0))],
        out_specs=pl.BlockSpec((1, tm, n), lambda l, i: (l, i, 0)),
        out_shape=jax.ShapeDtypeStruct((depth, b * m, n), F32),
        compiler_params=_cparams(("parallel", "parallel")),
        name="mem_kv_proj",
    )(x, w_mem_kv)
    return out.reshape(depth, b, m, n)


def _a_in_kernel(x_ref, w_ref, g_ref, b_ref, ws_ref, bs_ref, mixed_ref, qm_ref, *v_refs, chunk_shift):
    x = x_ref[0].astype(BF)
    proj = _dot(x, w_ref[...])
    uv = jax.nn.gelu(proj[:, :2 * A_WIDTH])
    u = uv[:, :A_WIDTH]
    v = _ln(uv[:, A_WIDTH:], g_ref[...], b_ref[...])
    if v_refs:
        v_refs[0][0] = v
    qm_ref[0] = proj[:, 2 * A_WIDTH:]
    tm = x.shape[0]
    r = ws_ref.shape[1]
    row = lax.broadcasted_iota(jnp.int32, (r, r), 0)
    col = lax.broadcasted_iota(jnp.int32, (r, r), 1)
    allowed = (col <= row) & ((row >> chunk_shift) == (col >> chunk_shift))
    lane = lax.broadcasted_iota(jnp.int32, (1, A_WIDTH), 1)
    wgs = [jnp.where(allowed, ws_ref[g], 0.0).astype(BF) for g in range(A_GROUPS)]
    gmask = [(lane >= g * A_GROUP_DIM) & (lane < (g + 1) * A_GROUP_DIM) for g in range(A_GROUPS)]
    vb = v.astype(BF)
    for ci in range(tm // r):
        vc = vb[ci * r:(ci + 1) * r]
        acc = bs_ref[...]
        for g in range(A_GROUPS):
            acc = acc + jnp.where(gmask[g], _dot(wgs[g], vc), 0.0)
        mixed_ref[0, ci * r:(ci + 1) * r, :] = (u[ci * r:(ci + 1) * r] * acc).astype(BF)


def _a_in(x, w_in, ln_g, ln_b, ws, bs, *, chunk, tm, emit_v):
    b, s, d = x.shape
    n = w_in.shape[1]
    r = ws.shape[1]
    outs = [jax.ShapeDtypeStruct((b, s, A_WIDTH), BF), jax.ShapeDtypeStruct((b, s, MEM_WIDTH), F32)]
    ospecs = [pl.BlockSpec((1, tm, A_WIDTH), lambda i, j: (i, j, 0)),
              pl.BlockSpec((1, tm, MEM_WIDTH), lambda i, j: (i, j, 0))]
    if emit_v:
        outs.append(jax.ShapeDtypeStruct((b, s, A_WIDTH), F32))
        ospecs.append(pl.BlockSpec((1, tm, A_WIDTH), lambda i, j: (i, j, 0)))
    return pl.pallas_call(
        functools.partial(_a_in_kernel, chunk_shift=int(math.log2(chunk))),
        grid=(b, s // tm),
        in_specs=[pl.BlockSpec((1, tm, d), lambda i, j: (i, j, 0)),
                  _const_spec((d, n)), _const_spec((1, A_WIDTH)), _const_spec((1, A_WIDTH)),
                  _const_spec((A_GROUPS, r, r)), _const_spec((r, A_WIDTH))],
        out_specs=ospecs,
        out_shape=outs,
        compiler_params=_cparams(("parallel", "parallel"), VMEM_LIMIT),
        name="a_in_gate",
    )(x, w_in, ln_g, ln_b, ws, bs)


def _mem_attn_kernel(q_ref, kv_ref, o_ref, *, rows_per_set):
    q = q_ref[0]
    nb = kv_ref.shape[0]
    kv = kv_ref[...].reshape(nb * MEM_TOKENS, 2 * MEM_WIDTH)
    k = kv[:, :MEM_WIDTH].astype(BF)
    v = kv[:, MEM_WIDTH:]
    lane = lax.broadcasted_iota(jnp.int32, (1, MEM_WIDTH), 1)
    if nb > 1:
        row_set = lax.broadcasted_iota(jnp.int32, (q.shape[0], 1), 0) >> int(math.log2(rows_per_set))
        col_set = lax.broadcasted_iota(jnp.int32, (1, nb * MEM_TOKENS), 1) >> int(math.log2(MEM_TOKENS))
        own = row_set == col_set
    out = jnp.zeros(q.shape, F32)
    for h in range(MEM_HEADS):
        hm = (lane >= h * HEAD_DIM) & (lane < (h + 1) * HEAD_DIM)
        s = _dot_nt(jnp.where(hm, q, 0.0).astype(BF), k) * (HEAD_DIM ** -0.5)
        if nb > 1:
            s = jnp.where(own, s, NEG)
        m = jnp.max(s, axis=-1, keepdims=True)
        p = jnp.exp(s - m)
        p = p / jnp.sum(p, axis=-1, keepdims=True)
        out = out + _dot(p.astype(BF), jnp.where(hm, v, 0.0).astype(BF))
    o_ref[0] = out


def _mem_attn(q, mem_kv, *, tq, sets_per_step=1):
    b, s, w = q.shape
    if sets_per_step == 1:
        kv_spec = pl.BlockSpec((1, MEM_TOKENS, 2 * w), lambda i, j: (i, 0, 0))
    else:
        assert b == 1
        kv_spec = pl.BlockSpec((sets_per_step, MEM_TOKENS, 2 * w), lambda i, j: (j, 0, 0))
    return pl.pallas_call(
        functools.partial(_mem_attn_kernel, rows_per_set=tq // sets_per_step),
        grid=(b, s // tq),
        in_specs=[pl.BlockSpec((1, tq, w), lambda i, j: (i, j, 0)), kv_spec],
        out_specs=pl.BlockSpec((1, tq, w), lambda i, j: (i, j, 0)),
        out_shape=jax.ShapeDtypeStruct((b, s, w), F32),
        compiler_params=_cparams(("parallel", "parallel")),
        name="mem_attn",
    )(q, mem_kv)


def _out_mlp_kernel(x_ref, a_ref, om_ref, wo_ref, g1_ref, b1_ref, wu_ref, wd_ref, g2_ref, b2_ref, o_ref,
                    *, a_transposed):
    x = x_ref[0]
    a = a_ref[0]
    a = a.T.astype(BF) if a_transposed else a.astype(BF)
    na = a.shape[1]
    mix = _dot(a, wo_ref[:na, :]) + _dot(om_ref[0].astype(BF), wo_ref[na:, :])
    y = _ln(ALPHA * x + mix, g1_ref[...], b1_ref[...])
    yb = y.astype(BF)
    acc = jnp.zeros(x.shape, F32)
    for c in range(wu_ref.shape[1] // D_FF_CHUNK):
        h = _dot(yb, wu_ref[:, c * D_FF_CHUNK:(c + 1) * D_FF_CHUNK])
        h = jnp.square(jnp.maximum(h, 0.0)).astype(BF)
        acc = acc + _dot(h, wd_ref[c * D_FF_CHUNK:(c + 1) * D_FF_CHUNK, :])
    o_ref[0] = _ln(ALPHA * y + acc, g2_ref[...], b2_ref[...])


def _out_mlp(x, a, om, wo, g1, b1, wu, wd, g2, b2, *, tm, a_transposed):
    b, s, d = x.shape
    na = wo.shape[0] - om.shape[2]
    dff = wu.shape[1]
    if a_transposed:
        a_spec = pl.BlockSpec((1, na, tm), lambda i, j: (i, 0, j))
    else:
        a_spec = pl.BlockSpec((1, tm, na), lambda i, j: (i, j, 0))
    return pl.pallas_call(
        functools.partial(_out_mlp_kernel, a_transposed=a_transposed),
        grid=(b, s // tm),
        in_specs=[pl.BlockSpec((1, tm, d), lambda i, j: (i, j, 0)),
                  a_spec,
                  pl.BlockSpec((1, tm, om.shape[2]), lambda i, j: (i, j, 0)),
                  _const_spec(wo.shape), _const_spec((1, d)), _const_spec((1, d)),
                  _const_spec((d, dff)), _const_spec((dff, d)), _const_spec((1, d)), _const_spec((1, d))],
        out_specs=pl.BlockSpec((1, tm, d), lambda i, j: (i, j, 0)),
        out_shape=jax.ShapeDtypeStruct((b, s, d), F32),
        compiler_params=_cparams(("parallel", "parallel"), VMEM_LIMIT),
        name="out_mlp",
    )(x, a, om, wo, g1, b1, wu, wd, g2, b2)


def _kv_shared_kernel(h_ref, w_ref, c_ref, a_ref, b_ref, cmp_ref, slc_ref, win_ref, *attn_refs):
    kv = _dot(h_ref[0].astype(BF), w_ref[...])
    c, a, b = c_ref[...], a_ref[...], b_ref[...]
    tm = kv.shape[0]
    for br, ref in enumerate((cmp_ref, slc_ref, win_ref)):
        k = _rope_cols(kv[:, br * ROW_WIDTH:br * ROW_WIDTH + KV_WIDTH], c, a, b)
        v = kv[:, br * ROW_WIDTH + KV_WIDTH:(br + 1) * ROW_WIDTH]
        ref[0, :, :KV_WIDTH] = k
        ref[0, :, KV_WIDTH:] = v
        if attn_refs and br > 0:
            k_ref, vt_ref = attn_refs[2 * (br - 1):2 * br]
            vt = v.T
            for g in range(KV_HEADS):
                k_ref[0, g] = k[:, g * HEAD_DIM:(g + 1) * HEAD_DIM].astype(BF)
                for u in range(tm // PAGE):
                    vt_ref[0, g, u] = vt[g * HEAD_DIM:(g + 1) * HEAD_DIM, u * PAGE:(u + 1) * PAGE].astype(BF)


def _kv_shared(h, w_kv, tabs, *, tm, attn_layouts):
    b, s, d = h.shape
    row_spec = pl.BlockSpec((1, tm, ROW_WIDTH), lambda i, j: (i, j, 0))
    tab_spec = pl.BlockSpec((tm, LANES), lambda i, j: (j, 0))
    out_specs = [row_spec, row_spec, row_spec]
    out_shape = [jax.ShapeDtypeStruct((b, s, ROW_WIDTH), F32)] * 3
    if attn_layouts:
        out_specs += [pl.BlockSpec((1, KV_HEADS, tm, HEAD_DIM), lambda i, j: (i, 0, j, 0)),
                      pl.BlockSpec((1, KV_HEADS, tm // PAGE, HEAD_DIM, PAGE), lambda i, j: (i, 0, j, 0, 0))] * 2
        out_shape += [jax.ShapeDtypeStruct((b, KV_HEADS, s, HEAD_DIM), BF),
                      jax.ShapeDtypeStruct((b, KV_HEADS, s // PAGE, HEAD_DIM, PAGE), BF)] * 2
    return pl.pallas_call(
        _kv_shared_kernel,
        grid=(b, s // tm),
        in_specs=[pl.BlockSpec((1, tm, d), lambda i, j: (i, j, 0)), _const_spec(w_kv.shape),
                  tab_spec, tab_spec, tab_spec],
        out_specs=out_specs,
        out_shape=out_shape,
        compiler_params=_cparams(("parallel", "parallel")),
        name="kv_shared",
    )(h, w_kv, *tabs)


def _b_in_kernel(x_ref, w_ref, c_ref, a_ref, b_ref, q_ref, qm_ref, gt_ref):
    proj = _dot(x_ref[0].astype(BF), w_ref[...])
    q = _rope_cols(proj[:, :NSA_WIDTH], c_ref[...], a_ref[...], b_ref[...]) * Q_SCALE
    for h in range(NSA_HEADS):
        q_ref[0, h] = q[:, h * HEAD_DIM:(h + 1) * HEAD_DIM].astype(BF)
    qm_ref[0] = proj[:, NSA_WIDTH:NSA_WIDTH + MEM_WIDTH]
    gt_ref[0] = jax.nn.sigmoid(proj[:, NSA_WIDTH + MEM_WIDTH:]).T


def _b_in(x, w_in, tabs, *, tm):
    b, s, d = x.shape
    tab_spec = pl.BlockSpec((tm, LANES), lambda i, j: (j, 0))
    return pl.pallas_call(
        _b_in_kernel,
        grid=(b, s // tm),
        in_specs=[pl.BlockSpec((1, tm, d), lambda i, j: (i, j, 0)), _const_spec(w_in.shape),
                  tab_spec, tab_spec, tab_spec],
        out_specs=[pl.BlockSpec((1, NSA_HEADS, tm, HEAD_DIM), lambda i, j: (i, 0, j, 0)),
                   pl.BlockSpec((1, tm, MEM_WIDTH), lambda i, j: (i, j, 0)),
                   pl.BlockSpec((1, GATE_PAD, tm), lambda i, j: (i, 0, j))],
        out_shape=[jax.ShapeDtypeStruct((b, NSA_HEADS, s, HEAD_DIM), BF),
                   jax.ShapeDtypeStruct((b, s, MEM_WIDTH), F32),
                   jax.ShapeDtypeStruct((b, GATE_PAD, s), F32)],
        compiler_params=_cparams(("parallel", "parallel")),
        name="b_in",
    )(x, w_in, *tabs)


def _kv_layout_kernel(x_ref, k_ref, vt_ref):
    x = x_ref[0]
    vt = x[:, KV_WIDTH:].T
    for g in range(KV_HEADS):
        k_ref[0, g] = x[:, g * HEAD_DIM:(g + 1) * HEAD_DIM].astype(BF)
        vt_ref[0, g, 0] = vt[g * HEAD_DIM:(g + 1) * HEAD_DIM].astype(BF)


def _rows_layout(rows):
    b, l, _ = rows.shape
    n = l // PAGE
    return pl.pallas_call(
        _kv_layout_kernel,
        grid=(b, n),
        in_specs=[pl.BlockSpec((1, PAGE, ROW_WIDTH), lambda i, p: (i, p, 0))],
        out_specs=[pl.BlockSpec((1, KV_HEADS, PAGE, HEAD_DIM), lambda i, p: (i, 0, p, 0)),
                   pl.BlockSpec((1, KV_HEADS, 1, HEAD_DIM, PAGE), lambda i, p: (i, 0, p, 0, 0))],
        out_shape=[jax.ShapeDtypeStruct((b, KV_HEADS, l, HEAD_DIM), BF),
                   jax.ShapeDtypeStruct((b, KV_HEADS, n, HEAD_DIM, PAGE), BF)],
        compiler_params=_cparams(("parallel", "parallel")),
        name="kv_layout",
    )(rows)


def _compress_kernel(pt_ref, *refs, n_slots, tail_group):
    page_refs = refs[:n_slots]
    rest = refs[n_slots:]
    if tail_group is not None:
        tail_ref, rest = rest[0], rest[1:]
    w1_ref, pe_ref, w2_ref, o_ref, xs_ref, carry_ref = rest
    k = pl.program_id(1)
    per_page = PAGE // CMP_STRIDE
    m = n_slots * per_page
    hid2 = 2 * CMP_HIDDEN

    @pl.when(k == 0)
    def _():
        carry_ref[...] = jnp.zeros(carry_ref.shape, F32)

    for i, pr in enumerate(page_refs):
        x = pr[0]
        if tail_group is not None and i == n_slots - 1:
            x = jnp.where(k == tail_group, tail_ref[0], x)
        rows = slice(i * PAGE, (i + 1) * PAGE)
        if x.ndim == 2:
            for vcol in range(4):
                xs_ref[vcol, rows, :] = x[:, vcol * LANES:(vcol + 1) * LANES]
        else:
            for c in range(2):
                xc = x[c].reshape(KV_WIDTH, PAGE).T
                for hp in range(2):
                    xs_ref[2 * c + hp, rows, :] = xc[:, hp * LANES:(hp + 1) * LANES]

    def strides(vcol):
        return jnp.concatenate(
            [xs_ref[vcol, pl.ds(s, m, stride=CMP_STRIDE), :] for s in range(CMP_STRIDE)], axis=1)

    first_row = lax.broadcasted_iota(jnp.int32, (m, 1), 0) == 0
    out_cols = []
    for c in range(2):
        w1 = w1_ref[c]
        lhs = jnp.concatenate([strides(2 * c + hp) for hp in range(2)] + [pe_ref[c, 0], pe_ref[c, 1]],
                              axis=0).astype(BF)
        a = _dot(lhs, w1)
        hpe = a[2 * m:2 * m + 1, :hid2] + a[2 * m + 8:2 * m + 9, hid2:]
        acc = jnp.zeros((m, KV_WIDTH), F32)
        for hp in range(2):
            a0 = a[hp * m:(hp + 1) * m, :hid2]
            a1 = a[hp * m:(hp + 1) * m, hid2:]
            prev = jnp.where(first_row, carry_ref[c, hp, 0:1, :], pltpu.roll(a0, 1, axis=0))
            carry_ref[c, hp, 0:1, :] = a0[m - 1:m]
            acc = acc + _dot(jax.nn.gelu(prev + a1 + hpe).astype(BF), w2_ref[c, hp])
        out_cols.append(acc)
    o_ref[0] = jnp.concatenate(out_cols, axis=1)


def _compress(pool, table, tail, w1, pe, w2):
    b, n_real = table.shape
    n_pages = n_real + (0 if tail is None else 1)
    n_slots = max(p for p in range(1, 17) if n_pages % p == 0)
    n_groups = n_pages // n_slots
    m = n_slots * (PAGE // CMP_STRIDE)
    page_shape = pool.shape[1:]
    zeros = (0,) * len(page_shape)

    def page_spec(slot):
        return pl.BlockSpec(
            (1,) + page_shape,
            lambda i, k, pt: (pt[i * n_real + jnp.minimum(k * n_slots + slot, n_real - 1)],) + zeros)

    in_specs = [page_spec(slot) for slot in range(n_slots)]
    args = [pool] * n_slots
    if tail is not None:
        in_specs.append(pl.BlockSpec((1,) + page_shape, lambda i, k, pt: (i,) + zeros))
        args.append(tail)
    in_specs += [pl.BlockSpec(x.shape, lambda i, k, pt, nd=x.ndim: (0,) * nd) for x in (w1, pe, w2)]
    return pl.pallas_call(
        functools.partial(_compress_kernel, n_slots=n_slots,
                          tail_group=None if tail is None else n_groups - 1),
        grid_spec=pltpu.PrefetchScalarGridSpec(
            num_scalar_prefetch=1, grid=(b, n_groups), in_specs=in_specs,
            out_specs=pl.BlockSpec((1, m, ROW_WIDTH), lambda i, k, pt: (i, k, 0)),
            scratch_shapes=[pltpu.VMEM((4, n_slots * PAGE, LANES), F32),
                            pltpu.VMEM((2, 2, 8, 2 * CMP_HIDDEN), F32)]),
        out_shape=jax.ShapeDtypeStruct((b, n_groups * m, ROW_WIDTH), F32),
        compiler_params=_cparams(("parallel", "arbitrary"), VMEM_LIMIT),
        name="compress",
    )(table.reshape(-1), *args, w1, pe, w2)


def _nsa_kernel(q_ref, gt_ref, kc_ref, vct_ref, ks_ref, vst_ref, kw_ref, vwt_ref, o_ref, psum_ref, bias_ref,
                sbuf0_ref, sbuf1_ref, *, t_base, wpos0, n_slc):
    g = pl.program_id(1)
    i = pl.program_id(2)
    nq = q_ref.shape[2]
    nl = GQA * nq
    ncp = kc_ref.shape[2]
    gk = SEL_GROUP * PAGE
    blocks_per_group = gk // SEL_BLOCK
    nsp = bias_ref.shape[0]
    nkt_s = vst_ref.shape[2]
    q = q_ref[0].reshape(nl, HEAD_DIM)
    lane = lax.broadcasted_iota(jnp.int32, (1, nl), 1)
    q0 = t_base + i * nq
    t = q0 + (lane & (nq - 1))

    s = _dot_nt(kc_ref[0, 0], q)
    n_idx = lax.broadcasted_iota(jnp.int32, (ncp, 1), 0)
    cmask = n_idx * CMP_STRIDE + (CMP_BLOCK - 1) <= t
    s = jnp.where(cmask, s, NEG)
    p = jnp.where(cmask, jnp.exp2(s - jnp.max(s, axis=0, keepdims=True)), 0.0)
    l = jnp.sum(p, axis=0, keepdims=True)
    p = p * jnp.where(l > 0.0, 1.0 / l, 0.0)
    o_cmp = jnp.zeros((HEAD_DIM, nl), F32)
    for kt in range(ncp // PAGE):
        o_cmp = o_cmp + _dot(vct_ref[0, 0, kt], p[kt * PAGE:(kt + 1) * PAGE].astype(BF))

    rel = (q0 - wpos0) // PAGE
    n_wt = (WINDOW + nq) // PAGE
    tile_row = lax.broadcasted_iota(jnp.int32, (PAGE, 1), 0)
    parts = []
    for u in range(n_wt):
        kt = rel - WINDOW // PAGE + u
        ktc = jnp.maximum(kt, 0)
        kb = kw_ref[0, 0, pl.ds(pl.multiple_of(ktc * PAGE, PAGE), PAGE), :]
        krel = kt * PAGE + tile_row
        kpos = jnp.where(krel >= 0, wpos0 + krel, -(1 << 30))
        d = lax.bitcast_convert_type(t - kpos, jnp.uint32)
        parts.append(jnp.where(d < WINDOW, _dot_nt(kb, q), NEG))
    sw = jnp.concatenate(parts, axis=0)
    pw = jnp.exp2(sw - jnp.max(sw, axis=0, keepdims=True))
    l_w = jnp.sum(pw, axis=0, keepdims=True)
    acc_w = jnp.zeros((HEAD_DIM, nl), F32)
    for u in range(n_wt):
        ktc = jnp.maximum(rel - WINDOW // PAGE + u, 0)
        acc_w = acc_w + _dot(vwt_ref[0, 0, ktc], pw[u * PAGE:(u + 1) * PAGE].astype(BF))
    o_win = acc_w * (1.0 / l_w)

    psum = p[:, :nq]
    for r in range(1, GQA):
        psum = psum + p[:, r * nq:(r + 1) * nq]
    front = 8
    ratio = SEL_BLOCK // CMP_STRIDE
    slabs = []
    for sl in range(nq // LANES):
        psum_ref[sl, :front, :] = jnp.zeros((front, LANES), F32)
        psum_ref[sl, front:front + ncp, :] = psum[:, sl * LANES:(sl + 1) * LANES]
        psum_ref[sl, front + ncp:, :] = jnp.zeros((psum_ref.shape[1] - front - ncp, LANES), F32)
        part = psum_ref[sl, pl.ds(front - 1, nsp, stride=ratio), :]
        for d in range(1, ratio + 1):
            part = part + psum_ref[sl, pl.ds(front - 1 + d, nsp, stride=ratio), :]
        slabs.append(part)
    imp = slabs[0] if len(slabs) == 1 else jnp.concatenate(slabs, axis=1)
    tq = t[:, :nq]
    j = lax.broadcasted_iota(jnp.int32, (nsp, 1), 0)
    cur = tq >> int(math.log2(SEL_BLOCK))
    forced = (j == 0) | (j == cur) | (j == cur - 1)
    score = jnp.where(j * SEL_BLOCK <= tq, imp + FORCE_BONUS * forced.astype(F32), NEG)
    sub = 8
    chunks = [score[c * sub:(c + 1) * sub] for c in range(nsp // sub)]
    jc = j[:sub]
    cnts = [jnp.zeros((sub, nq), F32) for _ in chunks]
    for ii in range(n_slc):
        ri = score[ii:ii + 1, :]
        for c, sc_c in enumerate(chunks):
            if c * sub > ii:
                one = jnp.where(ri >= sc_c, 1.0, 0.0)
            elif c * sub + sub - 1 <= ii:
                one = jnp.where(ri > sc_c, 1.0, 0.0)
            else:
                one = jnp.where(jc + c * sub > ii, jnp.where(ri >= sc_c, 1.0, 0.0), jnp.where(ri > sc_c, 1.0, 0.0))
            cnts[c] = cnts[c] + one
    cnt = jnp.concatenate(cnts, axis=0)
    bias_ref[...] = jnp.where(cnt < float(SEL_TOPK), jnp.where(score > NEG / 2, 0.0, NEG), NEG)

    key_row = lax.broadcasted_iota(jnp.int32, (gk, 1), 0)

    def fill(buf, gi):
        kb = ks_ref[0, 0, pl.ds(pl.multiple_of(gi * gk, gk), gk), :]
        rows = bias_ref[pl.ds(pl.multiple_of(gi * blocks_per_group, blocks_per_group), blocks_per_group), :]
        rows = jnp.concatenate([rows] * GQA, axis=1)
        bias = jnp.concatenate([jnp.broadcast_to(rows[b:b + 1], (SEL_BLOCK, nl)) for b in range(blocks_per_group)],
                               axis=0)
        buf[...] = _dot_nt(kb, q) + bias

    def consume(buf, gi, carry, causal):
        m_old, l_old, acc = carry
        sc = buf[...]
        if causal:
            sc = jnp.where(gi * gk + key_row <= t, sc, NEG)
        m_new = jnp.maximum(m_old, jnp.max(sc, axis=0, keepdims=True))
        alpha = jnp.exp2(m_old - m_new)
        pe = jnp.exp2(sc - m_new)
        acc = acc * alpha
        for u in range(SEL_GROUP):
            acc = acc + _dot(vst_ref[0, 0, gi * SEL_GROUP + u], pe[u * PAGE:(u + 1) * PAGE].astype(BF))
        return m_new, l_old * alpha + jnp.sum(pe, axis=0, keepdims=True), acc

    init = (jnp.full((1, nl), NEG, F32), jnp.zeros((1, nl), F32), jnp.zeros((HEAD_DIM, nl), F32))

    n_groups = jnp.minimum(nkt_s - 1, (q0 + nq - 1) // PAGE) // SEL_GROUP + 1
    fill(sbuf0_ref, 0)

    def pair_body(kk, carry):
        gi = 2 * kk
        fill(sbuf1_ref, gi + 1)
        carry = consume(sbuf0_ref, gi, carry, False)
        fill(sbuf0_ref, gi + 2)
        return consume(sbuf1_ref, gi + 1, carry, False)

    n_pairs = (n_groups - 1) // 2
    carry = lax.fori_loop(0, n_pairs, pair_body, init)
    gi = 2 * n_pairs

    def odd_tail(carry):
        fill(sbuf1_ref, gi + 1)
        carry = consume(sbuf0_ref, gi, carry, False)
        return consume(sbuf1_ref, gi + 1, carry, True)

    def even_tail(carry):
        return consume(sbuf0_ref, gi, carry, True)

    _, l_s, acc_s = lax.cond(n_groups - 1 - gi == 1, odd_tail, even_tail, carry)

    o_sel = acc_s * jnp.where(l_s > 0.0, 1.0 / l_s, 0.0)
    for r in range(GQA):
        sl = slice(r * nq, (r + 1) * nq)
        grow = (GQA * g + r) * 3
        o = (gt_ref[0, pl.ds(grow, 1), :] * o_cmp[:, sl]
             + gt_ref[0, pl.ds(grow + 1, 1), :] * o_sel[:, sl]
             + gt_ref[0, pl.ds(grow + 2, 1), :] * o_win[:, sl])
        o_ref[0, r * HEAD_DIM:(r + 1) * HEAD_DIM, :] = o


def _nsa(q, gt, kc, vct, ks, vst, kw, vwt, *, t_base, wpos0, n_slc):
    b, _, sq, _ = q.shape
    nq = NSA_QUERY_TILE
    assert sq % nq == 0 and t_base % PAGE == 0 and (t_base - wpos0) % PAGE == 0 and t_base >= wpos0
    assert vst.shape[2] % SEL_GROUP == 0
    nsp = max(-(-n_slc // 8) * 8, vst.shape[2] * (PAGE // SEL_BLOCK))
    ncp = kc.shape[2]
    psum_rows = 8 + max(ncp + 8, (SEL_BLOCK // CMP_STRIDE) * nsp + 8)
    psum_rows = -(-psum_rows // 8) * 8

    def kv_specs(k, vt):
        return [pl.BlockSpec((1, 1) + k.shape[2:], lambda bi, gi, ii: (bi, gi, 0, 0)),
                pl.BlockSpec((1, 1) + vt.shape[2:], lambda bi, gi, ii: (bi, gi, 0, 0, 0))]

    return pl.pallas_call(
        functools.partial(_nsa_kernel, t_base=t_base, wpos0=wpos0, n_slc=n_slc),
        grid=(b, KV_HEADS, sq // nq),
        in_specs=[pl.BlockSpec((1, GQA, nq, HEAD_DIM), lambda bi, gi, ii: (bi, gi, ii, 0)),
                  pl.BlockSpec((1, GATE_PAD, nq), lambda bi, gi, ii: (bi, 0, ii))]
        + kv_specs(kc, vct) + kv_specs(ks, vst) + kv_specs(kw, vwt),
        out_specs=pl.BlockSpec((1, GQA * HEAD_DIM, nq), lambda bi, gi, ii: (bi, gi, ii)),
        out_shape=jax.ShapeDtypeStruct((b, NSA_WIDTH, sq), F32),
        scratch_shapes=[pltpu.VMEM((nq // LANES, psum_rows, LANES), F32), pltpu.VMEM((nsp, nq), F32),
                        pltpu.VMEM((SEL_GROUP * PAGE, GQA * nq), F32), pltpu.VMEM((SEL_GROUP * PAGE, GQA * nq), F32)],
        compiler_params=_cparams(("parallel", "parallel", "parallel"), VMEM_LIMIT),
        name="nsa_attend",
    )(q, gt, kc, vct, ks, vst, kw, vwt)


SAMPLE_ROWS = 64
ROWS_PER_GROUP = SAMPLE_ROWS // KV_HEADS


def _nsa_sample_kernel(pt_ref, *refs, n_slots, t_base, wpos0, n_slc, t_q):
    page_refs = refs[:n_slots]
    (tail_ref, q_ref, gt_ref, kct_ref, vct_ref, kwt_ref, vwt_ref, o_ref,
     ind_ref, expand_ref, m_ref, l_ref, acc_ref, oc_ref, ow_ref) = refs[n_slots:]
    k = pl.program_id(1)
    n_steps = pl.num_programs(1)
    q = q_ref[0]
    row = lax.broadcasted_iota(jnp.int32, (SAMPLE_ROWS, 1), 0)
    t = t_base + (row & (t_q - 1))

    @pl.when(k == 0)
    def _():
        ncp = kct_ref.shape[2]
        s = _dot(q, kct_ref[0])
        n_idx = lax.broadcasted_iota(jnp.int32, (1, ncp), 1)
        cmask = n_idx * CMP_STRIDE + (CMP_BLOCK - 1) <= t
        s = jnp.where(cmask, s, NEG)
        p = jnp.where(cmask, jnp.exp2(s - jnp.max(s, axis=1, keepdims=True)), 0.0)
        l = jnp.sum(p, axis=1, keepdims=True)
        p = p * jnp.where(l > 0.0, 1.0 / l, 0.0)
        oc_ref[...] = _dot_nt(p.astype(BF), vct_ref[0])

        parts = []
        for g in range(KV_HEADS):
            base = g * ROWS_PER_GROUP
            acc = p[base:base + t_q]
            for r in range(1, GQA):
                acc = acc + p[base + r * t_q:base + (r + 1) * t_q]
            parts.append(acc)
        psum = jnp.concatenate(parts, axis=0)
        nsl = -(-n_slc // LANES) * LANES
        n_col = lax.broadcasted_iota(jnp.int32, (ncp, 1), 0)
        j_row = lax.broadcasted_iota(jnp.int32, (1, nsl), 1)
        ratio = SEL_BLOCK // CMP_STRIDE
        overlap = jnp.where((n_col >= ratio * j_row - 1) & (n_col <= ratio * j_row + ratio - 1), 1.0, 0.0).astype(BF)
        hi = psum.astype(BF)
        mid = (psum - hi.astype(F32)).astype(BF)
        lo = (psum - hi.astype(F32) - mid.astype(F32)).astype(BF)
        imp = _dot(hi, overlap) + _dot(mid, overlap) + _dot(lo, overlap)
        nsel_rows = KV_HEADS * t_q
        tq = t_base + (lax.broadcasted_iota(jnp.int32, (nsel_rows, 1), 0) & (t_q - 1))
        cur = tq >> int(math.log2(SEL_BLOCK))
        forced = (j_row == 0) | (j_row == cur) | (j_row == cur - 1)
        score = jnp.where(j_row * SEL_BLOCK <= tq, imp + FORCE_BONUS * forced.astype(F32), NEG)
        score_t = jnp.concatenate([score, jnp.full((LANES - nsel_rows, nsl), NEG, F32)], axis=0).T
        sub = 8
        nsp = -(-n_slc // sub) * sub
        jc = lax.broadcasted_iota(jnp.int32, (sub, 1), 0)
        chunks = [score_t[c * sub:(c + 1) * sub] for c in range(nsp // sub)]
        cnts = [jnp.zeros((sub, LANES), F32) for _ in chunks]
        for ii in range(n_slc):
            ri = score_t[ii:ii + 1, :]
            for c, sc_c in enumerate(chunks):
                if c * sub > ii:
                    one = jnp.where(ri >= sc_c, 1.0, 0.0)
                elif c * sub + sub - 1 <= ii:
                    one = jnp.where(ri > sc_c, 1.0, 0.0)
                else:
                    one = jnp.where(jc + c * sub > ii, jnp.where(ri >= sc_c, 1.0, 0.0), jnp.where(ri > sc_c, 1.0, 0.0))
                cnts[c] = cnts[c] + one
        chosen = [jnp.where(cn < float(SEL_TOPK), jnp.where(sc_c > NEG / 2, 1.0, 0.0), 0.0)
                  for cn, sc_c in zip(cnts, chunks)]
        chosen_t = jnp.concatenate(chosen + [jnp.zeros((nsl - nsp, LANES), F32)], axis=0)
        chosen_rows = chosen_t.T.astype(BF)
        col = lax.broadcasted_iota(jnp.int32, (1, LANES), 1)
        src = (row >> int(math.log2(ROWS_PER_GROUP))) * t_q + (row & (t_q - 1))
        expand = jnp.where(col == src, 1.0, 0.0).astype(BF)
        ind = _dot(expand, chosen_rows).astype(BF)
        blocks_per_step = n_slots * (PAGE // SEL_BLOCK)
        j_all = lax.broadcasted_iota(jnp.int32, (nsl, 1), 0)
        for st in range(ind_ref.shape[0]):
            pick = jnp.where((j_all == st * blocks_per_step + col) & (col < blocks_per_step), 1.0, 0.0).astype(BF)
            ind_ref[st] = _dot(ind, pick).astype(BF)
        key_lane = lax.broadcasted_iota(jnp.int32, (1, n_slots * PAGE), 1)
        local_block = lax.broadcasted_iota(jnp.int32, (LANES, 1), 0)
        expand_ref[...] = jnp.where(local_block == (key_lane >> int(math.log2(SEL_BLOCK))), 1.0, 0.0).astype(BF)

        nw = kwt_ref.shape[2]
        sw = _dot(q, kwt_ref[0])
        kpos = wpos0 + lax.broadcasted_iota(jnp.int32, (1, nw), 1)
        d = lax.bitcast_convert_type(t - kpos, jnp.uint32)
        sw = jnp.where(d < WINDOW, sw, NEG)
        pw = jnp.exp2(sw - jnp.max(sw, axis=1, keepdims=True))
        lw = jnp.sum(pw, axis=1, keepdims=True)
        ow_ref[...] = _dot_nt(pw.astype(BF), vwt_ref[0]) * (1.0 / lw)

        m_ref[...] = jnp.full(m_ref.shape, NEG, F32)
        l_ref[...] = jnp.zeros(l_ref.shape, F32)
        acc_ref[...] = jnp.zeros(acc_ref.shape, F32)

    vts = []
    scs = []
    for i, pr in enumerate(page_refs):
        x = pr[0]
        if i == n_slots - 1:
            x = jnp.where(k == n_steps - 1, tail_ref[0], x)
        scs.append(_dot(q, x[0].reshape(KV_WIDTH, PAGE).astype(BF)))
        vts.append(x[1].reshape(KV_WIDTH, PAGE).astype(BF))
    sc = jnp.concatenate(scs, axis=1)
    nk = n_slots * PAGE
    kidx = k * nk + lax.broadcasted_iota(jnp.int32, (1, nk), 1)
    picked = _dot(ind_ref[k], expand_ref[...])
    sc = jnp.where(picked > 0.5, jnp.where(kidx <= t, sc, NEG), NEG)
    m_old = m_ref[:, :1]
    m_new = jnp.maximum(m_old, jnp.max(sc, axis=1, keepdims=True))
    alpha = jnp.exp2(m_old - m_new)
    pe = jnp.exp2(sc - m_new)
    acc = acc_ref[...] * alpha
    for i in range(n_slots):
        acc = acc + _dot_nt(pe[:, i * PAGE:(i + 1) * PAGE].astype(BF), vts[i])
    l_new = l_ref[:, :1] * alpha + jnp.sum(pe, axis=1, keepdims=True)
    m_ref[...] = jnp.broadcast_to(m_new, m_ref.shape)
    l_ref[...] = jnp.broadcast_to(l_new, l_ref.shape)
    acc_ref[...] = acc

    @pl.when(k == n_steps - 1)
    def _():
        o_sel = acc * jnp.where(l_new > 0.0, 1.0 / l_new, 0.0)
        gts = gt_ref[0]
        o_ref[0] = gts[:, 0:1] * oc_ref[...] + gts[:, 1:2] * o_sel + gts[:, 2:3] * ow_ref[...]


def _nsa_sample(q_bd, gates, pool_t, table, tail_t, kct, vct, kwt, vwt, *, t_base, wpos0, n_slc, t_q):
    b, n_real = table.shape
    n_pages = n_real + 1
    n_slots = max(p for p in range(1, 17) if n_pages % p == 0)
    n_steps = n_pages // n_slots
    nsl = -(-n_slc // LANES) * LANES
    page_shape = pool_t.shape[1:]
    zeros = (0,) * len(page_shape)

    def page_spec(slot):
        return pl.BlockSpec(
            (1,) + page_shape,
            lambda i, k, pt: (pt[i * n_real + jnp.minimum(k * n_slots + slot, n_real - 1)],) + zeros)

    def per_batch(x):
        return pl.BlockSpec((1,) + x.shape[1:], lambda i, k, pt: (i,) + (0,) * (x.ndim - 1))

    acc_shape = pltpu.VMEM((SAMPLE_ROWS, KV_WIDTH), F32)
    stat_shape = pltpu.VMEM((SAMPLE_ROWS, LANES), F32)
    return pl.pallas_call(
        functools.partial(_nsa_sample_kernel, n_slots=n_slots, t_base=t_base, wpos0=wpos0, n_slc=n_slc, t_q=t_q),
        grid_spec=pltpu.PrefetchScalarGridSpec(
            num_scalar_prefetch=1, grid=(b, n_steps),
            in_specs=[page_spec(slot) for slot in range(n_slots)]
            + [per_batch(x) for x in (tail_t, q_bd, gates, kct, vct, kwt, vwt)],
            out_specs=pl.BlockSpec((1, SAMPLE_ROWS, KV_WIDTH), lambda i, k, pt: (i, 0, 0)),
            scratch_shapes=[pltpu.VMEM((n_steps, SAMPLE_ROWS, LANES), BF), pltpu.VMEM((LANES, n_slots * PAGE), BF),
                            stat_shape, stat_shape, acc_shape, acc_shape, acc_shape]),
        out_shape=jax.ShapeDtypeStruct((b, SAMPLE_ROWS, KV_WIDTH), F32),
        compiler_params=_cparams(("parallel", "arbitrary"), VMEM_LIMIT),
        name="nsa_sample",
    )(table.reshape(-1), *([pool_t] * n_slots), tail_t, q_bd, gates, kct, vct, kwt, vwt)


def _rope_tables(pos):
    half = ROPE_DIM // 2
    inv_freq = ROPE_THETA ** (-jnp.arange(half, dtype=F32) / half)
    ang = pos.astype(F32)[:, None] * inv_freq[None, :]
    cos, sin = jnp.cos(ang), jnp.sin(ang)
    n = pos.shape[0]
    rest = HEAD_DIM - ROPE_DIM
    c = jnp.concatenate([cos, cos, jnp.ones((n, rest), F32)], axis=1)
    a = jnp.concatenate([-sin, jnp.zeros((n, half + rest), F32)], axis=1)
    b = jnp.concatenate([jnp.zeros((n, half), F32), sin, jnp.zeros((n, rest), F32)], axis=1)
    reps = LANES // HEAD_DIM
    return tuple(jnp.tile(x, (1, reps)) for x in (c, a, b))


def _compress_weights(cmp_pe, w_phi1, w_phi2):
    ratio = CMP_BLOCK // CMP_STRIDE
    w1 = w_phi1.reshape(ratio, CMP_STRIDE, 2, HEAD_DIM, CMP_HIDDEN)
    eye2 = jnp.eye(2, dtype=w_phi1.dtype)
    big = jnp.einsum('jscde,hg->cjshdge', w1, eye2)
    big = big.reshape(2, ratio, CMP_STRIDE * 2 * HEAD_DIM, 2 * CMP_HIDDEN).astype(BF)
    big = jnp.concatenate([big[:, jj] for jj in range(ratio)], axis=-1)
    pe = cmp_pe.reshape(ratio, CMP_STRIDE, 2, HEAD_DIM)
    pe = jnp.broadcast_to(pe.transpose(2, 0, 1, 3)[:, :, :, None, :], (2, ratio, CMP_STRIDE, 2, HEAD_DIM))
    pe = pe.reshape(2, ratio, 1, CMP_STRIDE * 2 * HEAD_DIM)
    pe = jnp.broadcast_to(pe, (2, ratio, 8, pe.shape[-1]))
    eye4 = jnp.eye(KV_HEADS, dtype=w_phi2.dtype).reshape(2, 2, KV_HEADS)
    w2 = jnp.einsum('ced,pgh->cpgehd', w_phi2, eye4)
    w2 = w2.reshape(2, 2, 2 * CMP_HIDDEN, KV_WIDTH).astype(BF)
    return big, pe, w2


def _b_in_weight(w):
    n_gate = w.shape[1] - NSA_WIDTH - MEM_WIDTH
    gates = jnp.pad(w[:, NSA_WIDTH:NSA_WIDTH + n_gate], ((0, 0), (0, GATE_PAD - n_gate)))
    return jnp.concatenate([w[:, :NSA_WIDTH], w[:, NSA_WIDTH + n_gate:], gates], axis=1).astype(BF)


def _pad_rows(x, n):
    return jnp.pad(x, ((0, 0), (0, n - x.shape[1]), (0, 0)))


def _tail_page_t(new_rows):
    b = new_rows.shape[0]
    return _pad_rows(new_rows, PAGE).reshape(b, PAGE, 2, KV_HEADS, HEAD_DIM).transpose(0, 2, 3, 4, 1)


def _rows_t(rows):
    b, n, _ = rows.shape
    kv = rows.reshape(b, n, 2, KV_WIDTH).transpose(0, 2, 3, 1).astype(BF)
    return kv[:, 0], kv[:, 1]


def _row_tiles(s):
    return {"proj": min(s, 256), "gate": min(s, 512), "mlp": min(s, 512), "mem": min(s, 512)}


SAMPLE_MEM_SETS = 8


def _mem_attn_rows(qm, mem_kv, rows_per_set):
    return _mem_attn(qm, mem_kv, tq=SAMPLE_MEM_SETS * rows_per_set, sets_per_step=SAMPLE_MEM_SETS)


def _trunk_a(x, mem_kv, p, l, *, chunk, tiles, emit_v, per_row_sets):
    b, s, _ = x.shape
    if chunk == PAGE:
        ws, bs = p["w_spatial"][l], jnp.repeat(p["b_spatial"][l].T, A_GROUP_DIM, axis=1)
    else:
        reps = PAGE // chunk
        ws = jnp.tile(p["w_spatial"][l][:, :chunk, :chunk], (1, reps, reps))
        bs = jnp.tile(jnp.repeat(p["b_spatial"][l].T[:chunk], A_GROUP_DIM, axis=1), (reps, 1))
    res = _a_in(x, p["w_in_a"][l], p["ln_v_g"][l][None], p["ln_v_b"][l][None], ws, bs,
                chunk=chunk, tm=tiles["gate"], emit_v=emit_v)
    mixed, qm = res[0], res[1]
    om = _mem_attn_rows(qm, mem_kv, chunk) if per_row_sets else _mem_attn(qm, mem_kv, tq=tiles["mem"])
    x = _mlp_block(x, mixed, om, p["w_out_a"][l], p, l, tm=tiles["mlp"], a_transposed=False)
    return x, (res[2] if emit_v else None)


def _mlp_block(x, a, om, wo, p, l, *, tm, a_transposed):
    return _out_mlp(x, a, om, wo, p["ln1_g"][l][None], p["ln1_b"][l][None], p["w_up"][l], p["w_down"][l],
                    p["ln2_g"][l][None], p["ln2_b"][l][None], tm=tm, a_transposed=a_transposed)


def kernel(x_prompt, x_sample, cache_cmp_kv, cache_slc_kv, cache_win_kv, cache_mem_kv, page_table, mem_prompt,
           w_in_a, ln_v_g, ln_v_b, w_spatial, b_spatial, w_out_a, w_in_b, w_out_b, w_kv_shared, cmp_pe,
           w_phi1, w_phi2, w_mem_kv, ln1_g, ln1_b, ln2_g, ln2_b, w_up, w_down):
    bp, s, d = x_prompt.shape
    bs, t, _ = x_sample.shape
    n_pages = page_table.shape[1]
    past = n_pages * PAGE
    n_buf = cache_win_kv.shape[1]
    assert bs * t == PAGE and PAGE % t == 0 and s % PAGE == 0 and n_buf % PAGE == 0

    p = {"w_in_a": w_in_a.astype(BF), "ln_v_g": ln_v_g, "ln_v_b": ln_v_b, "w_spatial": w_spatial,
         "b_spatial": b_spatial, "w_out_a": w_out_a.astype(BF), "w_out_b": w_out_b.astype(BF),
         "ln1_g": ln1_g, "ln1_b": ln1_b, "ln2_g": ln2_g, "ln2_b": ln2_b,
         "w_up": w_up.astype(BF), "w_down": w_down.astype(BF)}
    w_in_b2 = [_b_in_weight(w_in_b[i]) for i in range(w_in_b.shape[0])]
    w_kv = w_kv_shared.astype(BF)
    cw1, cpe, cw2 = _compress_weights(cmp_pe, w_phi1, w_phi2)

    tl = _row_tiles(s)
    mem_kv_prompt = _mem_kv_proj(mem_prompt, w_mem_kv)
    x = x_prompt
    for l in range(N_A):
        x, _ = _trunk_a(x, mem_kv_prompt[l], p, l, chunk=PAGE, tiles=tl, emit_v=False, per_row_sets=False)
    tabs_p = _rope_tables(jnp.arange(s, dtype=jnp.int32))
    cmp_rows, slc_rows, win_rows, ks, vst, kw, vwt = _kv_shared(x, w_kv, tabs_p, tm=tl["proj"], attn_layouts=True)
    n_pg = s // PAGE
    ident = jnp.arange(bp * n_pg, dtype=jnp.int32).reshape(bp, n_pg)
    cmp_blk = _compress(cmp_rows.reshape(bp * n_pg, PAGE, ROW_WIDTH), ident, None, cw1, cpe, cw2)
    n_cmp = s // CMP_STRIDE - 1
    ncp = -(-n_cmp // PAGE) * PAGE
    kc, vct = _rows_layout(_pad_rows(cmp_blk[:, 1:1 + n_cmp], ncp))
    for i in range(DEPTH - N_A):
        l = N_A + i
        q, qm, gt = _b_in(x, w_in_b2[i], tabs_p, tm=tl["proj"])
        om = _mem_attn(qm, mem_kv_prompt[l], tq=tl["mem"])
        ot = _nsa(q, gt, kc, vct, ks, vst, kw, vwt, t_base=0, wpos0=0, n_slc=s // SEL_BLOCK)
        x = _mlp_block(x, ot, om, p["w_out_b"][i], p, l, tm=tl["mlp"], a_transposed=True)
    y_prompt = x
    shape5 = (2, KV_HEADS, HEAD_DIM)
    cmp_kv_prompt = cmp_rows.reshape(bp, s, *shape5)
    slc_kv_prompt = slc_rows.reshape(bp, s, *shape5)
    win_kv_prompt = win_rows[:, s - min(WINDOW, s):].reshape(bp, min(WINDOW, s), *shape5)
    mem_kv_prompt_out = mem_kv_prompt.reshape(DEPTH, bp, MEM_TOKENS, 2, MEM_HEADS, HEAD_DIM)

    rows = bs * t
    x = x_sample.reshape(1, rows, d)
    mem_s = cache_mem_kv.reshape(DEPTH, bs, MEM_TOKENS, 2 * MEM_WIDTH)
    v_rows = []
    for l in range(N_A):
        x, v = _trunk_a(x, mem_s[l], p, l, chunk=t, tiles=_row_tiles(rows), emit_v=True, per_row_sets=True)
        v_rows.append(v.reshape(bs, t, A_WIDTH))
    pos_s = past + (jnp.arange(rows, dtype=jnp.int32) % t)
    tabs_s = _rope_tables(pos_s)
    new_cmp, new_slc, new_win = (r.reshape(bs, t, ROW_WIDTH)
                                 for r in _kv_shared(x, w_kv, tabs_s, tm=rows, attn_layouts=False))
    cmp_pool_t = cache_cmp_kv.transpose(0, 2, 3, 4, 1)
    slc_pool_t = cache_slc_kv.transpose(0, 2, 3, 4, 1)
    cmp_blk = _compress(cmp_pool_t, page_table, _tail_page_t(new_cmp), cw1, cpe, cw2)
    n_cmp = -(-(past + t) // CMP_STRIDE) - 1
    ncp = -(-n_cmp // PAGE) * PAGE
    kct, vct = _rows_t(_pad_rows(cmp_blk[:, 1:1 + n_cmp], ncp))
    win_all = jnp.concatenate([cache_win_kv.reshape(bs, n_buf, ROW_WIDTH), new_win], axis=1)
    kwt, vwt = _rows_t(_pad_rows(win_all, n_buf + PAGE))
    slc_tail_t = _tail_page_t(new_slc)
    n_slc = -(-(past + t) // SEL_BLOCK)
    per_group = GQA * t
    eye = jnp.eye(KV_HEADS, dtype=BF)
    for i in range(DEPTH - N_A):
        l = N_A + i
        q, qm, gt = _b_in(x, w_in_b2[i], tabs_s, tm=rows)
        om = _mem_attn_rows(qm, mem_s[l], t)
        q = q.reshape(KV_HEADS, GQA, bs, t, HEAD_DIM).transpose(2, 0, 1, 3, 4).reshape(bs, KV_HEADS, per_group, HEAD_DIM)
        q = jnp.pad(q, ((0, 0), (0, 0), (0, ROWS_PER_GROUP - per_group), (0, 0)))
        q_bd = jnp.einsum('bgxd,gh->bgxhd', q, eye).reshape(bs, SAMPLE_ROWS, KV_WIDTH)
        gs = gt[0, :NSA_HEADS * 3].reshape(KV_HEADS, GQA, 3, bs, t).transpose(3, 0, 1, 4, 2)
        gs = gs.reshape(bs, KV_HEADS, per_group, 3)
        gs = jnp.pad(gs, ((0, 0), (0, 0), (0, ROWS_PER_GROUP - per_group), (0, LANES - 3)))
        o = _nsa_sample(q_bd, gs.reshape(bs, SAMPLE_ROWS, LANES), slc_pool_t, page_table, slc_tail_t,
                        kct, vct, kwt, vwt, t_base=past, wpos0=past - n_buf, n_slc=n_slc, t_q=t)
        o = o.reshape(bs, KV_HEADS, ROWS_PER_GROUP, KV_HEADS, HEAD_DIM)
        o = jnp.stack([o[:, g, :per_group, g] for g in range(KV_HEADS)], axis=1)
        o = o.reshape(bs, KV_HEADS, GQA, t, HEAD_DIM).transpose(0, 3, 1, 2, 4).reshape(1, rows, NSA_WIDTH)
        x = _mlp_block(x, o, om, p["w_out_b"][i], p, l, tm=rows, a_transposed=False)
    y_sample = x.reshape(bs, t, d)
    cmp_kv_sample = new_cmp.reshape(bs, t, *shape5)
    slc_kv_sample = new_slc.reshape(bs, t, *shape5)
    win_kv_sample = win_all[:, t:].reshape(bs, n_buf, *shape5)
    gmlp_v_sample = jnp.stack(v_rows)

    return (y_prompt, y_sample, cmp_kv_prompt, slc_kv_prompt, win_kv_prompt, mem_kv_prompt_out,
            cmp_kv_sample, slc_kv_sample, win_kv_sample, gmlp_v_sample)
```

```python
import functools
import math

import jax
import jax.numpy as jnp
from jax import lax
from jax.experimental import pallas as pl
from jax.experimental.pallas import tpu as pltpu

BF = jnp.bfloat16
F32 = jnp.float32

DEPTH = 4
N_A = 2
A_WIDTH = 768
A_GROUPS = 4
A_GROUP_DIM = 192
MEM_TOKENS = 256
MEM_HEADS = 4
MEM_WIDTH = 256
HEAD_DIM = 64
NSA_HEADS = 12
KV_HEADS = 4
GQA = 3
NSA_WIDTH = 768
KV_WIDTH = 256
ROW_WIDTH = 2 * KV_WIDTH
CMP_BLOCK = 32
CMP_STRIDE = 16
CMP_HIDDEN = 128
SEL_BLOCK = 64
SEL_TOPK = 16
WINDOW = 512
ROPE_THETA = 500000.0
ROPE_DIM = 16
D_FF_CHUNK = 1024
ALPHA = (2.0 * DEPTH) ** 0.25
LN_EPS = 1e-5
NEG = -1e30
FORCE_BONUS = 1e4
PAGE = 128
LANES = 128
GATE_PAD = 128
SEL_GROUP = 4
NSA_QUERY_TILE = 256
Q_SCALE = HEAD_DIM ** -0.5 * math.log2(math.e)
VMEM_LIMIT = 56 * 1024 * 1024


def _cparams(sem, vmem=None):
    return pltpu.CompilerParams(dimension_semantics=sem, vmem_limit_bytes=vmem)


def _const_spec(shape):
    return pl.BlockSpec(shape, lambda *_: (0,) * len(shape), pipeline_mode=pl.Buffered(1))


def _ln(x, g, b):
    mu = jnp.mean(x, axis=-1, keepdims=True)
    xc = x - mu
    var = jnp.mean(xc * xc, axis=-1, keepdims=True)
    return xc * lax.rsqrt(var + LN_EPS) * g + b


def _dot(a, b):
    return jnp.dot(a, b, preferred_element_type=F32)


def _dot_nt(a, b):
    return lax.dot_general(a, b, (((1,), (1,)), ((), ())), preferred_element_type=F32)


def _rope_cols(x, c, a, b):
    cols = []
    for j in range(x.shape[1] // LANES):
        xc = x[:, j * LANES:(j + 1) * LANES]
        up = pltpu.roll(xc, LANES - ROPE_DIM // 2, axis=1)
        dn = pltpu.roll(xc, ROPE_DIM // 2, axis=1)
        cols.append(xc * c + up * a + dn * b)
    return cols[0] if len(cols) == 1 else jnp.concatenate(cols, axis=1)


def _mem_kv_kernel(x_ref, w_ref, o_ref):
    o_ref[0] = _dot(x_ref[...].astype(BF), w_ref[0].astype(BF))


def _mem_kv_proj(mem_prompt, w_mem_kv):
    b, m, d = mem_prompt.shape
    x = mem_prompt.reshape(b * m, d)
    depth, _, n = w_mem_kv.shape
    tm = 512
    out = pl.pallas_call(
        _mem_kv_kernel,
        grid=(depth, (b * m) // tm),
        in_specs=[pl.BlockSpec((tm, d), lambda l, i: (i, 0)),
                  pl.BlockSpec((1, d, n), lambda l, i: (l, 0, 0))],
        out_specs=pl.BlockSpec((1, tm, n), lambda l, i: (l, i, 0)),
        out_shape=jax.ShapeDtypeStruct((depth, b * m, n), F32),
        compiler_params=_cparams(("parallel", "parallel")),
        name="mem_kv_proj",
    )(x, w_mem_kv)
    return out.reshape(depth, b, m, n)


def _a_in_kernel(x_ref, w_ref, g_ref, b_ref, ws_ref, bs_ref, mixed_ref, qm_ref, *v_refs, chunk_shift):
    x = x_ref[0].astype(BF)
    proj = _dot(x, w_ref[...])
    uv = jax.nn.gelu(proj[:, :2 * A_WIDTH])
    u = uv[:, :A_WIDTH]
    v = _ln(uv[:, A_WIDTH:], g_ref[...], b_ref[...])
    if v_refs:
        v_refs[0][0] = v
    qm_ref[0] = proj[:, 2 * A_WIDTH:]
    tm = x.shape[0]
    r = ws_ref.shape[1]
    row = lax.broadcasted_iota(jnp.int32, (r, r), 0)
    col = lax.broadcasted_iota(jnp.int32, (r, r), 1)
    allowed = (col <= row) & ((row >> chunk_shift) == (col >> chunk_shift))
    wgs = [jnp.where(allowed, ws_ref[g], 0.0).astype(BF) for g in range(A_GROUPS)]
    span = 2 * A_GROUP_DIM
    first = lax.broadcasted_iota(jnp.int32, (1, span), 1) < A_GROUP_DIM
    vb = v.astype(BF)
    for ci in range(tm // r):
        rows = slice(ci * r, (ci + 1) * r)
        for h in range(A_GROUPS // 2):
            cols = slice(h * span, (h + 1) * span)
            vc = vb[rows, cols]
            mix = jnp.where(first, _dot(wgs[2 * h], vc), _dot(wgs[2 * h + 1], vc)) + bs_ref[:, cols]
            mixed_ref[0, rows, cols] = (u[rows, cols] * mix).astype(BF)


def _a_in(x, w_in, ln_g, ln_b, ws, bs, *, chunk, tm, emit_v):
    b, s, d = x.shape
    n = w_in.shape[1]
    r = ws.shape[1]
    outs = [jax.ShapeDtypeStruct((b, s, A_WIDTH), BF), jax.ShapeDtypeStruct((b, s, MEM_WIDTH), F32)]
    ospecs = [pl.BlockSpec((1, tm, A_WIDTH), lambda i, j: (i, j, 0)),
              pl.BlockSpec((1, tm, MEM_WIDTH), lambda i, j: (i, j, 0))]
    if emit_v:
        outs.append(jax.ShapeDtypeStruct((b, s, A_WIDTH), F32))
        ospecs.append(pl.BlockSpec((1, tm, A_WIDTH), lambda i, j: (i, j, 0)))
    return pl.pallas_call(
        functools.partial(_a_in_kernel, chunk_shift=int(math.log2(chunk))),
        grid=(b, s // tm),
        in_specs=[pl.BlockSpec((1, tm, d), lambda i, j: (i, j, 0)),
                  _const_spec((d, n)), _const_spec((1, A_WIDTH)), _const_spec((1, A_WIDTH)),
                  _const_spec((A_GROUPS, r, r)), _const_spec((r, A_WIDTH))],
        out_specs=ospecs,
        out_shape=outs,
        compiler_params=_cparams(("parallel", "parallel"), VMEM_LIMIT),
        name="a_in_gate",
    )(x, w_in, ln_g, ln_b, ws, bs)


def _mem_attn_kernel(q_ref, kv_ref, o_ref, *, rows_per_set):
    q = q_ref[0]
    nb = kv_ref.shape[0]
    kv = kv_ref[...].reshape(nb * MEM_TOKENS, 2 * MEM_WIDTH)
    k = kv[:, :MEM_WIDTH].astype(BF)
    v = kv[:, MEM_WIDTH:]
    lane = lax.broadcasted_iota(jnp.int32, (1, MEM_WIDTH), 1)
    if nb > 1:
        row_set = lax.broadcasted_iota(jnp.int32, (q.shape[0], 1), 0) >> int(math.log2(rows_per_set))
        col_set = lax.broadcasted_iota(jnp.int32, (1, nb * MEM_TOKENS), 1) >> int(math.log2(MEM_TOKENS))
        own = row_set == col_set
    out = jnp.zeros(q.shape, F32)
    for h in range(MEM_HEADS):
        hm = (lane >= h * HEAD_DIM) & (lane < (h + 1) * HEAD_DIM)
        s = _dot_nt(jnp.where(hm, q, 0.0).astype(BF), k) * (HEAD_DIM ** -0.5)
        if nb > 1:
            s = jnp.where(own, s, NEG)
        m = jnp.max(s, axis=-1, keepdims=True)
        p = jnp.exp(s - m)
        p = p / jnp.sum(p, axis=-1, keepdims=True)
        out = out + _dot(p.astype(BF), jnp.where(hm, v, 0.0).astype(BF))
    o_ref[0] = out


def _mem_attn(q, mem_kv, *, tq, sets_per_step=1):
    b, s, w = q.shape
    if sets_per_step == 1:
        kv_spec = pl.BlockSpec((1, MEM_TOKENS, 2 * w), lambda i, j: (i, 0, 0))
    else:
        assert b == 1
        kv_spec = pl.BlockSpec((sets_per_step, MEM_TOKENS, 2 * w), lambda i, j: (j, 0, 0))
    return pl.pallas_call(
        functools.partial(_mem_attn_kernel, rows_per_set=tq // sets_per_step),
        grid=(b, s // tq),
        in_specs=[pl.BlockSpec((1, tq, w), lambda i, j: (i, j, 0)), kv_spec],
        out_specs=pl.BlockSpec((1, tq, w), lambda i, j: (i, j, 0)),
        out_shape=jax.ShapeDtypeStruct((b, s, w), F32),
        compiler_params=_cparams(("parallel", "parallel")),
        name="mem_attn",
    )(q, mem_kv)


def _out_mlp_kernel(x_ref, a_ref, om_ref, wo_ref, g1_ref, b1_ref, wu_ref, wd_ref, g2_ref, b2_ref, o_ref,
                    *, a_transposed):
    x = x_ref[0]
    a = a_ref[0]
    a = a.T.astype(BF) if a_transposed else a.astype(BF)
    na = a.shape[1]
    mix = _dot(a, wo_ref[:na, :]) + _dot(om_ref[0].astype(BF), wo_ref[na:, :])
    y = _ln(ALPHA * x + mix, g1_ref[...], b1_ref[...])
    yb = y.astype(BF)
    acc = jnp.zeros(x.shape, F32)
    for c in range(wu_ref.shape[1] // D_FF_CHUNK):
        h = _dot(yb, wu_ref[:, c * D_FF_CHUNK:(c + 1) * D_FF_CHUNK])
        h = jnp.square(jnp.maximum(h, 0.0)).astype(BF)
        acc = acc + _dot(h, wd_ref[c * D_FF_CHUNK:(c + 1) * D_FF_CHUNK, :])
    o_ref[0] = _ln(ALPHA * y + acc, g2_ref[...], b2_ref[...])


def _out_mlp(x, a, om, wo, g1, b1, wu, wd, g2, b2, *, tm, a_transposed):
    b, s, d = x.shape
    na = wo.shape[0] - om.shape[2]
    dff = wu.shape[1]
    if a_transposed:
        a_spec = pl.BlockSpec((1, na, tm), lambda i, j: (i, 0, j))
    else:
        a_spec = pl.BlockSpec((1, tm, na), lambda i, j: (i, j, 0))
    return pl.pallas_call(
        functools.partial(_out_mlp_kernel, a_transposed=a_transposed),
        grid=(b, s // tm),
        in_specs=[pl.BlockSpec((1, tm, d), lambda i, j: (i, j, 0)),
                  a_spec,
                  pl.BlockSpec((1, tm, om.shape[2]), lambda i, j: (i, j, 0)),
                  _const_spec(wo.shape), _const_spec((1, d)), _const_spec((1, d)),
                  _const_spec((d, dff)), _const_spec((dff, d)), _const_spec((1, d)), _const_spec((1, d))],
        out_specs=pl.BlockSpec((1, tm, d), lambda i, j: (i, j, 0)),
        out_shape=jax.ShapeDtypeStruct((b, s, d), F32),
        compiler_params=_cparams(("parallel", "parallel"), VMEM_LIMIT),
        name="out_mlp",
    )(x, a, om, wo, g1, b1, wu, wd, g2, b2)


def _kv_shared_kernel(h_ref, w_ref, c_ref, a_ref, b_ref, cmp_ref, slc_ref, win_ref, *attn_refs):
    kv = _dot(h_ref[0].astype(BF), w_ref[...])
    c, a, b = c_ref[...], a_ref[...], b_ref[...]
    tm = kv.shape[0]
    for br, ref in enumerate((cmp_ref, slc_ref, win_ref)):
        k = _rope_cols(kv[:, br * ROW_WIDTH:br * ROW_WIDTH + KV_WIDTH], c, a, b)
        v = kv[:, br * ROW_WIDTH + KV_WIDTH:(br + 1) * ROW_WIDTH]
        ref[0, :, :KV_WIDTH] = k
        ref[0, :, KV_WIDTH:] = v
        if attn_refs and br > 0:
            k_ref, vt_ref = attn_refs[2 * (br - 1):2 * br]
            vt = v.T
            if br == 1:
                lane = lax.broadcasted_iota(jnp.int32, (1, LANES), 1)
                row = pl.program_id(1) * tm + lax.broadcasted_iota(jnp.int32, (tm, 1), 0)
                block_hot = jnp.where(lane - HEAD_DIM == (row >> int(math.log2(SEL_BLOCK))), 1.0, 0.0)
            for g in range(KV_HEADS):
                if br == 1:
                    pair = k[:, (g // 2) * LANES:(g // 2 + 1) * LANES]
                    if g % 2:
                        pair = pltpu.roll(pair, HEAD_DIM, axis=1)
                    k_ref[0, g] = jnp.where(lane < HEAD_DIM, pair, block_hot).astype(BF)
                else:
                    k_ref[0, g] = k[:, g * HEAD_DIM:(g + 1) * HEAD_DIM].astype(BF)
                for u in range(tm // PAGE):
                    vt_ref[0, g, u] = vt[g * HEAD_DIM:(g + 1) * HEAD_DIM, u * PAGE:(u + 1) * PAGE].astype(BF)


def _kv_shared(h, w_kv, tabs, *, tm, attn_layouts):
    b, s, d = h.shape
    row_spec = pl.BlockSpec((1, tm, ROW_WIDTH), lambda i, j: (i, j, 0))
    tab_spec = pl.BlockSpec((tm, LANES), lambda i, j: (j, 0))
    out_specs = [row_spec, row_spec, row_spec]
    out_shape = [jax.ShapeDtypeStruct((b, s, ROW_WIDTH), F32)] * 3
    if attn_layouts:
        assert s // SEL_BLOCK <= LANES - HEAD_DIM
        vt_spec = pl.BlockSpec((1, KV_HEADS, tm // PAGE, HEAD_DIM, PAGE), lambda i, j: (i, 0, j, 0, 0))
        vt_shape = jax.ShapeDtypeStruct((b, KV_HEADS, s // PAGE, HEAD_DIM, PAGE), BF)
        for width in (LANES, HEAD_DIM):
            out_specs += [pl.BlockSpec((1, KV_HEADS, tm, width), lambda i, j: (i, 0, j, 0)), vt_spec]
            out_shape += [jax.ShapeDtypeStruct((b, KV_HEADS, s, width), BF), vt_shape]
    return pl.pallas_call(
        _kv_shared_kernel,
        grid=(b, s // tm),
        in_specs=[pl.BlockSpec((1, tm, d), lambda i, j: (i, j, 0)), _const_spec(w_kv.shape),
                  tab_spec, tab_spec, tab_spec],
        out_specs=out_specs,
        out_shape=out_shape,
        compiler_params=_cparams(("parallel", "parallel")),
        name="kv_shared",
    )(h, w_kv, *tabs)


def _b_in_kernel(x_ref, w_ref, c_ref, a_ref, b_ref, q_ref, qm_ref, gt_ref):
    proj = _dot(x_ref[0].astype(BF), w_ref[...])
    q = _rope_cols(proj[:, :NSA_WIDTH], c_ref[...], a_ref[...], b_ref[...]) * Q_SCALE
    for h in range(NSA_HEADS):
        q_ref[0, h] = q[:, h * HEAD_DIM:(h + 1) * HEAD_DIM].astype(BF)
    qm_ref[0] = proj[:, NSA_WIDTH:NSA_WIDTH + MEM_WIDTH]
    gt_ref[0] = jax.nn.sigmoid(proj[:, NSA_WIDTH + MEM_WIDTH:]).T


def _b_in(x, w_in, tabs, *, tm):
    b, s, d = x.shape
    tab_spec = pl.BlockSpec((tm, LANES), lambda i, j: (j, 0))
    return pl.pallas_call(
        _b_in_kernel,
        grid=(b, s // tm),
        in_specs=[pl.BlockSpec((1, tm, d), lambda i, j: (i, j, 0)), _const_spec(w_in.shape),
                  tab_spec, tab_spec, tab_spec],
        out_specs=[pl.BlockSpec((1, NSA_HEADS, tm, HEAD_DIM), lambda i, j: (i, 0, j, 0)),
                   pl.BlockSpec((1, tm, MEM_WIDTH), lambda i, j: (i, j, 0)),
                   pl.BlockSpec((1, GATE_PAD, tm), lambda i, j: (i, 0, j))],
        out_shape=[jax.ShapeDtypeStruct((b, NSA_HEADS, s, HEAD_DIM), BF),
                   jax.ShapeDtypeStruct((b, s, MEM_WIDTH), F32),
                   jax.ShapeDtypeStruct((b, GATE_PAD, s), F32)],
        compiler_params=_cparams(("parallel", "parallel")),
        name="b_in",
    )(x, w_in, *tabs)


def _kv_layout_kernel(x_ref, k_ref, vt_ref):
    x = x_ref[0]
    vt = x[:, KV_WIDTH:].T
    for g in range(KV_HEADS):
        k_ref[0, g] = x[:, g * HEAD_DIM:(g + 1) * HEAD_DIM].astype(BF)
        vt_ref[0, g, 0] = vt[g * HEAD_DIM:(g + 1) * HEAD_DIM].astype(BF)


def _rows_layout(rows):
    b, l, _ = rows.shape
    n = l // PAGE
    return pl.pallas_call(
        _kv_layout_kernel,
        grid=(b, n),
        in_specs=[pl.BlockSpec((1, PAGE, ROW_WIDTH), lambda i, p: (i, p, 0))],
        out_specs=[pl.BlockSpec((1, KV_HEADS, PAGE, HEAD_DIM), lambda i, p: (i, 0, p, 0)),
                   pl.BlockSpec((1, KV_HEADS, 1, HEAD_DIM, PAGE), lambda i, p: (i, 0, p, 0, 0))],
        out_shape=[jax.ShapeDtypeStruct((b, KV_HEADS, l, HEAD_DIM), BF),
                   jax.ShapeDtypeStruct((b, KV_HEADS, n, HEAD_DIM, PAGE), BF)],
        compiler_params=_cparams(("parallel", "parallel")),
        name="kv_layout",
    )(rows)


def _compress_kernel(pt_ref, *refs, n_slots, tail_group):
    page_refs = refs[:n_slots]
    rest = refs[n_slots:]
    if tail_group is not None:
        tail_ref, rest = rest[0], rest[1:]
    w1_ref, pe_ref, w2_ref, o_ref, xs_ref, carry_ref = rest
    k = pl.program_id(1)
    per_page = PAGE // CMP_STRIDE
    m = n_slots * per_page
    hid2 = 2 * CMP_HIDDEN

    @pl.when(k == 0)
    def _():
        carry_ref[...] = jnp.zeros(carry_ref.shape, F32)

    for i, pr in enumerate(page_refs):
        x = pr[0]
        if tail_group is not None and i == n_slots - 1:
            x = jnp.where(k == tail_group, tail_ref[0], x)
        rows = slice(i * PAGE, (i + 1) * PAGE)
        if x.ndim == 2:
            for vcol in range(4):
                xs_ref[vcol, rows, :] = x[:, vcol * LANES:(vcol + 1) * LANES]
        else:
            for c in range(2):
                xc = x[c].reshape(KV_WIDTH, PAGE).T
                for hp in range(2):
                    xs_ref[2 * c + hp, rows, :] = xc[:, hp * LANES:(hp + 1) * LANES]

    def strides(vcol):
        return jnp.concatenate(
            [xs_ref[vcol, pl.ds(s, m, stride=CMP_STRIDE), :] for s in range(CMP_STRIDE)], axis=1)

    first_row = lax.broadcasted_iota(jnp.int32, (m, 1), 0) == 0
    out_cols = []
    for c in range(2):
        w1 = w1_ref[c]
        lhs = jnp.concatenate([strides(2 * c + hp) for hp in range(2)] + [pe_ref[c, 0], pe_ref[c, 1]],
                              axis=0).astype(BF)
        a = _dot(lhs, w1)
        hpe = a[2 * m:2 * m + 1, :hid2] + a[2 * m + 8:2 * m + 9, hid2:]
        acc = jnp.zeros((m, KV_WIDTH), F32)
        for hp in range(2):
            a0 = a[hp * m:(hp + 1) * m, :hid2]
            a1 = a[hp * m:(hp + 1) * m, hid2:]
            prev = jnp.where(first_row, carry_ref[c, hp, 0:1, :], pltpu.roll(a0, 1, axis=0))
            carry_ref[c, hp, 0:1, :] = a0[m - 1:m]
            acc = acc + _dot(jax.nn.gelu(prev + a1 + hpe).astype(BF), w2_ref[c, hp])
        out_cols.append(acc)
    o_ref[0] = jnp.concatenate(out_cols, axis=1)


def _compress(pool, table, tail, w1, pe, w2):
    b, n_real = table.shape
    n_pages = n_real + (0 if tail is None else 1)
    n_slots = max(p for p in range(1, 17) if n_pages % p == 0)
    n_groups = n_pages // n_slots
    m = n_slots * (PAGE // CMP_STRIDE)
    page_shape = pool.shape[1:]
    zeros = (0,) * len(page_shape)

    def page_spec(slot):
        return pl.BlockSpec(
            (1,) + page_shape,
            lambda i, k, pt: (pt[i * n_real + jnp.minimum(k * n_slots + slot, n_real - 1)],) + zeros)

    in_specs = [page_spec(slot) for slot in range(n_slots)]
    args = [pool] * n_slots
    if tail is not None:
        in_specs.append(pl.BlockSpec((1,) + page_shape, lambda i, k, pt: (i,) + zeros))
        args.append(tail)
    in_specs += [pl.BlockSpec(x.shape, lambda i, k, pt, nd=x.ndim: (0,) * nd) for x in (w1, pe, w2)]
    return pl.pallas_call(
        functools.partial(_compress_kernel, n_slots=n_slots,
                          tail_group=None if tail is None else n_groups - 1),
        grid_spec=pltpu.PrefetchScalarGridSpec(
            num_scalar_prefetch=1, grid=(b, n_groups), in_specs=in_specs,
            out_specs=pl.BlockSpec((1, m, ROW_WIDTH), lambda i, k, pt: (i, k, 0)),
            scratch_shapes=[pltpu.VMEM((4, n_slots * PAGE, LANES), F32),
                            pltpu.VMEM((2, 2, 8, 2 * CMP_HIDDEN), F32)]),
        out_shape=jax.ShapeDtypeStruct((b, n_groups * m, ROW_WIDTH), F32),
        compiler_params=_cparams(("parallel", "arbitrary"), VMEM_LIMIT),
        name="compress",
    )(table.reshape(-1), *args, w1, pe, w2)


def _nsa_kernel(q_ref, gt_ref, kc_ref, vct_ref, ks_ref, vst_ref, kw_ref, vwt_ref, o_ref, psum_ref,
                sbuf0_ref, sbuf1_ref, *, t_base, wpos0, n_slc, nsp):
    g = pl.program_id(1)
    i = pl.program_id(2)
    nq = q_ref.shape[2]
    nl = GQA * nq
    ncp = kc_ref.shape[2]
    gk = SEL_GROUP * PAGE
    nkt_s = vst_ref.shape[2]
    q = q_ref[0].reshape(nl, HEAD_DIM)
    lane = lax.broadcasted_iota(jnp.int32, (1, nl), 1)
    q0 = t_base + i * nq
    t = q0 + (lane & (nq - 1))

    s = _dot_nt(kc_ref[0, 0], q)
    n_idx = lax.broadcasted_iota(jnp.int32, (ncp, 1), 0)
    cmask = n_idx * CMP_STRIDE + (CMP_BLOCK - 1) <= t
    s = jnp.where(cmask, s, NEG)
    p = jnp.where(cmask, jnp.exp2(s - jnp.max(s, axis=0, keepdims=True)), 0.0)
    l = jnp.sum(p, axis=0, keepdims=True)
    p = p * jnp.where(l > 0.0, 1.0 / l, 0.0)
    o_cmp = jnp.zeros((HEAD_DIM, nl), F32)
    for kt in range(ncp // PAGE):
        o_cmp = o_cmp + _dot(vct_ref[0, 0, kt], p[kt * PAGE:(kt + 1) * PAGE].astype(BF))

    rel = (q0 - wpos0) // PAGE
    n_wt = (WINDOW + nq) // PAGE
    tile_row = lax.broadcasted_iota(jnp.int32, (PAGE, 1), 0)
    parts = []
    for u in range(n_wt):
        kt = rel - WINDOW // PAGE + u
        ktc = jnp.maximum(kt, 0)
        kb = kw_ref[0, 0, pl.ds(pl.multiple_of(ktc * PAGE, PAGE), PAGE), :]
        krel = kt * PAGE + tile_row
        kpos = jnp.where(krel >= 0, wpos0 + krel, -(1 << 30))
        d = lax.bitcast_convert_type(t - kpos, jnp.uint32)
        parts.append(jnp.where(d < WINDOW, _dot_nt(kb, q), NEG))
    sw = jnp.concatenate(parts, axis=0)
    pw = jnp.exp2(sw - jnp.max(sw, axis=0, keepdims=True))
    l_w = jnp.sum(pw, axis=0, keepdims=True)
    acc_w = jnp.zeros((HEAD_DIM, nl), F32)
    for u in range(n_wt):
        ktc = jnp.maximum(rel - WINDOW // PAGE + u, 0)
        acc_w = acc_w + _dot(vwt_ref[0, 0, ktc], pw[u * PAGE:(u + 1) * PAGE].astype(BF))
    o_win = acc_w * (1.0 / l_w)

    psum = p[:, :nq]
    for r in range(1, GQA):
        psum = psum + p[:, r * nq:(r + 1) * nq]
    front = 8
    ratio = SEL_BLOCK // CMP_STRIDE
    slabs = []
    for sl in range(nq // LANES):
        psum_ref[sl, :front, :] = jnp.zeros((front, LANES), F32)
        psum_ref[sl, front:front + ncp, :] = psum[:, sl * LANES:(sl + 1) * LANES]
        psum_ref[sl, front + ncp:, :] = jnp.zeros((psum_ref.shape[1] - front - ncp, LANES), F32)
        part = psum_ref[sl, pl.ds(front - 1, nsp, stride=ratio), :]
        for d in range(1, ratio + 1):
            part = part + psum_ref[sl, pl.ds(front - 1 + d, nsp, stride=ratio), :]
        slabs.append(part)
    imp = slabs[0] if len(slabs) == 1 else jnp.concatenate(slabs, axis=1)
    tq = t[:, :nq]
    j = lax.broadcasted_iota(jnp.int32, (nsp, 1), 0)
    cur = tq >> int(math.log2(SEL_BLOCK))
    forced = (j == 0) | (j == cur) | (j == cur - 1)
    score = jnp.where(j * SEL_BLOCK <= tq, imp + FORCE_BONUS * forced.astype(F32), NEG)
    sub = 8
    chunks = [score[c * sub:(c + 1) * sub] for c in range(nsp // sub)]
    jc = j[:sub]
    cnts = [jnp.zeros((sub, nq), F32) for _ in chunks]
    for ii in range(n_slc):
        ri = score[ii:ii + 1, :]
        for c, sc_c in enumerate(chunks):
            if c * sub > ii:
                one = jnp.where(ri >= sc_c, 1.0, 0.0)
            elif c * sub + sub - 1 <= ii:
                one = jnp.where(ri > sc_c, 1.0, 0.0)
            else:
                one = jnp.where(jc + c * sub > ii, jnp.where(ri >= sc_c, 1.0, 0.0), jnp.where(ri > sc_c, 1.0, 0.0))
            cnts[c] = cnts[c] + one
    cnt = jnp.concatenate(cnts, axis=0)
    bias = jnp.where(cnt < float(SEL_TOPK), jnp.where(score > NEG / 2, 0.0, NEG), NEG)
    bias_t = jnp.concatenate([bias, jnp.zeros((LANES - nsp, nq), F32)], axis=0).T[:, :LANES - HEAD_DIM]
    q_ext = jnp.concatenate([q, jnp.concatenate([bias_t.astype(BF)] * GQA, axis=0)], axis=1)

    key_row = lax.broadcasted_iota(jnp.int32, (gk, 1), 0)

    def fill(buf, gi):
        kb = ks_ref[0, 0, pl.ds(pl.multiple_of(gi * gk, gk), gk), :]
        buf[...] = _dot_nt(kb, q_ext)

    def consume(buf, gi, carry, causal):
        m_old, l_old, acc = carry
        sc = buf[...]
        if causal:
            sc = jnp.where(gi * gk + key_row <= t, sc, NEG)
        m_new = jnp.maximum(m_old, jnp.max(sc, axis=0, keepdims=True))
        alpha = jnp.exp2(m_old - m_new)
        pe = jnp.exp2(sc - m_new)
        acc = acc * alpha
        for u in range(SEL_GROUP):
            acc = acc + _dot(vst_ref[0, 0, gi * SEL_GROUP + u], pe[u * PAGE:(u + 1) * PAGE].astype(BF))
        return m_new, l_old * alpha + jnp.sum(pe, axis=0, keepdims=True), acc

    init = (jnp.full((1, nl), NEG, F32), jnp.zeros((1, nl), F32), jnp.zeros((HEAD_DIM, nl), F32))

    n_groups = jnp.minimum(nkt_s - 1, (q0 + nq - 1) // PAGE) // SEL_GROUP + 1
    fill(sbuf0_ref, 0)

    def pair_body(kk, carry):
        gi = 2 * kk
        fill(sbuf1_ref, gi + 1)
        carry = consume(sbuf0_ref, gi, carry, False)
        fill(sbuf0_ref, gi + 2)
        return consume(sbuf1_ref, gi + 1, carry, False)

    n_pairs = (n_groups - 1) // 2
    carry = lax.fori_loop(0, n_pairs, pair_body, init)
    gi = 2 * n_pairs

    def odd_tail(carry):
        fill(sbuf1_ref, gi + 1)
        carry = consume(sbuf0_ref, gi, carry, False)
        return consume(sbuf1_ref, gi + 1, carry, True)

    def even_tail(carry):
        return consume(sbuf0_ref, gi, carry, True)

    _, l_s, acc_s = lax.cond(n_groups - 1 - gi == 1, odd_tail, even_tail, carry)

    o_sel = acc_s * jnp.where(l_s > 0.0, 1.0 / l_s, 0.0)
    for r in range(GQA):
        sl = slice(r * nq, (r + 1) * nq)
        grow = (GQA * g + r) * 3
        o = (gt_ref[0, pl.ds(grow, 1), :] * o_cmp[:, sl]
             + gt_ref[0, pl.ds(grow + 1, 1), :] * o_sel[:, sl]
             + gt_ref[0, pl.ds(grow + 2, 1), :] * o_win[:, sl])
        o_ref[0, r * HEAD_DIM:(r + 1) * HEAD_DIM, :] = o


def _nsa(q, gt, kc, vct, ks, vst, kw, vwt, *, t_base, wpos0, n_slc):
    b, _, sq, _ = q.shape
    nq = NSA_QUERY_TILE
    assert sq % nq == 0 and t_base % PAGE == 0 and (t_base - wpos0) % PAGE == 0 and t_base >= wpos0
    assert vst.shape[2] % SEL_GROUP == 0
    nsp = max(-(-n_slc // 8) * 8, vst.shape[2] * (PAGE // SEL_BLOCK))
    assert nsp <= LANES - HEAD_DIM and ks.shape[3] == LANES
    ncp = kc.shape[2]
    psum_rows = 8 + max(ncp + 8, (SEL_BLOCK // CMP_STRIDE) * nsp + 8)
    psum_rows = -(-psum_rows // 8) * 8

    def kv_specs(k, vt):
        return [pl.BlockSpec((1, 1) + k.shape[2:], lambda bi, gi, ii: (bi, gi, 0, 0)),
                pl.BlockSpec((1, 1) + vt.shape[2:], lambda bi, gi, ii: (bi, gi, 0, 0, 0))]

    return pl.pallas_call(
        functools.partial(_nsa_kernel, t_base=t_base, wpos0=wpos0, n_slc=n_slc, nsp=nsp),
        grid=(b, KV_HEADS, sq // nq),
        in_specs=[pl.BlockSpec((1, GQA, nq, HEAD_DIM), lambda bi, gi, ii: (bi, gi, ii, 0)),
                  pl.BlockSpec((1, GATE_PAD, nq), lambda bi, gi, ii: (bi, 0, ii))]
        + kv_specs(kc, vct) + kv_specs(ks, vst) + kv_specs(kw, vwt),
        out_specs=pl.BlockSpec((1, GQA * HEAD_DIM, nq), lambda bi, gi, ii: (bi, gi, ii)),
        out_shape=jax.ShapeDtypeStruct((b, NSA_WIDTH, sq), F32),
        scratch_shapes=[pltpu.VMEM((nq // LANES, psum_rows, LANES), F32),
                        pltpu.VMEM((SEL_GROUP * PAGE, GQA * nq), F32), pltpu.VMEM((SEL_GROUP * PAGE, GQA * nq), F32)],
        compiler_params=_cparams(("parallel", "parallel", "parallel"), VMEM_LIMIT),
        name="nsa_attend",
    )(q, gt, kc, vct, ks, vst, kw, vwt)


SAMPLE_ROWS = 64
ROWS_PER_GROUP = SAMPLE_ROWS // KV_HEADS


def _nsa_sample_kernel(pt_ref, *refs, n_slots, t_base, wpos0, n_slc, t_q):
    page_refs = refs[:n_slots]
    (tail_ref, q_ref, gt_ref, kct_ref, vct_ref, kwt_ref, vwt_ref, o_ref,
     ind_ref, expand_ref, m_ref, l_ref, acc_ref, oc_ref, ow_ref) = refs[n_slots:]
    k = pl.program_id(1)
    n_steps = pl.num_programs(1)
    q = q_ref[0]
    row = lax.broadcasted_iota(jnp.int32, (SAMPLE_ROWS, 1), 0)
    t = t_base + (row & (t_q - 1))

    @pl.when(k == 0)
    def _():
        ncp = kct_ref.shape[2]
        s = _dot(q, kct_ref[0])
        n_idx = lax.broadcasted_iota(jnp.int32, (1, ncp), 1)
        cmask = n_idx * CMP_STRIDE + (CMP_BLOCK - 1) <= t
        s = jnp.where(cmask, s, NEG)
        p = jnp.where(cmask, jnp.exp2(s - jnp.max(s, axis=1, keepdims=True)), 0.0)
        l = jnp.sum(p, axis=1, keepdims=True)
        p = p * jnp.where(l > 0.0, 1.0 / l, 0.0)
        oc_ref[...] = _dot_nt(p.astype(BF), vct_ref[0])

        parts = []
        for g in range(KV_HEADS):
            base = g * ROWS_PER_GROUP
            acc = p[base:base + t_q]
            for r in range(1, GQA):
                acc = acc + p[base + r * t_q:base + (r + 1) * t_q]
            parts.append(acc)
        psum = jnp.concatenate(parts, axis=0)
        nsl = -(-n_slc // LANES) * LANES
        n_col = lax.broadcasted_iota(jnp.int32, (ncp, 1), 0)
        j_row = lax.broadcasted_iota(jnp.int32, (1, nsl), 1)
        ratio = SEL_BLOCK // CMP_STRIDE
        overlap = jnp.where((n_col >= ratio * j_row - 1) & (n_col <= ratio * j_row + ratio - 1), 1.0, 0.0).astype(BF)
        hi = psum.astype(BF)
        mid = (psum - hi.astype(F32)).astype(BF)
        lo = (psum - hi.astype(F32) - mid.astype(F32)).astype(BF)
        imp = _dot(hi, overlap) + _dot(mid, overlap) + _dot(lo, overlap)
        nsel_rows = KV_HEADS * t_q
        tq = t_base + (lax.broadcasted_iota(jnp.int32, (nsel_rows, 1), 0) & (t_q - 1))
        cur = tq >> int(math.log2(SEL_BLOCK))
        forced = (j_row == 0) | (j_row == cur) | (j_row == cur - 1)
        score = jnp.where(j_row * SEL_BLOCK <= tq, imp + FORCE_BONUS * forced.astype(F32), NEG)
        score_t = jnp.concatenate([score, jnp.full((LANES - nsel_rows, nsl), NEG, F32)], axis=0).T
        sub = 8
        nsp = -(-n_slc // sub) * sub
        jc = lax.broadcasted_iota(jnp.int32, (sub, 1), 0)
        chunks = [score_t[c * sub:(c + 1) * sub] for c in range(nsp // sub)]
        cnts = [jnp.zeros((sub, LANES), F32) for _ in chunks]
        for ii in range(n_slc):
            ri = score_t[ii:ii + 1, :]
            for c, sc_c in enumerate(chunks):
                if c * sub > ii:
                    one = jnp.where(ri >= sc_c, 1.0, 0.0)
                elif c * sub + sub - 1 <= ii:
                    one = jnp.where(ri > sc_c, 1.0, 0.0)
                else:
                    one = jnp.where(jc + c * sub > ii, jnp.where(ri >= sc_c, 1.0, 0.0), jnp.where(ri > sc_c, 1.0, 0.0))
                cnts[c] = cnts[c] + one
        chosen = [jnp.where(cn < float(SEL_TOPK), jnp.where(sc_c > NEG / 2, 1.0, 0.0), 0.0)
                  for cn, sc_c in zip(cnts, chunks)]
        chosen_t = jnp.concatenate(chosen + [jnp.zeros((nsl - nsp, LANES), F32)], axis=0)
        chosen_rows = chosen_t.T.astype(BF)
        col = lax.broadcasted_iota(jnp.int32, (1, LANES), 1)
        src = (row >> int(math.log2(ROWS_PER_GROUP))) * t_q + (row & (t_q - 1))
        expand = jnp.where(col == src, 1.0, 0.0).astype(BF)
        ind = _dot(expand, chosen_rows).astype(BF)
        blocks_per_step = n_slots * (PAGE // SEL_BLOCK)
        j_all = lax.broadcasted_iota(jnp.int32, (nsl, 1), 0)
        for st in range(ind_ref.shape[0]):
            pick = jnp.where((j_all == st * blocks_per_step + col) & (col < blocks_per_step), 1.0, 0.0).astype(BF)
            ind_ref[st] = _dot(ind, pick).astype(BF)
        key_lane = lax.broadcasted_iota(jnp.int32, (1, n_slots * PAGE), 1)
        local_block = lax.broadcasted_iota(jnp.int32, (LANES, 1), 0)
        expand_ref[...] = jnp.where(local_block == (key_lane >> int(math.log2(SEL_BLOCK))), 1.0, 0.0).astype(BF)

        nw = kwt_ref.shape[2]
        sw = _dot(q, kwt_ref[0])
        kpos = wpos0 + lax.broadcasted_iota(jnp.int32, (1, nw), 1)
        d = lax.bitcast_convert_type(t - kpos, jnp.uint32)
        sw = jnp.where(d < WINDOW, sw, NEG)
        pw = jnp.exp2(sw - jnp.max(sw, axis=1, keepdims=True))
        lw = jnp.sum(pw, axis=1, keepdims=True)
        ow_ref[...] = _dot_nt(pw.astype(BF), vwt_ref[0]) * (1.0 / lw)

        m_ref[...] = jnp.full(m_ref.shape, NEG, F32)
        l_ref[...] = jnp.zeros(l_ref.shape, F32)
        acc_ref[...] = jnp.zeros(acc_ref.shape, F32)

    vts = []
    scs = []
    for i, pr in enumerate(page_refs):
        x = pr[0]
        if i == n_slots - 1:
            x = jnp.where(k == n_steps - 1, tail_ref[0], x)
        scs.append(_dot(q, x[0].reshape(KV_WIDTH, PAGE).astype(BF)))
        vts.append(x[1].reshape(KV_WIDTH, PAGE).astype(BF))
    sc = jnp.concatenate(scs, axis=1)
    nk = n_slots * PAGE
    kidx = k * nk + lax.broadcasted_iota(jnp.int32, (1, nk), 1)
    picked = _dot(ind_ref[k], expand_ref[...])
    sc = jnp.where(picked > 0.5, jnp.where(kidx <= t, sc, NEG), NEG)
    m_old = m_ref[:, :1]
    m_new = jnp.maximum(m_old, jnp.max(sc, axis=1, keepdims=True))
    alpha = jnp.exp2(m_old - m_new)
    pe = jnp.exp2(sc - m_new)
    acc = acc_ref[...] * alpha
    for i in range(n_slots):
        acc = acc + _dot_nt(pe[:, i * PAGE:(i + 1) * PAGE].astype(BF), vts[i])
    l_new = l_ref[:, :1] * alpha + jnp.sum(pe, axis=1, keepdims=True)
    m_ref[...] = jnp.broadcast_to(m_new, m_ref.shape)
    l_ref[...] = jnp.broadcast_to(l_new, l_ref.shape)
    acc_ref[...] = acc

    @pl.when(k == n_steps - 1)
    def _():
        o_sel = acc * jnp.where(l_new > 0.0, 1.0 / l_new, 0.0)
        gts = gt_ref[0]
        o_ref[0] = gts[:, 0:1] * oc_ref[...] + gts[:, 1:2] * o_sel + gts[:, 2:3] * ow_ref[...]


def _nsa_sample(q_bd, gates, pool_t, table, tail_t, kct, vct, kwt, vwt, *, t_base, wpos0, n_slc, t_q):
    b, n_real = table.shape
    n_pages = n_real + 1
    n_slots = max(p for p in range(1, 17) if n_pages % p == 0)
    n_steps = n_pages // n_slots
    nsl = -(-n_slc // LANES) * LANES
    page_shape = pool_t.shape[1:]
    zeros = (0,) * len(page_shape)

    def page_spec(slot):
        return pl.BlockSpec(
            (1,) + page_shape,
            lambda i, k, pt: (pt[i * n_real + jnp.minimum(k * n_slots + slot, n_real - 1)],) + zeros)

    def per_batch(x):
        return pl.BlockSpec((1,) + x.shape[1:], lambda i, k, pt: (i,) + (0,) * (x.ndim - 1))

    acc_shape = pltpu.VMEM((SAMPLE_ROWS, KV_WIDTH), F32)
    stat_shape = pltpu.VMEM((SAMPLE_ROWS, LANES), F32)
    return pl.pallas_call(
        functools.partial(_nsa_sample_kernel, n_slots=n_slots, t_base=t_base, wpos0=wpos0, n_slc=n_slc, t_q=t_q),
        grid_spec=pltpu.PrefetchScalarGridSpec(
            num_scalar_prefetch=1, grid=(b, n_steps),
            in_specs=[page_spec(slot) for slot in range(n_slots)]
            + [per_batch(x) for x in (tail_t, q_bd, gates, kct, vct, kwt, vwt)],
            out_specs=pl.BlockSpec((1, SAMPLE_ROWS, KV_WIDTH), lambda i, k, pt: (i, 0, 0)),
            scratch_shapes=[pltpu.VMEM((n_steps, SAMPLE_ROWS, LANES), BF), pltpu.VMEM((LANES, n_slots * PAGE), BF),
                            stat_shape, stat_shape, acc_shape, acc_shape, acc_shape]),
        out_shape=jax.ShapeDtypeStruct((b, SAMPLE_ROWS, KV_WIDTH), F32),
        compiler_params=_cparams(("parallel", "arbitrary"), VMEM_LIMIT),
        name="nsa_sample",
    )(table.reshape(-1), *([pool_t] * n_slots), tail_t, q_bd, gates, kct, vct, kwt, vwt)


def _rope_tables(pos):
    half = ROPE_DIM // 2
    inv_freq = ROPE_THETA ** (-jnp.arange(half, dtype=F32) / half)
    ang = pos.astype(F32)[:, None] * inv_freq[None, :]
    cos, sin = jnp.cos(ang), jnp.sin(ang)
    n = pos.shape[0]
    rest = HEAD_DIM - ROPE_DIM
    c = jnp.concatenate([cos, cos, jnp.ones((n, rest), F32)], axis=1)
    a = jnp.concatenate([-sin, jnp.zeros((n, half + rest), F32)], axis=1)
    b = jnp.concatenate([jnp.zeros((n, half), F32), sin, jnp.zeros((n, rest), F32)], axis=1)
    reps = LANES // HEAD_DIM
    return tuple(jnp.tile(x, (1, reps)) for x in (c, a, b))


def _compress_weights(cmp_pe, w_phi1, w_phi2):
    ratio = CMP_BLOCK // CMP_STRIDE
    w1 = w_phi1.reshape(ratio, CMP_STRIDE, 2, HEAD_DIM, CMP_HIDDEN)
    eye2 = jnp.eye(2, dtype=w_phi1.dtype)
    big = jnp.einsum('jscde,hg->cjshdge', w1, eye2)
    big = big.reshape(2, ratio, CMP_STRIDE * 2 * HEAD_DIM, 2 * CMP_HIDDEN).astype(BF)
    big = jnp.concatenate([big[:, jj] for jj in range(ratio)], axis=-1)
    pe = cmp_pe.reshape(ratio, CMP_STRIDE, 2, HEAD_DIM)
    pe = jnp.broadcast_to(pe.transpose(2, 0, 1, 3)[:, :, :, None, :], (2, ratio, CMP_STRIDE, 2, HEAD_DIM))
    pe = pe.reshape(2, ratio, 1, CMP_STRIDE * 2 * HEAD_DIM)
    pe = jnp.broadcast_to(pe, (2, ratio, 8, pe.shape[-1]))
    eye4 = jnp.eye(KV_HEADS, dtype=w_phi2.dtype).reshape(2, 2, KV_HEADS)
    w2 = jnp.einsum('ced,pgh->cpgehd', w_phi2, eye4)
    w2 = w2.reshape(2, 2, 2 * CMP_HIDDEN, KV_WIDTH).astype(BF)
    return big, pe, w2


def _b_in_weight(w):
    n_gate = w.shape[1] - NSA_WIDTH - MEM_WIDTH
    gates = jnp.pad(w[:, NSA_WIDTH:NSA_WIDTH + n_gate], ((0, 0), (0, GATE_PAD - n_gate)))
    return jnp.concatenate([w[:, :NSA_WIDTH], w[:, NSA_WIDTH + n_gate:], gates], axis=1).astype(BF)


def _pad_rows(x, n):
    return jnp.pad(x, ((0, 0), (0, n - x.shape[1]), (0, 0)))


def _tail_page_t(new_rows):
    b = new_rows.shape[0]
    return _pad_rows(new_rows, PAGE).reshape(b, PAGE, 2, KV_HEADS, HEAD_DIM).transpose(0, 2, 3, 4, 1)


def _rows_t(rows):
    b, n, _ = rows.shape
    kv = rows.reshape(b, n, 2, KV_WIDTH).transpose(0, 2, 3, 1).astype(BF)
    return kv[:, 0], kv[:, 1]


def _row_tiles(s):
    return {"proj": min(s, 256), "gate": min(s, 512), "mlp": min(s, 512), "mem": min(s, 512)}


SAMPLE_MEM_SETS = 8


def _mem_attn_rows(qm, mem_kv, rows_per_set):
    return _mem_attn(qm, mem_kv, tq=SAMPLE_MEM_SETS * rows_per_set, sets_per_step=SAMPLE_MEM_SETS)


def _trunk_a(x, mem_kv, p, l, *, chunk, tiles, emit_v, per_row_sets):
    b, s, _ = x.shape
    if chunk == PAGE:
        ws, bs = p["w_spatial"][l], jnp.repeat(p["b_spatial"][l].T, A_GROUP_DIM, axis=1)
    else:
        reps = PAGE // chunk
        ws = jnp.tile(p["w_spatial"][l][:, :chunk, :chunk], (1, reps, reps))
        bs = jnp.tile(jnp.repeat(p["b_spatial"][l].T[:chunk], A_GROUP_DIM, axis=1), (reps, 1))
    res = _a_in(x, p["w_in_a"][l], p["ln_v_g"][l][None], p["ln_v_b"][l][None], ws, bs,
                chunk=chunk, tm=tiles["gate"], emit_v=emit_v)
    mixed, qm = res[0], res[1]
    om = _mem_attn_rows(qm, mem_kv, chunk) if per_row_sets else _mem_attn(qm, mem_kv, tq=tiles["mem"])
    x = _mlp_block(x, mixed, om, p["w_out_a"][l], p, l, tm=tiles["mlp"], a_transposed=False)
    return x, (res[2] if emit_v else None)


def _mlp_block(x, a, om, wo, p, l, *, tm, a_transposed):
    return _out_mlp(x, a, om, wo, p["ln1_g"][l][None], p["ln1_b"][l][None], p["w_up"][l], p["w_down"][l],
                    p["ln2_g"][l][None], p["ln2_b"][l][None], tm=tm, a_transposed=a_transposed)


def kernel(x_prompt, x_sample, cache_cmp_kv, cache_slc_kv, cache_win_kv, cache_mem_kv, page_table, mem_prompt,
           w_in_a, ln_v_g, ln_v_b, w_spatial, b_spatial, w_out_a, w_in_b, w_out_b, w_kv_shared, cmp_pe,
           w_phi1, w_phi2, w_mem_kv, ln1_g, ln1_b, ln2_g, ln2_b, w_up, w_down):
    bp, s, d = x_prompt.shape
    bs, t, _ = x_sample.shape
    n_pages = page_table.shape[1]
    past = n_pages * PAGE
    n_buf = cache_win_kv.shape[1]
    assert bs * t == PAGE and PAGE % t == 0 and s % PAGE == 0 and n_buf % PAGE == 0

    p = {"w_in_a": w_in_a.astype(BF), "ln_v_g": ln_v_g, "ln_v_b": ln_v_b, "w_spatial": w_spatial,
         "b_spatial": b_spatial, "w_out_a": w_out_a.astype(BF), "w_out_b": w_out_b.astype(BF),
         "ln1_g": ln1_g, "ln1_b": ln1_b, "ln2_g": ln2_g, "ln2_b": ln2_b,
         "w_up": w_up.astype(BF), "w_down": w_down.astype(BF)}
    w_in_b2 = [_b_in_weight(w_in_b[i]) for i in range(w_in_b.shape[0])]
    w_kv = w_kv_shared.astype(BF)
    cw1, cpe, cw2 = _compress_weights(cmp_pe, w_phi1, w_phi2)

    tl = _row_tiles(s)
    mem_kv_prompt = _mem_kv_proj(mem_prompt, w_mem_kv)
    x = x_prompt
    for l in range(N_A):
        x, _ = _trunk_a(x, mem_kv_prompt[l], p, l, chunk=PAGE, tiles=tl, emit_v=False, per_row_sets=False)
    tabs_p = _rope_tables(jnp.arange(s, dtype=jnp.int32))
    cmp_rows, slc_rows, win_rows, ks, vst, kw, vwt = _kv_shared(x, w_kv, tabs_p, tm=tl["proj"], attn_layouts=True)
    n_pg = s // PAGE
    ident = jnp.arange(bp * n_pg, dtype=jnp.int32).reshape(bp, n_pg)
    cmp_blk = _compress(cmp_rows.reshape(bp * n_pg, PAGE, ROW_WIDTH), ident, None, cw1, cpe, cw2)
    n_cmp = s // CMP_STRIDE - 1
    ncp = -(-n_cmp // PAGE) * PAGE
    kc, vct = _rows_layout(_pad_rows(cmp_blk[:, 1:1 + n_cmp], ncp))
    for i in range(DEPTH - N_A):
        l = N_A + i
        q, qm, gt = _b_in(x, w_in_b2[i], tabs_p, tm=tl["proj"])
        om = _mem_attn(qm, mem_kv_prompt[l], tq=tl["mem"])
        ot = _nsa(q, gt, kc, vct, ks, vst, kw, vwt, t_base=0, wpos0=0, n_slc=s // SEL_BLOCK)
        x = _mlp_block(x, ot, om, p["w_out_b"][i], p, l, tm=tl["mlp"], a_transposed=True)
    y_prompt = x
    shape5 = (2, KV_HEADS, HEAD_DIM)
    cmp_kv_prompt = cmp_rows.reshape(bp, s, *shape5)
    slc_kv_prompt = slc_rows.reshape(bp, s, *shape5)
    win_kv_prompt = win_rows[:, s - min(WINDOW, s):].reshape(bp, min(WINDOW, s), *shape5)
    mem_kv_prompt_out = mem_kv_prompt.reshape(DEPTH, bp, MEM_TOKENS, 2, MEM_HEADS, HEAD_DIM)

    rows = bs * t
    x = x_sample.reshape(1, rows, d)
    mem_s = cache_mem_kv.reshape(DEPTH, bs, MEM_TOKENS, 2 * MEM_WIDTH)
    v_rows = []
    for l in range(N_A):
        x, v = _trunk_a(x, mem_s[l], p, l, chunk=t, tiles=_row_tiles(rows), emit_v=True, per_row_sets=True)
        v_rows.append(v.reshape(bs, t, A_WIDTH))
    pos_s = past + (jnp.arange(rows, dtype=jnp.int32) % t)
    tabs_s = _rope_tables(pos_s)
    new_cmp, new_slc, new_win = (r.reshape(bs, t, ROW_WIDTH)
                                 for r in _kv_shared(x, w_kv, tabs_s, tm=rows, attn_layouts=False))
    cmp_pool_t = cache_cmp_kv.transpose(0, 2, 3, 4, 1)
    slc_pool_t = cache_slc_kv.transpose(0, 2, 3, 4, 1)
    cmp_blk = _compress(cmp_pool_t, page_table, _tail_page_t(new_cmp), cw1, cpe, cw2)
    n_cmp = -(-(past + t) // CMP_STRIDE) - 1
    ncp = -(-n_cmp // PAGE) * PAGE
    kct, vct = _rows_t(_pad_rows(cmp_blk[:, 1:1 + n_cmp], ncp))
    win_all = jnp.concatenate([cache_win_kv.reshape(bs, n_buf, ROW_WIDTH), new_win], axis=1)
    kwt, vwt = _rows_t(_pad_rows(win_all, n_buf + PAGE))
    slc_tail_t = _tail_page_t(new_slc)
    n_slc = -(-(past + t) // SEL_BLOCK)
    per_group = GQA * t
    eye = jnp.eye(KV_HEADS, dtype=BF)
    for i in range(DEPTH - N_A):
        l = N_A + i
        q, qm, gt = _b_in(x, w_in_b2[i], tabs_s, tm=rows)
        om = _mem_attn_rows(qm, mem_s[l], t)
        q = q.reshape(KV_HEADS, GQA, bs, t, HEAD_DIM).transpose(2, 0, 1, 3, 4).reshape(bs, KV_HEADS, per_group, HEAD_DIM)
        q = jnp.pad(q, ((0, 0), (0, 0), (0, ROWS_PER_GROUP - per_group), (0, 0)))
        q_bd = jnp.einsum('bgxd,gh->bgxhd', q, eye).reshape(bs, SAMPLE_ROWS, KV_WIDTH)
        gs = gt[0, :NSA_HEADS * 3].reshape(KV_HEADS, GQA, 3, bs, t).transpose(3, 0, 1, 4, 2)
        gs = gs.reshape(bs, KV_HEADS, per_group, 3)
        gs = jnp.pad(gs, ((0, 0), (0, 0), (0, ROWS_PER_GROUP - per_group), (0, LANES - 3)))
        o = _nsa_sample(q_bd, gs.reshape(bs, SAMPLE_ROWS, LANES), slc_pool_t, page_table, slc_tail_t,
                        kct, vct, kwt, vwt, t_base=past, wpos0=past - n_buf, n_slc=n_slc, t_q=t)
        o = o.reshape(bs, KV_HEADS, ROWS_PER_GROUP, KV_HEADS, HEAD_DIM)
        o = jnp.stack([o[:, g, :per_group, g] for g in range(KV_HEADS)], axis=1)
        o = o.reshape(bs, KV_HEADS, GQA, t, HEAD_DIM).transpose(0, 3, 1, 2, 4).reshape(1, rows, NSA_WIDTH)
        x = _mlp_block(x, o, om, p["w_out_b"][i], p, l, tm=rows, a_transposed=False)
    y_sample = x.reshape(bs, t, d)
    cmp_kv_sample = new_cmp.reshape(bs, t, *shape5)
    slc_kv_sample = new_slc.reshape(bs, t, *shape5)
    win_kv_sample = win_all[:, t:].reshape(bs, n_buf, *shape5)
    gmlp_v_sample = jnp.stack(v_rows)

    return (y_prompt, y_sample, cmp_kv_prompt, slc_kv_prompt, win_kv_prompt, mem_kv_prompt_out,
            cmp_kv_sample, slc_kv_sample, win_kv_sample, gmlp_v_sample)
```

```python
import functools
import math

import jax
import jax.numpy as jnp
from jax import lax
from jax.experimental import pallas as pl
from jax.experimental.pallas import tpu as pltpu

BF = jnp.bfloat16
F32 = jnp.float32

DEPTH = 4
N_A = 2
A_WIDTH = 768
A_GROUPS = 4
A_GROUP_DIM = 192
MEM_TOKENS = 256
MEM_HEADS = 4
MEM_WIDTH = 256
HEAD_DIM = 64
NSA_HEADS = 12
KV_HEADS = 4
GQA = 3
NSA_WIDTH = 768
KV_WIDTH = 256
ROW_WIDTH = 2 * KV_WIDTH
CMP_BLOCK = 32
CMP_STRIDE = 16
CMP_HIDDEN = 128
SEL_BLOCK = 64
SEL_TOPK = 16
WINDOW = 512
ROPE_THETA = 500000.0
ROPE_DIM = 16
D_FF_CHUNK = 1024
ALPHA = (2.0 * DEPTH) ** 0.25
LN_EPS = 1e-5
NEG = -1e30
FORCE_BONUS = 1e4
PAGE = 128
LANES = 128
GATE_PAD = 128
SEL_GROUP = 4
NSA_QUERY_TILE = 256
Q_SCALE = HEAD_DIM ** -0.5 * math.log2(math.e)
VMEM_LIMIT = 56 * 1024 * 1024


def _cparams(sem, vmem=None):
    return pltpu.CompilerParams(dimension_semantics=sem, vmem_limit_bytes=vmem)


def _const_spec(shape):
    return pl.BlockSpec(shape, lambda *_: (0,) * len(shape), pipeline_mode=pl.Buffered(1))


def _ln(x, g, b):
    mu = jnp.mean(x, axis=-1, keepdims=True)
    xc = x - mu
    var = jnp.mean(xc * xc, axis=-1, keepdims=True)
    return xc * lax.rsqrt(var + LN_EPS) * g + b


def _dot(a, b):
    return jnp.dot(a, b, preferred_element_type=F32)


def _dot_nt(a, b):
    return lax.dot_general(a, b, (((1,), (1,)), ((), ())), preferred_element_type=F32)


def _rope_cols(x, c, a, b):
    cols = []
    for j in range(x.shape[1] // LANES):
        xc = x[:, j * LANES:(j + 1) * LANES]
        up = pltpu.roll(xc, LANES - ROPE_DIM // 2, axis=1)
        dn = pltpu.roll(xc, ROPE_DIM // 2, axis=1)
        cols.append(xc * c + up * a + dn * b)
    return cols[0] if len(cols) == 1 else jnp.concatenate(cols, axis=1)


def _mem_kv_kernel(x_ref, w_ref, o_ref):
    o_ref[0] = _dot(x_ref[...].astype(BF), w_ref[0].astype(BF))


def _mem_kv_proj(mem_prompt, w_mem_kv):
    b, m, d = mem_prompt.shape
    x = mem_prompt.reshape(b * m, d)
    depth, _, n = w_mem_kv.shape
    tm = 512
    out = pl.pallas_call(
        _mem_kv_kernel,
        grid=(depth, (b * m) // tm),
        in_specs=[pl.BlockSpec((tm, d), lambda l, i: (i, 0)),
                  pl.BlockSpec((1, d, n), lambda l, i: (l, 0, 0))],
        out_specs=pl.BlockSpec((1, tm, n), lambda l, i: (l, i, 0)),
        out_shape=jax.ShapeDtypeStruct((depth, b * m, n), F32),
        compiler_params=_cparams(("parallel", "parallel")),
        name="mem_kv_proj",
    )(x, w_mem_kv)
    return out.reshape(depth, b, m, n)


def _mem_heads(q, kv, own=None):
    k = kv[:, :MEM_WIDTH].astype(BF)
    v = kv[:, MEM_WIDTH:]
    lane = lax.broadcasted_iota(jnp.int32, (1, MEM_WIDTH), 1)
    out = jnp.zeros(q.shape, F32)
    for h in range(MEM_HEADS):
        hm = (lane >= h * HEAD_DIM) & (lane < (h + 1) * HEAD_DIM)
        s = _dot_nt(jnp.where(hm, q, 0.0).astype(BF), k) * (HEAD_DIM ** -0.5)
        if own is not None:
            s = jnp.where(own, s, NEG)
        m = jnp.max(s, axis=-1, keepdims=True)
        p = jnp.exp(s - m)
        p = p / jnp.sum(p, axis=-1, keepdims=True)
        out = out + _dot(p.astype(BF), jnp.where(hm, v, 0.0).astype(BF))
    return out


def _a_in_kernel(x_ref, w_ref, g_ref, b_ref, ws_ref, bs_ref, *rest, chunk_shift, fuse_mem, emit_v):
    mem_ref = rest[0] if fuse_mem else None
    mixed_ref, qm_ref = rest[int(fuse_mem):int(fuse_mem) + 2]
    x = x_ref[0].astype(BF)
    proj = _dot(x, w_ref[...])
    uv = jax.nn.gelu(proj[:, :2 * A_WIDTH])
    u = uv[:, :A_WIDTH]
    v = _ln(uv[:, A_WIDTH:], g_ref[...], b_ref[...])
    if emit_v:
        rest[-1][0] = v
    qm = proj[:, 2 * A_WIDTH:]
    qm_ref[0] = _mem_heads(qm, mem_ref[0]) if fuse_mem else qm
    tm = x.shape[0]
    r = ws_ref.shape[1]
    row = lax.broadcasted_iota(jnp.int32, (r, r), 0)
    col = lax.broadcasted_iota(jnp.int32, (r, r), 1)
    allowed = (col <= row) & ((row >> chunk_shift) == (col >> chunk_shift))
    wgs = [jnp.where(allowed, ws_ref[g], 0.0).astype(BF) for g in range(A_GROUPS)]
    span = 2 * A_GROUP_DIM
    first = lax.broadcasted_iota(jnp.int32, (1, span), 1) < A_GROUP_DIM
    vb = v.astype(BF)
    for ci in range(tm // r):
        rows = slice(ci * r, (ci + 1) * r)
        for h in range(A_GROUPS // 2):
            cols = slice(h * span, (h + 1) * span)
            vc = vb[rows, cols]
            mix = jnp.where(first, _dot(wgs[2 * h], vc), _dot(wgs[2 * h + 1], vc)) + bs_ref[:, cols]
            mixed_ref[0, rows, cols] = (u[rows, cols] * mix).astype(BF)


def _a_in(x, w_in, ln_g, ln_b, ws, bs, *, chunk, tm, emit_v, mem_kv=None):
    b, s, d = x.shape
    n = w_in.shape[1]
    r = ws.shape[1]
    in_specs = [pl.BlockSpec((1, tm, d), lambda i, j: (i, j, 0)),
                _const_spec((d, n)), _const_spec((1, A_WIDTH)), _const_spec((1, A_WIDTH)),
                _const_spec((A_GROUPS, r, r)), _const_spec((r, A_WIDTH))]
    args = [x, w_in, ln_g, ln_b, ws, bs]
    if mem_kv is not None:
        in_specs.append(pl.BlockSpec((1,) + mem_kv.shape[1:], lambda i, j: (i, 0, 0)))
        args.append(mem_kv)
    outs = [jax.ShapeDtypeStruct((b, s, A_WIDTH), BF), jax.ShapeDtypeStruct((b, s, MEM_WIDTH), F32)]
    ospecs = [pl.BlockSpec((1, tm, A_WIDTH), lambda i, j: (i, j, 0)),
              pl.BlockSpec((1, tm, MEM_WIDTH), lambda i, j: (i, j, 0))]
    if emit_v:
        outs.append(jax.ShapeDtypeStruct((b, s, A_WIDTH), F32))
        ospecs.append(pl.BlockSpec((1, tm, A_WIDTH), lambda i, j: (i, j, 0)))
    return pl.pallas_call(
        functools.partial(_a_in_kernel, chunk_shift=int(math.log2(chunk)), fuse_mem=mem_kv is not None,
                          emit_v=emit_v),
        grid=(b, s // tm),
        in_specs=in_specs,
        out_specs=ospecs,
        out_shape=outs,
        compiler_params=_cparams(("parallel", "parallel"), VMEM_LIMIT),
        name="a_in_gate",
    )(*args)


SAMPLE_MEM_SETS = 8


def _mem_attn_kernel(q_ref, kv_ref, o_ref, *, rows_per_set):
    q = q_ref[0]
    nb = kv_ref.shape[0]
    row_set = lax.broadcasted_iota(jnp.int32, (q.shape[0], 1), 0) >> int(math.log2(rows_per_set))
    col_set = lax.broadcasted_iota(jnp.int32, (1, nb * MEM_TOKENS), 1) >> int(math.log2(MEM_TOKENS))
    o_ref[0] = _mem_heads(q, kv_ref[...].reshape(nb * MEM_TOKENS, 2 * MEM_WIDTH), row_set == col_set)


def _mem_attn_rows(q, mem_kv, rows_per_set):
    _, s, w = q.shape
    tq = SAMPLE_MEM_SETS * rows_per_set
    return pl.pallas_call(
        functools.partial(_mem_attn_kernel, rows_per_set=rows_per_set),
        grid=(s // tq,),
        in_specs=[pl.BlockSpec((1, tq, w), lambda j: (0, j, 0)),
                  pl.BlockSpec((SAMPLE_MEM_SETS, MEM_TOKENS, 2 * w), lambda j: (j, 0, 0))],
        out_specs=pl.BlockSpec((1, tq, w), lambda j: (0, j, 0)),
        out_shape=jax.ShapeDtypeStruct((1, s, w), F32),
        compiler_params=_cparams(("parallel",)),
        name="mem_attn",
    )(q, mem_kv)


def _out_mlp_kernel(x_ref, a_ref, om_ref, wo_ref, g1_ref, b1_ref, wu_ref, wd_ref, g2_ref, b2_ref, o_ref,
                    *, a_transposed):
    x = x_ref[0]
    a = a_ref[0]
    a = a.T.astype(BF) if a_transposed else a.astype(BF)
    na = a.shape[1]
    mix = _dot(a, wo_ref[:na, :]) + _dot(om_ref[0].astype(BF), wo_ref[na:, :])
    y = _ln(ALPHA * x + mix, g1_ref[...], b1_ref[...])
    yb = y.astype(BF)
    acc = jnp.zeros(x.shape, F32)
    for c in range(wu_ref.shape[1] // D_FF_CHUNK):
        h = _dot(yb, wu_ref[:, c * D_FF_CHUNK:(c + 1) * D_FF_CHUNK])
        h = jnp.square(jnp.maximum(h, 0.0)).astype(BF)
        acc = acc + _dot(h, wd_ref[c * D_FF_CHUNK:(c + 1) * D_FF_CHUNK, :])
    o_ref[0] = _ln(ALPHA * y + acc, g2_ref[...], b2_ref[...])


def _out_mlp(x, a, om, wo, g1, b1, wu, wd, g2, b2, *, tm, a_transposed):
    b, s, d = x.shape
    na = wo.shape[0] - om.shape[2]
    dff = wu.shape[1]
    if a_transposed:
        a_spec = pl.BlockSpec((1, na, tm), lambda i, j: (i, 0, j))
    else:
        a_spec = pl.BlockSpec((1, tm, na), lambda i, j: (i, j, 0))
    return pl.pallas_call(
        functools.partial(_out_mlp_kernel, a_transposed=a_transposed),
        grid=(b, s // tm),
        in_specs=[pl.BlockSpec((1, tm, d), lambda i, j: (i, j, 0)),
                  a_spec,
                  pl.BlockSpec((1, tm, om.shape[2]), lambda i, j: (i, j, 0)),
                  _const_spec(wo.shape), _const_spec((1, d)), _const_spec((1, d)),
                  _const_spec((d, dff)), _const_spec((dff, d)), _const_spec((1, d)), _const_spec((1, d))],
        out_specs=pl.BlockSpec((1, tm, d), lambda i, j: (i, j, 0)),
        out_shape=jax.ShapeDtypeStruct((b, s, d), F32),
        compiler_params=_cparams(("parallel", "parallel"), VMEM_LIMIT),
        name="out_mlp",
    )(x, a, om, wo, g1, b1, wu, wd, g2, b2)


def _kv_shared_kernel(h_ref, w_ref, c_ref, a_ref, b_ref, cmp_ref, slc_ref, win_ref, *attn_refs):
    kv = _dot(h_ref[0].astype(BF), w_ref[...])
    c, a, b = c_ref[...], a_ref[...], b_ref[...]
    tm = kv.shape[0]
    for br, ref in enumerate((cmp_ref, slc_ref, win_ref)):
        k = _rope_cols(kv[:, br * ROW_WIDTH:br * ROW_WIDTH + KV_WIDTH], c, a, b)
        v = kv[:, br * ROW_WIDTH + KV_WIDTH:(br + 1) * ROW_WIDTH]
        ref[0, :, :KV_WIDTH] = k
        ref[0, :, KV_WIDTH:] = v
        if attn_refs and br > 0:
            k_ref, vt_ref = attn_refs[2 * (br - 1):2 * br]
            vt = v.T
            if br == 1:
                lane = lax.broadcasted_iota(jnp.int32, (1, LANES), 1)
                row = pl.program_id(1) * tm + lax.broadcasted_iota(jnp.int32, (tm, 1), 0)
                block_hot = jnp.where(lane - HEAD_DIM == (row >> int(math.log2(SEL_BLOCK))), 1.0, 0.0)
            for g in range(KV_HEADS):
                if br == 1:
                    pair = k[:, (g // 2) * LANES:(g // 2 + 1) * LANES]
                    if g % 2:
                        pair = pltpu.roll(pair, HEAD_DIM, axis=1)
                    k_ref[0, g] = jnp.where(lane < HEAD_DIM, pair, block_hot).astype(BF)
                else:
                    k_ref[0, g] = k[:, g * HEAD_DIM:(g + 1) * HEAD_DIM].astype(BF)
                for u in range(tm // PAGE):
                    vt_ref[0, g, u] = vt[g * HEAD_DIM:(g + 1) * HEAD_DIM, u * PAGE:(u + 1) * PAGE].astype(BF)


def _kv_shared(h, w_kv, tabs, *, tm, attn_layouts):
    b, s, d = h.shape
    row_spec = pl.BlockSpec((1, tm, ROW_WIDTH), lambda i, j: (i, j, 0))
    tab_spec = pl.BlockSpec((tm, LANES), lambda i, j: (j, 0))
    out_specs = [row_spec, row_spec, row_spec]
    out_shape = [jax.ShapeDtypeStruct((b, s, ROW_WIDTH), F32)] * 3
    if attn_layouts:
        assert s // SEL_BLOCK <= LANES - HEAD_DIM
        vt_spec = pl.BlockSpec((1, KV_HEADS, tm // PAGE, HEAD_DIM, PAGE), lambda i, j: (i, 0, j, 0, 0))
        vt_shape = jax.ShapeDtypeStruct((b, KV_HEADS, s // PAGE, HEAD_DIM, PAGE), BF)
        for width in (LANES, HEAD_DIM):
            out_specs += [pl.BlockSpec((1, KV_HEADS, tm, width), lambda i, j: (i, 0, j, 0)), vt_spec]
            out_shape += [jax.ShapeDtypeStruct((b, KV_HEADS, s, width), BF), vt_shape]
    return pl.pallas_call(
        _kv_shared_kernel,
        grid=(b, s // tm),
        in_specs=[pl.BlockSpec((1, tm, d), lambda i, j: (i, j, 0)), _const_spec(w_kv.shape),
                  tab_spec, tab_spec, tab_spec],
        out_specs=out_specs,
        out_shape=out_shape,
        compiler_params=_cparams(("parallel", "parallel")),
        name="kv_shared",
    )(h, w_kv, *tabs)


def _b_in_kernel(x_ref, w_ref, c_ref, a_ref, b_ref, *rest, fuse_mem):
    mem_ref = rest[0] if fuse_mem else None
    q_ref, qm_ref, gt_ref = rest[int(fuse_mem):]
    proj = _dot(x_ref[0].astype(BF), w_ref[...])
    q = _rope_cols(proj[:, :NSA_WIDTH], c_ref[...], a_ref[...], b_ref[...]) * Q_SCALE
    for h in range(NSA_HEADS):
        q_ref[0, h] = q[:, h * HEAD_DIM:(h + 1) * HEAD_DIM].astype(BF)
    qm = proj[:, NSA_WIDTH:NSA_WIDTH + MEM_WIDTH]
    qm_ref[0] = _mem_heads(qm, mem_ref[0]) if fuse_mem else qm
    gt_ref[0] = jax.nn.sigmoid(proj[:, NSA_WIDTH + MEM_WIDTH:]).T


def _b_in(x, w_in, tabs, *, tm, mem_kv=None):
    b, s, d = x.shape
    tab_spec = pl.BlockSpec((tm, LANES), lambda i, j: (j, 0))
    in_specs = [pl.BlockSpec((1, tm, d), lambda i, j: (i, j, 0)), _const_spec(w_in.shape),
                tab_spec, tab_spec, tab_spec]
    args = [x, w_in, *tabs]
    if mem_kv is not None:
        in_specs.append(pl.BlockSpec((1,) + mem_kv.shape[1:], lambda i, j: (i, 0, 0)))
        args.append(mem_kv)
    return pl.pallas_call(
        functools.partial(_b_in_kernel, fuse_mem=mem_kv is not None),
        grid=(b, s // tm),
        in_specs=in_specs,
        out_specs=[pl.BlockSpec((1, NSA_HEADS, tm, HEAD_DIM), lambda i, j: (i, 0, j, 0)),
                   pl.BlockSpec((1, tm, MEM_WIDTH), lambda i, j: (i, j, 0)),
                   pl.BlockSpec((1, GATE_PAD, tm), lambda i, j: (i, 0, j))],
        out_shape=[jax.ShapeDtypeStruct((b, NSA_HEADS, s, HEAD_DIM), BF),
                   jax.ShapeDtypeStruct((b, s, MEM_WIDTH), F32),
                   jax.ShapeDtypeStruct((b, GATE_PAD, s), F32)],
        compiler_params=_cparams(("parallel", "parallel")),
        name="b_in",
    )(*args)


def _kv_layout_kernel(x_ref, k_ref, vt_ref):
    x = x_ref[0]
    vt = x[:, KV_WIDTH:].T
    for g in range(KV_HEADS):
        k_ref[0, g] = x[:, g * HEAD_DIM:(g + 1) * HEAD_DIM].astype(BF)
        vt_ref[0, g, 0] = vt[g * HEAD_DIM:(g + 1) * HEAD_DIM].astype(BF)


def _rows_layout(rows):
    b, l, _ = rows.shape
    n = l // PAGE
    return pl.pallas_call(
        _kv_layout_kernel,
        grid=(b, n),
        in_specs=[pl.BlockSpec((1, PAGE, ROW_WIDTH), lambda i, p: (i, p, 0))],
        out_specs=[pl.BlockSpec((1, KV_HEADS, PAGE, HEAD_DIM), lambda i, p: (i, 0, p, 0)),
                   pl.BlockSpec((1, KV_HEADS, 1, HEAD_DIM, PAGE), lambda i, p: (i, 0, p, 0, 0))],
        out_shape=[jax.ShapeDtypeStruct((b, KV_HEADS, l, HEAD_DIM), BF),
                   jax.ShapeDtypeStruct((b, KV_HEADS, n, HEAD_DIM, PAGE), BF)],
        compiler_params=_cparams(("parallel", "parallel")),
        name="kv_layout",
    )(rows)


def _compress_kernel(pt_ref, *refs, n_slots, tail_group):
    page_refs = refs[:n_slots]
    rest = refs[n_slots:]
    if tail_group is not None:
        tail_ref, rest = rest[0], rest[1:]
    w1_ref, pe_ref, w2_ref, o_ref, xs_ref, carry_ref = rest
    k = pl.program_id(1)
    per_page = PAGE // CMP_STRIDE
    m = n_slots * per_page
    hid2 = 2 * CMP_HIDDEN

    @pl.when(k == 0)
    def _():
        carry_ref[...] = jnp.zeros(carry_ref.shape, F32)

    for i, pr in enumerate(page_refs):
        x = pr[0]
        if tail_group is not None and i == n_slots - 1:
            x = jnp.where(k == tail_group, tail_ref[0], x)
        rows = slice(i * PAGE, (i + 1) * PAGE)
        if x.ndim == 2:
            for vcol in range(4):
                xs_ref[vcol, rows, :] = x[:, vcol * LANES:(vcol + 1) * LANES]
        else:
            for c in range(2):
                xc = x[c].reshape(KV_WIDTH, PAGE).T
                for hp in range(2):
                    xs_ref[2 * c + hp, rows, :] = xc[:, hp * LANES:(hp + 1) * LANES]

    def strides(vcol):
        return jnp.concatenate(
            [xs_ref[vcol, pl.ds(s, m, stride=CMP_STRIDE), :] for s in range(CMP_STRIDE)], axis=1)

    first_row = lax.broadcasted_iota(jnp.int32, (m, 1), 0) == 0
    out_cols = []
    for c in range(2):
        w1 = w1_ref[c]
        lhs = jnp.concatenate([strides(2 * c + hp) for hp in range(2)] + [pe_ref[c, 0], pe_ref[c, 1]],
                              axis=0).astype(BF)
        a = _dot(lhs, w1)
        hpe = a[2 * m:2 * m + 1, :hid2] + a[2 * m + 8:2 * m + 9, hid2:]
        acc = jnp.zeros((m, KV_WIDTH), F32)
        for hp in range(2):
            a0 = a[hp * m:(hp + 1) * m, :hid2]
            a1 = a[hp * m:(hp + 1) * m, hid2:]
            prev = jnp.where(first_row, carry_ref[c, hp, 0:1, :], pltpu.roll(a0, 1, axis=0))
            carry_ref[c, hp, 0:1, :] = a0[m - 1:m]
            acc = acc + _dot(jax.nn.gelu(prev + a1 + hpe).astype(BF), w2_ref[c, hp])
        out_cols.append(acc)
    o_ref[0] = jnp.concatenate(out_cols, axis=1)


def _compress(pool, table, tail, w1, pe, w2):
    b, n_real = table.shape
    n_pages = n_real + (0 if tail is None else 1)
    n_slots = max(p for p in range(1, 17) if n_pages % p == 0)
    n_groups = n_pages // n_slots
    m = n_slots * (PAGE // CMP_STRIDE)
    page_shape = pool.shape[1:]
    zeros = (0,) * len(page_shape)

    def page_spec(slot):
        return pl.BlockSpec(
            (1,) + page_shape,
            lambda i, k, pt: (pt[i * n_real + jnp.minimum(k * n_slots + slot, n_real - 1)],) + zeros)

    in_specs = [page_spec(slot) for slot in range(n_slots)]
    args = [pool] * n_slots
    if tail is not None:
        in_specs.append(pl.BlockSpec((1,) + page_shape, lambda i, k, pt: (i,) + zeros))
        args.append(tail)
    in_specs += [pl.BlockSpec(x.shape, lambda i, k, pt, nd=x.ndim: (0,) * nd) for x in (w1, pe, w2)]
    return pl.pallas_call(
        functools.partial(_compress_kernel, n_slots=n_slots,
                          tail_group=None if tail is None else n_groups - 1),
        grid_spec=pltpu.PrefetchScalarGridSpec(
            num_scalar_prefetch=1, grid=(b, n_groups), in_specs=in_specs,
            out_specs=pl.BlockSpec((1, m, ROW_WIDTH), lambda i, k, pt: (i, k, 0)),
            scratch_shapes=[pltpu.VMEM((4, n_slots * PAGE, LANES), F32),
                            pltpu.VMEM((2, 2, 8, 2 * CMP_HIDDEN), F32)]),
        out_shape=jax.ShapeDtypeStruct((b, n_groups * m, ROW_WIDTH), F32),
        compiler_params=_cparams(("parallel", "arbitrary"), VMEM_LIMIT),
        name="compress",
    )(table.reshape(-1), *args, w1, pe, w2)


def _nsa_kernel(q_ref, gt_ref, kc_ref, vct_ref, ks_ref, vst_ref, kw_ref, vwt_ref, o_ref, psum_ref,
                sbuf0_ref, sbuf1_ref, *, t_base, wpos0, n_slc, nsp):
    g = pl.program_id(1)
    i = pl.program_id(2)
    nq = q_ref.shape[2]
    nl = GQA * nq
    ncp = kc_ref.shape[2]
    gk = SEL_GROUP * PAGE
    nkt_s = vst_ref.shape[2]
    q = q_ref[0].reshape(nl, HEAD_DIM)
    lane = lax.broadcasted_iota(jnp.int32, (1, nl), 1)
    q0 = t_base + i * nq
    t = q0 + (lane & (nq - 1))

    s = _dot_nt(kc_ref[0, 0], q)
    n_idx = lax.broadcasted_iota(jnp.int32, (ncp, 1), 0)
    cmask = n_idx * CMP_STRIDE + (CMP_BLOCK - 1) <= t
    s = jnp.where(cmask, s, NEG)
    p = jnp.where(cmask, jnp.exp2(s - jnp.max(s, axis=0, keepdims=True)), 0.0)
    l = jnp.sum(p, axis=0, keepdims=True)
    p = p * jnp.where(l > 0.0, 1.0 / l, 0.0)
    o_cmp = jnp.zeros((HEAD_DIM, nl), F32)
    for kt in range(ncp // PAGE):
        o_cmp = o_cmp + _dot(vct_ref[0, 0, kt], p[kt * PAGE:(kt + 1) * PAGE].astype(BF))

    rel = (q0 - wpos0) // PAGE
    n_wt = (WINDOW + nq) // PAGE
    tile_row = lax.broadcasted_iota(jnp.int32, (PAGE, 1), 0)
    parts = []
    for u in range(n_wt):
        kt = rel - WINDOW // PAGE + u
        ktc = jnp.maximum(kt, 0)
        kb = kw_ref[0, 0, pl.ds(pl.multiple_of(ktc * PAGE, PAGE), PAGE), :]
        krel = kt * PAGE + tile_row
        kpos = jnp.where(krel >= 0, wpos0 + krel, -(1 << 30))
        d = lax.bitcast_convert_type(t - kpos, jnp.uint32)
        parts.append(jnp.where(d < WINDOW, _dot_nt(kb, q), NEG))
    sw = jnp.concatenate(parts, axis=0)
    pw = jnp.exp2(sw - jnp.max(sw, axis=0, keepdims=True))
    l_w = jnp.sum(pw, axis=0, keepdims=True)
    acc_w = jnp.zeros((HEAD_DIM, nl), F32)
    for u in range(n_wt):
        ktc = jnp.maximum(rel - WINDOW // PAGE + u, 0)
        acc_w = acc_w + _dot(vwt_ref[0, 0, ktc], pw[u * PAGE:(u + 1) * PAGE].astype(BF))
    o_win = acc_w * (1.0 / l_w)

    psum = p[:, :nq]
    for r in range(1, GQA):
        psum = psum + p[:, r * nq:(r + 1) * nq]
    front = 8
    ratio = SEL_BLOCK // CMP_STRIDE
    slabs = []
    for sl in range(nq // LANES):
        psum_ref[sl, :front, :] = jnp.zeros((front, LANES), F32)
        psum_ref[sl, front:front + ncp, :] = psum[:, sl * LANES:(sl + 1) * LANES]
        psum_ref[sl, front + ncp:, :] = jnp.zeros((psum_ref.shape[1] - front - ncp, LANES), F32)
        part = psum_ref[sl, pl.ds(front - 1, nsp, stride=ratio), :]
        for d in range(1, ratio + 1):
            part = part + psum_ref[sl, pl.ds(front - 1 + d, nsp, stride=ratio), :]
        slabs.append(part)
    imp = slabs[0] if len(slabs) == 1 else jnp.concatenate(slabs, axis=1)
    tq = t[:, :nq]
    j = lax.broadcasted_iota(jnp.int32, (nsp, 1), 0)
    cur = tq >> int(math.log2(SEL_BLOCK))
    forced = (j == 0) | (j == cur) | (j == cur - 1)
    score = jnp.where(j * SEL_BLOCK <= tq, imp + FORCE_BONUS * forced.astype(F32), NEG)
    sub = 8
    chunks = [score[c * sub:(c + 1) * sub] for c in range(nsp // sub)]
    jc = j[:sub]
    cnts = [jnp.zeros((sub, nq), F32) for _ in chunks]
    for ii in range(n_slc):
        ri = score[ii:ii + 1, :]
        for c, sc_c in enumerate(chunks):
            if c * sub > ii:
                one = jnp.where(ri >= sc_c, 1.0, 0.0)
            elif c * sub + sub - 1 <= ii:
                one = jnp.where(ri > sc_c, 1.0, 0.0)
            else:
                one = jnp.where(jc + c * sub > ii, jnp.where(ri >= sc_c, 1.0, 0.0), jnp.where(ri > sc_c, 1.0, 0.0))
            cnts[c] = cnts[c] + one
    cnt = jnp.concatenate(cnts, axis=0)
    bias = jnp.where(cnt < float(SEL_TOPK), jnp.where(score > NEG / 2, 0.0, NEG), NEG)
    bias_t = jnp.concatenate([bias, jnp.zeros((LANES - nsp, nq), F32)], axis=0).T[:, :LANES - HEAD_DIM]
    q_ext = jnp.concatenate([q, jnp.concatenate([bias_t.astype(BF)] * GQA, axis=0)], axis=1)

    key_row = lax.broadcasted_iota(jnp.int32, (gk, 1), 0)

    def fill(buf, gi):
        kb = ks_ref[0, 0, pl.ds(pl.multiple_of(gi * gk, gk), gk), :]
        buf[...] = _dot_nt(kb, q_ext)

    def consume(buf, gi, carry, causal):
        m_old, l_old, acc = carry
        sc = buf[...]
        if causal:
            sc = jnp.where(gi * gk + key_row <= t, sc, NEG)
        m_new = jnp.maximum(m_old, jnp.max(sc, axis=0, keepdims=True))
        alpha = jnp.exp2(m_old - m_new)
        pe = jnp.exp2(sc - m_new)
        acc = acc * alpha
        for u in range(SEL_GROUP):
            acc = acc + _dot(vst_ref[0, 0, gi * SEL_GROUP + u], pe[u * PAGE:(u + 1) * PAGE].astype(BF))
        return m_new, l_old * alpha + jnp.sum(pe, axis=0, keepdims=True), acc

    init = (jnp.full((1, nl), NEG, F32), jnp.zeros((1, nl), F32), jnp.zeros((HEAD_DIM, nl), F32))

    n_groups = jnp.minimum(nkt_s - 1, (q0 + nq - 1) // PAGE) // SEL_GROUP + 1
    fill(sbuf0_ref, 0)

    def pair_body(kk, carry):
        gi = 2 * kk
        fill(sbuf1_ref, gi + 1)
        carry = consume(sbuf0_ref, gi, carry, False)
        fill(sbuf0_ref, gi + 2)
        return consume(sbuf1_ref, gi + 1, carry, False)

    n_pairs = (n_groups - 1) // 2
    carry = lax.fori_loop(0, n_pairs, pair_body, init)
    gi = 2 * n_pairs

    def odd_tail(carry):
        fill(sbuf1_ref, gi + 1)
        carry = consume(sbuf0_ref, gi, carry, False)
        return consume(sbuf1_ref, gi + 1, carry, True)

    def even_tail(carry):
        return consume(sbuf0_ref, gi, carry, True)

    _, l_s, acc_s = lax.cond(n_groups - 1 - gi == 1, odd_tail, even_tail, carry)

    o_sel = acc_s * jnp.where(l_s > 0.0, 1.0 / l_s, 0.0)
    for r in range(GQA):
        sl = slice(r * nq, (r + 1) * nq)
        grow = (GQA * g + r) * 3
        o = (gt_ref[0, pl.ds(grow, 1), :] * o_cmp[:, sl]
             + gt_ref[0, pl.ds(grow + 1, 1), :] * o_sel[:, sl]
             + gt_ref[0, pl.ds(grow + 2, 1), :] * o_win[:, sl])
        o_ref[0, r * HEAD_DIM:(r + 1) * HEAD_DIM, :] = o


def _nsa(q, gt, kc, vct, ks, vst, kw, vwt, *, t_base, wpos0, n_slc):
    b, _, sq, _ = q.shape
    nq = NSA_QUERY_TILE
    assert sq % nq == 0 and t_base % PAGE == 0 and (t_base - wpos0) % PAGE == 0 and t_base >= wpos0
    assert vst.shape[2] % SEL_GROUP == 0
    nsp = max(-(-n_slc // 8) * 8, vst.shape[2] * (PAGE // SEL_BLOCK))
    assert nsp <= LANES - HEAD_DIM and ks.shape[3] == LANES
    ncp = kc.shape[2]
    psum_rows = 8 + max(ncp + 8, (SEL_BLOCK // CMP_STRIDE) * nsp + 8)
    psum_rows = -(-psum_rows // 8) * 8

    def kv_specs(k, vt):
        return [pl.BlockSpec((1, 1) + k.shape[2:], lambda bi, gi, ii: (bi, gi, 0, 0)),
                pl.BlockSpec((1, 1) + vt.shape[2:], lambda bi, gi, ii: (bi, gi, 0, 0, 0))]

    return pl.pallas_call(
        functools.partial(_nsa_kernel, t_base=t_base, wpos0=wpos0, n_slc=n_slc, nsp=nsp),
        grid=(b, KV_HEADS, sq // nq),
        in_specs=[pl.BlockSpec((1, GQA, nq, HEAD_DIM), lambda bi, gi, ii: (bi, gi, ii, 0)),
                  pl.BlockSpec((1, GATE_PAD, nq), lambda bi, gi, ii: (bi, 0, ii))]
        + kv_specs(kc, vct) + kv_specs(ks, vst) + kv_specs(kw, vwt),
        out_specs=pl.BlockSpec((1, GQA * HEAD_DIM, nq), lambda bi, gi, ii: (bi, gi, ii)),
        out_shape=jax.ShapeDtypeStruct((b, NSA_WIDTH, sq), F32),
        scratch_shapes=[pltpu.VMEM((nq // LANES, psum_rows, LANES), F32),
                        pltpu.VMEM((SEL_GROUP * PAGE, GQA * nq), F32), pltpu.VMEM((SEL_GROUP * PAGE, GQA * nq), F32)],
        compiler_params=_cparams(("parallel", "parallel", "parallel"), VMEM_LIMIT),
        name="nsa_attend",
    )(q, gt, kc, vct, ks, vst, kw, vwt)


SAMPLE_ROWS = 64
ROWS_PER_GROUP = SAMPLE_ROWS // KV_HEADS


def _nsa_sample_kernel(pt_ref, *refs, n_slots, t_base, wpos0, n_slc, t_q):
    page_refs = refs[:n_slots]
    (tail_ref, q_ref, gt_ref, kct_ref, vct_ref, kwt_ref, vwt_ref, o_ref,
     ind_ref, expand_ref, m_ref, l_ref, acc_ref, oc_ref, ow_ref) = refs[n_slots:]
    k = pl.program_id(1)
    n_steps = pl.num_programs(1)
    q = q_ref[0]
    row = lax.broadcasted_iota(jnp.int32, (SAMPLE_ROWS, 1), 0)
    t = t_base + (row & (t_q - 1))

    @pl.when(k == 0)
    def _():
        ncp = kct_ref.shape[2]
        s = _dot(q, kct_ref[0])
        n_idx = lax.broadcasted_iota(jnp.int32, (1, ncp), 1)
        cmask = n_idx * CMP_STRIDE + (CMP_BLOCK - 1) <= t
        s = jnp.where(cmask, s, NEG)
        p = jnp.where(cmask, jnp.exp2(s - jnp.max(s, axis=1, keepdims=True)), 0.0)
        l = jnp.sum(p, axis=1, keepdims=True)
        p = p * jnp.where(l > 0.0, 1.0 / l, 0.0)
        oc_ref[...] = _dot_nt(p.astype(BF), vct_ref[0])

        parts = []
        for g in range(KV_HEADS):
            base = g * ROWS_PER_GROUP
            acc = p[base:base + t_q]
            for r in range(1, GQA):
                acc = acc + p[base + r * t_q:base + (r + 1) * t_q]
            parts.append(acc)
        psum = jnp.concatenate(parts, axis=0)
        nsl = -(-n_slc // LANES) * LANES
        n_col = lax.broadcasted_iota(jnp.int32, (ncp, 1), 0)
        j_row = lax.broadcasted_iota(jnp.int32, (1, nsl), 1)
        ratio = SEL_BLOCK // CMP_STRIDE
        overlap = jnp.where((n_col >= ratio * j_row - 1) & (n_col <= ratio * j_row + ratio - 1), 1.0, 0.0).astype(BF)
        hi = psum.astype(BF)
        mid = (psum - hi.astype(F32)).astype(BF)
        lo = (psum - hi.astype(F32) - mid.astype(F32)).astype(BF)
        imp = _dot(hi, overlap) + _dot(mid, overlap) + _dot(lo, overlap)
        nsel_rows = KV_HEADS * t_q
        tq = t_base + (lax.broadcasted_iota(jnp.int32, (nsel_rows, 1), 0) & (t_q - 1))
        cur = tq >> int(math.log2(SEL_BLOCK))
        forced = (j_row == 0) | (j_row == cur) | (j_row == cur - 1)
        score = jnp.where(j_row * SEL_BLOCK <= tq, imp + FORCE_BONUS * forced.astype(F32), NEG)
        score_t = jnp.concatenate([score, jnp.full((LANES - nsel_rows, nsl), NEG, F32)], axis=0).T
        sub = 8
        nsp = -(-n_slc // sub) * sub
        jc = lax.broadcasted_iota(jnp.int32, (sub, 1), 0)
        chunks = [score_t[c * sub:(c + 1) * sub] for c in range(nsp // sub)]
        cnts = [jnp.zeros((sub, LANES), F32) for _ in chunks]
        for ii in range(n_slc):
            ri = score_t[ii:ii + 1, :]
            for c, sc_c in enumerate(chunks):
                if c * sub > ii:
                    one = jnp.where(ri >= sc_c, 1.0, 0.0)
                elif c * sub + sub - 1 <= ii:
                    one = jnp.where(ri > sc_c, 1.0, 0.0)
                else:
                    one = jnp.where(jc + c * sub > ii, jnp.where(ri >= sc_c, 1.0, 0.0), jnp.where(ri > sc_c, 1.0, 0.0))
                cnts[c] = cnts[c] + one
        chosen = [jnp.where(cn < float(SEL_TOPK), jnp.where(sc_c > NEG / 2, 1.0, 0.0), 0.0)
                  for cn, sc_c in zip(cnts, chunks)]
        chosen_t = jnp.concatenate(chosen + [jnp.zeros((nsl - nsp, LANES), F32)], axis=0)
        chosen_rows = chosen_t.T.astype(BF)
        col = lax.broadcasted_iota(jnp.int32, (1, LANES), 1)
        src = (row >> int(math.log2(ROWS_PER_GROUP))) * t_q + (row & (t_q - 1))
        expand = jnp.where(col == src, 1.0, 0.0).astype(BF)
        ind = _dot(expand, chosen_rows).astype(BF)
        blocks_per_step = n_slots * (PAGE // SEL_BLOCK)
        j_all = lax.broadcasted_iota(jnp.int32, (nsl, 1), 0)
        for st in range(ind_ref.shape[0]):
            pick = jnp.where((j_all == st * blocks_per_step + col) & (col < blocks_per_step), 1.0, 0.0).astype(BF)
            ind_ref[st] = _dot(ind, pick).astype(BF)
        key_lane = lax.broadcasted_iota(jnp.int32, (1, n_slots * PAGE), 1)
        local_block = lax.broadcasted_iota(jnp.int32, (LANES, 1), 0)
        expand_ref[...] = jnp.where(local_block == (key_lane >> int(math.log2(SEL_BLOCK))), 1.0, 0.0).astype(BF)

        nw = kwt_ref.shape[2]
        sw = _dot(q, kwt_ref[0])
        kpos = wpos0 + lax.broadcasted_iota(jnp.int32, (1, nw), 1)
        d = lax.bitcast_convert_type(t - kpos, jnp.uint32)
        sw = jnp.where(d < WINDOW, sw, NEG)
        pw = jnp.exp2(sw - jnp.max(sw, axis=1, keepdims=True))
        lw = jnp.sum(pw, axis=1, keepdims=True)
        ow_ref[...] = _dot_nt(pw.astype(BF), vwt_ref[0]) * (1.0 / lw)

        m_ref[...] = jnp.full(m_ref.shape, NEG, F32)
        l_ref[...] = jnp.zeros(l_ref.shape, F32)
        acc_ref[...] = jnp.zeros(acc_ref.shape, F32)

    vts = []
    scs = []
    for i, pr in enumerate(page_refs):
        x = pr[0]
        if i == n_slots - 1:
            x = jnp.where(k == n_steps - 1, tail_ref[0], x)
        scs.append(_dot(q, x[0].reshape(KV_WIDTH, PAGE).astype(BF)))
        vts.append(x[1].reshape(KV_WIDTH, PAGE).astype(BF))
    sc = jnp.concatenate(scs, axis=1)
    nk = n_slots * PAGE
    kidx = k * nk + lax.broadcasted_iota(jnp.int32, (1, nk), 1)
    picked = _dot(ind_ref[k], expand_ref[...])
    sc = jnp.where(picked > 0.5, jnp.where(kidx <= t, sc, NEG), NEG)
    m_old = m_ref[:, :1]
    m_new = jnp.maximum(m_old, jnp.max(sc, axis=1, keepdims=True))
    alpha = jnp.exp2(m_old - m_new)
    pe = jnp.exp2(sc - m_new)
    acc = acc_ref[...] * alpha
    for i in range(n_slots):
        acc = acc + _dot_nt(pe[:, i * PAGE:(i + 1) * PAGE].astype(BF), vts[i])
    l_new = l_ref[:, :1] * alpha + jnp.sum(pe, axis=1, keepdims=True)
    m_ref[...] = jnp.broadcast_to(m_new, m_ref.shape)
    l_ref[...] = jnp.broadcast_to(l_new, l_ref.shape)
    acc_ref[...] = acc

    @pl.when(k == n_steps - 1)
    def _():
        o_sel = acc * jnp.where(l_new > 0.0, 1.0 / l_new, 0.0)
        gts = gt_ref[0]
        o_ref[0] = gts[:, 0:1] * oc_ref[...] + gts[:, 1:2] * o_sel + gts[:, 2:3] * ow_ref[...]


def _nsa_sample(q_bd, gates, pool_t, table, tail_t, kct, vct, kwt, vwt, *, t_base, wpos0, n_slc, t_q):
    b, n_real = table.shape
    n_pages = n_real + 1
    n_slots = max(p for p in range(1, 17) if n_pages % p == 0)
    n_steps = n_pages // n_slots
    nsl = -(-n_slc // LANES) * LANES
    page_shape = pool_t.shape[1:]
    zeros = (0,) * len(page_shape)

    def page_spec(slot):
        return pl.BlockSpec(
            (1,) + page_shape,
            lambda i, k, pt: (pt[i * n_real + jnp.minimum(k * n_slots + slot, n_real - 1)],) + zeros)

    def per_batch(x):
        return pl.BlockSpec((1,) + x.shape[1:], lambda i, k, pt: (i,) + (0,) * (x.ndim - 1))

    acc_shape = pltpu.VMEM((SAMPLE_ROWS, KV_WIDTH), F32)
    stat_shape = pltpu.VMEM((SAMPLE_ROWS, LANES), F32)
    return pl.pallas_call(
        functools.partial(_nsa_sample_kernel, n_slots=n_slots, t_base=t_base, wpos0=wpos0, n_slc=n_slc, t_q=t_q),
        grid_spec=pltpu.PrefetchScalarGridSpec(
            num_scalar_prefetch=1, grid=(b, n_steps),
            in_specs=[page_spec(slot) for slot in range(n_slots)]
            + [per_batch(x) for x in (tail_t, q_bd, gates, kct, vct, kwt, vwt)],
            out_specs=pl.BlockSpec((1, SAMPLE_ROWS, KV_WIDTH), lambda i, k, pt: (i, 0, 0)),
            scratch_shapes=[pltpu.VMEM((n_steps, SAMPLE_ROWS, LANES), BF), pltpu.VMEM((LANES, n_slots * PAGE), BF),
                            stat_shape, stat_shape, acc_shape, acc_shape, acc_shape]),
        out_shape=jax.ShapeDtypeStruct((b, SAMPLE_ROWS, KV_WIDTH), F32),
        compiler_params=_cparams(("parallel", "arbitrary"), VMEM_LIMIT),
        name="nsa_sample",
    )(table.reshape(-1), *([pool_t] * n_slots), tail_t, q_bd, gates, kct, vct, kwt, vwt)


def _rope_tables(pos):
    half = ROPE_DIM // 2
    inv_freq = ROPE_THETA ** (-jnp.arange(half, dtype=F32) / half)
    ang = pos.astype(F32)[:, None] * inv_freq[None, :]
    cos, sin = jnp.cos(ang), jnp.sin(ang)
    n = pos.shape[0]
    rest = HEAD_DIM - ROPE_DIM
    c = jnp.concatenate([cos, cos, jnp.ones((n, rest), F32)], axis=1)
    a = jnp.concatenate([-sin, jnp.zeros((n, half + rest), F32)], axis=1)
    b = jnp.concatenate([jnp.zeros((n, half), F32), sin, jnp.zeros((n, rest), F32)], axis=1)
    reps = LANES // HEAD_DIM
    return tuple(jnp.tile(x, (1, reps)) for x in (c, a, b))


def _compress_weights(cmp_pe, w_phi1, w_phi2):
    ratio = CMP_BLOCK // CMP_STRIDE
    w1 = w_phi1.reshape(ratio, CMP_STRIDE, 2, HEAD_DIM, CMP_HIDDEN)
    eye2 = jnp.eye(2, dtype=w_phi1.dtype)
    big = jnp.einsum('jscde,hg->cjshdge', w1, eye2)
    big = big.reshape(2, ratio, CMP_STRIDE * 2 * HEAD_DIM, 2 * CMP_HIDDEN).astype(BF)
    big = jnp.concatenate([big[:, jj] for jj in range(ratio)], axis=-1)
    pe = cmp_pe.reshape(ratio, CMP_STRIDE, 2, HEAD_DIM)
    pe = jnp.broadcast_to(pe.transpose(2, 0, 1, 3)[:, :, :, None, :], (2, ratio, CMP_STRIDE, 2, HEAD_DIM))
    pe = pe.reshape(2, ratio, 1, CMP_STRIDE * 2 * HEAD_DIM)
    pe = jnp.broadcast_to(pe, (2, ratio, 8, pe.shape[-1]))
    eye4 = jnp.eye(KV_HEADS, dtype=w_phi2.dtype).reshape(2, 2, KV_HEADS)
    w2 = jnp.einsum('ced,pgh->cpgehd', w_phi2, eye4)
    w2 = w2.reshape(2, 2, 2 * CMP_HIDDEN, KV_WIDTH).astype(BF)
    return big, pe, w2


def _b_in_weight(w):
    n_gate = w.shape[1] - NSA_WIDTH - MEM_WIDTH
    gates = jnp.pad(w[:, NSA_WIDTH:NSA_WIDTH + n_gate], ((0, 0), (0, GATE_PAD - n_gate)))
    return jnp.concatenate([w[:, :NSA_WIDTH], w[:, NSA_WIDTH + n_gate:], gates], axis=1).astype(BF)


def _pad_rows(x, n):
    return jnp.pad(x, ((0, 0), (0, n - x.shape[1]), (0, 0)))


def _tail_page_t(new_rows):
    b = new_rows.shape[0]
    return _pad_rows(new_rows, PAGE).reshape(b, PAGE, 2, KV_HEADS, HEAD_DIM).transpose(0, 2, 3, 4, 1)


def _rows_t(rows):
    b, n, _ = rows.shape
    kv = rows.reshape(b, n, 2, KV_WIDTH).transpose(0, 2, 3, 1).astype(BF)
    return kv[:, 0], kv[:, 1]


def _row_tiles(s):
    return {"proj": min(s, 256), "gate": min(s, 512), "mlp": min(s, 512)}


def _trunk_a(x, mem_kv, p, l, *, chunk, tiles, emit_v, per_row_sets):
    if chunk == PAGE:
        ws, bs = p["w_spatial"][l], jnp.repeat(p["b_spatial"][l].T, A_GROUP_DIM, axis=1)
    else:
        reps = PAGE // chunk
        ws = jnp.tile(p["w_spatial"][l][:, :chunk, :chunk], (1, reps, reps))
        bs = jnp.tile(jnp.repeat(p["b_spatial"][l].T[:chunk], A_GROUP_DIM, axis=1), (reps, 1))
    res = _a_in(x, p["w_in_a"][l], p["ln_v_g"][l][None], p["ln_v_b"][l][None], ws, bs,
                chunk=chunk, tm=tiles["gate"], emit_v=emit_v, mem_kv=None if per_row_sets else mem_kv)
    mixed, om = res[0], res[1]
    if per_row_sets:
        om = _mem_attn_rows(om, mem_kv, chunk)
    x = _mlp_block(x, mixed, om, p["w_out_a"][l], p, l, tm=tiles["mlp"], a_transposed=False)
    return x, (res[2] if emit_v else None)


def _mlp_block(x, a, om, wo, p, l, *, tm, a_transposed):
    return _out_mlp(x, a, om, wo, p["ln1_g"][l][None], p["ln1_b"][l][None], p["w_up"][l], p["w_down"][l],
                    p["ln2_g"][l][None], p["ln2_b"][l][None], tm=tm, a_transposed=a_transposed)


def kernel(x_prompt, x_sample, cache_cmp_kv, cache_slc_kv, cache_win_kv, cache_mem_kv, page_table, mem_prompt,
           w_in_a, ln_v_g, ln_v_b, w_spatial, b_spatial, w_out_a, w_in_b, w_out_b, w_kv_shared, cmp_pe,
           w_phi1, w_phi2, w_mem_kv, ln1_g, ln1_b, ln2_g, ln2_b, w_up, w_down):
    bp, s, d = x_prompt.shape
    bs, t, _ = x_sample.shape
    n_pages = page_table.shape[1]
    past = n_pages * PAGE
    n_buf = cache_win_kv.shape[1]
    assert bs * t == PAGE and PAGE % t == 0 and s % PAGE == 0 and n_buf % PAGE == 0

    p = {"w_in_a": w_in_a.astype(BF), "ln_v_g": ln_v_g, "ln_v_b": ln_v_b, "w_spatial": w_spatial,
         "b_spatial": b_spatial, "w_out_a": w_out_a.astype(BF), "w_out_b": w_out_b.astype(BF),
         "ln1_g": ln1_g, "ln1_b": ln1_b, "ln2_g": ln2_g, "ln2_b": ln2_b,
         "w_up": w_up.astype(BF), "w_down": w_down.astype(BF)}
    w_in_b2 = [_b_in_weight(w_in_b[i]) for i in range(w_in_b.shape[0])]
    w_kv = w_kv_shared.astype(BF)
    cw1, cpe, cw2 = _compress_weights(cmp_pe, w_phi1, w_phi2)

    tl = _row_tiles(s)
    mem_kv_prompt = _mem_kv_proj(mem_prompt, w_mem_kv)
    x = x_prompt
    for l in range(N_A):
        x, _ = _trunk_a(x, mem_kv_prompt[l], p, l, chunk=PAGE, tiles=tl, emit_v=False, per_row_sets=False)
    tabs_p = _rope_tables(jnp.arange(s, dtype=jnp.int32))
    cmp_rows, slc_rows, win_rows, ks, vst, kw, vwt = _kv_shared(x, w_kv, tabs_p, tm=tl["proj"], attn_layouts=True)
    n_pg = s // PAGE
    ident = jnp.arange(bp * n_pg, dtype=jnp.int32).reshape(bp, n_pg)
    cmp_blk = _compress(cmp_rows.reshape(bp * n_pg, PAGE, ROW_WIDTH), ident, None, cw1, cpe, cw2)
    n_cmp = s // CMP_STRIDE - 1
    ncp = -(-n_cmp // PAGE) * PAGE
    kc, vct = _rows_layout(_pad_rows(cmp_blk[:, 1:1 + n_cmp], ncp))
    for i in range(DEPTH - N_A):
        l = N_A + i
        q, om, gt = _b_in(x, w_in_b2[i], tabs_p, tm=tl["proj"], mem_kv=mem_kv_prompt[l])
        ot = _nsa(q, gt, kc, vct, ks, vst, kw, vwt, t_base=0, wpos0=0, n_slc=s // SEL_BLOCK)
        x = _mlp_block(x, ot, om, p["w_out_b"][i], p, l, tm=tl["mlp"], a_transposed=True)
    y_prompt = x
    shape5 = (2, KV_HEADS, HEAD_DIM)
    cmp_kv_prompt = cmp_rows.reshape(bp, s, *shape5)
    slc_kv_prompt = slc_rows.reshape(bp, s, *shape5)
    win_kv_prompt = win_rows[:, s - min(WINDOW, s):].reshape(bp, min(WINDOW, s), *shape5)
    mem_kv_prompt_out = mem_kv_prompt.reshape(DEPTH, bp, MEM_TOKENS, 2, MEM_HEADS, HEAD_DIM)

    rows = bs * t
    x = x_sample.reshape(1, rows, d)
    mem_s = cache_mem_kv.reshape(DEPTH, bs, MEM_TOKENS, 2 * MEM_WIDTH)
    v_rows = []
    for l in range(N_A):
        x, v = _trunk_a(x, mem_s[l], p, l, chunk=t, tiles=_row_tiles(rows), emit_v=True, per_row_sets=True)
        v_rows.append(v.reshape(bs, t, A_WIDTH))
    pos_s = past + (jnp.arange(rows, dtype=jnp.int32) % t)
    tabs_s = _rope_tables(pos_s)
    new_cmp, new_slc, new_win = (r.reshape(bs, t, ROW_WIDTH)
                                 for r in _kv_shared(x, w_kv, tabs_s, tm=rows, attn_layouts=False))
    cmp_pool_t = cache_cmp_kv.transpose(0, 2, 3, 4, 1)
    slc_pool_t = cache_slc_kv.transpose(0, 2, 3, 4, 1)
    cmp_blk = _compress(cmp_pool_t, page_table, _tail_page_t(new_cmp), cw1, cpe, cw2)
    n_cmp = -(-(past + t) // CMP_STRIDE) - 1
    ncp = -(-n_cmp // PAGE) * PAGE
    kct, vct = _rows_t(_pad_rows(cmp_blk[:, 1:1 + n_cmp], ncp))
    win_all = jnp.concatenate([cache_win_kv.reshape(bs, n_buf, ROW_WIDTH), new_win], axis=1)
    kwt, vwt = _rows_t(_pad_rows(win_all, n_buf + PAGE))
    slc_tail_t = _tail_page_t(new_slc)
    n_slc = -(-(past + t) // SEL_BLOCK)
    per_group = GQA * t
    eye = jnp.eye(KV_HEADS, dtype=BF)
    for i in range(DEPTH - N_A):
        l = N_A + i
        q, qm, gt = _b_in(x, w_in_b2[i], tabs_s, tm=rows)
        om = _mem_attn_rows(qm, mem_s[l], t)
        q = q.reshape(KV_HEADS, GQA, bs, t, HEAD_DIM).transpose(2, 0, 1, 3, 4).reshape(bs, KV_HEADS, per_group, HEAD_DIM)
        q = jnp.pad(q, ((0, 0), (0, 0), (0, ROWS_PER_GROUP - per_group), (0, 0)))
        q_bd = jnp.einsum('bgxd,gh->bgxhd', q, eye).reshape(bs, SAMPLE_ROWS, KV_WIDTH)
        gs = gt[0, :NSA_HEADS * 3].reshape(KV_HEADS, GQA, 3, bs, t).transpose(3, 0, 1, 4, 2)
        gs = gs.reshape(bs, KV_HEADS, per_group, 3)
        gs = jnp.pad(gs, ((0, 0), (0, 0), (0, ROWS_PER_GROUP - per_group), (0, LANES - 3)))
        o = _nsa_sample(q_bd, gs.reshape(bs, SAMPLE_ROWS, LANES), slc_pool_t, page_table, slc_tail_t,
                        kct, vct, kwt, vwt, t_base=past, wpos0=past - n_buf, n_slc=n_slc, t_q=t)
        o = o.reshape(bs, KV_HEADS, ROWS_PER_GROUP, KV_HEADS, HEAD_DIM)
        o = jnp.stack([o[:, g, :per_group, g] for g in range(KV_HEADS)], axis=1)
        o = o.reshape(bs, KV_HEADS, GQA, t, HEAD_DIM).transpose(0, 3, 1, 2, 4).reshape(1, rows, NSA_WIDTH)
        x = _mlp_block(x, o, om, p["w_out_b"][i], p, l, tm=rows, a_transposed=False)
    y_sample = x.reshape(bs, t, d)
    cmp_kv_sample = new_cmp.reshape(bs, t, *shape5)
    slc_kv_sample = new_slc.reshape(bs, t, *shape5)
    win_kv_sample = win_all[:, t:].reshape(bs, n_buf, *shape5)
    gmlp_v_sample = jnp.stack(v_rows)

    return (y_prompt, y_sample, cmp_kv_prompt, slc_kv_prompt, win_kv_prompt, mem_kv_prompt_out,
            cmp_kv_sample, slc_kv_sample, win_kv_sample, gmlp_v_sample)
```

```python
import functools
import math

import jax
import jax.numpy as jnp
from jax import lax
from jax.experimental import pallas as pl
from jax.experimental.pallas import tpu as pltpu

BF = jnp.bfloat16
F32 = jnp.float32

DEPTH = 4
N_A = 2
A_WIDTH = 768
A_GROUPS = 4
A_GROUP_DIM = 192
MEM_TOKENS = 256
MEM_HEADS = 4
MEM_WIDTH = 256
HEAD_DIM = 64
NSA_HEADS = 12
KV_HEADS = 4
GQA = 3
NSA_WIDTH = 768
KV_WIDTH = 256
ROW_WIDTH = 2 * KV_WIDTH
CMP_BLOCK = 32
CMP_STRIDE = 16
CMP_HIDDEN = 128
SEL_BLOCK = 64
SEL_TOPK = 16
WINDOW = 512
ROPE_THETA = 500000.0
ROPE_DIM = 16
D_FF_CHUNK = 1024
ALPHA = (2.0 * DEPTH) ** 0.25
LN_EPS = 1e-5
NEG = -1e30
FORCE_BONUS = 1e4
PAGE = 128
LANES = 128
GATE_PAD = 128
SEL_GROUP = 4
NSA_QUERY_TILE = 256
Q_SCALE = HEAD_DIM ** -0.5 * math.log2(math.e)
VMEM_LIMIT = 56 * 1024 * 1024


def _cparams(sem, vmem=None):
    return pltpu.CompilerParams(dimension_semantics=sem, vmem_limit_bytes=vmem)


def _const_spec(shape):
    return pl.BlockSpec(shape, lambda *_: (0,) * len(shape), pipeline_mode=pl.Buffered(1))


def _ln(x, g, b):
    mu = jnp.mean(x, axis=-1, keepdims=True)
    xc = x - mu
    var = jnp.mean(xc * xc, axis=-1, keepdims=True)
    return xc * lax.rsqrt(var + LN_EPS) * g + b


def _dot(a, b):
    return jnp.dot(a, b, preferred_element_type=F32)


def _dot_nt(a, b):
    return lax.dot_general(a, b, (((1,), (1,)), ((), ())), preferred_element_type=F32)


def _rope_cols(x, c, a, b):
    cols = []
    for j in range(x.shape[1] // LANES):
        xc = x[:, j * LANES:(j + 1) * LANES]
        up = pltpu.roll(xc, LANES - ROPE_DIM // 2, axis=1)
        dn = pltpu.roll(xc, ROPE_DIM // 2, axis=1)
        cols.append(xc * c + up * a + dn * b)
    return cols[0] if len(cols) == 1 else jnp.concatenate(cols, axis=1)


def _mem_kv_kernel(x_ref, w_ref, o_ref):
    o_ref[0] = _dot(x_ref[...].astype(BF), w_ref[0].astype(BF))


def _mem_kv_proj(mem_prompt, w_mem_kv):
    b, m, d = mem_prompt.shape
    x = mem_prompt.reshape(b * m, d)
    depth, _, n = w_mem_kv.shape
    tm = 512
    out = pl.pallas_call(
        _mem_kv_kernel,
        grid=(depth, (b * m) // tm),
        in_specs=[pl.BlockSpec((tm, d), lambda l, i: (i, 0)),
                  pl.BlockSpec((1, d, n), lambda l, i: (l, 0, 0))],
        out_specs=pl.BlockSpec((1, tm, n), lambda l, i: (l, i, 0)),
        out_shape=jax.ShapeDtypeStruct((depth, b * m, n), F32),
        compiler_params=_cparams(("parallel", "parallel")),
        name="mem_kv_proj",
    )(x, w_mem_kv)
    return out.reshape(depth, b, m, n)


def _mem_heads(q, kv, own=None):
    k = kv[:, :MEM_WIDTH].astype(BF)
    v = kv[:, MEM_WIDTH:]
    lane = lax.broadcasted_iota(jnp.int32, (1, MEM_WIDTH), 1)
    out = jnp.zeros(q.shape, F32)
    for h in range(MEM_HEADS):
        hm = (lane >= h * HEAD_DIM) & (lane < (h + 1) * HEAD_DIM)
        s = _dot_nt(jnp.where(hm, q, 0.0).astype(BF), k) * (HEAD_DIM ** -0.5)
        if own is not None:
            s = jnp.where(own, s, NEG)
        m = jnp.max(s, axis=-1, keepdims=True)
        p = jnp.exp(s - m)
        p = p / jnp.sum(p, axis=-1, keepdims=True)
        out = out + _dot(p.astype(BF), jnp.where(hm, v, 0.0).astype(BF))
    return out


def _a_in_kernel(x_ref, w_ref, g_ref, b_ref, ws_ref, bs_ref, *rest, chunk_shift, fuse_mem, emit_v):
    mem_ref = rest[0] if fuse_mem else None
    mixed_ref, qm_ref = rest[int(fuse_mem):int(fuse_mem) + 2]
    x = x_ref[0].astype(BF)
    proj = _dot(x, w_ref[...])
    uv = jax.nn.gelu(proj[:, :2 * A_WIDTH])
    u = uv[:, :A_WIDTH]
    v = _ln(uv[:, A_WIDTH:], g_ref[...], b_ref[...])
    if emit_v:
        rest[-1][0] = v
    qm = proj[:, 2 * A_WIDTH:]
    qm_ref[0] = _mem_heads(qm, mem_ref[0]) if fuse_mem else qm
    tm = x.shape[0]
    r = ws_ref.shape[1]
    row = lax.broadcasted_iota(jnp.int32, (r, r), 0)
    col = lax.broadcasted_iota(jnp.int32, (r, r), 1)
    allowed = (col <= row) & ((row >> chunk_shift) == (col >> chunk_shift))
    wgs = [jnp.where(allowed, ws_ref[g], 0.0).astype(BF) for g in range(A_GROUPS)]
    span = 2 * A_GROUP_DIM
    first = lax.broadcasted_iota(jnp.int32, (1, span), 1) < A_GROUP_DIM
    vb = v.astype(BF)
    for ci in range(tm // r):
        rows = slice(ci * r, (ci + 1) * r)
        for h in range(A_GROUPS // 2):
            cols = slice(h * span, (h + 1) * span)
            vc = vb[rows, cols]
            mix = jnp.where(first, _dot(wgs[2 * h], vc), _dot(wgs[2 * h + 1], vc)) + bs_ref[:, cols]
            mixed_ref[0, rows, cols] = (u[rows, cols] * mix).astype(BF)


def _a_in(x, w_in, ln_g, ln_b, ws, bs, *, chunk, tm, emit_v, mem_kv=None):
    b, s, d = x.shape
    n = w_in.shape[1]
    r = ws.shape[1]
    in_specs = [pl.BlockSpec((1, tm, d), lambda i, j: (i, j, 0)),
                _const_spec((d, n)), _const_spec((1, A_WIDTH)), _const_spec((1, A_WIDTH)),
                _const_spec((A_GROUPS, r, r)), _const_spec((r, A_WIDTH))]
    args = [x, w_in, ln_g, ln_b, ws, bs]
    if mem_kv is not None:
        in_specs.append(pl.BlockSpec((1,) + mem_kv.shape[1:], lambda i, j: (i, 0, 0)))
        args.append(mem_kv)
    outs = [jax.ShapeDtypeStruct((b, s, A_WIDTH), BF), jax.ShapeDtypeStruct((b, s, MEM_WIDTH), F32)]
    ospecs = [pl.BlockSpec((1, tm, A_WIDTH), lambda i, j: (i, j, 0)),
              pl.BlockSpec((1, tm, MEM_WIDTH), lambda i, j: (i, j, 0))]
    if emit_v:
        outs.append(jax.ShapeDtypeStruct((b, s, A_WIDTH), F32))
        ospecs.append(pl.BlockSpec((1, tm, A_WIDTH), lambda i, j: (i, j, 0)))
    return pl.pallas_call(
        functools.partial(_a_in_kernel, chunk_shift=int(math.log2(chunk)), fuse_mem=mem_kv is not None,
                          emit_v=emit_v),
        grid=(b, s // tm),
        in_specs=in_specs,
        out_specs=ospecs,
        out_shape=outs,
        compiler_params=_cparams(("parallel", "parallel"), VMEM_LIMIT),
        name="a_in_gate",
    )(*args)


SAMPLE_MEM_SETS = 8


def _mem_attn_kernel(q_ref, kv_ref, o_ref, *, rows_per_set):
    q = q_ref[0]
    nb = kv_ref.shape[0]
    row_set = lax.broadcasted_iota(jnp.int32, (q.shape[0], 1), 0) >> int(math.log2(rows_per_set))
    col_set = lax.broadcasted_iota(jnp.int32, (1, nb * MEM_TOKENS), 1) >> int(math.log2(MEM_TOKENS))
    o_ref[0] = _mem_heads(q, kv_ref[...].reshape(nb * MEM_TOKENS, 2 * MEM_WIDTH), row_set == col_set)


def _mem_attn_rows(q, mem_kv, rows_per_set):
    _, s, w = q.shape
    tq = SAMPLE_MEM_SETS * rows_per_set
    return pl.pallas_call(
        functools.partial(_mem_attn_kernel, rows_per_set=rows_per_set),
        grid=(s // tq,),
        in_specs=[pl.BlockSpec((1, tq, w), lambda j: (0, j, 0)),
                  pl.BlockSpec((SAMPLE_MEM_SETS, MEM_TOKENS, 2 * w), lambda j: (j, 0, 0))],
        out_specs=pl.BlockSpec((1, tq, w), lambda j: (0, j, 0)),
        out_shape=jax.ShapeDtypeStruct((1, s, w), F32),
        compiler_params=_cparams(("parallel",)),
        name="mem_attn",
    )(q, mem_kv)


def _out_mlp_kernel(x_ref, a_ref, om_ref, wo_ref, g1_ref, b1_ref, wu_ref, wd_ref, g2_ref, b2_ref, o_ref,
                    *, a_transposed):
    x = x_ref[0]
    a = a_ref[0]
    a = a.T.astype(BF) if a_transposed else a.astype(BF)
    na = a.shape[1]
    mix = _dot(a, wo_ref[:na, :]) + _dot(om_ref[0].astype(BF), wo_ref[na:, :])
    y = _ln(ALPHA * x + mix, g1_ref[...], b1_ref[...])
    yb = y.astype(BF)
    acc = jnp.zeros(x.shape, F32)
    for c in range(wu_ref.shape[1] // D_FF_CHUNK):
        h = _dot(yb, wu_ref[:, c * D_FF_CHUNK:(c + 1) * D_FF_CHUNK])
        h = jnp.square(jnp.maximum(h, 0.0)).astype(BF)
        acc = acc + _dot(h, wd_ref[c * D_FF_CHUNK:(c + 1) * D_FF_CHUNK, :])
    o_ref[0] = _ln(ALPHA * y + acc, g2_ref[...], b2_ref[...])


def _out_mlp(x, a, om, wo, g1, b1, wu, wd, g2, b2, *, tm, a_transposed):
    b, s, d = x.shape
    na = wo.shape[0] - om.shape[2]
    dff = wu.shape[1]
    if a_transposed:
        a_spec = pl.BlockSpec((1, na, tm), lambda i, j: (i, 0, j))
    else:
        a_spec = pl.BlockSpec((1, tm, na), lambda i, j: (i, j, 0))
    return pl.pallas_call(
        functools.partial(_out_mlp_kernel, a_transposed=a_transposed),
        grid=(b, s // tm),
        in_specs=[pl.BlockSpec((1, tm, d), lambda i, j: (i, j, 0)),
                  a_spec,
                  pl.BlockSpec((1, tm, om.shape[2]), lambda i, j: (i, j, 0)),
                  _const_spec(wo.shape), _const_spec((1, d)), _const_spec((1, d)),
                  _const_spec((d, dff)), _const_spec((dff, d)), _const_spec((1, d)), _const_spec((1, d))],
        out_specs=pl.BlockSpec((1, tm, d), lambda i, j: (i, j, 0)),
        out_shape=jax.ShapeDtypeStruct((b, s, d), F32),
        compiler_params=_cparams(("parallel", "parallel"), VMEM_LIMIT),
        name="out_mlp",
    )(x, a, om, wo, g1, b1, wu, wd, g2, b2)


def _kv_shared_kernel(h_ref, w_ref, c_ref, a_ref, b_ref, cmp_ref, slc_ref, win_ref, *attn_refs):
    kv = _dot(h_ref[0].astype(BF), w_ref[...])
    c, a, b = c_ref[...], a_ref[...], b_ref[...]
    tm = kv.shape[0]
    for br, ref in enumerate((cmp_ref, slc_ref, win_ref)):
        k = _rope_cols(kv[:, br * ROW_WIDTH:br * ROW_WIDTH + KV_WIDTH], c, a, b)
        v = kv[:, br * ROW_WIDTH + KV_WIDTH:(br + 1) * ROW_WIDTH]
        vt = v.T
        ref[0, :KV_WIDTH, :] = k.T
        ref[0, KV_WIDTH:, :] = vt
        if attn_refs and br > 0:
            k_ref, vt_ref = attn_refs[2 * (br - 1):2 * br]
            if br == 1:
                lane = lax.broadcasted_iota(jnp.int32, (1, LANES), 1)
                row = pl.program_id(1) * tm + lax.broadcasted_iota(jnp.int32, (tm, 1), 0)
                block_hot = jnp.where(lane - HEAD_DIM == (row >> int(math.log2(SEL_BLOCK))), 1.0, 0.0)
            for g in range(KV_HEADS):
                if br == 1:
                    pair = k[:, (g // 2) * LANES:(g // 2 + 1) * LANES]
                    if g % 2:
                        pair = pltpu.roll(pair, HEAD_DIM, axis=1)
                    k_ref[0, g] = jnp.where(lane < HEAD_DIM, pair, block_hot).astype(BF)
                else:
                    k_ref[0, g] = k[:, g * HEAD_DIM:(g + 1) * HEAD_DIM].astype(BF)
                for u in range(tm // PAGE):
                    vt_ref[0, g, u] = vt[g * HEAD_DIM:(g + 1) * HEAD_DIM, u * PAGE:(u + 1) * PAGE].astype(BF)


def _kv_shared(h, w_kv, tabs, *, tm, attn_layouts):
    b, s, d = h.shape
    row_spec = pl.BlockSpec((1, ROW_WIDTH, tm), lambda i, j: (i, 0, j))
    tab_spec = pl.BlockSpec((tm, LANES), lambda i, j: (j, 0))
    out_specs = [row_spec, row_spec, row_spec]
    out_shape = [jax.ShapeDtypeStruct((b, ROW_WIDTH, s), F32)] * 3
    if attn_layouts:
        assert s // SEL_BLOCK <= LANES - HEAD_DIM
        vt_spec = pl.BlockSpec((1, KV_HEADS, tm // PAGE, HEAD_DIM, PAGE), lambda i, j: (i, 0, j, 0, 0))
        vt_shape = jax.ShapeDtypeStruct((b, KV_HEADS, s // PAGE, HEAD_DIM, PAGE), BF)
        for width in (LANES, HEAD_DIM):
            out_specs += [pl.BlockSpec((1, KV_HEADS, tm, width), lambda i, j: (i, 0, j, 0)), vt_spec]
            out_shape += [jax.ShapeDtypeStruct((b, KV_HEADS, s, width), BF), vt_shape]
    return pl.pallas_call(
        _kv_shared_kernel,
        grid=(b, s // tm),
        in_specs=[pl.BlockSpec((1, tm, d), lambda i, j: (i, j, 0)), _const_spec(w_kv.shape),
                  tab_spec, tab_spec, tab_spec],
        out_specs=out_specs,
        out_shape=out_shape,
        compiler_params=_cparams(("parallel", "parallel")),
        name="kv_shared",
    )(h, w_kv, *tabs)


def _b_in_kernel(x_ref, w_ref, c_ref, a_ref, b_ref, *rest, fuse_mem):
    mem_ref = rest[0] if fuse_mem else None
    q_ref, qm_ref, gt_ref = rest[int(fuse_mem):]
    proj = _dot(x_ref[0].astype(BF), w_ref[...])
    q = _rope_cols(proj[:, :NSA_WIDTH], c_ref[...], a_ref[...], b_ref[...]) * Q_SCALE
    for h in range(NSA_HEADS):
        q_ref[0, h] = q[:, h * HEAD_DIM:(h + 1) * HEAD_DIM].astype(BF)
    qm = proj[:, NSA_WIDTH:NSA_WIDTH + MEM_WIDTH]
    qm_ref[0] = _mem_heads(qm, mem_ref[0]) if fuse_mem else qm
    gt_ref[0] = jax.nn.sigmoid(proj[:, NSA_WIDTH + MEM_WIDTH:]).T


def _b_in(x, w_in, tabs, *, tm, mem_kv=None):
    b, s, d = x.shape
    tab_spec = pl.BlockSpec((tm, LANES), lambda i, j: (j, 0))
    in_specs = [pl.BlockSpec((1, tm, d), lambda i, j: (i, j, 0)), _const_spec(w_in.shape),
                tab_spec, tab_spec, tab_spec]
    args = [x, w_in, *tabs]
    if mem_kv is not None:
        in_specs.append(pl.BlockSpec((1,) + mem_kv.shape[1:], lambda i, j: (i, 0, 0)))
        args.append(mem_kv)
    return pl.pallas_call(
        functools.partial(_b_in_kernel, fuse_mem=mem_kv is not None),
        grid=(b, s // tm),
        in_specs=in_specs,
        out_specs=[pl.BlockSpec((1, NSA_HEADS, tm, HEAD_DIM), lambda i, j: (i, 0, j, 0)),
                   pl.BlockSpec((1, tm, MEM_WIDTH), lambda i, j: (i, j, 0)),
                   pl.BlockSpec((1, GATE_PAD, tm), lambda i, j: (i, 0, j))],
        out_shape=[jax.ShapeDtypeStruct((b, NSA_HEADS, s, HEAD_DIM), BF),
                   jax.ShapeDtypeStruct((b, s, MEM_WIDTH), F32),
                   jax.ShapeDtypeStruct((b, GATE_PAD, s), F32)],
        compiler_params=_cparams(("parallel", "parallel")),
        name="b_in",
    )(*args)


def _kv_layout_kernel(x_ref, k_ref, vt_ref):
    x = x_ref[0]
    vt = x[:, KV_WIDTH:].T
    for g in range(KV_HEADS):
        k_ref[0, g] = x[:, g * HEAD_DIM:(g + 1) * HEAD_DIM].astype(BF)
        vt_ref[0, g, 0] = vt[g * HEAD_DIM:(g + 1) * HEAD_DIM].astype(BF)


def _rows_layout(rows):
    b, l, _ = rows.shape
    n = l // PAGE
    return pl.pallas_call(
        _kv_layout_kernel,
        grid=(b, n),
        in_specs=[pl.BlockSpec((1, PAGE, ROW_WIDTH), lambda i, p: (i, p, 0))],
        out_specs=[pl.BlockSpec((1, KV_HEADS, PAGE, HEAD_DIM), lambda i, p: (i, 0, p, 0)),
                   pl.BlockSpec((1, KV_HEADS, 1, HEAD_DIM, PAGE), lambda i, p: (i, 0, p, 0, 0))],
        out_shape=[jax.ShapeDtypeStruct((b, KV_HEADS, l, HEAD_DIM), BF),
                   jax.ShapeDtypeStruct((b, KV_HEADS, n, HEAD_DIM, PAGE), BF)],
        compiler_params=_cparams(("parallel", "parallel")),
        name="kv_layout",
    )(rows)


def _compress_kernel(pt_ref, *refs, n_slots, tail_group):
    page_refs = refs[:n_slots]
    rest = refs[n_slots:]
    if tail_group is not None:
        tail_ref, rest = rest[0], rest[1:]
    w1_ref, pe_ref, w2_ref, o_ref, xs_ref, carry_ref = rest
    k = pl.program_id(1)
    per_page = PAGE // CMP_STRIDE
    m = n_slots * per_page
    hid2 = 2 * CMP_HIDDEN

    @pl.when(k == 0)
    def _():
        carry_ref[...] = jnp.zeros(carry_ref.shape, F32)

    for i, pr in enumerate(page_refs):
        x = pr[0]
        if tail_group is not None and i == n_slots - 1:
            x = jnp.where(k == tail_group, tail_ref[0], x)
        rows = slice(i * PAGE, (i + 1) * PAGE)
        for c in range(2):
            xc = x[c].reshape(KV_WIDTH, PAGE).T
            for hp in range(2):
                xs_ref[2 * c + hp, rows, :] = xc[:, hp * LANES:(hp + 1) * LANES]

    def strides(vcol):
        return jnp.concatenate(
            [xs_ref[vcol, pl.ds(s, m, stride=CMP_STRIDE), :] for s in range(CMP_STRIDE)], axis=1)

    first_row = lax.broadcasted_iota(jnp.int32, (m, 1), 0) == 0
    out_cols = []
    for c in range(2):
        w1 = w1_ref[c]
        lhs = jnp.concatenate([strides(2 * c + hp) for hp in range(2)] + [pe_ref[c, 0], pe_ref[c, 1]],
                              axis=0).astype(BF)
        a = _dot(lhs, w1)
        hpe = a[2 * m:2 * m + 1, :hid2] + a[2 * m + 8:2 * m + 9, hid2:]
        acc = jnp.zeros((m, KV_WIDTH), F32)
        for hp in range(2):
            a0 = a[hp * m:(hp + 1) * m, :hid2]
            a1 = a[hp * m:(hp + 1) * m, hid2:]
            prev = jnp.where(first_row, carry_ref[c, hp, 0:1, :], pltpu.roll(a0, 1, axis=0))
            carry_ref[c, hp, 0:1, :] = a0[m - 1:m]
            acc = acc + _dot(jax.nn.gelu(prev + a1 + hpe).astype(BF), w2_ref[c, hp])
        out_cols.append(acc)
    o_ref[0] = jnp.concatenate(out_cols, axis=1)


def _compress(pool, table, tail, w1, pe, w2):
    page_shape = pool.shape[1:-1] + (PAGE,)
    zeros = (0,) * (len(page_shape) - 1)
    paged = table is not None
    if paged:
        b, n_real = table.shape
    else:
        b, n_real = pool.shape[0], pool.shape[-1] // PAGE
        table = jnp.zeros((1, 1), jnp.int32)
    n_pages = n_real + (0 if tail is None else 1)
    n_slots = max(p for p in range(1, 17) if n_pages % p == 0)
    n_groups = n_pages // n_slots
    m = n_slots * (PAGE // CMP_STRIDE)

    def page_spec(slot):
        if paged:
            return pl.BlockSpec(
                (1,) + page_shape,
                lambda i, k, pt: (pt[i * n_real + jnp.minimum(k * n_slots + slot, n_real - 1)],) + zeros + (0,))
        return pl.BlockSpec((1,) + page_shape, lambda i, k, pt: (i,) + zeros + (k * n_slots + slot,))

    in_specs = [page_spec(slot) for slot in range(n_slots)]
    args = [pool] * n_slots
    if tail is not None:
        in_specs.append(pl.BlockSpec((1,) + page_shape, lambda i, k, pt: (i,) + zeros + (0,)))
        args.append(tail)
    in_specs += [pl.BlockSpec(x.shape, lambda i, k, pt, nd=x.ndim: (0,) * nd) for x in (w1, pe, w2)]
    return pl.pallas_call(
        functools.partial(_compress_kernel, n_slots=n_slots,
                          tail_group=None if tail is None else n_groups - 1),
        grid_spec=pltpu.PrefetchScalarGridSpec(
            num_scalar_prefetch=1, grid=(b, n_groups), in_specs=in_specs,
            out_specs=pl.BlockSpec((1, m, ROW_WIDTH), lambda i, k, pt: (i, k, 0)),
            scratch_shapes=[pltpu.VMEM((4, n_slots * PAGE, LANES), F32),
                            pltpu.VMEM((2, 2, 8, 2 * CMP_HIDDEN), F32)]),
        out_shape=jax.ShapeDtypeStruct((b, n_groups * m, ROW_WIDTH), F32),
        compiler_params=_cparams(("parallel", "arbitrary"), VMEM_LIMIT),
        name="compress",
    )(table.reshape(-1), *args, w1, pe, w2)


def _nsa_kernel(q_ref, gt_ref, kc_ref, vct_ref, ks_ref, vst_ref, kw_ref, vwt_ref, o_ref, psum_ref,
                sbuf0_ref, sbuf1_ref, *, t_base, wpos0, n_slc, nsp):
    g = pl.program_id(1)
    i = pl.program_id(2)
    nq = q_ref.shape[2]
    nl = GQA * nq
    ncp = kc_ref.shape[2]
    gk = SEL_GROUP * PAGE
    nkt_s = vst_ref.shape[2]
    q = q_ref[0].reshape(nl, HEAD_DIM)
    lane = lax.broadcasted_iota(jnp.int32, (1, nl), 1)
    q0 = t_base + i * nq
    t = q0 + (lane & (nq - 1))

    s = _dot_nt(kc_ref[0, 0], q)
    n_idx = lax.broadcasted_iota(jnp.int32, (ncp, 1), 0)
    cmask = n_idx * CMP_STRIDE + (CMP_BLOCK - 1) <= t
    s = jnp.where(cmask, s, NEG)
    p = jnp.where(cmask, jnp.exp2(s - jnp.max(s, axis=0, keepdims=True)), 0.0)
    l = jnp.sum(p, axis=0, keepdims=True)
    p = p * jnp.where(l > 0.0, 1.0 / l, 0.0)
    o_cmp = jnp.zeros((HEAD_DIM, nl), F32)
    for kt in range(ncp // PAGE):
        o_cmp = o_cmp + _dot(vct_ref[0, 0, kt], p[kt * PAGE:(kt + 1) * PAGE].astype(BF))

    rel = (q0 - wpos0) // PAGE
    n_wt = (WINDOW + nq) // PAGE
    tile_row = lax.broadcasted_iota(jnp.int32, (PAGE, 1), 0)
    parts = []
    for u in range(n_wt):
        kt = rel - WINDOW // PAGE + u
        ktc = jnp.maximum(kt, 0)
        kb = kw_ref[0, 0, pl.ds(pl.multiple_of(ktc * PAGE, PAGE), PAGE), :]
        krel = kt * PAGE + tile_row
        kpos = jnp.where(krel >= 0, wpos0 + krel, -(1 << 30))
        d = lax.bitcast_convert_type(t - kpos, jnp.uint32)
        parts.append(jnp.where(d < WINDOW, _dot_nt(kb, q), NEG))
    sw = jnp.concatenate(parts, axis=0)
    pw = jnp.exp2(sw - jnp.max(sw, axis=0, keepdims=True))
    l_w = jnp.sum(pw, axis=0, keepdims=True)
    acc_w = jnp.zeros((HEAD_DIM, nl), F32)
    for u in range(n_wt):
        ktc = jnp.maximum(rel - WINDOW // PAGE + u, 0)
        acc_w = acc_w + _dot(vwt_ref[0, 0, ktc], pw[u * PAGE:(u + 1) * PAGE].astype(BF))
    o_win = acc_w * (1.0 / l_w)

    psum = p[:, :nq]
    for r in range(1, GQA):
        psum = psum + p[:, r * nq:(r + 1) * nq]
    front = 8
    ratio = SEL_BLOCK // CMP_STRIDE
    slabs = []
    for sl in range(nq // LANES):
        psum_ref[sl, :front, :] = jnp.zeros((front, LANES), F32)
        psum_ref[sl, front:front + ncp, :] = psum[:, sl * LANES:(sl + 1) * LANES]
        psum_ref[sl, front + ncp:, :] = jnp.zeros((psum_ref.shape[1] - front - ncp, LANES), F32)
        part = psum_ref[sl, pl.ds(front - 1, nsp, stride=ratio), :]
        for d in range(1, ratio + 1):
            part = part + psum_ref[sl, pl.ds(front - 1 + d, nsp, stride=ratio), :]
        slabs.append(part)
    imp = slabs[0] if len(slabs) == 1 else jnp.concatenate(slabs, axis=1)
    tq = t[:, :nq]
    j = lax.broadcasted_iota(jnp.int32, (nsp, 1), 0)
    cur = tq >> int(math.log2(SEL_BLOCK))
    forced = (j == 0) | (j == cur) | (j == cur - 1)
    score = jnp.where(j * SEL_BLOCK <= tq, imp + FORCE_BONUS * forced.astype(F32), NEG)
    sub = 8
    chunks = [score[c * sub:(c + 1) * sub] for c in range(nsp // sub)]
    jc = j[:sub]
    cnts = [jnp.zeros((sub, nq), F32) for _ in chunks]
    for ii in range(n_slc):
        ri = score[ii:ii + 1, :]
        for c, sc_c in enumerate(chunks):
            if c * sub > ii:
                one = jnp.where(ri >= sc_c, 1.0, 0.0)
            elif c * sub + sub - 1 <= ii:
                one = jnp.where(ri > sc_c, 1.0, 0.0)
            else:
                one = jnp.where(jc + c * sub > ii, jnp.where(ri >= sc_c, 1.0, 0.0), jnp.where(ri > sc_c, 1.0, 0.0))
            cnts[c] = cnts[c] + one
    cnt = jnp.concatenate(cnts, axis=0)
    bias = jnp.where(cnt < float(SEL_TOPK), jnp.where(score > NEG / 2, 0.0, NEG), NEG)
    bias_t = jnp.concatenate([bias, jnp.zeros((LANES - nsp, nq), F32)], axis=0).T[:, :LANES - HEAD_DIM]
    q_ext = jnp.concatenate([q, jnp.concatenate([bias_t.astype(BF)] * GQA, axis=0)], axis=1)

    key_row = lax.broadcasted_iota(jnp.int32, (gk, 1), 0)

    def fill(buf, gi):
        kb = ks_ref[0, 0, pl.ds(pl.multiple_of(gi * gk, gk), gk), :]
        buf[...] = _dot_nt(kb, q_ext)

    def consume(buf, gi, carry, causal):
        m_old, l_old, acc = carry
        sc = buf[...]
        if causal:
            sc = jnp.where(gi * gk + key_row <= t, sc, NEG)
        m_new = jnp.maximum(m_old, jnp.max(sc, axis=0, keepdims=True))
        alpha = jnp.exp2(m_old - m_new)
        pe = jnp.exp2(sc - m_new)
        acc = acc * alpha
        for u in range(SEL_GROUP):
            acc = acc + _dot(vst_ref[0, 0, gi * SEL_GROUP + u], pe[u * PAGE:(u + 1) * PAGE].astype(BF))
        return m_new, l_old * alpha + jnp.sum(pe, axis=0, keepdims=True), acc

    init = (jnp.full((1, nl), NEG, F32), jnp.zeros((1, nl), F32), jnp.zeros((HEAD_DIM, nl), F32))

    n_groups = jnp.minimum(nkt_s - 1, (q0 + nq - 1) // PAGE) // SEL_GROUP + 1
    fill(sbuf0_ref, 0)

    def pair_body(kk, carry):
        gi = 2 * kk
        fill(sbuf1_ref, gi + 1)
        carry = consume(sbuf0_ref, gi, carry, False)
        fill(sbuf0_ref, gi + 2)
        return consume(sbuf1_ref, gi + 1, carry, False)

    n_pairs = (n_groups - 1) // 2
    carry = lax.fori_loop(0, n_pairs, pair_body, init)
    gi = 2 * n_pairs

    def odd_tail(carry):
        fill(sbuf1_ref, gi + 1)
        carry = consume(sbuf0_ref, gi, carry, False)
        return consume(sbuf1_ref, gi + 1, carry, True)

    def even_tail(carry):
        return consume(sbuf0_ref, gi, carry, True)

    _, l_s, acc_s = lax.cond(n_groups - 1 - gi == 1, odd_tail, even_tail, carry)

    o_sel = acc_s * jnp.where(l_s > 0.0, 1.0 / l_s, 0.0)
    for r in range(GQA):
        sl = slice(r * nq, (r + 1) * nq)
        grow = (GQA * g + r) * 3
        o = (gt_ref[0, pl.ds(grow, 1), :] * o_cmp[:, sl]
             + gt_ref[0, pl.ds(grow + 1, 1), :] * o_sel[:, sl]
             + gt_ref[0, pl.ds(grow + 2, 1), :] * o_win[:, sl])
        o_ref[0, r * HEAD_DIM:(r + 1) * HEAD_DIM, :] = o


def _nsa(q, gt, kc, vct, ks, vst, kw, vwt, *, t_base, wpos0, n_slc):
    b, _, sq, _ = q.shape
    nq = NSA_QUERY_TILE
    assert sq % nq == 0 and t_base % PAGE == 0 and (t_base - wpos0) % PAGE == 0 and t_base >= wpos0
    assert vst.shape[2] % SEL_GROUP == 0
    nsp = max(-(-n_slc // 8) * 8, vst.shape[2] * (PAGE // SEL_BLOCK))
    assert nsp <= LANES - HEAD_DIM and ks.shape[3] == LANES
    ncp = kc.shape[2]
    psum_rows = 8 + max(ncp + 8, (SEL_BLOCK // CMP_STRIDE) * nsp + 8)
    psum_rows = -(-psum_rows // 8) * 8

    def kv_specs(k, vt):
        return [pl.BlockSpec((1, 1) + k.shape[2:], lambda bi, gi, ii: (bi, gi, 0, 0)),
                pl.BlockSpec((1, 1) + vt.shape[2:], lambda bi, gi, ii: (bi, gi, 0, 0, 0))]

    return pl.pallas_call(
        functools.partial(_nsa_kernel, t_base=t_base, wpos0=wpos0, n_slc=n_slc, nsp=nsp),
        grid=(b, KV_HEADS, sq // nq),
        in_specs=[pl.BlockSpec((1, GQA, nq, HEAD_DIM), lambda bi, gi, ii: (bi, gi, ii, 0)),
                  pl.BlockSpec((1, GATE_PAD, nq), lambda bi, gi, ii: (bi, 0, ii))]
        + kv_specs(kc, vct) + kv_specs(ks, vst) + kv_specs(kw, vwt),
        out_specs=pl.BlockSpec((1, GQA * HEAD_DIM, nq), lambda bi, gi, ii: (bi, gi, ii)),
        out_shape=jax.ShapeDtypeStruct((b, NSA_WIDTH, sq), F32),
        scratch_shapes=[pltpu.VMEM((nq // LANES, psum_rows, LANES), F32),
                        pltpu.VMEM((SEL_GROUP * PAGE, GQA * nq), F32), pltpu.VMEM((SEL_GROUP * PAGE, GQA * nq), F32)],
        compiler_params=_cparams(("parallel", "parallel", "parallel"), VMEM_LIMIT),
        name="nsa_attend",
    )(q, gt, kc, vct, ks, vst, kw, vwt)


SAMPLE_ROWS = 64
ROWS_PER_GROUP = SAMPLE_ROWS // KV_HEADS


def _nsa_sample_kernel(pt_ref, *refs, n_slots, t_base, wpos0, n_slc, t_q):
    page_refs = refs[:n_slots]
    (tail_ref, q_ref, gt_ref, kct_ref, vct_ref, kwt_ref, vwt_ref, o_ref,
     ind_ref, expand_ref, m_ref, l_ref, acc_ref, oc_ref, ow_ref) = refs[n_slots:]
    k = pl.program_id(1)
    n_steps = pl.num_programs(1)
    q = q_ref[0]
    row = lax.broadcasted_iota(jnp.int32, (SAMPLE_ROWS, 1), 0)
    t = t_base + (row & (t_q - 1))

    @pl.when(k == 0)
    def _():
        ncp = kct_ref.shape[2]
        s = _dot(q, kct_ref[0])
        n_idx = lax.broadcasted_iota(jnp.int32, (1, ncp), 1)
        cmask = n_idx * CMP_STRIDE + (CMP_BLOCK - 1) <= t
        s = jnp.where(cmask, s, NEG)
        p = jnp.where(cmask, jnp.exp2(s - jnp.max(s, axis=1, keepdims=True)), 0.0)
        l = jnp.sum(p, axis=1, keepdims=True)
        p = p * jnp.where(l > 0.0, 1.0 / l, 0.0)
        oc_ref[...] = _dot_nt(p.astype(BF), vct_ref[0])

        parts = []
        for g in range(KV_HEADS):
            base = g * ROWS_PER_GROUP
            acc = p[base:base + t_q]
            for r in range(1, GQA):
                acc = acc + p[base + r * t_q:base + (r + 1) * t_q]
            parts.append(acc)
        psum = jnp.concatenate(parts, axis=0)
        nsl = -(-n_slc // LANES) * LANES
        n_col = lax.broadcasted_iota(jnp.int32, (ncp, 1), 0)
        j_row = lax.broadcasted_iota(jnp.int32, (1, nsl), 1)
        ratio = SEL_BLOCK // CMP_STRIDE
        overlap = jnp.where((n_col >= ratio * j_row - 1) & (n_col <= ratio * j_row + ratio - 1), 1.0, 0.0).astype(BF)
        hi = psum.astype(BF)
        mid = (psum - hi.astype(F32)).astype(BF)
        lo = (psum - hi.astype(F32) - mid.astype(F32)).astype(BF)
        imp = _dot(hi, overlap) + _dot(mid, overlap) + _dot(lo, overlap)
        nsel_rows = KV_HEADS * t_q
        tq = t_base + (lax.broadcasted_iota(jnp.int32, (nsel_rows, 1), 0) & (t_q - 1))
        cur = tq >> int(math.log2(SEL_BLOCK))
        forced = (j_row == 0) | (j_row == cur) | (j_row == cur - 1)
        score = jnp.where(j_row * SEL_BLOCK <= tq, imp + FORCE_BONUS * forced.astype(F32), NEG)
        score_t = jnp.concatenate([score, jnp.full((LANES - nsel_rows, nsl), NEG, F32)], axis=0).T
        sub = 8
        nsp = -(-n_slc // sub) * sub
        jc = lax.broadcasted_iota(jnp.int32, (sub, 1), 0)
        chunks = [score_t[c * sub:(c + 1) * sub] for c in range(nsp // sub)]
        cnts = [jnp.zeros((sub, LANES), F32) for _ in chunks]
        for ii in range(n_slc):
            ri = score_t[ii:ii + 1, :]
            for c, sc_c in enumerate(chunks):
                if c * sub > ii:
                    one = jnp.where(ri >= sc_c, 1.0, 0.0)
                elif c * sub + sub - 1 <= ii:
                    one = jnp.where(ri > sc_c, 1.0, 0.0)
                else:
                    one = jnp.where(jc + c * sub > ii, jnp.where(ri >= sc_c, 1.0, 0.0), jnp.where(ri > sc_c, 1.0, 0.0))
                cnts[c] = cnts[c] + one
        chosen = [jnp.where(cn < float(SEL_TOPK), jnp.where(sc_c > NEG / 2, 1.0, 0.0), 0.0)
                  for cn, sc_c in zip(cnts, chunks)]
        chosen_t = jnp.concatenate(chosen + [jnp.zeros((nsl - nsp, LANES), F32)], axis=0)
        chosen_rows = chosen_t.T.astype(BF)
        col = lax.broadcasted_iota(jnp.int32, (1, LANES), 1)
        src = (row >> int(math.log2(ROWS_PER_GROUP))) * t_q + (row & (t_q - 1))
        expand = jnp.where(col == src, 1.0, 0.0).astype(BF)
        ind = _dot(expand, chosen_rows).astype(BF)
        blocks_per_step = n_slots * (PAGE // SEL_BLOCK)
        j_all = lax.broadcasted_iota(jnp.int32, (nsl, 1), 0)
        for st in range(ind_ref.shape[0]):
            pick = jnp.where((j_all == st * blocks_per_step + col) & (col < blocks_per_step), 1.0, 0.0).astype(BF)
            ind_ref[st] = _dot(ind, pick).astype(BF)
        key_lane = lax.broadcasted_iota(jnp.int32, (1, n_slots * PAGE), 1)
        local_block = lax.broadcasted_iota(jnp.int32, (LANES, 1), 0)
        expand_ref[...] = jnp.where(local_block == (key_lane >> int(math.log2(SEL_BLOCK))), 1.0, 0.0).astype(BF)

        nw = kwt_ref.shape[2]
        sw = _dot(q, kwt_ref[0])
        kpos = wpos0 + lax.broadcasted_iota(jnp.int32, (1, nw), 1)
        d = lax.bitcast_convert_type(t - kpos, jnp.uint32)
        sw = jnp.where(d < WINDOW, sw, NEG)
        pw = jnp.exp2(sw - jnp.max(sw, axis=1, keepdims=True))
        lw = jnp.sum(pw, axis=1, keepdims=True)
        ow_ref[...] = _dot_nt(pw.astype(BF), vwt_ref[0]) * (1.0 / lw)

        m_ref[...] = jnp.full(m_ref.shape, NEG, F32)
        l_ref[...] = jnp.zeros(l_ref.shape, F32)
        acc_ref[...] = jnp.zeros(acc_ref.shape, F32)

    vts = []
    scs = []
    for i, pr in enumerate(page_refs):
        x = pr[0]
        if i == n_slots - 1:
            x = jnp.where(k == n_steps - 1, tail_ref[0], x)
        scs.append(_dot(q, x[0].reshape(KV_WIDTH, PAGE).astype(BF)))
        vts.append(x[1].reshape(KV_WIDTH, PAGE).astype(BF))
    sc = jnp.concatenate(scs, axis=1)
    nk = n_slots * PAGE
    kidx = k * nk + lax.broadcasted_iota(jnp.int32, (1, nk), 1)
    picked = _dot(ind_ref[k], expand_ref[...])
    sc = jnp.where(picked > 0.5, jnp.where(kidx <= t, sc, NEG), NEG)
    m_old = m_ref[:, :1]
    m_new = jnp.maximum(m_old, jnp.max(sc, axis=1, keepdims=True))
    alpha = jnp.exp2(m_old - m_new)
    pe = jnp.exp2(sc - m_new)
    acc = acc_ref[...] * alpha
    for i in range(n_slots):
        acc = acc + _dot_nt(pe[:, i * PAGE:(i + 1) * PAGE].astype(BF), vts[i])
    l_new = l_ref[:, :1] * alpha + jnp.sum(pe, axis=1, keepdims=True)
    m_ref[...] = jnp.broadcast_to(m_new, m_ref.shape)
    l_ref[...] = jnp.broadcast_to(l_new, l_ref.shape)
    acc_ref[...] = acc

    @pl.when(k == n_steps - 1)
    def _():
        o_sel = acc * jnp.where(l_new > 0.0, 1.0 / l_new, 0.0)
        gts = gt_ref[0]
        o_ref[0] = gts[:, 0:1] * oc_ref[...] + gts[:, 1:2] * o_sel + gts[:, 2:3] * ow_ref[...]


def _nsa_sample(q_bd, gates, pool_t, table, tail_t, kct, vct, kwt, vwt, *, t_base, wpos0, n_slc, t_q):
    b, n_real = table.shape
    n_pages = n_real + 1
    n_slots = max(p for p in range(1, 17) if n_pages % p == 0)
    n_steps = n_pages // n_slots
    nsl = -(-n_slc // LANES) * LANES
    page_shape = pool_t.shape[1:]
    zeros = (0,) * len(page_shape)

    def page_spec(slot):
        return pl.BlockSpec(
            (1,) + page_shape,
            lambda i, k, pt: (pt[i * n_real + jnp.minimum(k * n_slots + slot, n_real - 1)],) + zeros)

    def per_batch(x):
        return pl.BlockSpec((1,) + x.shape[1:], lambda i, k, pt: (i,) + (0,) * (x.ndim - 1))

    acc_shape = pltpu.VMEM((SAMPLE_ROWS, KV_WIDTH), F32)
    stat_shape = pltpu.VMEM((SAMPLE_ROWS, LANES), F32)
    return pl.pallas_call(
        functools.partial(_nsa_sample_kernel, n_slots=n_slots, t_base=t_base, wpos0=wpos0, n_slc=n_slc, t_q=t_q),
        grid_spec=pltpu.PrefetchScalarGridSpec(
            num_scalar_prefetch=1, grid=(b, n_steps),
            in_specs=[page_spec(slot) for slot in range(n_slots)]
            + [per_batch(x) for x in (tail_t, q_bd, gates, kct, vct, kwt, vwt)],
            out_specs=pl.BlockSpec((1, SAMPLE_ROWS, KV_WIDTH), lambda i, k, pt: (i, 0, 0)),
            scratch_shapes=[pltpu.VMEM((n_steps, SAMPLE_ROWS, LANES), BF), pltpu.VMEM((LANES, n_slots * PAGE), BF),
                            stat_shape, stat_shape, acc_shape, acc_shape, acc_shape]),
        out_shape=jax.ShapeDtypeStruct((b, SAMPLE_ROWS, KV_WIDTH), F32),
        compiler_params=_cparams(("parallel", "arbitrary"), VMEM_LIMIT),
        name="nsa_sample",
    )(table.reshape(-1), *([pool_t] * n_slots), tail_t, q_bd, gates, kct, vct, kwt, vwt)


def _rope_tables(pos):
    half = ROPE_DIM // 2
    inv_freq = ROPE_THETA ** (-jnp.arange(half, dtype=F32) / half)
    ang = pos.astype(F32)[:, None] * inv_freq[None, :]
    cos, sin = jnp.cos(ang), jnp.sin(ang)
    n = pos.shape[0]
    rest = HEAD_DIM - ROPE_DIM
    c = jnp.concatenate([cos, cos, jnp.ones((n, rest), F32)], axis=1)
    a = jnp.concatenate([-sin, jnp.zeros((n, half + rest), F32)], axis=1)
    b = jnp.concatenate([jnp.zeros((n, half), F32), sin, jnp.zeros((n, rest), F32)], axis=1)
    reps = LANES // HEAD_DIM
    return tuple(jnp.tile(x, (1, reps)) for x in (c, a, b))


def _compress_weights(cmp_pe, w_phi1, w_phi2):
    ratio = CMP_BLOCK // CMP_STRIDE
    w1 = w_phi1.reshape(ratio, CMP_STRIDE, 2, HEAD_DIM, CMP_HIDDEN)
    eye2 = jnp.eye(2, dtype=w_phi1.dtype)
    big = jnp.einsum('jscde,hg->cjshdge', w1, eye2)
    big = big.reshape(2, ratio, CMP_STRIDE * 2 * HEAD_DIM, 2 * CMP_HIDDEN).astype(BF)
    big = jnp.concatenate([big[:, jj] for jj in range(ratio)], axis=-1)
    pe = cmp_pe.reshape(ratio, CMP_STRIDE, 2, HEAD_DIM)
    pe = jnp.broadcast_to(pe.transpose(2, 0, 1, 3)[:, :, :, None, :], (2, ratio, CMP_STRIDE, 2, HEAD_DIM))
    pe = pe.reshape(2, ratio, 1, CMP_STRIDE * 2 * HEAD_DIM)
    pe = jnp.broadcast_to(pe, (2, ratio, 8, pe.shape[-1]))
    eye4 = jnp.eye(KV_HEADS, dtype=w_phi2.dtype).reshape(2, 2, KV_HEADS)
    w2 = jnp.einsum('ced,pgh->cpgehd', w_phi2, eye4)
    w2 = w2.reshape(2, 2, 2 * CMP_HIDDEN, KV_WIDTH).astype(BF)
    return big, pe, w2


def _b_in_weight(w):
    n_gate = w.shape[1] - NSA_WIDTH - MEM_WIDTH
    gates = jnp.pad(w[:, NSA_WIDTH:NSA_WIDTH + n_gate], ((0, 0), (0, GATE_PAD - n_gate)))
    return jnp.concatenate([w[:, :NSA_WIDTH], w[:, NSA_WIDTH + n_gate:], gates], axis=1).astype(BF)


def _pad_rows(x, n):
    return jnp.pad(x, ((0, 0), (0, n - x.shape[1]), (0, 0)))


def _tail_page_t(new_rows):
    b = new_rows.shape[0]
    return _pad_rows(new_rows, PAGE).reshape(b, PAGE, 2, KV_HEADS, HEAD_DIM).transpose(0, 2, 3, 4, 1)


def _rows_t(rows):
    b, n, _ = rows.shape
    kv = rows.reshape(b, n, 2, KV_WIDTH).transpose(0, 2, 3, 1).astype(BF)
    return kv[:, 0], kv[:, 1]


def _row_tiles(s):
    return {"proj": min(s, 256), "gate": min(s, 512), "mlp": min(s, 512)}


def _trunk_a(x, mem_kv, p, l, *, chunk, tiles, emit_v, per_row_sets):
    if chunk == PAGE:
        ws, bs = p["w_spatial"][l], jnp.repeat(p["b_spatial"][l].T, A_GROUP_DIM, axis=1)
    else:
        reps = PAGE // chunk
        ws = jnp.tile(p["w_spatial"][l][:, :chunk, :chunk], (1, reps, reps))
        bs = jnp.tile(jnp.repeat(p["b_spatial"][l].T[:chunk], A_GROUP_DIM, axis=1), (reps, 1))
    res = _a_in(x, p["w_in_a"][l], p["ln_v_g"][l][None], p["ln_v_b"][l][None], ws, bs,
                chunk=chunk, tm=tiles["gate"], emit_v=emit_v, mem_kv=None if per_row_sets else mem_kv)
    mixed, om = res[0], res[1]
    if per_row_sets:
        om = _mem_attn_rows(om, mem_kv, chunk)
    x = _mlp_block(x, mixed, om, p["w_out_a"][l], p, l, tm=tiles["mlp"], a_transposed=False)
    return x, (res[2] if emit_v else None)


def _mlp_block(x, a, om, wo, p, l, *, tm, a_transposed):
    return _out_mlp(x, a, om, wo, p["ln1_g"][l][None], p["ln1_b"][l][None], p["w_up"][l], p["w_down"][l],
                    p["ln2_g"][l][None], p["ln2_b"][l][None], tm=tm, a_transposed=a_transposed)


def kernel(x_prompt, x_sample, cache_cmp_kv, cache_slc_kv, cache_win_kv, cache_mem_kv, page_table, mem_prompt,
           w_in_a, ln_v_g, ln_v_b, w_spatial, b_spatial, w_out_a, w_in_b, w_out_b, w_kv_shared, cmp_pe,
           w_phi1, w_phi2, w_mem_kv, ln1_g, ln1_b, ln2_g, ln2_b, w_up, w_down):
    bp, s, d = x_prompt.shape
    bs, t, _ = x_sample.shape
    n_pages = page_table.shape[1]
    past = n_pages * PAGE
    n_buf = cache_win_kv.shape[1]
    assert bs * t == PAGE and PAGE % t == 0 and s % PAGE == 0 and n_buf % PAGE == 0

    p = {"w_in_a": w_in_a.astype(BF), "ln_v_g": ln_v_g, "ln_v_b": ln_v_b, "w_spatial": w_spatial,
         "b_spatial": b_spatial, "w_out_a": w_out_a.astype(BF), "w_out_b": w_out_b.astype(BF),
         "ln1_g": ln1_g, "ln1_b": ln1_b, "ln2_g": ln2_g, "ln2_b": ln2_b,
         "w_up": w_up.astype(BF), "w_down": w_down.astype(BF)}
    w_in_b2 = [_b_in_weight(w_in_b[i]) for i in range(w_in_b.shape[0])]
    w_kv = w_kv_shared.astype(BF)
    cw1, cpe, cw2 = _compress_weights(cmp_pe, w_phi1, w_phi2)

    tl = _row_tiles(s)
    mem_kv_prompt = _mem_kv_proj(mem_prompt, w_mem_kv)
    x = x_prompt
    for l in range(N_A):
        x, _ = _trunk_a(x, mem_kv_prompt[l], p, l, chunk=PAGE, tiles=tl, emit_v=False, per_row_sets=False)
    tabs_p = _rope_tables(jnp.arange(s, dtype=jnp.int32))
    cmp_t, slc_t, win_t, ks, vst, kw, vwt = _kv_shared(x, w_kv, tabs_p, tm=tl["proj"], attn_layouts=True)
    shape_t = (2, KV_HEADS, HEAD_DIM)
    cmp_blk = _compress(cmp_t.reshape(bp, *shape_t, s), None, None, cw1, cpe, cw2)
    n_cmp = s // CMP_STRIDE - 1
    ncp = -(-n_cmp // PAGE) * PAGE
    kc, vct = _rows_layout(_pad_rows(cmp_blk[:, 1:1 + n_cmp], ncp))
    for i in range(DEPTH - N_A):
        l = N_A + i
        q, om, gt = _b_in(x, w_in_b2[i], tabs_p, tm=tl["proj"], mem_kv=mem_kv_prompt[l])
        ot = _nsa(q, gt, kc, vct, ks, vst, kw, vwt, t_base=0, wpos0=0, n_slc=s // SEL_BLOCK)
        x = _mlp_block(x, ot, om, p["w_out_b"][i], p, l, tm=tl["mlp"], a_transposed=True)
    y_prompt = x
    shape5 = (2, KV_HEADS, HEAD_DIM)

    def rows_major(x_t):
        return x_t.reshape(x_t.shape[0], *shape_t, x_t.shape[2]).transpose(0, 4, 1, 2, 3)

    cmp_kv_prompt = rows_major(cmp_t)
    slc_kv_prompt = rows_major(slc_t)
    win_kv_prompt = rows_major(win_t[:, :, s - min(WINDOW, s):])
    mem_kv_prompt_out = mem_kv_prompt.reshape(DEPTH, bp, MEM_TOKENS, 2, MEM_HEADS, HEAD_DIM)

    rows = bs * t
    x = x_sample.reshape(1, rows, d)
    mem_s = cache_mem_kv.reshape(DEPTH, bs, MEM_TOKENS, 2 * MEM_WIDTH)
    v_rows = []
    for l in range(N_A):
        x, v = _trunk_a(x, mem_s[l], p, l, chunk=t, tiles=_row_tiles(rows), emit_v=True, per_row_sets=True)
        v_rows.append(v.reshape(bs, t, A_WIDTH))
    pos_s = past + (jnp.arange(rows, dtype=jnp.int32) % t)
    tabs_s = _rope_tables(pos_s)
    new_cmp, new_slc, new_win = (r[0].T.reshape(bs, t, ROW_WIDTH)
                                 for r in _kv_shared(x, w_kv, tabs_s, tm=rows, attn_layouts=False))
    cmp_pool_t = cache_cmp_kv.transpose(0, 2, 3, 4, 1)
    slc_pool_t = cache_slc_kv.transpose(0, 2, 3, 4, 1)
    cmp_blk = _compress(cmp_pool_t, page_table, _tail_page_t(new_cmp), cw1, cpe, cw2)
    n_cmp = -(-(past + t) // CMP_STRIDE) - 1
    ncp = -(-n_cmp // PAGE) * PAGE
    kct, vct = _rows_t(_pad_rows(cmp_blk[:, 1:1 + n_cmp], ncp))
    win_all = jnp.concatenate([cache_win_kv.reshape(bs, n_buf, ROW_WIDTH), new_win], axis=1)
    kwt, vwt = _rows_t(_pad_rows(win_all, n_buf + PAGE))
    slc_tail_t = _tail_page_t(new_slc)
    n_slc = -(-(past + t) // SEL_BLOCK)
    per_group = GQA * t
    eye = jnp.eye(KV_HEADS, dtype=BF)
    for i in range(DEPTH - N_A):
        l = N_A + i
        q, qm, gt = _b_in(x, w_in_b2[i], tabs_s, tm=rows)
        om = _mem_attn_rows(qm, mem_s[l], t)
        q = q.reshape(KV_HEADS, GQA, bs, t, HEAD_DIM).transpose(2, 0, 1, 3, 4).reshape(bs, KV_HEADS, per_group, HEAD_DIM)
        q = jnp.pad(q, ((0, 0), (0, 0), (0, ROWS_PER_GROUP - per_group), (0, 0)))
        q_bd = jnp.einsum('bgxd,gh->bgxhd', q, eye).reshape(bs, SAMPLE_ROWS, KV_WIDTH)
        gs = gt[0, :NSA_HEADS * 3].reshape(KV_HEADS, GQA, 3, bs, t).transpose(3, 0, 1, 4, 2)
        gs = gs.reshape(bs, KV_HEADS, per_group, 3)
        gs = jnp.pad(gs, ((0, 0), (0, 0), (0, ROWS_PER_GROUP - per_group), (0, LANES - 3)))
        o = _nsa_sample(q_bd, gs.reshape(bs, SAMPLE_ROWS, LANES), slc_pool_t, page_table, slc_tail_t,
                        kct, vct, kwt, vwt, t_base=past, wpos0=past - n_buf, n_slc=n_slc, t_q=t)
        o = o.reshape(bs, KV_HEADS, ROWS_PER_GROUP, KV_HEADS, HEAD_DIM)
        o = jnp.stack([o[:, g, :per_group, g] for g in range(KV_HEADS)], axis=1)
        o = o.reshape(bs, KV_HEADS, GQA, t, HEAD_DIM).transpose(0, 3, 1, 2, 4).reshape(1, rows, NSA_WIDTH)
        x = _mlp_block(x, o, om, p["w_out_b"][i], p, l, tm=rows, a_transposed=False)
    y_sample = x.reshape(bs, t, d)
    cmp_kv_sample = new_cmp.reshape(bs, t, *shape5)
    slc_kv_sample = new_slc.reshape(bs, t, *shape5)
    win_kv_sample = win_all[:, t:].reshape(bs, n_buf, *shape5)
    gmlp_v_sample = jnp.stack(v_rows)

    return (y_prompt, y_sample, cmp_kv_prompt, slc_kv_prompt, win_kv_prompt, mem_kv_prompt_out,
            cmp_kv_sample, slc_kv_sample, win_kv_sample, gmlp_v_sample)
```

```python
import functools
import math

import jax
import jax.numpy as jnp
from jax import lax
from jax.experimental import pallas as pl
from jax.experimental.pallas import tpu as pltpu

BF = jnp.bfloat16
F32 = jnp.float32

DEPTH = 4
N_A = 2
A_WIDTH = 768
A_GROUPS = 4
A_GROUP_DIM = 192
MEM_TOKENS = 256
MEM_HEADS = 4
MEM_WIDTH = 256
HEAD_DIM = 64
NSA_HEADS = 12
KV_HEADS = 4
GQA = 3
NSA_WIDTH = 768
KV_WIDTH = 256
ROW_WIDTH = 2 * KV_WIDTH
CMP_BLOCK = 32
CMP_STRIDE = 16
CMP_HIDDEN = 128
SEL_BLOCK = 64
SEL_TOPK = 16
WINDOW = 512
ROPE_THETA = 500000.0
ROPE_DIM = 16
D_FF_CHUNK = 1024
ALPHA = (2.0 * DEPTH) ** 0.25
LN_EPS = 1e-5
NEG = -1e30
FORCE_BONUS = 1e4
OUT_OF_WINDOW = -(1 << 30)
PAGE = 128
LANES = 128
SUBLANES = 8
MAX_PAGE_SLOTS = 16
GATE_PAD = 128
SEL_GROUP = 4
NSA_QUERY_TILE = 256
Q_SCALE = HEAD_DIM ** -0.5 * math.log2(math.e)
VMEM_LIMIT = 56 * 1024 * 1024


def _cparams(sem, vmem=None):
    return pltpu.CompilerParams(dimension_semantics=sem, vmem_limit_bytes=vmem)


def _const_spec(shape):
    return pl.BlockSpec(shape, lambda *_: (0,) * len(shape), pipeline_mode=pl.Buffered(1))


def _ln(x, g, b):
    mu = jnp.mean(x, axis=-1, keepdims=True)
    xc = x - mu
    var = jnp.mean(xc * xc, axis=-1, keepdims=True)
    return xc * lax.rsqrt(var + LN_EPS) * g + b


def _dot(a, b):
    return jnp.dot(a, b, preferred_element_type=F32)


def _dot_nt(a, b):
    return lax.dot_general(a, b, (((1,), (1,)), ((), ())), preferred_element_type=F32)


def _rope_cols(x, c, a, b):
    cols = []
    for j in range(x.shape[1] // LANES):
        xc = x[:, j * LANES:(j + 1) * LANES]
        up = pltpu.roll(xc, LANES - ROPE_DIM // 2, axis=1)
        dn = pltpu.roll(xc, ROPE_DIM // 2, axis=1)
        cols.append(xc * c + up * a + dn * b)
    return cols[0] if len(cols) == 1 else jnp.concatenate(cols, axis=1)


def _mem_kv_kernel(x_ref, w_ref, o_ref):
    o_ref[0] = _dot(x_ref[...].astype(BF), w_ref[0].astype(BF))


def _mem_kv_proj(mem_prompt, w_mem_kv):
    b, m, d = mem_prompt.shape
    x = mem_prompt.reshape(b * m, d)
    depth, _, n = w_mem_kv.shape
    tm = _row_tiles(b * m)["mlp"]
    out = pl.pallas_call(
        _mem_kv_kernel,
        grid=(depth, (b * m) // tm),
        in_specs=[pl.BlockSpec((tm, d), lambda l, i: (i, 0)),
                  pl.BlockSpec((1, d, n), lambda l, i: (l, 0, 0))],
        out_specs=pl.BlockSpec((1, tm, n), lambda l, i: (l, i, 0)),
        out_shape=jax.ShapeDtypeStruct((depth, b * m, n), F32),
        compiler_params=_cparams(("parallel", "parallel")),
        name="mem_kv_proj",
    )(x, w_mem_kv)
    return out.reshape(depth, b, m, n)


def _mem_heads(q, kv, own=None):
    k = kv[:, :MEM_WIDTH].astype(BF)
    v = kv[:, MEM_WIDTH:]
    lane = lax.broadcasted_iota(jnp.int32, (1, MEM_WIDTH), 1)
    out = jnp.zeros(q.shape, F32)
    for h in range(MEM_HEADS):
        hm = (lane >= h * HEAD_DIM) & (lane < (h + 1) * HEAD_DIM)
        s = _dot_nt(jnp.where(hm, q, 0.0).astype(BF), k) * (HEAD_DIM ** -0.5)
        if own is not None:
            s = jnp.where(own, s, NEG)
        m = jnp.max(s, axis=-1, keepdims=True)
        p = jnp.exp(s - m)
        p = p / jnp.sum(p, axis=-1, keepdims=True)
        out = out + _dot(p.astype(BF), jnp.where(hm, v, 0.0).astype(BF))
    return out


def _a_in_kernel(x_ref, w_ref, g_ref, b_ref, ws_ref, bs_ref, *rest, chunk_shift, fuse_mem, emit_v):
    mem_ref = rest[0] if fuse_mem else None
    mixed_ref, qm_ref = rest[int(fuse_mem):int(fuse_mem) + 2]
    x = x_ref[0].astype(BF)
    proj = _dot(x, w_ref[...])
    uv = jax.nn.gelu(proj[:, :2 * A_WIDTH])
    u = uv[:, :A_WIDTH]
    v = _ln(uv[:, A_WIDTH:], g_ref[...], b_ref[...])
    if emit_v:
        rest[-1][0] = v
    qm = proj[:, 2 * A_WIDTH:]
    qm_ref[0] = _mem_heads(qm, mem_ref[0]) if fuse_mem else qm
    tm = x.shape[0]
    r = ws_ref.shape[1]
    row = lax.broadcasted_iota(jnp.int32, (r, r), 0)
    col = lax.broadcasted_iota(jnp.int32, (r, r), 1)
    allowed = (col <= row) & ((row >> chunk_shift) == (col >> chunk_shift))
    wgs = [jnp.where(allowed, ws_ref[g], 0.0).astype(BF) for g in range(A_GROUPS)]
    span = 2 * A_GROUP_DIM
    first = lax.broadcasted_iota(jnp.int32, (1, span), 1) < A_GROUP_DIM
    vb = v.astype(BF)
    for ci in range(tm // r):
        rows = slice(ci * r, (ci + 1) * r)
        for h in range(A_GROUPS // 2):
            cols = slice(h * span, (h + 1) * span)
            vc = vb[rows, cols]
            mix = jnp.where(first, _dot(wgs[2 * h], vc), _dot(wgs[2 * h + 1], vc)) + bs_ref[:, cols]
            mixed_ref[0, rows, cols] = (u[rows, cols] * mix).astype(BF)


def _a_in(x, w_in, ln_g, ln_b, ws, bs, *, chunk, tm, emit_v, mem_kv=None):
    b, s, d = x.shape
    n = w_in.shape[1]
    r = ws.shape[1]
    in_specs = [pl.BlockSpec((1, tm, d), lambda i, j: (i, j, 0)),
                _const_spec((d, n)), _const_spec((1, A_WIDTH)), _const_spec((1, A_WIDTH)),
                _const_spec((A_GROUPS, r, r)), _const_spec((r, A_WIDTH))]
    args = [x, w_in, ln_g, ln_b, ws, bs]
    if mem_kv is not None:
        in_specs.append(pl.BlockSpec((1,) + mem_kv.shape[1:], lambda i, j: (i, 0, 0)))
        args.append(mem_kv)
    outs = [jax.ShapeDtypeStruct((b, s, A_WIDTH), BF), jax.ShapeDtypeStruct((b, s, MEM_WIDTH), F32)]
    ospecs = [pl.BlockSpec((1, tm, A_WIDTH), lambda i, j: (i, j, 0)),
              pl.BlockSpec((1, tm, MEM_WIDTH), lambda i, j: (i, j, 0))]
    if emit_v:
        outs.append(jax.ShapeDtypeStruct((b, s, A_WIDTH), F32))
        ospecs.append(pl.BlockSpec((1, tm, A_WIDTH), lambda i, j: (i, j, 0)))
    return pl.pallas_call(
        functools.partial(_a_in_kernel, chunk_shift=int(math.log2(chunk)), fuse_mem=mem_kv is not None,
                          emit_v=emit_v),
        grid=(b, s // tm),
        in_specs=in_specs,
        out_specs=ospecs,
        out_shape=outs,
        compiler_params=_cparams(("parallel", "parallel"), VMEM_LIMIT),
        name="a_in_gate",
    )(*args)


SAMPLE_MEM_SETS = 8


def _mem_attn_kernel(q_ref, kv_ref, o_ref, *, rows_per_set):
    q = q_ref[0]
    nb = kv_ref.shape[0]
    row_set = lax.broadcasted_iota(jnp.int32, (q.shape[0], 1), 0) >> int(math.log2(rows_per_set))
    col_set = lax.broadcasted_iota(jnp.int32, (1, nb * MEM_TOKENS), 1) >> int(math.log2(MEM_TOKENS))
    o_ref[0] = _mem_heads(q, kv_ref[...].reshape(nb * MEM_TOKENS, 2 * MEM_WIDTH), row_set == col_set)


def _mem_attn_rows(q, mem_kv, rows_per_set):
    _, s, w = q.shape
    tq = SAMPLE_MEM_SETS * rows_per_set
    return pl.pallas_call(
        functools.partial(_mem_attn_kernel, rows_per_set=rows_per_set),
        grid=(s // tq,),
        in_specs=[pl.BlockSpec((1, tq, w), lambda j: (0, j, 0)),
                  pl.BlockSpec((SAMPLE_MEM_SETS, MEM_TOKENS, 2 * w), lambda j: (j, 0, 0))],
        out_specs=pl.BlockSpec((1, tq, w), lambda j: (0, j, 0)),
        out_shape=jax.ShapeDtypeStruct((1, s, w), F32),
        compiler_params=_cparams(("parallel",)),
        name="mem_attn",
    )(q, mem_kv)


def _out_mlp_kernel(x_ref, a_ref, om_ref, wo_ref, g1_ref, b1_ref, wu_ref, wd_ref, g2_ref, b2_ref, o_ref,
                    *, a_transposed):
    x = x_ref[0]
    a = a_ref[0]
    a = a.T.astype(BF) if a_transposed else a.astype(BF)
    na = a.shape[1]
    mix = _dot(a, wo_ref[:na, :]) + _dot(om_ref[0].astype(BF), wo_ref[na:, :])
    y = _ln(ALPHA * x + mix, g1_ref[...], b1_ref[...])
    yb = y.astype(BF)
    acc = jnp.zeros(x.shape, F32)
    for c in range(wu_ref.shape[1] // D_FF_CHUNK):
        h = _dot(yb, wu_ref[:, c * D_FF_CHUNK:(c + 1) * D_FF_CHUNK])
        h = jnp.square(jnp.maximum(h, 0.0)).astype(BF)
        acc = acc + _dot(h, wd_ref[c * D_FF_CHUNK:(c + 1) * D_FF_CHUNK, :])
    o_ref[0] = _ln(ALPHA * y + acc, g2_ref[...], b2_ref[...])


def _out_mlp(x, a, om, wo, g1, b1, wu, wd, g2, b2, *, tm, a_transposed):
    b, s, d = x.shape
    na = wo.shape[0] - om.shape[2]
    dff = wu.shape[1]
    if a_transposed:
        a_spec = pl.BlockSpec((1, na, tm), lambda i, j: (i, 0, j))
    else:
        a_spec = pl.BlockSpec((1, tm, na), lambda i, j: (i, j, 0))
    return pl.pallas_call(
        functools.partial(_out_mlp_kernel, a_transposed=a_transposed),
        grid=(b, s // tm),
        in_specs=[pl.BlockSpec((1, tm, d), lambda i, j: (i, j, 0)),
                  a_spec,
                  pl.BlockSpec((1, tm, om.shape[2]), lambda i, j: (i, j, 0)),
                  _const_spec(wo.shape), _const_spec((1, d)), _const_spec((1, d)),
                  _const_spec((d, dff)), _const_spec((dff, d)), _const_spec((1, d)), _const_spec((1, d))],
        out_specs=pl.BlockSpec((1, tm, d), lambda i, j: (i, j, 0)),
        out_shape=jax.ShapeDtypeStruct((b, s, d), F32),
        compiler_params=_cparams(("parallel", "parallel"), VMEM_LIMIT),
        name="out_mlp",
    )(x, a, om, wo, g1, b1, wu, wd, g2, b2)


def _kv_shared_kernel(h_ref, w_ref, c_ref, a_ref, b_ref, cmp_ref, slc_ref, win_ref, *attn_refs):
    kv = _dot(h_ref[0].astype(BF), w_ref[...])
    c, a, b = c_ref[...], a_ref[...], b_ref[...]
    tm = kv.shape[0]
    for br, ref in enumerate((cmp_ref, slc_ref, win_ref)):
        k = _rope_cols(kv[:, br * ROW_WIDTH:br * ROW_WIDTH + KV_WIDTH], c, a, b)
        v = kv[:, br * ROW_WIDTH + KV_WIDTH:(br + 1) * ROW_WIDTH]
        vt = v.T
        ref[0, :KV_WIDTH, :] = k.T
        ref[0, KV_WIDTH:, :] = vt
        if attn_refs and br > 0:
            k_ref, vt_ref = attn_refs[2 * (br - 1):2 * br]
            if br == 1:
                lane = lax.broadcasted_iota(jnp.int32, (1, LANES), 1)
                row = pl.program_id(1) * tm + lax.broadcasted_iota(jnp.int32, (tm, 1), 0)
                block_hot = jnp.where(lane - HEAD_DIM == (row >> int(math.log2(SEL_BLOCK))), 1.0, 0.0)
            for g in range(KV_HEADS):
                if br == 1:
                    pair = k[:, (g // 2) * LANES:(g // 2 + 1) * LANES]
                    if g % 2:
                        pair = pltpu.roll(pair, HEAD_DIM, axis=1)
                    k_ref[0, g] = jnp.where(lane < HEAD_DIM, pair, block_hot).astype(BF)
                else:
                    k_ref[0, g] = k[:, g * HEAD_DIM:(g + 1) * HEAD_DIM].astype(BF)
                for u in range(tm // PAGE):
                    vt_ref[0, g, u] = vt[g * HEAD_DIM:(g + 1) * HEAD_DIM, u * PAGE:(u + 1) * PAGE].astype(BF)


def _kv_shared(h, w_kv, tabs, *, tm, attn_layouts):
    b, s, d = h.shape
    row_spec = pl.BlockSpec((1, ROW_WIDTH, tm), lambda i, j: (i, 0, j))
    tab_spec = pl.BlockSpec((tm, LANES), lambda i, j: (j, 0))
    out_specs = [row_spec, row_spec, row_spec]
    out_shape = [jax.ShapeDtypeStruct((b, ROW_WIDTH, s), F32)] * 3
    if attn_layouts:
        assert s // SEL_BLOCK <= LANES - HEAD_DIM
        vt_spec = pl.BlockSpec((1, KV_HEADS, tm // PAGE, HEAD_DIM, PAGE), lambda i, j: (i, 0, j, 0, 0))
        vt_shape = jax.ShapeDtypeStruct((b, KV_HEADS, s // PAGE, HEAD_DIM, PAGE), BF)
        for width in (LANES, HEAD_DIM):
            out_specs += [pl.BlockSpec((1, KV_HEADS, tm, width), lambda i, j: (i, 0, j, 0)), vt_spec]
            out_shape += [jax.ShapeDtypeStruct((b, KV_HEADS, s, width), BF), vt_shape]
    return pl.pallas_call(
        _kv_shared_kernel,
        grid=(b, s // tm),
        in_specs=[pl.BlockSpec((1, tm, d), lambda i, j: (i, j, 0)), _const_spec(w_kv.shape),
                  tab_spec, tab_spec, tab_spec],
        out_specs=out_specs,
        out_shape=out_shape,
        compiler_params=_cparams(("parallel", "parallel")),
        name="kv_shared",
    )(h, w_kv, *tabs)


def _b_in_kernel(x_ref, w_ref, c_ref, a_ref, b_ref, *rest, fuse_mem):
    mem_ref = rest[0] if fuse_mem else None
    q_ref, qm_ref, gt_ref = rest[int(fuse_mem):]
    proj = _dot(x_ref[0].astype(BF), w_ref[...])
    q = _rope_cols(proj[:, :NSA_WIDTH], c_ref[...], a_ref[...], b_ref[...]) * Q_SCALE
    for h in range(NSA_HEADS):
        q_ref[0, h] = q[:, h * HEAD_DIM:(h + 1) * HEAD_DIM].astype(BF)
    qm = proj[:, NSA_WIDTH:NSA_WIDTH + MEM_WIDTH]
    qm_ref[0] = _mem_heads(qm, mem_ref[0]) if fuse_mem else qm
    gt_ref[0] = jax.nn.sigmoid(proj[:, NSA_WIDTH + MEM_WIDTH:]).T


def _b_in(x, w_in, tabs, *, tm, mem_kv=None):
    b, s, d = x.shape
    tab_spec = pl.BlockSpec((tm, LANES), lambda i, j: (j, 0))
    in_specs = [pl.BlockSpec((1, tm, d), lambda i, j: (i, j, 0)), _const_spec(w_in.shape),
                tab_spec, tab_spec, tab_spec]
    args = [x, w_in, *tabs]
    if mem_kv is not None:
        in_specs.append(pl.BlockSpec((1,) + mem_kv.shape[1:], lambda i, j: (i, 0, 0)))
        args.append(mem_kv)
    return pl.pallas_call(
        functools.partial(_b_in_kernel, fuse_mem=mem_kv is not None),
        grid=(b, s // tm),
        in_specs=in_specs,
        out_specs=[pl.BlockSpec((1, NSA_HEADS, tm, HEAD_DIM), lambda i, j: (i, 0, j, 0)),
                   pl.BlockSpec((1, tm, MEM_WIDTH), lambda i, j: (i, j, 0)),
                   pl.BlockSpec((1, GATE_PAD, tm), lambda i, j: (i, 0, j))],
        out_shape=[jax.ShapeDtypeStruct((b, NSA_HEADS, s, HEAD_DIM), BF),
                   jax.ShapeDtypeStruct((b, s, MEM_WIDTH), F32),
                   jax.ShapeDtypeStruct((b, GATE_PAD, s), F32)],
        compiler_params=_cparams(("parallel", "parallel")),
        name="b_in",
    )(*args)


def _kv_layout_kernel(x_ref, k_ref, vt_ref):
    x = x_ref[0]
    vt = x[:, KV_WIDTH:].T
    for g in range(KV_HEADS):
        k_ref[0, g] = x[:, g * HEAD_DIM:(g + 1) * HEAD_DIM].astype(BF)
        vt_ref[0, g, 0] = vt[g * HEAD_DIM:(g + 1) * HEAD_DIM].astype(BF)


def _rows_layout(rows):
    b, l, _ = rows.shape
    n = l // PAGE
    return pl.pallas_call(
        _kv_layout_kernel,
        grid=(b, n),
        in_specs=[pl.BlockSpec((1, PAGE, ROW_WIDTH), lambda i, p: (i, p, 0))],
        out_specs=[pl.BlockSpec((1, KV_HEADS, PAGE, HEAD_DIM), lambda i, p: (i, 0, p, 0)),
                   pl.BlockSpec((1, KV_HEADS, 1, HEAD_DIM, PAGE), lambda i, p: (i, 0, p, 0, 0))],
        out_shape=[jax.ShapeDtypeStruct((b, KV_HEADS, l, HEAD_DIM), BF),
                   jax.ShapeDtypeStruct((b, KV_HEADS, n, HEAD_DIM, PAGE), BF)],
        compiler_params=_cparams(("parallel", "parallel")),
        name="kv_layout",
    )(rows)


def _compress_kernel(pt_ref, *refs, n_slots, tail_group):
    page_refs = refs[:n_slots]
    rest = refs[n_slots:]
    if tail_group is not None:
        tail_ref, rest = rest[0], rest[1:]
    w1_ref, pe_ref, w2_ref, o_ref, xs_ref, carry_ref = rest
    k = pl.program_id(1)
    per_page = PAGE // CMP_STRIDE
    m = n_slots * per_page
    hid2 = 2 * CMP_HIDDEN

    @pl.when(k == 0)
    def _():
        carry_ref[...] = jnp.zeros(carry_ref.shape, F32)

    for i, pr in enumerate(page_refs):
        x = pr[0]
        if tail_group is not None and i == n_slots - 1:
            x = jnp.where(k == tail_group, tail_ref[0], x)
        rows = slice(i * PAGE, (i + 1) * PAGE)
        for c in range(2):
            xc = x[c].reshape(KV_WIDTH, PAGE).T
            for hp in range(2):
                xs_ref[2 * c + hp, rows, :] = xc[:, hp * LANES:(hp + 1) * LANES]

    def strides(vcol):
        return jnp.concatenate(
            [xs_ref[vcol, pl.ds(s, m, stride=CMP_STRIDE), :] for s in range(CMP_STRIDE)], axis=1)

    first_row = lax.broadcasted_iota(jnp.int32, (m, 1), 0) == 0
    out_cols = []
    for c in range(2):
        w1 = w1_ref[c]
        lhs = jnp.concatenate([strides(2 * c + hp) for hp in range(2)] + [pe_ref[c, 0], pe_ref[c, 1]],
                              axis=0).astype(BF)
        a = _dot(lhs, w1)
        hpe = a[2 * m:2 * m + 1, :hid2] + a[2 * m + 8:2 * m + 9, hid2:]
        acc = jnp.zeros((m, KV_WIDTH), F32)
        for hp in range(2):
            a0 = a[hp * m:(hp + 1) * m, :hid2]
            a1 = a[hp * m:(hp + 1) * m, hid2:]
            prev = jnp.where(first_row, carry_ref[c, hp, 0:1, :], pltpu.roll(a0, 1, axis=0))
            carry_ref[c, hp, 0:1, :] = a0[m - 1:m]
            acc = acc + _dot(jax.nn.gelu(prev + a1 + hpe).astype(BF), w2_ref[c, hp])
        out_cols.append(acc)
    o_ref[0] = jnp.concatenate(out_cols, axis=1)


def _compress(pool, table, tail, w1, pe, w2):
    page_shape = pool.shape[1:-1] + (PAGE,)
    zeros = (0,) * (len(page_shape) - 1)
    paged = table is not None
    if paged:
        b, n_real = table.shape
    else:
        b, n_real = pool.shape[0], pool.shape[-1] // PAGE
        table = jnp.zeros((1, 1), jnp.int32)
    n_pages = n_real + (0 if tail is None else 1)
    n_slots = max(p for p in range(1, MAX_PAGE_SLOTS + 1) if n_pages % p == 0)
    n_groups = n_pages // n_slots
    m = n_slots * (PAGE // CMP_STRIDE)

    def page_spec(slot):
        if paged:
            return pl.BlockSpec(
                (1,) + page_shape,
                lambda i, k, pt: (pt[i * n_real + jnp.minimum(k * n_slots + slot, n_real - 1)],) + zeros + (0,))
        return pl.BlockSpec((1,) + page_shape, lambda i, k, pt: (i,) + zeros + (k * n_slots + slot,))

    in_specs = [page_spec(slot) for slot in range(n_slots)]
    args = [pool] * n_slots
    if tail is not None:
        in_specs.append(pl.BlockSpec((1,) + page_shape, lambda i, k, pt: (i,) + zeros + (0,)))
        args.append(tail)
    in_specs += [pl.BlockSpec(x.shape, lambda i, k, pt, nd=x.ndim: (0,) * nd) for x in (w1, pe, w2)]
    return pl.pallas_call(
        functools.partial(_compress_kernel, n_slots=n_slots,
                          tail_group=None if tail is None else n_groups - 1),
        grid_spec=pltpu.PrefetchScalarGridSpec(
            num_scalar_prefetch=1, grid=(b, n_groups), in_specs=in_specs,
            out_specs=pl.BlockSpec((1, m, ROW_WIDTH), lambda i, k, pt: (i, k, 0)),
            scratch_shapes=[pltpu.VMEM((4, n_slots * PAGE, LANES), F32),
                            pltpu.VMEM((2, 2, 8, 2 * CMP_HIDDEN), F32)]),
        out_shape=jax.ShapeDtypeStruct((b, n_groups * m, ROW_WIDTH), F32),
        compiler_params=_cparams(("parallel", "arbitrary"), VMEM_LIMIT),
        name="compress",
    )(table.reshape(-1), *args, w1, pe, w2)


def _nsa_kernel(q_ref, gt_ref, kc_ref, vct_ref, ks_ref, vst_ref, kw_ref, vwt_ref, o_ref, psum_ref,
                sbuf0_ref, sbuf1_ref, *, t_base, wpos0, n_slc, nsp):
    g = pl.program_id(1)
    i = pl.program_id(2)
    nq = q_ref.shape[2]
    nl = GQA * nq
    ncp = kc_ref.shape[2]
    gk = SEL_GROUP * PAGE
    nkt_s = vst_ref.shape[2]
    q = q_ref[0].reshape(nl, HEAD_DIM)
    lane = lax.broadcasted_iota(jnp.int32, (1, nl), 1)
    q0 = t_base + i * nq
    t = q0 + (lane & (nq - 1))

    s = _dot_nt(kc_ref[0, 0], q)
    n_idx = lax.broadcasted_iota(jnp.int32, (ncp, 1), 0)
    cmask = n_idx * CMP_STRIDE + (CMP_BLOCK - 1) <= t
    s = jnp.where(cmask, s, NEG)
    p = jnp.where(cmask, jnp.exp2(s - jnp.max(s, axis=0, keepdims=True)), 0.0)
    l = jnp.sum(p, axis=0, keepdims=True)
    p = p * jnp.where(l > 0.0, 1.0 / l, 0.0)
    o_cmp = jnp.zeros((HEAD_DIM, nl), F32)
    for kt in range(ncp // PAGE):
        o_cmp = o_cmp + _dot(vct_ref[0, 0, kt], p[kt * PAGE:(kt + 1) * PAGE].astype(BF))

    rel = (q0 - wpos0) // PAGE
    n_wt = (WINDOW + nq) // PAGE
    tile_row = lax.broadcasted_iota(jnp.int32, (PAGE, 1), 0)
    parts = []
    for u in range(n_wt):
        kt = rel - WINDOW // PAGE + u
        ktc = jnp.maximum(kt, 0)
        kb = kw_ref[0, 0, pl.ds(pl.multiple_of(ktc * PAGE, PAGE), PAGE), :]
        krel = kt * PAGE + tile_row
        kpos = jnp.where(krel >= 0, wpos0 + krel, OUT_OF_WINDOW)
        d = lax.bitcast_convert_type(t - kpos, jnp.uint32)
        parts.append(jnp.where(d < WINDOW, _dot_nt(kb, q), NEG))
    sw = jnp.concatenate(parts, axis=0)
    pw = jnp.exp2(sw - jnp.max(sw, axis=0, keepdims=True))
    l_w = jnp.sum(pw, axis=0, keepdims=True)
    acc_w = jnp.zeros((HEAD_DIM, nl), F32)
    for u in range(n_wt):
        ktc = jnp.maximum(rel - WINDOW // PAGE + u, 0)
        acc_w = acc_w + _dot(vwt_ref[0, 0, ktc], pw[u * PAGE:(u + 1) * PAGE].astype(BF))
    o_win = acc_w * (1.0 / l_w)

    psum = p[:, :nq]
    for r in range(1, GQA):
        psum = psum + p[:, r * nq:(r + 1) * nq]
    front = SUBLANES
    ratio = SEL_BLOCK // CMP_STRIDE
    slabs = []
    for sl in range(nq // LANES):
        psum_ref[sl, :front, :] = jnp.zeros((front, LANES), F32)
        psum_ref[sl, front:front + ncp, :] = psum[:, sl * LANES:(sl + 1) * LANES]
        psum_ref[sl, front + ncp:, :] = jnp.zeros((psum_ref.shape[1] - front - ncp, LANES), F32)
        part = psum_ref[sl, pl.ds(front - 1, nsp, stride=ratio), :]
        for d in range(1, ratio + 1):
            part = part + psum_ref[sl, pl.ds(front - 1 + d, nsp, stride=ratio), :]
        slabs.append(part)
    imp = slabs[0] if len(slabs) == 1 else jnp.concatenate(slabs, axis=1)
    tq = t[:, :nq]
    j = lax.broadcasted_iota(jnp.int32, (nsp, 1), 0)
    cur = tq >> int(math.log2(SEL_BLOCK))
    forced = (j == 0) | (j == cur) | (j == cur - 1)
    score = jnp.where(j * SEL_BLOCK <= tq, imp + FORCE_BONUS * forced.astype(F32), NEG)
    sub = SUBLANES
    chunks = [score[c * sub:(c + 1) * sub] for c in range(nsp // sub)]
    jc = j[:sub]
    cnts = [jnp.zeros((sub, nq), F32) for _ in chunks]
    for ii in range(n_slc):
        ri = score[ii:ii + 1, :]
        for c, sc_c in enumerate(chunks):
            if c * sub > ii:
                one = jnp.where(ri >= sc_c, 1.0, 0.0)
            elif c * sub + sub - 1 <= ii:
                one = jnp.where(ri > sc_c, 1.0, 0.0)
            else:
                one = jnp.where(jc + c * sub > ii, jnp.where(ri >= sc_c, 1.0, 0.0), jnp.where(ri > sc_c, 1.0, 0.0))
            cnts[c] = cnts[c] + one
    cnt = jnp.concatenate(cnts, axis=0)
    bias = jnp.where(cnt < float(SEL_TOPK), jnp.where(score > NEG / 2, 0.0, NEG), NEG)
    bias_t = jnp.concatenate([bias, jnp.zeros((LANES - nsp, nq), F32)], axis=0).T[:, :LANES - HEAD_DIM]
    q_ext = jnp.concatenate([q, jnp.concatenate([bias_t.astype(BF)] * GQA, axis=0)], axis=1)

    key_row = lax.broadcasted_iota(jnp.int32, (gk, 1), 0)

    def fill(buf, gi):
        kb = ks_ref[0, 0, pl.ds(pl.multiple_of(gi * gk, gk), gk), :]
        buf[...] = _dot_nt(kb, q_ext)

    def consume(buf, gi, carry, causal):
        m_old, l_old, acc = carry
        sc = buf[...]
        if causal:
            sc = jnp.where(gi * gk + key_row <= t, sc, NEG)
        m_new = jnp.maximum(m_old, jnp.max(sc, axis=0, keepdims=True))
        alpha = jnp.exp2(m_old - m_new)
        pe = jnp.exp2(sc - m_new)
        acc = acc * alpha
        for u in range(SEL_GROUP):
            acc = acc + _dot(vst_ref[0, 0, gi * SEL_GROUP + u], pe[u * PAGE:(u + 1) * PAGE].astype(BF))
        return m_new, l_old * alpha + jnp.sum(pe, axis=0, keepdims=True), acc

    init = (jnp.full((1, nl), NEG, F32), jnp.zeros((1, nl), F32), jnp.zeros((HEAD_DIM, nl), F32))

    n_groups = jnp.minimum(nkt_s - 1, (q0 + nq - 1) // PAGE) // SEL_GROUP + 1
    fill(sbuf0_ref, 0)

    def pair_body(kk, carry):
        gi = 2 * kk
        fill(sbuf1_ref, gi + 1)
        carry = consume(sbuf0_ref, gi, carry, False)
        fill(sbuf0_ref, gi + 2)
        return consume(sbuf1_ref, gi + 1, carry, False)

    n_pairs = (n_groups - 1) // 2
    carry = lax.fori_loop(0, n_pairs, pair_body, init)
    gi = 2 * n_pairs

    def odd_tail(carry):
        fill(sbuf1_ref, gi + 1)
        carry = consume(sbuf0_ref, gi, carry, False)
        return consume(sbuf1_ref, gi + 1, carry, True)

    def even_tail(carry):
        return consume(sbuf0_ref, gi, carry, True)

    _, l_s, acc_s = lax.cond(n_groups - 1 - gi == 1, odd_tail, even_tail, carry)

    o_sel = acc_s * jnp.where(l_s > 0.0, 1.0 / l_s, 0.0)
    for r in range(GQA):
        sl = slice(r * nq, (r + 1) * nq)
        grow = (GQA * g + r) * 3
        o = (gt_ref[0, pl.ds(grow, 1), :] * o_cmp[:, sl]
             + gt_ref[0, pl.ds(grow + 1, 1), :] * o_sel[:, sl]
             + gt_ref[0, pl.ds(grow + 2, 1), :] * o_win[:, sl])
        o_ref[0, r * HEAD_DIM:(r + 1) * HEAD_DIM, :] = o


def _nsa(q, gt, kc, vct, ks, vst, kw, vwt, *, t_base, wpos0, n_slc):
    b, _, sq, _ = q.shape
    nq = NSA_QUERY_TILE
    assert sq % nq == 0 and t_base % PAGE == 0 and (t_base - wpos0) % PAGE == 0 and t_base >= wpos0
    assert vst.shape[2] % SEL_GROUP == 0
    nsp = max(-(-n_slc // 8) * 8, vst.shape[2] * (PAGE // SEL_BLOCK))
    assert nsp <= LANES - HEAD_DIM and ks.shape[3] == LANES
    ncp = kc.shape[2]
    psum_rows = 8 + max(ncp + 8, (SEL_BLOCK // CMP_STRIDE) * nsp + 8)
    psum_rows = -(-psum_rows // 8) * 8

    def kv_specs(k, vt):
        return [pl.BlockSpec((1, 1) + k.shape[2:], lambda bi, gi, ii: (bi, gi, 0, 0)),
                pl.BlockSpec((1, 1) + vt.shape[2:], lambda bi, gi, ii: (bi, gi, 0, 0, 0))]

    return pl.pallas_call(
        functools.partial(_nsa_kernel, t_base=t_base, wpos0=wpos0, n_slc=n_slc, nsp=nsp),
        grid=(b, KV_HEADS, sq // nq),
        in_specs=[pl.BlockSpec((1, GQA, nq, HEAD_DIM), lambda bi, gi, ii: (bi, gi, ii, 0)),
                  pl.BlockSpec((1, GATE_PAD, nq), lambda bi, gi, ii: (bi, 0, ii))]
        + kv_specs(kc, vct) + kv_specs(ks, vst) + kv_specs(kw, vwt),
        out_specs=pl.BlockSpec((1, GQA * HEAD_DIM, nq), lambda bi, gi, ii: (bi, gi, ii)),
        out_shape=jax.ShapeDtypeStruct((b, NSA_WIDTH, sq), F32),
        scratch_shapes=[pltpu.VMEM((nq // LANES, psum_rows, LANES), F32),
                        pltpu.VMEM((SEL_GROUP * PAGE, GQA * nq), F32), pltpu.VMEM((SEL_GROUP * PAGE, GQA * nq), F32)],
        compiler_params=_cparams(("parallel", "parallel", "parallel"), VMEM_LIMIT),
        name="nsa_attend",
    )(q, gt, kc, vct, ks, vst, kw, vwt)


SAMPLE_ROWS = 64
ROWS_PER_GROUP = SAMPLE_ROWS // KV_HEADS


def _nsa_sample_kernel(pt_ref, *refs, n_slots, t_base, wpos0, n_slc, t_q):
    page_refs = refs[:n_slots]
    (tail_ref, q_ref, gt_ref, kct_ref, vct_ref, kwt_ref, vwt_ref, o_ref,
     ind_ref, expand_ref, m_ref, l_ref, acc_ref, oc_ref, ow_ref) = refs[n_slots:]
    k = pl.program_id(1)
    n_steps = pl.num_programs(1)
    q = q_ref[0]
    row = lax.broadcasted_iota(jnp.int32, (SAMPLE_ROWS, 1), 0)
    t = t_base + (row & (t_q - 1))

    @pl.when(k == 0)
    def _():
        ncp = kct_ref.shape[2]
        s = _dot(q, kct_ref[0])
        n_idx = lax.broadcasted_iota(jnp.int32, (1, ncp), 1)
        cmask = n_idx * CMP_STRIDE + (CMP_BLOCK - 1) <= t
        s = jnp.where(cmask, s, NEG)
        p = jnp.where(cmask, jnp.exp2(s - jnp.max(s, axis=1, keepdims=True)), 0.0)
        l = jnp.sum(p, axis=1, keepdims=True)
        p = p * jnp.where(l > 0.0, 1.0 / l, 0.0)
        oc_ref[...] = _dot_nt(p.astype(BF), vct_ref[0])

        parts = []
        for g in range(KV_HEADS):
            base = g * ROWS_PER_GROUP
            acc = p[base:base + t_q]
            for r in range(1, GQA):
                acc = acc + p[base + r * t_q:base + (r + 1) * t_q]
            parts.append(acc)
        psum = jnp.concatenate(parts, axis=0)
        nsl = -(-n_slc // LANES) * LANES
        n_col = lax.broadcasted_iota(jnp.int32, (ncp, 1), 0)
        j_row = lax.broadcasted_iota(jnp.int32, (1, nsl), 1)
        ratio = SEL_BLOCK // CMP_STRIDE
        overlap = jnp.where((n_col >= ratio * j_row - 1) & (n_col <= ratio * j_row + ratio - 1), 1.0, 0.0).astype(BF)
        hi = psum.astype(BF)
        mid = (psum - hi.astype(F32)).astype(BF)
        lo = (psum - hi.astype(F32) - mid.astype(F32)).astype(BF)
        imp = _dot(hi, overlap) + _dot(mid, overlap) + _dot(lo, overlap)
        nsel_rows = KV_HEADS * t_q
        tq = t_base + (lax.broadcasted_iota(jnp.int32, (nsel_rows, 1), 0) & (t_q - 1))
        cur = tq >> int(math.log2(SEL_BLOCK))
        forced = (j_row == 0) | (j_row == cur) | (j_row == cur - 1)
        score = jnp.where(j_row * SEL_BLOCK <= tq, imp + FORCE_BONUS * forced.astype(F32), NEG)
        score_t = jnp.concatenate([score, jnp.full((LANES - nsel_rows, nsl), NEG, F32)], axis=0).T
        sub = SUBLANES
        nsp = -(-n_slc // sub) * sub
        jc = lax.broadcasted_iota(jnp.int32, (sub, 1), 0)
        chunks = [score_t[c * sub:(c + 1) * sub] for c in range(nsp // sub)]
        cnts = [jnp.zeros((sub, LANES), F32) for _ in chunks]
        for ii in range(n_slc):
            ri = score_t[ii:ii + 1, :]
            for c, sc_c in enumerate(chunks):
                if c * sub > ii:
                    one = jnp.where(ri >= sc_c, 1.0, 0.0)
                elif c * sub + sub - 1 <= ii:
                    one = jnp.where(ri > sc_c, 1.0, 0.0)
                else:
                    one = jnp.where(jc + c * sub > ii, jnp.where(ri >= sc_c, 1.0, 0.0), jnp.where(ri > sc_c, 1.0, 0.0))
                cnts[c] = cnts[c] + one
        chosen = [jnp.where(cn < float(SEL_TOPK), jnp.where(sc_c > NEG / 2, 1.0, 0.0), 0.0)
                  for cn, sc_c in zip(cnts, chunks)]
        chosen_t = jnp.concatenate(chosen + [jnp.zeros((nsl - nsp, LANES), F32)], axis=0)
        chosen_rows = chosen_t.T.astype(BF)
        col = lax.broadcasted_iota(jnp.int32, (1, LANES), 1)
        src = (row >> int(math.log2(ROWS_PER_GROUP))) * t_q + (row & (t_q - 1))
        expand = jnp.where(col == src, 1.0, 0.0).astype(BF)
        ind = _dot(expand, chosen_rows).astype(BF)
        blocks_per_step = n_slots * (PAGE // SEL_BLOCK)
        j_all = lax.broadcasted_iota(jnp.int32, (nsl, 1), 0)
        for st in range(ind_ref.shape[0]):
            pick = jnp.where((j_all == st * blocks_per_step + col) & (col < blocks_per_step), 1.0, 0.0).astype(BF)
            ind_ref[st] = _dot(ind, pick).astype(BF)
        key_lane = lax.broadcasted_iota(jnp.int32, (1, n_slots * PAGE), 1)
        local_block = lax.broadcasted_iota(jnp.int32, (LANES, 1), 0)
        expand_ref[...] = jnp.where(local_block == (key_lane >> int(math.log2(SEL_BLOCK))), 1.0, 0.0).astype(BF)

        nw = kwt_ref.shape[2]
        sw = _dot(q, kwt_ref[0])
        kpos = wpos0 + lax.broadcasted_iota(jnp.int32, (1, nw), 1)
        d = lax.bitcast_convert_type(t - kpos, jnp.uint32)
        sw = jnp.where(d < WINDOW, sw, NEG)
        pw = jnp.exp2(sw - jnp.max(sw, axis=1, keepdims=True))
        lw = jnp.sum(pw, axis=1, keepdims=True)
        ow_ref[...] = _dot_nt(pw.astype(BF), vwt_ref[0]) * (1.0 / lw)

        m_ref[...] = jnp.full(m_ref.shape, NEG, F32)
        l_ref[...] = jnp.zeros(l_ref.shape, F32)
        acc_ref[...] = jnp.zeros(acc_ref.shape, F32)

    vts = []
    scs = []
    for i, pr in enumerate(page_refs):
        x = pr[0]
        if i == n_slots - 1:
            x = jnp.where(k == n_steps - 1, tail_ref[0], x)
        scs.append(_dot(q, x[0].reshape(KV_WIDTH, PAGE).astype(BF)))
        vts.append(x[1].reshape(KV_WIDTH, PAGE).astype(BF))
    sc = jnp.concatenate(scs, axis=1)
    nk = n_slots * PAGE
    kidx = k * nk + lax.broadcasted_iota(jnp.int32, (1, nk), 1)
    picked = _dot(ind_ref[k], expand_ref[...])
    sc = jnp.where(picked > 0.5, jnp.where(kidx <= t, sc, NEG), NEG)
    m_old = m_ref[:, :1]
    m_new = jnp.maximum(m_old, jnp.max(sc, axis=1, keepdims=True))
    alpha = jnp.exp2(m_old - m_new)
    pe = jnp.exp2(sc - m_new)
    acc = acc_ref[...] * alpha
    for i in range(n_slots):
        acc = acc + _dot_nt(pe[:, i * PAGE:(i + 1) * PAGE].astype(BF), vts[i])
    l_new = l_ref[:, :1] * alpha + jnp.sum(pe, axis=1, keepdims=True)
    m_ref[...] = jnp.broadcast_to(m_new, m_ref.shape)
    l_ref[...] = jnp.broadcast_to(l_new, l_ref.shape)
    acc_ref[...] = acc

    @pl.when(k == n_steps - 1)
    def _():
        o_sel = acc * jnp.where(l_new > 0.0, 1.0 / l_new, 0.0)
        gts = gt_ref[0]
        o_ref[0] = gts[:, 0:1] * oc_ref[...] + gts[:, 1:2] * o_sel + gts[:, 2:3] * ow_ref[...]


def _nsa_sample(q_bd, gates, pool_t, table, tail_t, kct, vct, kwt, vwt, *, t_base, wpos0, n_slc, t_q):
    b, n_real = table.shape
    n_pages = n_real + 1
    n_slots = max(p for p in range(1, MAX_PAGE_SLOTS + 1) if n_pages % p == 0)
    n_steps = n_pages // n_slots
    assert n_slots * (PAGE // SEL_BLOCK) <= LANES
    page_shape = pool_t.shape[1:]
    zeros = (0,) * len(page_shape)

    def page_spec(slot):
        return pl.BlockSpec(
            (1,) + page_shape,
            lambda i, k, pt: (pt[i * n_real + jnp.minimum(k * n_slots + slot, n_real - 1)],) + zeros)

    def per_batch(x):
        return pl.BlockSpec((1,) + x.shape[1:], lambda i, k, pt: (i,) + (0,) * (x.ndim - 1))

    acc_shape = pltpu.VMEM((SAMPLE_ROWS, KV_WIDTH), F32)
    stat_shape = pltpu.VMEM((SAMPLE_ROWS, LANES), F32)
    return pl.pallas_call(
        functools.partial(_nsa_sample_kernel, n_slots=n_slots, t_base=t_base, wpos0=wpos0, n_slc=n_slc, t_q=t_q),
        grid_spec=pltpu.PrefetchScalarGridSpec(
            num_scalar_prefetch=1, grid=(b, n_steps),
            in_specs=[page_spec(slot) for slot in range(n_slots)]
            + [per_batch(x) for x in (tail_t, q_bd, gates, kct, vct, kwt, vwt)],
            out_specs=pl.BlockSpec((1, SAMPLE_ROWS, KV_WIDTH), lambda i, k, pt: (i, 0, 0)),
            scratch_shapes=[pltpu.VMEM((n_steps, SAMPLE_ROWS, LANES), BF), pltpu.VMEM((LANES, n_slots * PAGE), BF),
                            stat_shape, stat_shape, acc_shape, acc_shape, acc_shape]),
        out_shape=jax.ShapeDtypeStruct((b, SAMPLE_ROWS, KV_WIDTH), F32),
        compiler_params=_cparams(("parallel", "arbitrary"), VMEM_LIMIT),
        name="nsa_sample",
    )(table.reshape(-1), *([pool_t] * n_slots), tail_t, q_bd, gates, kct, vct, kwt, vwt)


def _rope_tables(pos):
    half = ROPE_DIM // 2
    inv_freq = ROPE_THETA ** (-jnp.arange(half, dtype=F32) / half)
    ang = pos.astype(F32)[:, None] * inv_freq[None, :]
    cos, sin = jnp.cos(ang), jnp.sin(ang)
    n = pos.shape[0]
    rest = HEAD_DIM - ROPE_DIM
    c = jnp.concatenate([cos, cos, jnp.ones((n, rest), F32)], axis=1)
    a = jnp.concatenate([-sin, jnp.zeros((n, half + rest), F32)], axis=1)
    b = jnp.concatenate([jnp.zeros((n, half), F32), sin, jnp.zeros((n, rest), F32)], axis=1)
    reps = LANES // HEAD_DIM
    return tuple(jnp.tile(x, (1, reps)) for x in (c, a, b))


def _compress_weights(cmp_pe, w_phi1, w_phi2):
    ratio = CMP_BLOCK // CMP_STRIDE
    w1 = w_phi1.reshape(ratio, CMP_STRIDE, 2, HEAD_DIM, CMP_HIDDEN)
    eye2 = jnp.eye(2, dtype=w_phi1.dtype)
    big = jnp.einsum('jscde,hg->cjshdge', w1, eye2)
    big = big.reshape(2, ratio, CMP_STRIDE * 2 * HEAD_DIM, 2 * CMP_HIDDEN).astype(BF)
    big = jnp.concatenate([big[:, jj] for jj in range(ratio)], axis=-1)
    pe = cmp_pe.reshape(ratio, CMP_STRIDE, 2, HEAD_DIM)
    pe = jnp.broadcast_to(pe.transpose(2, 0, 1, 3)[:, :, :, None, :], (2, ratio, CMP_STRIDE, 2, HEAD_DIM))
    pe = pe.reshape(2, ratio, 1, CMP_STRIDE * 2 * HEAD_DIM)
    pe = jnp.broadcast_to(pe, (2, ratio, 8, pe.shape[-1]))
    eye4 = jnp.eye(KV_HEADS, dtype=w_phi2.dtype).reshape(2, 2, KV_HEADS)
    w2 = jnp.einsum('ced,pgh->cpgehd', w_phi2, eye4)
    w2 = w2.reshape(2, 2, 2 * CMP_HIDDEN, KV_WIDTH).astype(BF)
    return big, pe, w2


def _b_in_weight(w):
    n_gate = w.shape[1] - NSA_WIDTH - MEM_WIDTH
    gates = jnp.pad(w[:, NSA_WIDTH:NSA_WIDTH + n_gate], ((0, 0), (0, GATE_PAD - n_gate)))
    return jnp.concatenate([w[:, :NSA_WIDTH], w[:, NSA_WIDTH + n_gate:], gates], axis=1).astype(BF)


def _pad_rows(x, n):
    return jnp.pad(x, ((0, 0), (0, n - x.shape[1]), (0, 0)))


def _tail_page_t(new_rows):
    b = new_rows.shape[0]
    return _pad_rows(new_rows, PAGE).reshape(b, PAGE, 2, KV_HEADS, HEAD_DIM).transpose(0, 2, 3, 4, 1)


def _rows_t(rows):
    b, n, _ = rows.shape
    kv = rows.reshape(b, n, 2, KV_WIDTH).transpose(0, 2, 3, 1).astype(BF)
    return kv[:, 0], kv[:, 1]


def _row_tiles(s):
    return {"proj": min(s, 512), "gate": min(s, 512), "mlp": min(s, 512)}


def _trunk_a(x, mem_kv, p, l, *, chunk, tiles, emit_v, per_row_sets):
    if chunk == PAGE:
        ws, bs = p["w_spatial"][l], jnp.repeat(p["b_spatial"][l].T, A_GROUP_DIM, axis=1)
    else:
        reps = PAGE // chunk
        ws = jnp.tile(p["w_spatial"][l][:, :chunk, :chunk], (1, reps, reps))
        bs = jnp.tile(jnp.repeat(p["b_spatial"][l].T[:chunk], A_GROUP_DIM, axis=1), (reps, 1))
    res = _a_in(x, p["w_in_a"][l], p["ln_v_g"][l][None], p["ln_v_b"][l][None], ws, bs,
                chunk=chunk, tm=tiles["gate"], emit_v=emit_v, mem_kv=None if per_row_sets else mem_kv)
    mixed, om = res[0], res[1]
    if per_row_sets:
        om = _mem_attn_rows(om, mem_kv, chunk)
    x = _mlp_block(x, mixed, om, p["w_out_a"][l], p, l, tm=tiles["mlp"], a_transposed=False)
    return x, (res[2] if emit_v else None)


def _mlp_block(x, a, om, wo, p, l, *, tm, a_transposed):
    return _out_mlp(x, a, om, wo, p["ln1_g"][l][None], p["ln1_b"][l][None], p["w_up"][l], p["w_down"][l],
                    p["ln2_g"][l][None], p["ln2_b"][l][None], tm=tm, a_transposed=a_transposed)


def kernel(x_prompt, x_sample, cache_cmp_kv, cache_slc_kv, cache_win_kv, cache_mem_kv, page_table, mem_prompt,
           w_in_a, ln_v_g, ln_v_b, w_spatial, b_spatial, w_out_a, w_in_b, w_out_b, w_kv_shared, cmp_pe,
           w_phi1, w_phi2, w_mem_kv, ln1_g, ln1_b, ln2_g, ln2_b, w_up, w_down):
    bp, s, d = x_prompt.shape
    bs, t, _ = x_sample.shape
    n_pages = page_table.shape[1]
    past = n_pages * PAGE
    n_buf = cache_win_kv.shape[1]
    assert bs * t == PAGE and PAGE % t == 0 and s % PAGE == 0 and n_buf % PAGE == 0

    p = {"w_in_a": w_in_a.astype(BF), "ln_v_g": ln_v_g, "ln_v_b": ln_v_b, "w_spatial": w_spatial,
         "b_spatial": b_spatial, "w_out_a": w_out_a.astype(BF), "w_out_b": w_out_b.astype(BF),
         "ln1_g": ln1_g, "ln1_b": ln1_b, "ln2_g": ln2_g, "ln2_b": ln2_b,
         "w_up": w_up.astype(BF), "w_down": w_down.astype(BF)}
    w_in_b2 = [_b_in_weight(w_in_b[i]) for i in range(w_in_b.shape[0])]
    w_kv = w_kv_shared.astype(BF)
    cw1, cpe, cw2 = _compress_weights(cmp_pe, w_phi1, w_phi2)

    tl = _row_tiles(s)
    mem_kv_prompt = _mem_kv_proj(mem_prompt, w_mem_kv)
    x = x_prompt
    for l in range(N_A):
        x, _ = _trunk_a(x, mem_kv_prompt[l], p, l, chunk=PAGE, tiles=tl, emit_v=False, per_row_sets=False)
    tabs_p = _rope_tables(jnp.arange(s, dtype=jnp.int32))
    cmp_t, slc_t, win_t, ks, vst, kw, vwt = _kv_shared(x, w_kv, tabs_p, tm=tl["proj"], attn_layouts=True)
    shape_t = (2, KV_HEADS, HEAD_DIM)
    cmp_blk = _compress(cmp_t.reshape(bp, *shape_t, s), None, None, cw1, cpe, cw2)
    n_cmp = s // CMP_STRIDE - 1
    ncp = -(-n_cmp // PAGE) * PAGE
    kc, vct = _rows_layout(_pad_rows(cmp_blk[:, 1:1 + n_cmp], ncp))
    for i in range(DEPTH - N_A):
        l = N_A + i
        q, om, gt = _b_in(x, w_in_b2[i], tabs_p, tm=tl["proj"], mem_kv=mem_kv_prompt[l])
        ot = _nsa(q, gt, kc, vct, ks, vst, kw, vwt, t_base=0, wpos0=0, n_slc=s // SEL_BLOCK)
        x = _mlp_block(x, ot, om, p["w_out_b"][i], p, l, tm=tl["mlp"], a_transposed=True)
    y_prompt = x
    shape5 = (2, KV_HEADS, HEAD_DIM)

    def rows_major(x_t):
        return x_t.reshape(x_t.shape[0], *shape_t, x_t.shape[2]).transpose(0, 4, 1, 2, 3)

    cmp_kv_prompt = rows_major(cmp_t)
    slc_kv_prompt = rows_major(slc_t)
    win_kv_prompt = rows_major(win_t[:, :, s - min(WINDOW, s):])
    mem_kv_prompt_out = mem_kv_prompt.reshape(DEPTH, bp, MEM_TOKENS, 2, MEM_HEADS, HEAD_DIM)

    rows = bs * t
    x = x_sample.reshape(1, rows, d)
    mem_s = cache_mem_kv.reshape(DEPTH, bs, MEM_TOKENS, 2 * MEM_WIDTH)
    v_rows = []
    for l in range(N_A):
        x, v = _trunk_a(x, mem_s[l], p, l, chunk=t, tiles=_row_tiles(rows), emit_v=True, per_row_sets=True)
        v_rows.append(v.reshape(bs, t, A_WIDTH))
    pos_s = past + (jnp.arange(rows, dtype=jnp.int32) % t)
    tabs_s = _rope_tables(pos_s)
    new_cmp, new_slc, new_win = (r[0].T.reshape(bs, t, ROW_WIDTH)
                                 for r in _kv_shared(x, w_kv, tabs_s, tm=rows, attn_layouts=False))
    cmp_pool_t = cache_cmp_kv.transpose(0, 2, 3, 4, 1)
    slc_pool_t = cache_slc_kv.transpose(0, 2, 3, 4, 1)
    cmp_blk = _compress(cmp_pool_t, page_table, _tail_page_t(new_cmp), cw1, cpe, cw2)
    n_cmp = -(-(past + t) // CMP_STRIDE) - 1
    ncp = -(-n_cmp // PAGE) * PAGE
    kct, vct = _rows_t(_pad_rows(cmp_blk[:, 1:1 + n_cmp], ncp))
    win_all = jnp.concatenate([cache_win_kv.reshape(bs, n_buf, ROW_WIDTH), new_win], axis=1)
    kwt, vwt = _rows_t(_pad_rows(win_all, n_buf + PAGE))
    slc_tail_t = _tail_page_t(new_slc)
    n_slc = -(-(past + t) // SEL_BLOCK)
    per_group = GQA * t
    eye = jnp.eye(KV_HEADS, dtype=BF)
    for i in range(DEPTH - N_A):
        l = N_A + i
        q, qm, gt = _b_in(x, w_in_b2[i], tabs_s, tm=rows)
        om = _mem_attn_rows(qm, mem_s[l], t)
        q = q.reshape(KV_HEADS, GQA, bs, t, HEAD_DIM).transpose(2, 0, 1, 3, 4).reshape(bs, KV_HEADS, per_group, HEAD_DIM)
        q = jnp.pad(q, ((0, 0), (0, 0), (0, ROWS_PER_GROUP - per_group), (0, 0)))
        q_bd = jnp.einsum('bgxd,gh->bgxhd', q, eye).reshape(bs, SAMPLE_ROWS, KV_WIDTH)
        gs = gt[0, :NSA_HEADS * 3].reshape(KV_HEADS, GQA, 3, bs, t).transpose(3, 0, 1, 4, 2)
        gs = gs.reshape(bs, KV_HEADS, per_group, 3)
        gs = jnp.pad(gs, ((0, 0), (0, 0), (0, ROWS_PER_GROUP - per_group), (0, LANES - 3)))
        o = _nsa_sample(q_bd, gs.reshape(bs, SAMPLE_ROWS, LANES), slc_pool_t, page_table, slc_tail_t,
                        kct, vct, kwt, vwt, t_base=past, wpos0=past - n_buf, n_slc=n_slc, t_q=t)
        o = o.reshape(bs, KV_HEADS, ROWS_PER_GROUP, KV_HEADS, HEAD_DIM)
        o = jnp.stack([o[:, g, :per_group, g] for g in range(KV_HEADS)], axis=1)
        o = o.reshape(bs, KV_HEADS, GQA, t, HEAD_DIM).transpose(0, 3, 1, 2, 4).reshape(1, rows, NSA_WIDTH)
        x = _mlp_block(x, o, om, p["w_out_b"][i], p, l, tm=rows, a_transposed=False)
    y_sample = x.reshape(bs, t, d)
    cmp_kv_sample = new_cmp.reshape(bs, t, *shape5)
    slc_kv_sample = new_slc.reshape(bs, t, *shape5)
    win_kv_sample = win_all[:, t:].reshape(bs, n_buf, *shape5)
    gmlp_v_sample = jnp.stack(v_rows)

    return (y_prompt, y_sample, cmp_kv_prompt, slc_kv_prompt, win_kv_prompt, mem_kv_prompt_out,
            cmp_kv_sample, slc_kv_sample, win_kv_sample, gmlp_v_sample)
```

```python
import functools
import math

import jax
import jax.numpy as jnp
from jax import lax
from jax.experimental import pallas as pl
from jax.experimental.pallas import tpu as pltpu

BF = jnp.bfloat16
F32 = jnp.float32

DEPTH = 4
N_A = 2
A_WIDTH = 768
A_GROUPS = 4
A_GROUP_DIM = 192
MEM_TOKENS = 256
MEM_HEADS = 4
MEM_WIDTH = 256
HEAD_DIM = 64
NSA_HEADS = 12
KV_HEADS = 4
GQA = 3
NSA_WIDTH = 768
KV_WIDTH = 256
ROW_WIDTH = 2 * KV_WIDTH
CMP_BLOCK = 32
CMP_STRIDE = 16
CMP_HIDDEN = 128
SEL_BLOCK = 64
SEL_TOPK = 16
WINDOW = 512
ROPE_THETA = 500000.0
ROPE_DIM = 16
D_FF_CHUNK = 1024
ALPHA = (2.0 * DEPTH) ** 0.25
LN_EPS = 1e-5
NEG = -1e30
FORCE_BONUS = 1e4
OUT_OF_WINDOW = -(1 << 30)
PAGE = 128
LANES = 128
SUBLANES = 8
MAX_PAGE_SLOTS = 16
GATE_PAD = 128
SEL_GROUP = 4
NSA_QUERY_TILE = 512
Q_SCALE = HEAD_DIM ** -0.5 * math.log2(math.e)
VMEM_LIMIT = 56 * 1024 * 1024


def _cparams(sem, vmem=None):
    return pltpu.CompilerParams(dimension_semantics=sem, vmem_limit_bytes=vmem)


def _const_spec(shape):
    return pl.BlockSpec(shape, lambda *_: (0,) * len(shape), pipeline_mode=pl.Buffered(1))


def _ln(x, g, b):
    mu = jnp.mean(x, axis=-1, keepdims=True)
    xc = x - mu
    var = jnp.mean(xc * xc, axis=-1, keepdims=True)
    return xc * lax.rsqrt(var + LN_EPS) * g + b


def _dot(a, b):
    return jnp.dot(a, b, preferred_element_type=F32)


def _dot_nt(a, b):
    return lax.dot_general(a, b, (((1,), (1,)), ((), ())), preferred_element_type=F32)


def _rope_cols(x, c, a, b):
    cols = []
    for j in range(x.shape[1] // LANES):
        xc = x[:, j * LANES:(j + 1) * LANES]
        up = pltpu.roll(xc, LANES - ROPE_DIM // 2, axis=1)
        dn = pltpu.roll(xc, ROPE_DIM // 2, axis=1)
        cols.append(xc * c + up * a + dn * b)
    return cols[0] if len(cols) == 1 else jnp.concatenate(cols, axis=1)


def _mem_kv_kernel(x_ref, w_ref, o_ref):
    o_ref[0] = _dot(x_ref[...].astype(BF), w_ref[0].astype(BF))


def _mem_kv_proj(mem_prompt, w_mem_kv):
    b, m, d = mem_prompt.shape
    x = mem_prompt.reshape(b * m, d)
    depth, _, n = w_mem_kv.shape
    tm = _row_tiles(b * m)["mlp"]
    out = pl.pallas_call(
        _mem_kv_kernel,
        grid=(depth, (b * m) // tm),
        in_specs=[pl.BlockSpec((tm, d), lambda l, i: (i, 0)),
                  pl.BlockSpec((1, d, n), lambda l, i: (l, 0, 0))],
        out_specs=pl.BlockSpec((1, tm, n), lambda l, i: (l, i, 0)),
        out_shape=jax.ShapeDtypeStruct((depth, b * m, n), F32),
        compiler_params=_cparams(("parallel", "parallel")),
        name="mem_kv_proj",
    )(x, w_mem_kv)
    return out.reshape(depth, b, m, n)


def _mem_heads(q, kv, own=None):
    k = kv[:, :MEM_WIDTH].astype(BF)
    v = kv[:, MEM_WIDTH:]
    lane = lax.broadcasted_iota(jnp.int32, (1, MEM_WIDTH), 1)
    out = jnp.zeros(q.shape, F32)
    for h in range(MEM_HEADS):
        hm = (lane >= h * HEAD_DIM) & (lane < (h + 1) * HEAD_DIM)
        s = _dot_nt(jnp.where(hm, q, 0.0).astype(BF), k) * (HEAD_DIM ** -0.5)
        if own is not None:
            s = jnp.where(own, s, NEG)
        m = jnp.max(s, axis=-1, keepdims=True)
        p = jnp.exp(s - m)
        p = p / jnp.sum(p, axis=-1, keepdims=True)
        out = out + _dot(p.astype(BF), jnp.where(hm, v, 0.0).astype(BF))
    return out


def _a_in_kernel(x_ref, w_ref, g_ref, b_ref, ws_ref, bs_ref, *rest, chunk_shift, fuse_mem, emit_v):
    mem_ref = rest[0] if fuse_mem else None
    mixed_ref, qm_ref = rest[int(fuse_mem):int(fuse_mem) + 2]
    x = x_ref[0].astype(BF)
    proj = _dot(x, w_ref[...])
    uv = jax.nn.gelu(proj[:, :2 * A_WIDTH])
    u = uv[:, :A_WIDTH]
    v = _ln(uv[:, A_WIDTH:], g_ref[...], b_ref[...])
    if emit_v:
        rest[-1][0] = v
    qm = proj[:, 2 * A_WIDTH:]
    qm_ref[0] = _mem_heads(qm, mem_ref[0]) if fuse_mem else qm
    tm = x.shape[0]
    r = ws_ref.shape[1]
    row = lax.broadcasted_iota(jnp.int32, (r, r), 0)
    col = lax.broadcasted_iota(jnp.int32, (r, r), 1)
    allowed = (col <= row) & ((row >> chunk_shift) == (col >> chunk_shift))
    wgs = [jnp.where(allowed, ws_ref[g], 0.0).astype(BF) for g in range(A_GROUPS)]
    span = 2 * A_GROUP_DIM
    first = lax.broadcasted_iota(jnp.int32, (1, span), 1) < A_GROUP_DIM
    vb = v.astype(BF)
    for ci in range(tm // r):
        rows = slice(ci * r, (ci + 1) * r)
        for h in range(A_GROUPS // 2):
            cols = slice(h * span, (h + 1) * span)
            vc = vb[rows, cols]
            mix = jnp.where(first, _dot(wgs[2 * h], vc), _dot(wgs[2 * h + 1], vc)) + bs_ref[:, cols]
            mixed_ref[0, rows, cols] = (u[rows, cols] * mix).astype(BF)


def _a_in(x, w_in, ln_g, ln_b, ws, bs, *, chunk, tm, emit_v, mem_kv=None):
    b, s, d = x.shape
    n = w_in.shape[1]
    r = ws.shape[1]
    in_specs = [pl.BlockSpec((1, tm, d), lambda i, j: (i, j, 0)),
                _const_spec((d, n)), _const_spec((1, A_WIDTH)), _const_spec((1, A_WIDTH)),
                _const_spec((A_GROUPS, r, r)), _const_spec((r, A_WIDTH))]
    args = [x, w_in, ln_g, ln_b, ws, bs]
    if mem_kv is not None:
        in_specs.append(pl.BlockSpec((1,) + mem_kv.shape[1:], lambda i, j: (i, 0, 0)))
        args.append(mem_kv)
    outs = [jax.ShapeDtypeStruct((b, s, A_WIDTH), BF), jax.ShapeDtypeStruct((b, s, MEM_WIDTH), F32)]
    ospecs = [pl.BlockSpec((1, tm, A_WIDTH), lambda i, j: (i, j, 0)),
              pl.BlockSpec((1, tm, MEM_WIDTH), lambda i, j: (i, j, 0))]
    if emit_v:
        outs.append(jax.ShapeDtypeStruct((b, s, A_WIDTH), F32))
        ospecs.append(pl.BlockSpec((1, tm, A_WIDTH), lambda i, j: (i, j, 0)))
    return pl.pallas_call(
        functools.partial(_a_in_kernel, chunk_shift=int(math.log2(chunk)), fuse_mem=mem_kv is not None,
                          emit_v=emit_v),
        grid=(b, s // tm),
        in_specs=in_specs,
        out_specs=ospecs,
        out_shape=outs,
        compiler_params=_cparams(("parallel", "parallel"), VMEM_LIMIT),
        name="a_in_gate",
    )(*args)


SAMPLE_MEM_SETS = 8


def _mem_attn_kernel(q_ref, kv_ref, o_ref, *, rows_per_set):
    q = q_ref[0]
    nb = kv_ref.shape[0]
    row_set = lax.broadcasted_iota(jnp.int32, (q.shape[0], 1), 0) >> int(math.log2(rows_per_set))
    col_set = lax.broadcasted_iota(jnp.int32, (1, nb * MEM_TOKENS), 1) >> int(math.log2(MEM_TOKENS))
    o_ref[0] = _mem_heads(q, kv_ref[...].reshape(nb * MEM_TOKENS, 2 * MEM_WIDTH), row_set == col_set)


def _mem_attn_rows(q, mem_kv, rows_per_set):
    _, s, w = q.shape
    tq = SAMPLE_MEM_SETS * rows_per_set
    return pl.pallas_call(
        functools.partial(_mem_attn_kernel, rows_per_set=rows_per_set),
        grid=(s // tq,),
        in_specs=[pl.BlockSpec((1, tq, w), lambda j: (0, j, 0)),
                  pl.BlockSpec((SAMPLE_MEM_SETS, MEM_TOKENS, 2 * w), lambda j: (j, 0, 0))],
        out_specs=pl.BlockSpec((1, tq, w), lambda j: (0, j, 0)),
        out_shape=jax.ShapeDtypeStruct((1, s, w), F32),
        compiler_params=_cparams(("parallel",)),
        name="mem_attn",
    )(q, mem_kv)


def _out_mlp_kernel(x_ref, a_ref, om_ref, wo_ref, g1_ref, b1_ref, wu_ref, wd_ref, g2_ref, b2_ref, o_ref,
                    *, a_transposed):
    x = x_ref[0]
    a = a_ref[0]
    a = a.T.astype(BF) if a_transposed else a.astype(BF)
    na = a.shape[1]
    mix = _dot(a, wo_ref[:na, :]) + _dot(om_ref[0].astype(BF), wo_ref[na:, :])
    y = _ln(ALPHA * x + mix, g1_ref[...], b1_ref[...])
    yb = y.astype(BF)
    acc = jnp.zeros(x.shape, F32)
    for c in range(wu_ref.shape[1] // D_FF_CHUNK):
        h = _dot(yb, wu_ref[:, c * D_FF_CHUNK:(c + 1) * D_FF_CHUNK])
        h = jnp.square(jnp.maximum(h, 0.0)).astype(BF)
        acc = acc + _dot(h, wd_ref[c * D_FF_CHUNK:(c + 1) * D_FF_CHUNK, :])
    o_ref[0] = _ln(ALPHA * y + acc, g2_ref[...], b2_ref[...])


def _out_mlp(x, a, om, wo, g1, b1, wu, wd, g2, b2, *, tm, a_transposed):
    b, s, d = x.shape
    na = wo.shape[0] - om.shape[2]
    dff = wu.shape[1]
    if a_transposed:
        a_spec = pl.BlockSpec((1, na, tm), lambda i, j: (i, 0, j))
    else:
        a_spec = pl.BlockSpec((1, tm, na), lambda i, j: (i, j, 0))
    return pl.pallas_call(
        functools.partial(_out_mlp_kernel, a_transposed=a_transposed),
        grid=(b, s // tm),
        in_specs=[pl.BlockSpec((1, tm, d), lambda i, j: (i, j, 0)),
                  a_spec,
                  pl.BlockSpec((1, tm, om.shape[2]), lambda i, j: (i, j, 0)),
                  _const_spec(wo.shape), _const_spec((1, d)), _const_spec((1, d)),
                  _const_spec((d, dff)), _const_spec((dff, d)), _const_spec((1, d)), _const_spec((1, d))],
        out_specs=pl.BlockSpec((1, tm, d), lambda i, j: (i, j, 0)),
        out_shape=jax.ShapeDtypeStruct((b, s, d), F32),
        compiler_params=_cparams(("parallel", "parallel"), VMEM_LIMIT),
        name="out_mlp",
    )(x, a, om, wo, g1, b1, wu, wd, g2, b2)


def _kv_shared_kernel(h_ref, w_ref, c_ref, a_ref, b_ref, cmp_ref, slc_ref, win_ref, *attn_refs):
    kv = _dot(h_ref[0].astype(BF), w_ref[...])
    c, a, b = c_ref[...], a_ref[...], b_ref[...]
    tm = kv.shape[0]
    for br, ref in enumerate((cmp_ref, slc_ref, win_ref)):
        k = _rope_cols(kv[:, br * ROW_WIDTH:br * ROW_WIDTH + KV_WIDTH], c, a, b)
        v = kv[:, br * ROW_WIDTH + KV_WIDTH:(br + 1) * ROW_WIDTH]
        vt = v.T
        ref[0, :KV_WIDTH, :] = k.T
        ref[0, KV_WIDTH:, :] = vt
        if attn_refs and br > 0:
            k_ref, vt_ref = attn_refs[2 * (br - 1):2 * br]
            if br == 1:
                lane = lax.broadcasted_iota(jnp.int32, (1, LANES), 1)
                row = pl.program_id(1) * tm + lax.broadcasted_iota(jnp.int32, (tm, 1), 0)
                block_hot = jnp.where(lane - HEAD_DIM == (row >> int(math.log2(SEL_BLOCK))), 1.0, 0.0)
            for g in range(KV_HEADS):
                if br == 1:
                    pair = k[:, (g // 2) * LANES:(g // 2 + 1) * LANES]
                    if g % 2:
                        pair = pltpu.roll(pair, HEAD_DIM, axis=1)
                    k_ref[0, g] = jnp.where(lane < HEAD_DIM, pair, block_hot).astype(BF)
                else:
                    k_ref[0, g] = k[:, g * HEAD_DIM:(g + 1) * HEAD_DIM].astype(BF)
                for u in range(tm // PAGE):
                    vt_ref[0, g, u] = vt[g * HEAD_DIM:(g + 1) * HEAD_DIM, u * PAGE:(u + 1) * PAGE].astype(BF)


def _kv_shared(h, w_kv, tabs, *, tm, attn_layouts):
    b, s, d = h.shape
    row_spec = pl.BlockSpec((1, ROW_WIDTH, tm), lambda i, j: (i, 0, j))
    tab_spec = pl.BlockSpec((tm, LANES), lambda i, j: (j, 0))
    out_specs = [row_spec, row_spec, row_spec]
    out_shape = [jax.ShapeDtypeStruct((b, ROW_WIDTH, s), F32)] * 3
    if attn_layouts:
        assert s // SEL_BLOCK <= LANES - HEAD_DIM
        vt_spec = pl.BlockSpec((1, KV_HEADS, tm // PAGE, HEAD_DIM, PAGE), lambda i, j: (i, 0, j, 0, 0))
        vt_shape = jax.ShapeDtypeStruct((b, KV_HEADS, s // PAGE, HEAD_DIM, PAGE), BF)
        for width in (LANES, HEAD_DIM):
            out_specs += [pl.BlockSpec((1, KV_HEADS, tm, width), lambda i, j: (i, 0, j, 0)), vt_spec]
            out_shape += [jax.ShapeDtypeStruct((b, KV_HEADS, s, width), BF), vt_shape]
    return pl.pallas_call(
        _kv_shared_kernel,
        grid=(b, s // tm),
        in_specs=[pl.BlockSpec((1, tm, d), lambda i, j: (i, j, 0)), _const_spec(w_kv.shape),
                  tab_spec, tab_spec, tab_spec],
        out_specs=out_specs,
        out_shape=out_shape,
        compiler_params=_cparams(("parallel", "parallel")),
        name="kv_shared",
    )(h, w_kv, *tabs)


def _b_in_kernel(x_ref, w_ref, c_ref, a_ref, b_ref, *rest, fuse_mem):
    mem_ref = rest[0] if fuse_mem else None
    q_ref, qm_ref, gt_ref = rest[int(fuse_mem):]
    proj = _dot(x_ref[0].astype(BF), w_ref[...])
    q = _rope_cols(proj[:, :NSA_WIDTH], c_ref[...], a_ref[...], b_ref[...]) * Q_SCALE
    for h in range(NSA_HEADS):
        q_ref[0, h] = q[:, h * HEAD_DIM:(h + 1) * HEAD_DIM].astype(BF)
    qm = proj[:, NSA_WIDTH:NSA_WIDTH + MEM_WIDTH]
    qm_ref[0] = _mem_heads(qm, mem_ref[0]) if fuse_mem else qm
    gt_ref[0] = jax.nn.sigmoid(proj[:, NSA_WIDTH + MEM_WIDTH:]).T


def _b_in(x, w_in, tabs, *, tm, mem_kv=None):
    b, s, d = x.shape
    tab_spec = pl.BlockSpec((tm, LANES), lambda i, j: (j, 0))
    in_specs = [pl.BlockSpec((1, tm, d), lambda i, j: (i, j, 0)), _const_spec(w_in.shape),
                tab_spec, tab_spec, tab_spec]
    args = [x, w_in, *tabs]
    if mem_kv is not None:
        in_specs.append(pl.BlockSpec((1,) + mem_kv.shape[1:], lambda i, j: (i, 0, 0)))
        args.append(mem_kv)
    return pl.pallas_call(
        functools.partial(_b_in_kernel, fuse_mem=mem_kv is not None),
        grid=(b, s // tm),
        in_specs=in_specs,
        out_specs=[pl.BlockSpec((1, NSA_HEADS, tm, HEAD_DIM), lambda i, j: (i, 0, j, 0)),
                   pl.BlockSpec((1, tm, MEM_WIDTH), lambda i, j: (i, j, 0)),
                   pl.BlockSpec((1, GATE_PAD, tm), lambda i, j: (i, 0, j))],
        out_shape=[jax.ShapeDtypeStruct((b, NSA_HEADS, s, HEAD_DIM), BF),
                   jax.ShapeDtypeStruct((b, s, MEM_WIDTH), F32),
                   jax.ShapeDtypeStruct((b, GATE_PAD, s), F32)],
        compiler_params=_cparams(("parallel", "parallel")),
        name="b_in",
    )(*args)


def _kv_layout_kernel(x_ref, k_ref, vt_ref):
    x = x_ref[0]
    vt = x[:, KV_WIDTH:].T
    for g in range(KV_HEADS):
        k_ref[0, g] = x[:, g * HEAD_DIM:(g + 1) * HEAD_DIM].astype(BF)
        vt_ref[0, g, 0] = vt[g * HEAD_DIM:(g + 1) * HEAD_DIM].astype(BF)


def _rows_layout(rows):
    b, l, _ = rows.shape
    n = l // PAGE
    return pl.pallas_call(
        _kv_layout_kernel,
        grid=(b, n),
        in_specs=[pl.BlockSpec((1, PAGE, ROW_WIDTH), lambda i, p: (i, p, 0))],
        out_specs=[pl.BlockSpec((1, KV_HEADS, PAGE, HEAD_DIM), lambda i, p: (i, 0, p, 0)),
                   pl.BlockSpec((1, KV_HEADS, 1, HEAD_DIM, PAGE), lambda i, p: (i, 0, p, 0, 0))],
        out_shape=[jax.ShapeDtypeStruct((b, KV_HEADS, l, HEAD_DIM), BF),
                   jax.ShapeDtypeStruct((b, KV_HEADS, n, HEAD_DIM, PAGE), BF)],
        compiler_params=_cparams(("parallel", "parallel")),
        name="kv_layout",
    )(rows)


def _compress_kernel(pt_ref, *refs, n_slots, tail_group):
    page_refs = refs[:n_slots]
    rest = refs[n_slots:]
    if tail_group is not None:
        tail_ref, rest = rest[0], rest[1:]
    w1_ref, pe_ref, w2_ref, o_ref, xs_ref, carry_ref = rest
    k = pl.program_id(1)
    per_page = PAGE // CMP_STRIDE
    m = n_slots * per_page
    hid2 = 2 * CMP_HIDDEN

    @pl.when(k == 0)
    def _():
        carry_ref[...] = jnp.zeros(carry_ref.shape, F32)

    for i, pr in enumerate(page_refs):
        x = pr[0]
        if tail_group is not None and i == n_slots - 1:
            x = jnp.where(k == tail_group, tail_ref[0], x)
        rows = slice(i * PAGE, (i + 1) * PAGE)
        for c in range(2):
            xc = x[c].reshape(KV_WIDTH, PAGE).T
            for hp in range(2):
                xs_ref[2 * c + hp, rows, :] = xc[:, hp * LANES:(hp + 1) * LANES]

    def strides(vcol):
        return jnp.concatenate(
            [xs_ref[vcol, pl.ds(s, m, stride=CMP_STRIDE), :] for s in range(CMP_STRIDE)], axis=1)

    first_row = lax.broadcasted_iota(jnp.int32, (m, 1), 0) == 0
    out_cols = []
    for c in range(2):
        w1 = w1_ref[c]
        lhs = jnp.concatenate([strides(2 * c + hp) for hp in range(2)] + [pe_ref[c, 0], pe_ref[c, 1]],
                              axis=0).astype(BF)
        a = _dot(lhs, w1)
        hpe = a[2 * m:2 * m + 1, :hid2] + a[2 * m + 8:2 * m + 9, hid2:]
        acc = jnp.zeros((m, KV_WIDTH), F32)
        for hp in range(2):
            a0 = a[hp * m:(hp + 1) * m, :hid2]
            a1 = a[hp * m:(hp + 1) * m, hid2:]
            prev = jnp.where(first_row, carry_ref[c, hp, 0:1, :], pltpu.roll(a0, 1, axis=0))
            carry_ref[c, hp, 0:1, :] = a0[m - 1:m]
            acc = acc + _dot(jax.nn.gelu(prev + a1 + hpe).astype(BF), w2_ref[c, hp])
        out_cols.append(acc)
    o_ref[0] = jnp.concatenate(out_cols, axis=1)


def _compress(pool, table, tail, w1, pe, w2):
    page_shape = pool.shape[1:-1] + (PAGE,)
    zeros = (0,) * (len(page_shape) - 1)
    paged = table is not None
    if paged:
        b, n_real = table.shape
    else:
        b, n_real = pool.shape[0], pool.shape[-1] // PAGE
        table = jnp.zeros((1, 1), jnp.int32)
    n_pages = n_real + (0 if tail is None else 1)
    n_slots = max(p for p in range(1, MAX_PAGE_SLOTS + 1) if n_pages % p == 0)
    n_groups = n_pages // n_slots
    m = n_slots * (PAGE // CMP_STRIDE)

    def page_spec(slot):
        if paged:
            return pl.BlockSpec(
                (1,) + page_shape,
                lambda i, k, pt: (pt[i * n_real + jnp.minimum(k * n_slots + slot, n_real - 1)],) + zeros + (0,))
        return pl.BlockSpec((1,) + page_shape, lambda i, k, pt: (i,) + zeros + (k * n_slots + slot,))

    in_specs = [page_spec(slot) for slot in range(n_slots)]
    args = [pool] * n_slots
    if tail is not None:
        in_specs.append(pl.BlockSpec((1,) + page_shape, lambda i, k, pt: (i,) + zeros + (0,)))
        args.append(tail)
    in_specs += [pl.BlockSpec(x.shape, lambda i, k, pt, nd=x.ndim: (0,) * nd) for x in (w1, pe, w2)]
    return pl.pallas_call(
        functools.partial(_compress_kernel, n_slots=n_slots,
                          tail_group=None if tail is None else n_groups - 1),
        grid_spec=pltpu.PrefetchScalarGridSpec(
            num_scalar_prefetch=1, grid=(b, n_groups), in_specs=in_specs,
            out_specs=pl.BlockSpec((1, m, ROW_WIDTH), lambda i, k, pt: (i, k, 0)),
            scratch_shapes=[pltpu.VMEM((4, n_slots * PAGE, LANES), F32),
                            pltpu.VMEM((2, 2, 8, 2 * CMP_HIDDEN), F32)]),
        out_shape=jax.ShapeDtypeStruct((b, n_groups * m, ROW_WIDTH), F32),
        compiler_params=_cparams(("parallel", "arbitrary"), VMEM_LIMIT),
        name="compress",
    )(table.reshape(-1), *args, w1, pe, w2)


def _nsa_kernel(q_ref, gt_ref, kc_ref, vct_ref, ks_ref, vst_ref, kw_ref, vwt_ref, o_ref, psum_ref,
                sbuf0_ref, sbuf1_ref, *, t_base, wpos0, n_slc, nsp):
    g = pl.program_id(1)
    i = pl.program_id(2)
    nq = q_ref.shape[2]
    nl = GQA * nq
    ncp = kc_ref.shape[2]
    gk = SEL_GROUP * PAGE
    nkt_s = vst_ref.shape[2]
    q = q_ref[0].reshape(nl, HEAD_DIM)
    lane = lax.broadcasted_iota(jnp.int32, (1, nl), 1)
    q0 = t_base + i * nq
    t = q0 + (lane & (nq - 1))

    s = _dot_nt(kc_ref[0, 0], q)
    n_idx = lax.broadcasted_iota(jnp.int32, (ncp, 1), 0)
    cmask = n_idx * CMP_STRIDE + (CMP_BLOCK - 1) <= t
    s = jnp.where(cmask, s, NEG)
    p = jnp.where(cmask, jnp.exp2(s - jnp.max(s, axis=0, keepdims=True)), 0.0)
    l = jnp.sum(p, axis=0, keepdims=True)
    p = p * jnp.where(l > 0.0, 1.0 / l, 0.0)
    o_cmp = jnp.zeros((HEAD_DIM, nl), F32)
    for kt in range(ncp // PAGE):
        o_cmp = o_cmp + _dot(vct_ref[0, 0, kt], p[kt * PAGE:(kt + 1) * PAGE].astype(BF))

    rel = (q0 - wpos0) // PAGE
    n_wt = (WINDOW + nq) // PAGE
    tile_row = lax.broadcasted_iota(jnp.int32, (PAGE, 1), 0)
    parts = []
    for u in range(n_wt):
        kt = rel - WINDOW // PAGE + u
        ktc = jnp.maximum(kt, 0)
        kb = kw_ref[0, 0, pl.ds(pl.multiple_of(ktc * PAGE, PAGE), PAGE), :]
        krel = kt * PAGE + tile_row
        kpos = jnp.where(krel >= 0, wpos0 + krel, OUT_OF_WINDOW)
        d = lax.bitcast_convert_type(t - kpos, jnp.uint32)
        parts.append(jnp.where(d < WINDOW, _dot_nt(kb, q), NEG))
    sw = jnp.concatenate(parts, axis=0)
    pw = jnp.exp2(sw - jnp.max(sw, axis=0, keepdims=True))
    l_w = jnp.sum(pw, axis=0, keepdims=True)
    acc_w = jnp.zeros((HEAD_DIM, nl), F32)
    for u in range(n_wt):
        ktc = jnp.maximum(rel - WINDOW // PAGE + u, 0)
        acc_w = acc_w + _dot(vwt_ref[0, 0, ktc], pw[u * PAGE:(u + 1) * PAGE].astype(BF))
    o_win = acc_w * (1.0 / l_w)

    psum = p[:, :nq]
    for r in range(1, GQA):
        psum = psum + p[:, r * nq:(r + 1) * nq]
    front = SUBLANES
    ratio = SEL_BLOCK // CMP_STRIDE
    slabs = []
    for sl in range(nq // LANES):
        psum_ref[sl, :front, :] = jnp.zeros((front, LANES), F32)
        psum_ref[sl, front:front + ncp, :] = psum[:, sl * LANES:(sl + 1) * LANES]
        psum_ref[sl, front + ncp:, :] = jnp.zeros((psum_ref.shape[1] - front - ncp, LANES), F32)
        part = psum_ref[sl, pl.ds(front - 1, nsp, stride=ratio), :]
        for d in range(1, ratio + 1):
            part = part + psum_ref[sl, pl.ds(front - 1 + d, nsp, stride=ratio), :]
        slabs.append(part)
    imp = slabs[0] if len(slabs) == 1 else jnp.concatenate(slabs, axis=1)
    tq = t[:, :nq]
    j = lax.broadcasted_iota(jnp.int32, (nsp, 1), 0)
    cur = tq >> int(math.log2(SEL_BLOCK))
    forced = (j == 0) | (j == cur) | (j == cur - 1)
    score = jnp.where(j * SEL_BLOCK <= tq, imp + FORCE_BONUS * forced.astype(F32), NEG)
    sub = SUBLANES
    chunks = [score[c * sub:(c + 1) * sub] for c in range(nsp // sub)]
    jc = j[:sub]
    cnts = [jnp.zeros((sub, nq), F32) for _ in chunks]
    for ii in range(n_slc):
        ri = score[ii:ii + 1, :]
        for c, sc_c in enumerate(chunks):
            if c * sub > ii:
                one = jnp.where(ri >= sc_c, 1.0, 0.0)
            elif c * sub + sub - 1 <= ii:
                one = jnp.where(ri > sc_c, 1.0, 0.0)
            else:
                one = jnp.where(jc + c * sub > ii, jnp.where(ri >= sc_c, 1.0, 0.0), jnp.where(ri > sc_c, 1.0, 0.0))
            cnts[c] = cnts[c] + one
    cnt = jnp.concatenate(cnts, axis=0)
    bias = jnp.where(cnt < float(SEL_TOPK), jnp.where(score > NEG / 2, 0.0, NEG), NEG)
    bias_t = jnp.concatenate([bias, jnp.zeros((LANES - nsp, nq), F32)], axis=0).T[:, :LANES - HEAD_DIM]
    q_ext = jnp.concatenate([q, jnp.concatenate([bias_t.astype(BF)] * GQA, axis=0)], axis=1)

    key_row = lax.broadcasted_iota(jnp.int32, (gk, 1), 0)

    def fill(buf, gi):
        kb = ks_ref[0, 0, pl.ds(pl.multiple_of(gi * gk, gk), gk), :]
        buf[...] = _dot_nt(kb, q_ext)

    def consume(buf, gi, carry, causal):
        m_old, l_old, acc = carry
        sc = buf[...]
        if causal:
            sc = jnp.where(gi * gk + key_row <= t, sc, NEG)
        m_new = jnp.maximum(m_old, jnp.max(sc, axis=0, keepdims=True))
        alpha = jnp.exp2(m_old - m_new)
        pe = jnp.exp2(sc - m_new)
        acc = acc * alpha
        for u in range(SEL_GROUP):
            acc = acc + _dot(vst_ref[0, 0, gi * SEL_GROUP + u], pe[u * PAGE:(u + 1) * PAGE].astype(BF))
        return m_new, l_old * alpha + jnp.sum(pe, axis=0, keepdims=True), acc

    init = (jnp.full((1, nl), NEG, F32), jnp.zeros((1, nl), F32), jnp.zeros((HEAD_DIM, nl), F32))

    n_groups = jnp.minimum(nkt_s - 1, (q0 + nq - 1) // PAGE) // SEL_GROUP + 1
    fill(sbuf0_ref, 0)

    def pair_body(kk, carry):
        gi = 2 * kk
        fill(sbuf1_ref, gi + 1)
        carry = consume(sbuf0_ref, gi, carry, False)
        fill(sbuf0_ref, gi + 2)
        return consume(sbuf1_ref, gi + 1, carry, False)

    n_pairs = (n_groups - 1) // 2
    carry = lax.fori_loop(0, n_pairs, pair_body, init)
    gi = 2 * n_pairs

    def odd_tail(carry):
        fill(sbuf1_ref, gi + 1)
        carry = consume(sbuf0_ref, gi, carry, False)
        return consume(sbuf1_ref, gi + 1, carry, True)

    def even_tail(carry):
        return consume(sbuf0_ref, gi, carry, True)

    _, l_s, acc_s = lax.cond(n_groups - 1 - gi == 1, odd_tail, even_tail, carry)

    o_sel = acc_s * jnp.where(l_s > 0.0, 1.0 / l_s, 0.0)
    for r in range(GQA):
        sl = slice(r * nq, (r + 1) * nq)
        grow = (GQA * g + r) * 3
        o = (gt_ref[0, pl.ds(grow, 1), :] * o_cmp[:, sl]
             + gt_ref[0, pl.ds(grow + 1, 1), :] * o_sel[:, sl]
             + gt_ref[0, pl.ds(grow + 2, 1), :] * o_win[:, sl])
        o_ref[0, r * HEAD_DIM:(r + 1) * HEAD_DIM, :] = o


def _nsa(q, gt, kc, vct, ks, vst, kw, vwt, *, t_base, wpos0, n_slc):
    b, _, sq, _ = q.shape
    nq = NSA_QUERY_TILE
    assert sq % nq == 0 and t_base % PAGE == 0 and (t_base - wpos0) % PAGE == 0 and t_base >= wpos0
    assert vst.shape[2] % SEL_GROUP == 0
    nsp = max(-(-n_slc // 8) * 8, vst.shape[2] * (PAGE // SEL_BLOCK))
    assert nsp <= LANES - HEAD_DIM and ks.shape[3] == LANES
    ncp = kc.shape[2]
    psum_rows = 8 + max(ncp + 8, (SEL_BLOCK // CMP_STRIDE) * nsp + 8)
    psum_rows = -(-psum_rows // 8) * 8

    def kv_specs(k, vt):
        return [pl.BlockSpec((1, 1) + k.shape[2:], lambda bi, gi, ii: (bi, gi, 0, 0)),
                pl.BlockSpec((1, 1) + vt.shape[2:], lambda bi, gi, ii: (bi, gi, 0, 0, 0))]

    return pl.pallas_call(
        functools.partial(_nsa_kernel, t_base=t_base, wpos0=wpos0, n_slc=n_slc, nsp=nsp),
        grid=(b, KV_HEADS, sq // nq),
        in_specs=[pl.BlockSpec((1, GQA, nq, HEAD_DIM), lambda bi, gi, ii: (bi, gi, ii, 0)),
                  pl.BlockSpec((1, GATE_PAD, nq), lambda bi, gi, ii: (bi, 0, ii))]
        + kv_specs(kc, vct) + kv_specs(ks, vst) + kv_specs(kw, vwt),
        out_specs=pl.BlockSpec((1, GQA * HEAD_DIM, nq), lambda bi, gi, ii: (bi, gi, ii)),
        out_shape=jax.ShapeDtypeStruct((b, NSA_WIDTH, sq), F32),
        scratch_shapes=[pltpu.VMEM((nq // LANES, psum_rows, LANES), F32),
                        pltpu.VMEM((SEL_GROUP * PAGE, GQA * nq), F32), pltpu.VMEM((SEL_GROUP * PAGE, GQA * nq), F32)],
        compiler_params=_cparams(("parallel", "parallel", "parallel"), VMEM_LIMIT),
        name="nsa_attend",
    )(q, gt, kc, vct, ks, vst, kw, vwt)


SAMPLE_ROWS = 64
ROWS_PER_GROUP = SAMPLE_ROWS // KV_HEADS


def _nsa_sample_kernel(pt_ref, *refs, n_slots, t_base, wpos0, n_slc, t_q):
    page_refs = refs[:n_slots]
    (tail_ref, q_ref, gt_ref, kct_ref, vct_ref, kwt_ref, vwt_ref, o_ref,
     ind_ref, expand_ref, m_ref, l_ref, acc_ref, oc_ref, ow_ref) = refs[n_slots:]
    k = pl.program_id(1)
    n_steps = pl.num_programs(1)
    q = q_ref[0]
    row = lax.broadcasted_iota(jnp.int32, (SAMPLE_ROWS, 1), 0)
    t = t_base + (row & (t_q - 1))

    @pl.when(k == 0)
    def _():
        ncp = kct_ref.shape[2]
        s = _dot(q, kct_ref[0])
        n_idx = lax.broadcasted_iota(jnp.int32, (1, ncp), 1)
        cmask = n_idx * CMP_STRIDE + (CMP_BLOCK - 1) <= t
        s = jnp.where(cmask, s, NEG)
        p = jnp.where(cmask, jnp.exp2(s - jnp.max(s, axis=1, keepdims=True)), 0.0)
        l = jnp.sum(p, axis=1, keepdims=True)
        p = p * jnp.where(l > 0.0, 1.0 / l, 0.0)
        oc_ref[...] = _dot_nt(p.astype(BF), vct_ref[0])

        parts = []
        for g in range(KV_HEADS):
            base = g * ROWS_PER_GROUP
            acc = p[base:base + t_q]
            for r in range(1, GQA):
                acc = acc + p[base + r * t_q:base + (r + 1) * t_q]
            parts.append(acc)
        psum = jnp.concatenate(parts, axis=0)
        nsl = -(-n_slc // LANES) * LANES
        n_col = lax.broadcasted_iota(jnp.int32, (ncp, 1), 0)
        j_row = lax.broadcasted_iota(jnp.int32, (1, nsl), 1)
        ratio = SEL_BLOCK // CMP_STRIDE
        overlap = jnp.where((n_col >= ratio * j_row - 1) & (n_col <= ratio * j_row + ratio - 1), 1.0, 0.0).astype(BF)
        hi = psum.astype(BF)
        mid = (psum - hi.astype(F32)).astype(BF)
        lo = (psum - hi.astype(F32) - mid.astype(F32)).astype(BF)
        imp = _dot(hi, overlap) + _dot(mid, overlap) + _dot(lo, overlap)
        nsel_rows = KV_HEADS * t_q
        tq = t_base + (lax.broadcasted_iota(jnp.int32, (nsel_rows, 1), 0) & (t_q - 1))
        cur = tq >> int(math.log2(SEL_BLOCK))
        forced = (j_row == 0) | (j_row == cur) | (j_row == cur - 1)
        score = jnp.where(j_row * SEL_BLOCK <= tq, imp + FORCE_BONUS * forced.astype(F32), NEG)
        score_t = jnp.concatenate([score, jnp.full((LANES - nsel_rows, nsl), NEG, F32)], axis=0).T
        sub = SUBLANES
        nsp = -(-n_slc // sub) * sub
        jc = lax.broadcasted_iota(jnp.int32, (sub, 1), 0)
        chunks = [score_t[c * sub:(c + 1) * sub] for c in range(nsp // sub)]
        cnts = [jnp.zeros((sub, LANES), F32) for _ in chunks]
        for ii in range(n_slc):
            ri = score_t[ii:ii + 1, :]
            for c, sc_c in enumerate(chunks):
                if c * sub > ii:
                    one = jnp.where(ri >= sc_c, 1.0, 0.0)
                elif c * sub + sub - 1 <= ii:
                    one = jnp.where(ri > sc_c, 1.0, 0.0)
                else:
                    one = jnp.where(jc + c * sub > ii, jnp.where(ri >= sc_c, 1.0, 0.0), jnp.where(ri > sc_c, 1.0, 0.0))
                cnts[c] = cnts[c] + one
        chosen = [jnp.where(cn < float(SEL_TOPK), jnp.where(sc_c > NEG / 2, 1.0, 0.0), 0.0)
                  for cn, sc_c in zip(cnts, chunks)]
        chosen_t = jnp.concatenate(chosen + [jnp.zeros((nsl - nsp, LANES), F32)], axis=0)
        chosen_rows = chosen_t.T.astype(BF)
        col = lax.broadcasted_iota(jnp.int32, (1, LANES), 1)
        src = (row >> int(math.log2(ROWS_PER_GROUP))) * t_q + (row & (t_q - 1))
        expand = jnp.where(col == src, 1.0, 0.0).astype(BF)
        ind = _dot(expand, chosen_rows).astype(BF)
        blocks_per_step = n_slots * (PAGE // SEL_BLOCK)
        j_all = lax.broadcasted_iota(jnp.int32, (nsl, 1), 0)
        for st in range(ind_ref.shape[0]):
            pick = jnp.where((j_all == st * blocks_per_step + col) & (col < blocks_per_step), 1.0, 0.0).astype(BF)
            ind_ref[st] = _dot(ind, pick).astype(BF)
        key_lane = lax.broadcasted_iota(jnp.int32, (1, n_slots * PAGE), 1)
        local_block = lax.broadcasted_iota(jnp.int32, (LANES, 1), 0)
        expand_ref[...] = jnp.where(local_block == (key_lane >> int(math.log2(SEL_BLOCK))), 1.0, 0.0).astype(BF)

        nw = kwt_ref.shape[2]
        sw = _dot(q, kwt_ref[0])
        kpos = wpos0 + lax.broadcasted_iota(jnp.int32, (1, nw), 1)
        d = lax.bitcast_convert_type(t - kpos, jnp.uint32)
        sw = jnp.where(d < WINDOW, sw, NEG)
        pw = jnp.exp2(sw - jnp.max(sw, axis=1, keepdims=True))
        lw = jnp.sum(pw, axis=1, keepdims=True)
        ow_ref[...] = _dot_nt(pw.astype(BF), vwt_ref[0]) * (1.0 / lw)

        m_ref[...] = jnp.full(m_ref.shape, NEG, F32)
        l_ref[...] = jnp.zeros(l_ref.shape, F32)
        acc_ref[...] = jnp.zeros(acc_ref.shape, F32)

    vts = []
    scs = []
    for i, pr in enumerate(page_refs):
        x = pr[0]
        if i == n_slots - 1:
            x = jnp.where(k == n_steps - 1, tail_ref[0], x)
        scs.append(_dot(q, x[0].reshape(KV_WIDTH, PAGE).astype(BF)))
        vts.append(x[1].reshape(KV_WIDTH, PAGE).astype(BF))
    sc = jnp.concatenate(scs, axis=1)
    nk = n_slots * PAGE
    kidx = k * nk + lax.broadcasted_iota(jnp.int32, (1, nk), 1)
    picked = _dot(ind_ref[k], expand_ref[...])
    sc = jnp.where(picked > 0.5, jnp.where(kidx <= t, sc, NEG), NEG)
    m_old = m_ref[:, :1]
    m_new = jnp.maximum(m_old, jnp.max(sc, axis=1, keepdims=True))
    alpha = jnp.exp2(m_old - m_new)
    pe = jnp.exp2(sc - m_new)
    acc = acc_ref[...] * alpha
    for i in range(n_slots):
        acc = acc + _dot_nt(pe[:, i * PAGE:(i + 1) * PAGE].astype(BF), vts[i])
    l_new = l_ref[:, :1] * alpha + jnp.sum(pe, axis=1, keepdims=True)
    m_ref[...] = jnp.broadcast_to(m_new, m_ref.shape)
    l_ref[...] = jnp.broadcast_to(l_new, l_ref.shape)
    acc_ref[...] = acc

    @pl.when(k == n_steps - 1)
    def _():
        o_sel = acc * jnp.where(l_new > 0.0, 1.0 / l_new, 0.0)
        gts = gt_ref[0]
        o_ref[0] = gts[:, 0:1] * oc_ref[...] + gts[:, 1:2] * o_sel + gts[:, 2:3] * ow_ref[...]


def _nsa_sample(q_bd, gates, pool_t, table, tail_t, kct, vct, kwt, vwt, *, t_base, wpos0, n_slc, t_q):
    b, n_real = table.shape
    n_pages = n_real + 1
    n_slots = max(p for p in range(1, MAX_PAGE_SLOTS + 1) if n_pages % p == 0)
    n_steps = n_pages // n_slots
    assert n_slots * (PAGE // SEL_BLOCK) <= LANES
    page_shape = pool_t.shape[1:]
    zeros = (0,) * len(page_shape)

    def page_spec(slot):
        return pl.BlockSpec(
            (1,) + page_shape,
            lambda i, k, pt: (pt[i * n_real + jnp.minimum(k * n_slots + slot, n_real - 1)],) + zeros)

    def per_batch(x):
        return pl.BlockSpec((1,) + x.shape[1:], lambda i, k, pt: (i,) + (0,) * (x.ndim - 1))

    acc_shape = pltpu.VMEM((SAMPLE_ROWS, KV_WIDTH), F32)
    stat_shape = pltpu.VMEM((SAMPLE_ROWS, LANES), F32)
    return pl.pallas_call(
        functools.partial(_nsa_sample_kernel, n_slots=n_slots, t_base=t_base, wpos0=wpos0, n_slc=n_slc, t_q=t_q),
        grid_spec=pltpu.PrefetchScalarGridSpec(
            num_scalar_prefetch=1, grid=(b, n_steps),
            in_specs=[page_spec(slot) for slot in range(n_slots)]
            + [per_batch(x) for x in (tail_t, q_bd, gates, kct, vct, kwt, vwt)],
            out_specs=pl.BlockSpec((1, SAMPLE_ROWS, KV_WIDTH), lambda i, k, pt: (i, 0, 0)),
            scratch_shapes=[pltpu.VMEM((n_steps, SAMPLE_ROWS, LANES), BF), pltpu.VMEM((LANES, n_slots * PAGE), BF),
                            stat_shape, stat_shape, acc_shape, acc_shape, acc_shape]),
        out_shape=jax.ShapeDtypeStruct((b, SAMPLE_ROWS, KV_WIDTH), F32),
        compiler_params=_cparams(("parallel", "arbitrary"), VMEM_LIMIT),
        name="nsa_sample",
    )(table.reshape(-1), *([pool_t] * n_slots), tail_t, q_bd, gates, kct, vct, kwt, vwt)


def _rope_tables(pos):
    half = ROPE_DIM // 2
    inv_freq = ROPE_THETA ** (-jnp.arange(half, dtype=F32) / half)
    ang = pos.astype(F32)[:, None] * inv_freq[None, :]
    cos, sin = jnp.cos(ang), jnp.sin(ang)
    n = pos.shape[0]
    rest = HEAD_DIM - ROPE_DIM
    c = jnp.concatenate([cos, cos, jnp.ones((n, rest), F32)], axis=1)
    a = jnp.concatenate([-sin, jnp.zeros((n, half + rest), F32)], axis=1)
    b = jnp.concatenate([jnp.zeros((n, half), F32), sin, jnp.zeros((n, rest), F32)], axis=1)
    reps = LANES // HEAD_DIM
    return tuple(jnp.tile(x, (1, reps)) for x in (c, a, b))


def _compress_weights(cmp_pe, w_phi1, w_phi2):
    ratio = CMP_BLOCK // CMP_STRIDE
    w1 = w_phi1.reshape(ratio, CMP_STRIDE, 2, HEAD_DIM, CMP_HIDDEN)
    eye2 = jnp.eye(2, dtype=w_phi1.dtype)
    big = jnp.einsum('jscde,hg->cjshdge', w1, eye2)
    big = big.reshape(2, ratio, CMP_STRIDE * 2 * HEAD_DIM, 2 * CMP_HIDDEN).astype(BF)
    big = jnp.concatenate([big[:, jj] for jj in range(ratio)], axis=-1)
    pe = cmp_pe.reshape(ratio, CMP_STRIDE, 2, HEAD_DIM)
    pe = jnp.broadcast_to(pe.transpose(2, 0, 1, 3)[:, :, :, None, :], (2, ratio, CMP_STRIDE, 2, HEAD_DIM))
    pe = pe.reshape(2, ratio, 1, CMP_STRIDE * 2 * HEAD_DIM)
    pe = jnp.broadcast_to(pe, (2, ratio, 8, pe.shape[-1]))
    eye4 = jnp.eye(KV_HEADS, dtype=w_phi2.dtype).reshape(2, 2, KV_HEADS)
    w2 = jnp.einsum('ced,pgh->cpgehd', w_phi2, eye4)
    w2 = w2.reshape(2, 2, 2 * CMP_HIDDEN, KV_WIDTH).astype(BF)
    return big, pe, w2


def _b_in_weight(w):
    n_gate = w.shape[1] - NSA_WIDTH - MEM_WIDTH
    gates = jnp.pad(w[:, NSA_WIDTH:NSA_WIDTH + n_gate], ((0, 0), (0, GATE_PAD - n_gate)))
    return jnp.concatenate([w[:, :NSA_WIDTH], w[:, NSA_WIDTH + n_gate:], gates], axis=1).astype(BF)


def _pad_rows(x, n):
    return jnp.pad(x, ((0, 0), (0, n - x.shape[1]), (0, 0)))


def _tail_page_t(new_rows):
    b = new_rows.shape[0]
    return _pad_rows(new_rows, PAGE).reshape(b, PAGE, 2, KV_HEADS, HEAD_DIM).transpose(0, 2, 3, 4, 1)


def _rows_t(rows):
    b, n, _ = rows.shape
    kv = rows.reshape(b, n, 2, KV_WIDTH).transpose(0, 2, 3, 1).astype(BF)
    return kv[:, 0], kv[:, 1]


def _row_tiles(s):
    return {"proj": min(s, 512), "gate": min(s, 512), "mlp": min(s, 512)}


def _trunk_a(x, mem_kv, p, l, *, chunk, tiles, emit_v, per_row_sets):
    if chunk == PAGE:
        ws, bs = p["w_spatial"][l], jnp.repeat(p["b_spatial"][l].T, A_GROUP_DIM, axis=1)
    else:
        reps = PAGE // chunk
        ws = jnp.tile(p["w_spatial"][l][:, :chunk, :chunk], (1, reps, reps))
        bs = jnp.tile(jnp.repeat(p["b_spatial"][l].T[:chunk], A_GROUP_DIM, axis=1), (reps, 1))
    res = _a_in(x, p["w_in_a"][l], p["ln_v_g"][l][None], p["ln_v_b"][l][None], ws, bs,
                chunk=chunk, tm=tiles["gate"], emit_v=emit_v, mem_kv=None if per_row_sets else mem_kv)
    mixed, om = res[0], res[1]
    if per_row_sets:
        om = _mem_attn_rows(om, mem_kv, chunk)
    x = _mlp_block(x, mixed, om, p["w_out_a"][l], p, l, tm=tiles["mlp"], a_transposed=False)
    return x, (res[2] if emit_v else None)


def _mlp_block(x, a, om, wo, p, l, *, tm, a_transposed):
    return _out_mlp(x, a, om, wo, p["ln1_g"][l][None], p["ln1_b"][l][None], p["w_up"][l], p["w_down"][l],
                    p["ln2_g"][l][None], p["ln2_b"][l][None], tm=tm, a_transposed=a_transposed)


def kernel(x_prompt, x_sample, cache_cmp_kv, cache_slc_kv, cache_win_kv, cache_mem_kv, page_table, mem_prompt,
           w_in_a, ln_v_g, ln_v_b, w_spatial, b_spatial, w_out_a, w_in_b, w_out_b, w_kv_shared, cmp_pe,
           w_phi1, w_phi2, w_mem_kv, ln1_g, ln1_b, ln2_g, ln2_b, w_up, w_down):
    bp, s, d = x_prompt.shape
    bs, t, _ = x_sample.shape
    n_pages = page_table.shape[1]
    past = n_pages * PAGE
    n_buf = cache_win_kv.shape[1]
    assert bs * t == PAGE and PAGE % t == 0 and s % PAGE == 0 and n_buf % PAGE == 0

    p = {"w_in_a": w_in_a.astype(BF), "ln_v_g": ln_v_g, "ln_v_b": ln_v_b, "w_spatial": w_spatial,
         "b_spatial": b_spatial, "w_out_a": w_out_a.astype(BF), "w_out_b": w_out_b.astype(BF),
         "ln1_g": ln1_g, "ln1_b": ln1_b, "ln2_g": ln2_g, "ln2_b": ln2_b,
         "w_up": w_up.astype(BF), "w_down": w_down.astype(BF)}
    w_in_b2 = [_b_in_weight(w_in_b[i]) for i in range(w_in_b.shape[0])]
    w_kv = w_kv_shared.astype(BF)
    cw1, cpe, cw2 = _compress_weights(cmp_pe, w_phi1, w_phi2)

    tl = _row_tiles(s)
    mem_kv_prompt = _mem_kv_proj(mem_prompt, w_mem_kv)
    x = x_prompt
    for l in range(N_A):
        x, _ = _trunk_a(x, mem_kv_prompt[l], p, l, chunk=PAGE, tiles=tl, emit_v=False, per_row_sets=False)
    tabs_p = _rope_tables(jnp.arange(s, dtype=jnp.int32))
    cmp_t, slc_t, win_t, ks, vst, kw, vwt = _kv_shared(x, w_kv, tabs_p, tm=tl["proj"], attn_layouts=True)
    shape_t = (2, KV_HEADS, HEAD_DIM)
    cmp_blk = _compress(cmp_t.reshape(bp, *shape_t, s), None, None, cw1, cpe, cw2)
    n_cmp = s // CMP_STRIDE - 1
    ncp = -(-n_cmp // PAGE) * PAGE
    kc, vct = _rows_layout(_pad_rows(cmp_blk[:, 1:1 + n_cmp], ncp))
    for i in range(DEPTH - N_A):
        l = N_A + i
        q, om, gt = _b_in(x, w_in_b2[i], tabs_p, tm=tl["proj"], mem_kv=mem_kv_prompt[l])
        ot = _nsa(q, gt, kc, vct, ks, vst, kw, vwt, t_base=0, wpos0=0, n_slc=s // SEL_BLOCK)
        x = _mlp_block(x, ot, om, p["w_out_b"][i], p, l, tm=tl["mlp"], a_transposed=True)
    y_prompt = x
    shape5 = (2, KV_HEADS, HEAD_DIM)

    def rows_major(x_t):
        return x_t.reshape(x_t.shape[0], *shape_t, x_t.shape[2]).transpose(0, 4, 1, 2, 3)

    cmp_kv_prompt = rows_major(cmp_t)
    slc_kv_prompt = rows_major(slc_t)
    win_kv_prompt = rows_major(win_t[:, :, s - min(WINDOW, s):])
    mem_kv_prompt_out = mem_kv_prompt.reshape(DEPTH, bp, MEM_TOKENS, 2, MEM_HEADS, HEAD_DIM)

    rows = bs * t
    x = x_sample.reshape(1, rows, d)
    mem_s = cache_mem_kv.reshape(DEPTH, bs, MEM_TOKENS, 2 * MEM_WIDTH)
    v_rows = []
    for l in range(N_A):
        x, v = _trunk_a(x, mem_s[l], p, l, chunk=t, tiles=_row_tiles(rows), emit_v=True, per_row_sets=True)
        v_rows.append(v.reshape(bs, t, A_WIDTH))
    pos_s = past + (jnp.arange(rows, dtype=jnp.int32) % t)
    tabs_s = _rope_tables(pos_s)
    new_cmp, new_slc, new_win = (r[0].T.reshape(bs, t, ROW_WIDTH)
                                 for r in _kv_shared(x, w_kv, tabs_s, tm=rows, attn_layouts=False))
    cmp_pool_t = cache_cmp_kv.transpose(0, 2, 3, 4, 1)
    slc_pool_t = cache_slc_kv.transpose(0, 2, 3, 4, 1)
    cmp_blk = _compress(cmp_pool_t, page_table, _tail_page_t(new_cmp), cw1, cpe, cw2)
    n_cmp = -(-(past + t) // CMP_STRIDE) - 1
    ncp = -(-n_cmp // PAGE) * PAGE
    kct, vct = _rows_t(_pad_rows(cmp_blk[:, 1:1 + n_cmp], ncp))
    win_all = jnp.concatenate([cache_win_kv.reshape(bs, n_buf, ROW_WIDTH), new_win], axis=1)
    kwt, vwt = _rows_t(_pad_rows(win_all, n_buf + PAGE))
    slc_tail_t = _tail_page_t(new_slc)
    n_slc = -(-(past + t) // SEL_BLOCK)
    per_group = GQA * t
    eye = jnp.eye(KV_HEADS, dtype=BF)
    for i in range(DEPTH - N_A):
        l = N_A + i
        q, qm, gt = _b_in(x, w_in_b2[i], tabs_s, tm=rows)
        om = _mem_attn_rows(qm, mem_s[l], t)
        q = q.reshape(KV_HEADS, GQA, bs, t, HEAD_DIM).transpose(2, 0, 1, 3, 4).reshape(bs, KV_HEADS, per_group, HEAD_DIM)
        q = jnp.pad(q, ((0, 0), (0, 0), (0, ROWS_PER_GROUP - per_group), (0, 0)))
        q_bd = jnp.einsum('bgxd,gh->bgxhd', q, eye).reshape(bs, SAMPLE_ROWS, KV_WIDTH)
        gs = gt[0, :NSA_HEADS * 3].reshape(KV_HEADS, GQA, 3, bs, t).transpose(3, 0, 1, 4, 2)
        gs = gs.reshape(bs, KV_HEADS, per_group, 3)
        gs = jnp.pad(gs, ((0, 0), (0, 0), (0, ROWS_PER_GROUP - per_group), (0, LANES - 3)))
        o = _nsa_sample(q_bd, gs.reshape(bs, SAMPLE_ROWS, LANES), slc_pool_t, page_table, slc_tail_t,
                        kct, vct, kwt, vwt, t_base=past, wpos0=past - n_buf, n_slc=n_slc, t_q=t)
        o = o.reshape(bs, KV_HEADS, ROWS_PER_GROUP, KV_HEADS, HEAD_DIM)
        o = jnp.stack([o[:, g, :per_group, g] for g in range(KV_HEADS)], axis=1)
        o = o.reshape(bs, KV_HEADS, GQA, t, HEAD_DIM).transpose(0, 3, 1, 2, 4).reshape(1, rows, NSA_WIDTH)
        x = _mlp_block(x, o, om, p["w_out_b"][i], p, l, tm=rows, a_transposed=False)
    y_sample = x.reshape(bs, t, d)
    cmp_kv_sample = new_cmp.reshape(bs, t, *shape5)
    slc_kv_sample = new_slc.reshape(bs, t, *shape5)
    win_kv_sample = win_all[:, t:].reshape(bs, n_buf, *shape5)
    gmlp_v_sample = jnp.stack(v_rows)

    return (y_prompt, y_sample, cmp_kv_prompt, slc_kv_prompt, win_kv_prompt, mem_kv_prompt_out,
            cmp_kv_sample, slc_kv_sample, win_kv_sample, gmlp_v_sample)
```

```python
import functools
import math

import jax
import jax.numpy as jnp
from jax import lax
from jax.experimental import pallas as pl
from jax.experimental.pallas import tpu as pltpu

BF = jnp.bfloat16
F32 = jnp.float32

DEPTH = 4
N_A = 2
A_WIDTH = 768
A_GROUPS = 4
A_GROUP_DIM = 192
MEM_TOKENS = 256
MEM_HEADS = 4
MEM_WIDTH = 256
HEAD_DIM = 64
NSA_HEADS = 12
KV_HEADS = 4
GQA = 3
NSA_WIDTH = 768
KV_WIDTH = 256
ROW_WIDTH = 2 * KV_WIDTH
CMP_BLOCK = 32
CMP_STRIDE = 16
CMP_HIDDEN = 128
SEL_BLOCK = 64
SEL_TOPK = 16
WINDOW = 512
ROPE_THETA = 500000.0
ROPE_DIM = 16
D_FF_CHUNK = 1024
ALPHA = (2.0 * DEPTH) ** 0.25
LN_EPS = 1e-5
NEG = -1e30
FORCE_BONUS = 1e4
OUT_OF_WINDOW = -(1 << 30)
PAGE = 128
LANES = 128
SUBLANES = 8
MAX_PAGE_SLOTS = 16
GATE_PAD = 128
SEL_GROUP = 4
NSA_QUERY_TILE = 256
Q_SCALE = HEAD_DIM ** -0.5 * math.log2(math.e)
VMEM_LIMIT = 56 * 1024 * 1024


def _cparams(sem, vmem=None):
    return pltpu.CompilerParams(dimension_semantics=sem, vmem_limit_bytes=vmem)


def _const_spec(shape):
    return pl.BlockSpec(shape, lambda *_: (0,) * len(shape), pipeline_mode=pl.Buffered(1))


def _ln(x, g, b):
    mu = jnp.mean(x, axis=-1, keepdims=True)
    xc = x - mu
    var = jnp.mean(xc * xc, axis=-1, keepdims=True)
    return xc * lax.rsqrt(var + LN_EPS) * g + b


def _dot(a, b):
    return jnp.dot(a, b, preferred_element_type=F32)


def _dot_nt(a, b):
    return lax.dot_general(a, b, (((1,), (1,)), ((), ())), preferred_element_type=F32)


def _rope_cols(x, c, a, b):
    cols = []
    for j in range(x.shape[1] // LANES):
        xc = x[:, j * LANES:(j + 1) * LANES]
        up = pltpu.roll(xc, LANES - ROPE_DIM // 2, axis=1)
        dn = pltpu.roll(xc, ROPE_DIM // 2, axis=1)
        cols.append(xc * c + up * a + dn * b)
    return cols[0] if len(cols) == 1 else jnp.concatenate(cols, axis=1)


def _mem_kv_kernel(x_ref, w_ref, o_ref):
    o_ref[0] = _dot(x_ref[...].astype(BF), w_ref[0].astype(BF))


def _mem_kv_proj(mem_prompt, w_mem_kv):
    b, m, d = mem_prompt.shape
    x = mem_prompt.reshape(b * m, d)
    depth, _, n = w_mem_kv.shape
    tm = _row_tiles(b * m)["mlp"]
    out = pl.pallas_call(
        _mem_kv_kernel,
        grid=(depth, (b * m) // tm),
        in_specs=[pl.BlockSpec((tm, d), lambda l, i: (i, 0)),
                  pl.BlockSpec((1, d, n), lambda l, i: (l, 0, 0))],
        out_specs=pl.BlockSpec((1, tm, n), lambda l, i: (l, i, 0)),
        out_shape=jax.ShapeDtypeStruct((depth, b * m, n), F32),
        compiler_params=_cparams(("parallel", "parallel")),
        name="mem_kv_proj",
    )(x, w_mem_kv)
    return out.reshape(depth, b, m, n)


def _mem_heads(q, kv, own=None):
    k = kv[:, :MEM_WIDTH].astype(BF)
    v = kv[:, MEM_WIDTH:]
    lane = lax.broadcasted_iota(jnp.int32, (1, MEM_WIDTH), 1)
    out = jnp.zeros(q.shape, F32)
    for h in range(MEM_HEADS):
        hm = (lane >= h * HEAD_DIM) & (lane < (h + 1) * HEAD_DIM)
        s = _dot_nt(jnp.where(hm, q, 0.0).astype(BF), k) * (HEAD_DIM ** -0.5)
        if own is not None:
            s = jnp.where(own, s, NEG)
        m = jnp.max(s, axis=-1, keepdims=True)
        p = jnp.exp(s - m)
        p = p / jnp.sum(p, axis=-1, keepdims=True)
        out = out + _dot(p.astype(BF), jnp.where(hm, v, 0.0).astype(BF))
    return out


def _a_in_kernel(x_ref, w_ref, g_ref, b_ref, ws_ref, bs_ref, *rest, chunk_shift, fuse_mem, emit_v):
    mem_ref = rest[0] if fuse_mem else None
    mixed_ref, qm_ref = rest[int(fuse_mem):int(fuse_mem) + 2]
    x = x_ref[0].astype(BF)
    proj = _dot(x, w_ref[...])
    uv = jax.nn.gelu(proj[:, :2 * A_WIDTH])
    u = uv[:, :A_WIDTH]
    v = _ln(uv[:, A_WIDTH:], g_ref[...], b_ref[...])
    if emit_v:
        rest[-1][0] = v
    qm = proj[:, 2 * A_WIDTH:]
    qm_ref[0] = _mem_heads(qm, mem_ref[0]) if fuse_mem else qm
    tm = x.shape[0]
    r = ws_ref.shape[1]
    row = lax.broadcasted_iota(jnp.int32, (r, r), 0)
    col = lax.broadcasted_iota(jnp.int32, (r, r), 1)
    allowed = (col <= row) & ((row >> chunk_shift) == (col >> chunk_shift))
    wgs = [jnp.where(allowed, ws_ref[g], 0.0).astype(BF) for g in range(A_GROUPS)]
    span = 2 * A_GROUP_DIM
    first = lax.broadcasted_iota(jnp.int32, (1, span), 1) < A_GROUP_DIM
    vb = v.astype(BF)
    for ci in range(tm // r):
        rows = slice(ci * r, (ci + 1) * r)
        for h in range(A_GROUPS // 2):
            cols = slice(h * span, (h + 1) * span)
            vc = vb[rows, cols]
            mix = jnp.where(first, _dot(wgs[2 * h], vc), _dot(wgs[2 * h + 1], vc)) + bs_ref[:, cols]
            mixed_ref[0, rows, cols] = (u[rows, cols] * mix).astype(BF)


def _a_in(x, w_in, ln_g, ln_b, ws, bs, *, chunk, tm, emit_v, mem_kv=None):
    b, s, d = x.shape
    n = w_in.shape[1]
    r = ws.shape[1]
    in_specs = [pl.BlockSpec((1, tm, d), lambda i, j: (i, j, 0)),
                _const_spec((d, n)), _const_spec((1, A_WIDTH)), _const_spec((1, A_WIDTH)),
                _const_spec((A_GROUPS, r, r)), _const_spec((r, A_WIDTH))]
    args = [x, w_in, ln_g, ln_b, ws, bs]
    if mem_kv is not None:
        in_specs.append(pl.BlockSpec((1,) + mem_kv.shape[1:], lambda i, j: (i, 0, 0)))
        args.append(mem_kv)
    outs = [jax.ShapeDtypeStruct((b, s, A_WIDTH), BF), jax.ShapeDtypeStruct((b, s, MEM_WIDTH), F32)]
    ospecs = [pl.BlockSpec((1, tm, A_WIDTH), lambda i, j: (i, j, 0)),
              pl.BlockSpec((1, tm, MEM_WIDTH), lambda i, j: (i, j, 0))]
    if emit_v:
        outs.append(jax.ShapeDtypeStruct((b, s, A_WIDTH), F32))
        ospecs.append(pl.BlockSpec((1, tm, A_WIDTH), lambda i, j: (i, j, 0)))
    return pl.pallas_call(
        functools.partial(_a_in_kernel, chunk_shift=int(math.log2(chunk)), fuse_mem=mem_kv is not None,
                          emit_v=emit_v),
        grid=(b, s // tm),
        in_specs=in_specs,
        out_specs=ospecs,
        out_shape=outs,
        compiler_params=_cparams(("parallel", "parallel"), VMEM_LIMIT),
        name="a_in_gate",
    )(*args)


SAMPLE_MEM_SETS = 8


def _mem_attn_kernel(q_ref, kv_ref, o_ref, *, rows_per_set):
    q = q_ref[0]
    nb = kv_ref.shape[0]
    row_set = lax.broadcasted_iota(jnp.int32, (q.shape[0], 1), 0) >> int(math.log2(rows_per_set))
    col_set = lax.broadcasted_iota(jnp.int32, (1, nb * MEM_TOKENS), 1) >> int(math.log2(MEM_TOKENS))
    o_ref[0] = _mem_heads(q, kv_ref[...].reshape(nb * MEM_TOKENS, 2 * MEM_WIDTH), row_set == col_set)


def _mem_attn_rows(q, mem_kv, rows_per_set):
    _, s, w = q.shape
    tq = SAMPLE_MEM_SETS * rows_per_set
    return pl.pallas_call(
        functools.partial(_mem_attn_kernel, rows_per_set=rows_per_set),
        grid=(s // tq,),
        in_specs=[pl.BlockSpec((1, tq, w), lambda j: (0, j, 0)),
                  pl.BlockSpec((SAMPLE_MEM_SETS, MEM_TOKENS, 2 * w), lambda j: (j, 0, 0))],
        out_specs=pl.BlockSpec((1, tq, w), lambda j: (0, j, 0)),
        out_shape=jax.ShapeDtypeStruct((1, s, w), F32),
        compiler_params=_cparams(("parallel",)),
        name="mem_attn",
    )(q, mem_kv)


def _out_mlp_kernel(x_ref, a_ref, om_ref, wo_ref, g1_ref, b1_ref, wu_ref, wd_ref, g2_ref, b2_ref, o_ref,
                    *, a_transposed):
    x = x_ref[0]
    a = a_ref[0]
    a = a.T.astype(BF) if a_transposed else a.astype(BF)
    na = a.shape[1]
    mix = _dot(a, wo_ref[:na, :]) + _dot(om_ref[0].astype(BF), wo_ref[na:, :])
    y = _ln(ALPHA * x + mix, g1_ref[...], b1_ref[...])
    yb = y.astype(BF)
    acc = jnp.zeros(x.shape, F32)
    for c in range(wu_ref.shape[1] // D_FF_CHUNK):
        h = _dot(yb, wu_ref[:, c * D_FF_CHUNK:(c + 1) * D_FF_CHUNK])
        h = jnp.square(jnp.maximum(h, 0.0)).astype(BF)
        acc = acc + _dot(h, wd_ref[c * D_FF_CHUNK:(c + 1) * D_FF_CHUNK, :])
    o_ref[0] = _ln(ALPHA * y + acc, g2_ref[...], b2_ref[...])


def _out_mlp(x, a, om, wo, g1, b1, wu, wd, g2, b2, *, tm, a_transposed):
    b, s, d = x.shape
    na = wo.shape[0] - om.shape[2]
    dff = wu.shape[1]
    if a_transposed:
        a_spec = pl.BlockSpec((1, na, tm), lambda i, j: (i, 0, j))
    else:
        a_spec = pl.BlockSpec((1, tm, na), lambda i, j: (i, j, 0))
    return pl.pallas_call(
        functools.partial(_out_mlp_kernel, a_transposed=a_transposed),
        grid=(b, s // tm),
        in_specs=[pl.BlockSpec((1, tm, d), lambda i, j: (i, j, 0)),
                  a_spec,
                  pl.BlockSpec((1, tm, om.shape[2]), lambda i, j: (i, j, 0)),
                  _const_spec(wo.shape), _const_spec((1, d)), _const_spec((1, d)),
                  _const_spec((d, dff)), _const_spec((dff, d)), _const_spec((1, d)), _const_spec((1, d))],
        out_specs=pl.BlockSpec((1, tm, d), lambda i, j: (i, j, 0)),
        out_shape=jax.ShapeDtypeStruct((b, s, d), F32),
        compiler_params=_cparams(("parallel", "parallel"), VMEM_LIMIT),
        name="out_mlp",
    )(x, a, om, wo, g1, b1, wu, wd, g2, b2)


def _kv_shared_kernel(h_ref, w_ref, c_ref, a_ref, b_ref, cmp_ref, slc_ref, win_ref, *attn_refs):
    kv = _dot(h_ref[0].astype(BF), w_ref[...])
    c, a, b = c_ref[...], a_ref[...], b_ref[...]
    tm = kv.shape[0]
    for br, ref in enumerate((cmp_ref, slc_ref, win_ref)):
        k = _rope_cols(kv[:, br * ROW_WIDTH:br * ROW_WIDTH + KV_WIDTH], c, a, b)
        v = kv[:, br * ROW_WIDTH + KV_WIDTH:(br + 1) * ROW_WIDTH]
        vt = v.T
        ref[0, :KV_WIDTH, :] = k.T
        ref[0, KV_WIDTH:, :] = vt
        if attn_refs and br > 0:
            k_ref, vt_ref = attn_refs[2 * (br - 1):2 * br]
            if br == 1:
                lane = lax.broadcasted_iota(jnp.int32, (1, LANES), 1)
                row = pl.program_id(1) * tm + lax.broadcasted_iota(jnp.int32, (tm, 1), 0)
                block_hot = jnp.where(lane - HEAD_DIM == (row >> int(math.log2(SEL_BLOCK))), 1.0, 0.0)
            for g in range(KV_HEADS):
                if br == 1:
                    pair = k[:, (g // 2) * LANES:(g // 2 + 1) * LANES]
                    if g % 2:
                        pair = pltpu.roll(pair, HEAD_DIM, axis=1)
                    k_ref[0, g] = jnp.where(lane < HEAD_DIM, pair, block_hot).astype(BF)
                else:
                    k_ref[0, g] = k[:, g * HEAD_DIM:(g + 1) * HEAD_DIM].astype(BF)
                for u in range(tm // PAGE):
                    vt_ref[0, g, u] = vt[g * HEAD_DIM:(g + 1) * HEAD_DIM, u * PAGE:(u + 1) * PAGE].astype(BF)


def _kv_shared(h, w_kv, tabs, *, tm, attn_layouts):
    b, s, d = h.shape
    row_spec = pl.BlockSpec((1, ROW_WIDTH, tm), lambda i, j: (i, 0, j))
    tab_spec = pl.BlockSpec((tm, LANES), lambda i, j: (j, 0))
    out_specs = [row_spec, row_spec, row_spec]
    out_shape = [jax.ShapeDtypeStruct((b, ROW_WIDTH, s), F32)] * 3
    if attn_layouts:
        assert s // SEL_BLOCK <= LANES - HEAD_DIM
        vt_spec = pl.BlockSpec((1, KV_HEADS, tm // PAGE, HEAD_DIM, PAGE), lambda i, j: (i, 0, j, 0, 0))
        vt_shape = jax.ShapeDtypeStruct((b, KV_HEADS, s // PAGE, HEAD_DIM, PAGE), BF)
        for width in (LANES, HEAD_DIM):
            out_specs += [pl.BlockSpec((1, KV_HEADS, tm, width), lambda i, j: (i, 0, j, 0)), vt_spec]
            out_shape += [jax.ShapeDtypeStruct((b, KV_HEADS, s, width), BF), vt_shape]
    return pl.pallas_call(
        _kv_shared_kernel,
        grid=(b, s // tm),
        in_specs=[pl.BlockSpec((1, tm, d), lambda i, j: (i, j, 0)), _const_spec(w_kv.shape),
                  tab_spec, tab_spec, tab_spec],
        out_specs=out_specs,
        out_shape=out_shape,
        compiler_params=_cparams(("parallel", "parallel")),
        name="kv_shared",
    )(h, w_kv, *tabs)


def _b_in_kernel(x_ref, w_ref, c_ref, a_ref, b_ref, *rest, fuse_mem):
    mem_ref = rest[0] if fuse_mem else None
    q_ref, qm_ref, gt_ref = rest[int(fuse_mem):]
    proj = _dot(x_ref[0].astype(BF), w_ref[...])
    q = _rope_cols(proj[:, :NSA_WIDTH], c_ref[...], a_ref[...], b_ref[...]) * Q_SCALE
    for h in range(NSA_HEADS):
        q_ref[0, h] = q[:, h * HEAD_DIM:(h + 1) * HEAD_DIM].astype(BF)
    qm = proj[:, NSA_WIDTH:NSA_WIDTH + MEM_WIDTH]
    qm_ref[0] = _mem_heads(qm, mem_ref[0]) if fuse_mem else qm
    gt_ref[0] = jax.nn.sigmoid(proj[:, NSA_WIDTH + MEM_WIDTH:]).T


def _b_in(x, w_in, tabs, *, tm, mem_kv=None):
    b, s, d = x.shape
    tab_spec = pl.BlockSpec((tm, LANES), lambda i, j: (j, 0))
    in_specs = [pl.BlockSpec((1, tm, d), lambda i, j: (i, j, 0)), _const_spec(w_in.shape),
                tab_spec, tab_spec, tab_spec]
    args = [x, w_in, *tabs]
    if mem_kv is not None:
        in_specs.append(pl.BlockSpec((1,) + mem_kv.shape[1:], lambda i, j: (i, 0, 0)))
        args.append(mem_kv)
    return pl.pallas_call(
        functools.partial(_b_in_kernel, fuse_mem=mem_kv is not None),
        grid=(b, s // tm),
        in_specs=in_specs,
        out_specs=[pl.BlockSpec((1, NSA_HEADS, tm, HEAD_DIM), lambda i, j: (i, 0, j, 0)),
                   pl.BlockSpec((1, tm, MEM_WIDTH), lambda i, j: (i, j, 0)),
                   pl.BlockSpec((1, GATE_PAD, tm), lambda i, j: (i, 0, j))],
        out_shape=[jax.ShapeDtypeStruct((b, NSA_HEADS, s, HEAD_DIM), BF),
                   jax.ShapeDtypeStruct((b, s, MEM_WIDTH), F32),
                   jax.ShapeDtypeStruct((b, GATE_PAD, s), F32)],
        compiler_params=_cparams(("parallel", "parallel")),
        name="b_in",
    )(*args)


def _kv_layout_kernel(x_ref, k_ref, vt_ref):
    x = x_ref[0]
    vt = x[:, KV_WIDTH:].T
    for g in range(KV_HEADS):
        k_ref[0, g] = x[:, g * HEAD_DIM:(g + 1) * HEAD_DIM].astype(BF)
        vt_ref[0, g, 0] = vt[g * HEAD_DIM:(g + 1) * HEAD_DIM].astype(BF)


def _rows_layout(rows):
    b, l, _ = rows.shape
    n = l // PAGE
    return pl.pallas_call(
        _kv_layout_kernel,
        grid=(b, n),
        in_specs=[pl.BlockSpec((1, PAGE, ROW_WIDTH), lambda i, p: (i, p, 0))],
        out_specs=[pl.BlockSpec((1, KV_HEADS, PAGE, HEAD_DIM), lambda i, p: (i, 0, p, 0)),
                   pl.BlockSpec((1, KV_HEADS, 1, HEAD_DIM, PAGE), lambda i, p: (i, 0, p, 0, 0))],
        out_shape=[jax.ShapeDtypeStruct((b, KV_HEADS, l, HEAD_DIM), BF),
                   jax.ShapeDtypeStruct((b, KV_HEADS, n, HEAD_DIM, PAGE), BF)],
        compiler_params=_cparams(("parallel", "parallel")),
        name="kv_layout",
    )(rows)


def _compress_kernel(pt_ref, *refs, n_slots, tail_group):
    page_refs = refs[:n_slots]
    rest = refs[n_slots:]
    if tail_group is not None:
        tail_ref, rest = rest[0], rest[1:]
    w1_ref, pe_ref, w2_ref, o_ref, xs_ref, carry_ref = rest
    k = pl.program_id(1)
    per_page = PAGE // CMP_STRIDE
    m = n_slots * per_page
    hid2 = 2 * CMP_HIDDEN

    @pl.when(k == 0)
    def _():
        carry_ref[...] = jnp.zeros(carry_ref.shape, F32)

    for i, pr in enumerate(page_refs):
        x = pr[0]
        if tail_group is not None and i == n_slots - 1:
            x = jnp.where(k == tail_group, tail_ref[0], x)
        rows = slice(i * PAGE, (i + 1) * PAGE)
        for c in range(2):
            xc = x[c].reshape(KV_WIDTH, PAGE).T
            for hp in range(2):
                xs_ref[2 * c + hp, rows, :] = xc[:, hp * LANES:(hp + 1) * LANES]

    def strides(vcol):
        return jnp.concatenate(
            [xs_ref[vcol, pl.ds(s, m, stride=CMP_STRIDE), :] for s in range(CMP_STRIDE)], axis=1)

    first_row = lax.broadcasted_iota(jnp.int32, (m, 1), 0) == 0
    out_cols = []
    for c in range(2):
        w1 = w1_ref[c]
        lhs = jnp.concatenate([strides(2 * c + hp) for hp in range(2)] + [pe_ref[c, 0], pe_ref[c, 1]],
                              axis=0).astype(BF)
        a = _dot(lhs, w1)
        hpe = a[2 * m:2 * m + 1, :hid2] + a[2 * m + 8:2 * m + 9, hid2:]
        acc = jnp.zeros((m, KV_WIDTH), F32)
        for hp in range(2):
            a0 = a[hp * m:(hp + 1) * m, :hid2]
            a1 = a[hp * m:(hp + 1) * m, hid2:]
            prev = jnp.where(first_row, carry_ref[c, hp, 0:1, :], pltpu.roll(a0, 1, axis=0))
            carry_ref[c, hp, 0:1, :] = a0[m - 1:m]
            acc = acc + _dot(jax.nn.gelu(prev + a1 + hpe).astype(BF), w2_ref[c, hp])
        out_cols.append(acc)
    o_ref[0] = jnp.concatenate(out_cols, axis=1)


def _compress(pool, table, tail, w1, pe, w2):
    page_shape = pool.shape[1:-1] + (PAGE,)
    zeros = (0,) * (len(page_shape) - 1)
    paged = table is not None
    if paged:
        b, n_real = table.shape
    else:
        b, n_real = pool.shape[0], pool.shape[-1] // PAGE
        table = jnp.zeros((1, 1), jnp.int32)
    n_pages = n_real + (0 if tail is None else 1)
    n_slots = max(p for p in range(1, MAX_PAGE_SLOTS + 1) if n_pages % p == 0)
    n_groups = n_pages // n_slots
    m = n_slots * (PAGE // CMP_STRIDE)

    def page_spec(slot):
        if paged:
            return pl.BlockSpec(
                (1,) + page_shape,
                lambda i, k, pt: (pt[i * n_real + jnp.minimum(k * n_slots + slot, n_real - 1)],) + zeros + (0,))
        return pl.BlockSpec((1,) + page_shape, lambda i, k, pt: (i,) + zeros + (k * n_slots + slot,))

    in_specs = [page_spec(slot) for slot in range(n_slots)]
    args = [pool] * n_slots
    if tail is not None:
        in_specs.append(pl.BlockSpec((1,) + page_shape, lambda i, k, pt: (i,) + zeros + (0,)))
        args.append(tail)
    in_specs += [pl.BlockSpec(x.shape, lambda i, k, pt, nd=x.ndim: (0,) * nd) for x in (w1, pe, w2)]
    return pl.pallas_call(
        functools.partial(_compress_kernel, n_slots=n_slots,
                          tail_group=None if tail is None else n_groups - 1),
        grid_spec=pltpu.PrefetchScalarGridSpec(
            num_scalar_prefetch=1, grid=(b, n_groups), in_specs=in_specs,
            out_specs=pl.BlockSpec((1, m, ROW_WIDTH), lambda i, k, pt: (i, k, 0)),
            scratch_shapes=[pltpu.VMEM((4, n_slots * PAGE, LANES), F32),
                            pltpu.VMEM((2, 2, 8, 2 * CMP_HIDDEN), F32)]),
        out_shape=jax.ShapeDtypeStruct((b, n_groups * m, ROW_WIDTH), F32),
        compiler_params=_cparams(("parallel", "arbitrary"), VMEM_LIMIT),
        name="compress",
    )(table.reshape(-1), *args, w1, pe, w2)


def _nsa_kernel(q_ref, gt_ref, kc_ref, vct_ref, ks_ref, vst_ref, kw_ref, vwt_ref, o_ref, psum_ref,
                sbuf0_ref, sbuf1_ref, *, t_base, wpos0, n_slc, nsp):
    g = pl.program_id(1)
    i = pl.program_id(2)
    nq = q_ref.shape[2]
    nl = GQA * nq
    ncp = kc_ref.shape[2]
    gk = SEL_GROUP * PAGE
    nkt_s = vst_ref.shape[2]
    q = q_ref[0].reshape(nl, HEAD_DIM)
    lane = lax.broadcasted_iota(jnp.int32, (1, nl), 1)
    q0 = t_base + i * nq
    t = q0 + (lane & (nq - 1))

    s = _dot_nt(kc_ref[0, 0], q)
    n_idx = lax.broadcasted_iota(jnp.int32, (ncp, 1), 0)
    cmask = n_idx * CMP_STRIDE + (CMP_BLOCK - 1) <= t
    s = jnp.where(cmask, s, NEG)
    p = jnp.where(cmask, jnp.exp2(s - jnp.max(s, axis=0, keepdims=True)), 0.0)
    l = jnp.sum(p, axis=0, keepdims=True)
    p = p * jnp.where(l > 0.0, 1.0 / l, 0.0)
    o_cmp = jnp.zeros((HEAD_DIM, nl), F32)
    for kt in range(ncp // PAGE):
        o_cmp = o_cmp + _dot(vct_ref[0, 0, kt], p[kt * PAGE:(kt + 1) * PAGE].astype(BF))

    rel = (q0 - wpos0) // PAGE
    n_wt = (WINDOW + nq) // PAGE
    tile_row = lax.broadcasted_iota(jnp.int32, (PAGE, 1), 0)
    parts = []
    for u in range(n_wt):
        kt = rel - WINDOW // PAGE + u
        ktc = jnp.maximum(kt, 0)
        kb = kw_ref[0, 0, pl.ds(pl.multiple_of(ktc * PAGE, PAGE), PAGE), :]
        krel = kt * PAGE + tile_row
        kpos = jnp.where(krel >= 0, wpos0 + krel, OUT_OF_WINDOW)
        d = lax.bitcast_convert_type(t - kpos, jnp.uint32)
        parts.append(jnp.where(d < WINDOW, _dot_nt(kb, q), NEG))
    sw = jnp.concatenate(parts, axis=0)
    pw = jnp.exp2(sw - jnp.max(sw, axis=0, keepdims=True))
    l_w = jnp.sum(pw, axis=0, keepdims=True)
    acc_w = jnp.zeros((HEAD_DIM, nl), F32)
    for u in range(n_wt):
        ktc = jnp.maximum(rel - WINDOW // PAGE + u, 0)
        acc_w = acc_w + _dot(vwt_ref[0, 0, ktc], pw[u * PAGE:(u + 1) * PAGE].astype(BF))
    o_win = acc_w * (1.0 / l_w)

    psum = p[:, :nq]
    for r in range(1, GQA):
        psum = psum + p[:, r * nq:(r + 1) * nq]
    front = SUBLANES
    ratio = SEL_BLOCK // CMP_STRIDE
    slabs = []
    for sl in range(nq // LANES):
        psum_ref[sl, :front, :] = jnp.zeros((front, LANES), F32)
        psum_ref[sl, front:front + ncp, :] = psum[:, sl * LANES:(sl + 1) * LANES]
        psum_ref[sl, front + ncp:, :] = jnp.zeros((psum_ref.shape[1] - front - ncp, LANES), F32)
        part = psum_ref[sl, pl.ds(front - 1, nsp, stride=ratio), :]
        for d in range(1, ratio + 1):
            part = part + psum_ref[sl, pl.ds(front - 1 + d, nsp, stride=ratio), :]
        slabs.append(part)
    imp = slabs[0] if len(slabs) == 1 else jnp.concatenate(slabs, axis=1)
    tq = t[:, :nq]
    j = lax.broadcasted_iota(jnp.int32, (nsp, 1), 0)
    cur = tq >> int(math.log2(SEL_BLOCK))
    forced = (j == 0) | (j == cur) | (j == cur - 1)
    score = jnp.where(j * SEL_BLOCK <= tq, imp + FORCE_BONUS * forced.astype(F32), NEG)
    sub = SUBLANES
    chunks = [score[c * sub:(c + 1) * sub] for c in range(nsp // sub)]
    jc = j[:sub]
    cnts = [jnp.zeros((sub, nq), F32) for _ in chunks]
    for ii in range(n_slc):
        ri = score[ii:ii + 1, :]
        for c, sc_c in enumerate(chunks):
            if c * sub > ii:
                one = jnp.where(ri >= sc_c, 1.0, 0.0)
            elif c * sub + sub - 1 <= ii:
                one = jnp.where(ri > sc_c, 1.0, 0.0)
            else:
                one = jnp.where(jc + c * sub > ii, jnp.where(ri >= sc_c, 1.0, 0.0), jnp.where(ri > sc_c, 1.0, 0.0))
            cnts[c] = cnts[c] + one
    cnt = jnp.concatenate(cnts, axis=0)
    bias = jnp.where(cnt < float(SEL_TOPK), jnp.where(score > NEG / 2, 0.0, NEG), NEG)
    bias_t = jnp.concatenate([bias, jnp.zeros((LANES - nsp, nq), F32)], axis=0).T[:, :LANES - HEAD_DIM]
    q_ext = jnp.concatenate([q, jnp.concatenate([bias_t.astype(BF)] * GQA, axis=0)], axis=1)

    key_row = lax.broadcasted_iota(jnp.int32, (gk, 1), 0)

    def fill(buf, gi):
        kb = ks_ref[0, 0, pl.ds(pl.multiple_of(gi * gk, gk), gk), :]
        buf[...] = _dot_nt(kb, q_ext)

    def consume(buf, gi, carry, causal):
        m_old, l_old, acc = carry
        sc = buf[...]
        if causal:
            sc = jnp.where(gi * gk + key_row <= t, sc, NEG)
        m_new = jnp.maximum(m_old, jnp.max(sc, axis=0, keepdims=True))
        alpha = jnp.exp2(m_old - m_new)
        pe = jnp.exp2(sc - m_new)
        acc = acc * alpha
        for u in range(SEL_GROUP):
            acc = acc + _dot(vst_ref[0, 0, gi * SEL_GROUP + u], pe[u * PAGE:(u + 1) * PAGE].astype(BF))
        return m_new, l_old * alpha + jnp.sum(pe, axis=0, keepdims=True), acc

    init = (jnp.full((1, nl), NEG, F32), jnp.zeros((1, nl), F32), jnp.zeros((HEAD_DIM, nl), F32))

    n_groups = jnp.minimum(nkt_s - 1, (q0 + nq - 1) // PAGE) // SEL_GROUP + 1
    fill(sbuf0_ref, 0)

    def pair_body(kk, carry):
        gi = 2 * kk
        fill(sbuf1_ref, gi + 1)
        carry = consume(sbuf0_ref, gi, carry, False)
        fill(sbuf0_ref, gi + 2)
        return consume(sbuf1_ref, gi + 1, carry, False)

    n_pairs = (n_groups - 1) // 2
    carry = lax.fori_loop(0, n_pairs, pair_body, init)
    gi = 2 * n_pairs

    def odd_tail(carry):
        fill(sbuf1_ref, gi + 1)
        carry = consume(sbuf0_ref, gi, carry, False)
        return consume(sbuf1_ref, gi + 1, carry, True)

    def even_tail(carry):
        return consume(sbuf0_ref, gi, carry, True)

    _, l_s, acc_s = lax.cond(n_groups - 1 - gi == 1, odd_tail, even_tail, carry)

    o_sel = acc_s * jnp.where(l_s > 0.0, 1.0 / l_s, 0.0)
    for r in range(GQA):
        sl = slice(r * nq, (r + 1) * nq)
        grow = (GQA * g + r) * 3
        o = (gt_ref[0, pl.ds(grow, 1), :] * o_cmp[:, sl]
             + gt_ref[0, pl.ds(grow + 1, 1), :] * o_sel[:, sl]
             + gt_ref[0, pl.ds(grow + 2, 1), :] * o_win[:, sl])
        o_ref[0, r * HEAD_DIM:(r + 1) * HEAD_DIM, :] = o


def _nsa(q, gt, kc, vct, ks, vst, kw, vwt, *, t_base, wpos0, n_slc):
    b, _, sq, _ = q.shape
    nq = NSA_QUERY_TILE
    assert sq % nq == 0 and t_base % PAGE == 0 and (t_base - wpos0) % PAGE == 0 and t_base >= wpos0
    assert vst.shape[2] % SEL_GROUP == 0
    nsp = max(-(-n_slc // 8) * 8, vst.shape[2] * (PAGE // SEL_BLOCK))
    assert nsp <= LANES - HEAD_DIM and ks.shape[3] == LANES
    ncp = kc.shape[2]
    psum_rows = 8 + max(ncp + 8, (SEL_BLOCK // CMP_STRIDE) * nsp + 8)
    psum_rows = -(-psum_rows // 8) * 8

    def kv_specs(k, vt):
        return [pl.BlockSpec((1, 1) + k.shape[2:], lambda bi, gi, ii: (bi, gi, 0, 0)),
                pl.BlockSpec((1, 1) + vt.shape[2:], lambda bi, gi, ii: (bi, gi, 0, 0, 0))]

    return pl.pallas_call(
        functools.partial(_nsa_kernel, t_base=t_base, wpos0=wpos0, n_slc=n_slc, nsp=nsp),
        grid=(b, KV_HEADS, sq // nq),
        in_specs=[pl.BlockSpec((1, GQA, nq, HEAD_DIM), lambda bi, gi, ii: (bi, gi, ii, 0)),
                  pl.BlockSpec((1, GATE_PAD, nq), lambda bi, gi, ii: (bi, 0, ii))]
        + kv_specs(kc, vct) + kv_specs(ks, vst) + kv_specs(kw, vwt),
        out_specs=pl.BlockSpec((1, GQA * HEAD_DIM, nq), lambda bi, gi, ii: (bi, gi, ii)),
        out_shape=jax.ShapeDtypeStruct((b, NSA_WIDTH, sq), F32),
        scratch_shapes=[pltpu.VMEM((nq // LANES, psum_rows, LANES), F32),
                        pltpu.VMEM((SEL_GROUP * PAGE, GQA * nq), F32), pltpu.VMEM((SEL_GROUP * PAGE, GQA * nq), F32)],
        compiler_params=_cparams(("parallel", "parallel", "parallel"), VMEM_LIMIT),
        name="nsa_attend",
    )(q, gt, kc, vct, ks, vst, kw, vwt)


SAMPLE_ROWS = 64
ROWS_PER_GROUP = SAMPLE_ROWS // KV_HEADS


def _nsa_sample_kernel(pt_ref, *refs, n_slots, t_base, wpos0, n_slc, t_q):
    page_refs = refs[:n_slots]
    (tail_ref, q_ref, gt_ref, kct_ref, vct_ref, kwt_ref, vwt_ref, o_ref,
     ind_ref, expand_ref, m_ref, l_ref, acc_ref, oc_ref, ow_ref) = refs[n_slots:]
    k = pl.program_id(1)
    n_steps = pl.num_programs(1)
    q = q_ref[0]
    row = lax.broadcasted_iota(jnp.int32, (SAMPLE_ROWS, 1), 0)
    t = t_base + (row & (t_q - 1))

    @pl.when(k == 0)
    def _():
        ncp = kct_ref.shape[2]
        s = _dot(q, kct_ref[0])
        n_idx = lax.broadcasted_iota(jnp.int32, (1, ncp), 1)
        cmask = n_idx * CMP_STRIDE + (CMP_BLOCK - 1) <= t
        s = jnp.where(cmask, s, NEG)
        p = jnp.where(cmask, jnp.exp2(s - jnp.max(s, axis=1, keepdims=True)), 0.0)
        l = jnp.sum(p, axis=1, keepdims=True)
        p = p * jnp.where(l > 0.0, 1.0 / l, 0.0)
        oc_ref[...] = _dot_nt(p.astype(BF), vct_ref[0])

        parts = []
        for g in range(KV_HEADS):
            base = g * ROWS_PER_GROUP
            acc = p[base:base + t_q]
            for r in range(1, GQA):
                acc = acc + p[base + r * t_q:base + (r + 1) * t_q]
            parts.append(acc)
        psum = jnp.concatenate(parts, axis=0)
        nsl = -(-n_slc // LANES) * LANES
        n_col = lax.broadcasted_iota(jnp.int32, (ncp, 1), 0)
        j_row = lax.broadcasted_iota(jnp.int32, (1, nsl), 1)
        ratio = SEL_BLOCK // CMP_STRIDE
        overlap = jnp.where((n_col >= ratio * j_row - 1) & (n_col <= ratio * j_row + ratio - 1), 1.0, 0.0).astype(BF)
        hi = psum.astype(BF)
        mid = (psum - hi.astype(F32)).astype(BF)
        lo = (psum - hi.astype(F32) - mid.astype(F32)).astype(BF)
        imp = _dot(hi, overlap) + _dot(mid, overlap) + _dot(lo, overlap)
        nsel_rows = KV_HEADS * t_q
        tq = t_base + (lax.broadcasted_iota(jnp.int32, (nsel_rows, 1), 0) & (t_q - 1))
        cur = tq >> int(math.log2(SEL_BLOCK))
        forced = (j_row == 0) | (j_row == cur) | (j_row == cur - 1)
        score = jnp.where(j_row * SEL_BLOCK <= tq, imp + FORCE_BONUS * forced.astype(F32), NEG)
        score_t = jnp.concatenate([score, jnp.full((LANES - nsel_rows, nsl), NEG, F32)], axis=0).T
        sub = SUBLANES
        nsp = -(-n_slc // sub) * sub
        jc = lax.broadcasted_iota(jnp.int32, (sub, 1), 0)
        chunks = [score_t[c * sub:(c + 1) * sub] for c in range(nsp // sub)]
        cnts = [jnp.zeros((sub, LANES), F32) for _ in chunks]
        for ii in range(n_slc):
            ri = score_t[ii:ii + 1, :]
            for c, sc_c in enumerate(chunks):
                if c * sub > ii:
                    one = jnp.where(ri >= sc_c, 1.0, 0.0)
                elif c * sub + sub - 1 <= ii:
                    one = jnp.where(ri > sc_c, 1.0, 0.0)
                else:
                    one = jnp.where(jc + c * sub > ii, jnp.where(ri >= sc_c, 1.0, 0.0), jnp.where(ri > sc_c, 1.0, 0.0))
                cnts[c] = cnts[c] + one
        chosen = [jnp.where(cn < float(SEL_TOPK), jnp.where(sc_c > NEG / 2, 1.0, 0.0), 0.0)
                  for cn, sc_c in zip(cnts, chunks)]
        chosen_t = jnp.concatenate(chosen + [jnp.zeros((nsl - nsp, LANES), F32)], axis=0)
        chosen_rows = chosen_t.T.astype(BF)
        col = lax.broadcasted_iota(jnp.int32, (1, LANES), 1)
        src = (row >> int(math.log2(ROWS_PER_GROUP))) * t_q + (row & (t_q - 1))
        expand = jnp.where(col == src, 1.0, 0.0).astype(BF)
        ind = _dot(expand, chosen_rows).astype(BF)
        blocks_per_step = n_slots * (PAGE // SEL_BLOCK)
        j_all = lax.broadcasted_iota(jnp.int32, (nsl, 1), 0)
        for st in range(ind_ref.shape[0]):
            pick = jnp.where((j_all == st * blocks_per_step + col) & (col < blocks_per_step), 1.0, 0.0).astype(BF)
            ind_ref[st] = _dot(ind, pick).astype(BF)
        key_lane = lax.broadcasted_iota(jnp.int32, (1, n_slots * PAGE), 1)
        local_block = lax.broadcasted_iota(jnp.int32, (LANES, 1), 0)
        expand_ref[...] = jnp.where(local_block == (key_lane >> int(math.log2(SEL_BLOCK))), 1.0, 0.0).astype(BF)

        nw = kwt_ref.shape[2]
        sw = _dot(q, kwt_ref[0])
        kpos = wpos0 + lax.broadcasted_iota(jnp.int32, (1, nw), 1)
        d = lax.bitcast_convert_type(t - kpos, jnp.uint32)
        sw = jnp.where(d < WINDOW, sw, NEG)
        pw = jnp.exp2(sw - jnp.max(sw, axis=1, keepdims=True))
        lw = jnp.sum(pw, axis=1, keepdims=True)
        ow_ref[...] = _dot_nt(pw.astype(BF), vwt_ref[0]) * (1.0 / lw)

        m_ref[...] = jnp.full(m_ref.shape, NEG, F32)
        l_ref[...] = jnp.zeros(l_ref.shape, F32)
        acc_ref[...] = jnp.zeros(acc_ref.shape, F32)

    vts = []
    scs = []
    for i, pr in enumerate(page_refs):
        x = pr[0]
        if i == n_slots - 1:
            x = jnp.where(k == n_steps - 1, tail_ref[0], x)
        scs.append(_dot(q, x[0].reshape(KV_WIDTH, PAGE).astype(BF)))
        vts.append(x[1].reshape(KV_WIDTH, PAGE).astype(BF))
    sc = jnp.concatenate(scs, axis=1)
    nk = n_slots * PAGE
    kidx = k * nk + lax.broadcasted_iota(jnp.int32, (1, nk), 1)
    picked = _dot(ind_ref[k], expand_ref[...])
    sc = jnp.where(picked > 0.5, jnp.where(kidx <= t, sc, NEG), NEG)
    m_old = m_ref[:, :1]
    m_new = jnp.maximum(m_old, jnp.max(sc, axis=1, keepdims=True))
    alpha = jnp.exp2(m_old - m_new)
    pe = jnp.exp2(sc - m_new)
    acc = acc_ref[...] * alpha
    for i in range(n_slots):
        acc = acc + _dot_nt(pe[:, i * PAGE:(i + 1) * PAGE].astype(BF), vts[i])
    l_new = l_ref[:, :1] * alpha + jnp.sum(pe, axis=1, keepdims=True)
    m_ref[...] = jnp.broadcast_to(m_new, m_ref.shape)
    l_ref[...] = jnp.broadcast_to(l_new, l_ref.shape)
    acc_ref[...] = acc

    @pl.when(k == n_steps - 1)
    def _():
        o_sel = acc * jnp.where(l_new > 0.0, 1.0 / l_new, 0.0)
        gts = gt_ref[0]
        o_ref[0] = gts[:, 0:1] * oc_ref[...] + gts[:, 1:2] * o_sel + gts[:, 2:3] * ow_ref[...]


def _nsa_sample(q_bd, gates, pool_t, table, tail_t, kct, vct, kwt, vwt, *, t_base, wpos0, n_slc, t_q):
    b, n_real = table.shape
    n_pages = n_real + 1
    n_slots = max(p for p in range(1, MAX_PAGE_SLOTS + 1) if n_pages % p == 0)
    n_steps = n_pages // n_slots
    assert n_slots * (PAGE // SEL_BLOCK) <= LANES
    page_shape = pool_t.shape[1:]
    zeros = (0,) * len(page_shape)

    def page_spec(slot):
        return pl.BlockSpec(
            (1,) + page_shape,
            lambda i, k, pt: (pt[i * n_real + jnp.minimum(k * n_slots + slot, n_real - 1)],) + zeros)

    def per_batch(x):
        return pl.BlockSpec((1,) + x.shape[1:], lambda i, k, pt: (i,) + (0,) * (x.ndim - 1))

    acc_shape = pltpu.VMEM((SAMPLE_ROWS, KV_WIDTH), F32)
    stat_shape = pltpu.VMEM((SAMPLE_ROWS, LANES), F32)
    return pl.pallas_call(
        functools.partial(_nsa_sample_kernel, n_slots=n_slots, t_base=t_base, wpos0=wpos0, n_slc=n_slc, t_q=t_q),
        grid_spec=pltpu.PrefetchScalarGridSpec(
            num_scalar_prefetch=1, grid=(b, n_steps),
            in_specs=[page_spec(slot) for slot in range(n_slots)]
            + [per_batch(x) for x in (tail_t, q_bd, gates, kct, vct, kwt, vwt)],
            out_specs=pl.BlockSpec((1, SAMPLE_ROWS, KV_WIDTH), lambda i, k, pt: (i, 0, 0)),
            scratch_shapes=[pltpu.VMEM((n_steps, SAMPLE_ROWS, LANES), BF), pltpu.VMEM((LANES, n_slots * PAGE), BF),
                            stat_shape, stat_shape, acc_shape, acc_shape, acc_shape]),
        out_shape=jax.ShapeDtypeStruct((b, SAMPLE_ROWS, KV_WIDTH), F32),
        compiler_params=_cparams(("parallel", "arbitrary"), VMEM_LIMIT),
        name="nsa_sample",
    )(table.reshape(-1), *([pool_t] * n_slots), tail_t, q_bd, gates, kct, vct, kwt, vwt)


def _rope_tables(pos):
    half = ROPE_DIM // 2
    inv_freq = ROPE_THETA ** (-jnp.arange(half, dtype=F32) / half)
    ang = pos.astype(F32)[:, None] * inv_freq[None, :]
    cos, sin = jnp.cos(ang), jnp.sin(ang)
    n = pos.shape[0]
    rest = HEAD_DIM - ROPE_DIM
    c = jnp.concatenate([cos, cos, jnp.ones((n, rest), F32)], axis=1)
    a = jnp.concatenate([-sin, jnp.zeros((n, half + rest), F32)], axis=1)
    b = jnp.concatenate([jnp.zeros((n, half), F32), sin, jnp.zeros((n, rest), F32)], axis=1)
    reps = LANES // HEAD_DIM
    return tuple(jnp.tile(x, (1, reps)) for x in (c, a, b))


def _compress_weights(cmp_pe, w_phi1, w_phi2):
    ratio = CMP_BLOCK // CMP_STRIDE
    w1 = w_phi1.reshape(ratio, CMP_STRIDE, 2, HEAD_DIM, CMP_HIDDEN)
    eye2 = jnp.eye(2, dtype=w_phi1.dtype)
    big = jnp.einsum('jscde,hg->cjshdge', w1, eye2)
    big = big.reshape(2, ratio, CMP_STRIDE * 2 * HEAD_DIM, 2 * CMP_HIDDEN).astype(BF)
    big = jnp.concatenate([big[:, jj] for jj in range(ratio)], axis=-1)
    pe = cmp_pe.reshape(ratio, CMP_STRIDE, 2, HEAD_DIM)
    pe = jnp.broadcast_to(pe.transpose(2, 0, 1, 3)[:, :, :, None, :], (2, ratio, CMP_STRIDE, 2, HEAD_DIM))
    pe = pe.reshape(2, ratio, 1, CMP_STRIDE * 2 * HEAD_DIM)
    pe = jnp.broadcast_to(pe, (2, ratio, 8, pe.shape[-1]))
    eye4 = jnp.eye(KV_HEADS, dtype=w_phi2.dtype).reshape(2, 2, KV_HEADS)
    w2 = jnp.einsum('ced,pgh->cpgehd', w_phi2, eye4)
    w2 = w2.reshape(2, 2, 2 * CMP_HIDDEN, KV_WIDTH).astype(BF)
    return big, pe, w2


def _b_in_weight(w):
    n_gate = w.shape[1] - NSA_WIDTH - MEM_WIDTH
    gates = jnp.pad(w[:, NSA_WIDTH:NSA_WIDTH + n_gate], ((0, 0), (0, GATE_PAD - n_gate)))
    return jnp.concatenate([w[:, :NSA_WIDTH], w[:, NSA_WIDTH + n_gate:], gates], axis=1).astype(BF)


def _pad_rows(x, n):
    return jnp.pad(x, ((0, 0), (0, n - x.shape[1]), (0, 0)))


def _tail_page_t(new_rows):
    b = new_rows.shape[0]
    return _pad_rows(new_rows, PAGE).reshape(b, PAGE, 2, KV_HEADS, HEAD_DIM).transpose(0, 2, 3, 4, 1)


def _rows_t(rows):
    b, n, _ = rows.shape
    kv = rows.reshape(b, n, 2, KV_WIDTH).transpose(0, 2, 3, 1).astype(BF)
    return kv[:, 0], kv[:, 1]


def _row_tiles(s):
    return {"proj": min(s, 1024), "gate": min(s, 1024), "mlp": min(s, 512)}


def _trunk_a(x, mem_kv, p, l, *, chunk, tiles, emit_v, per_row_sets):
    if chunk == PAGE:
        ws, bs = p["w_spatial"][l], jnp.repeat(p["b_spatial"][l].T, A_GROUP_DIM, axis=1)
    else:
        reps = PAGE // chunk
        ws = jnp.tile(p["w_spatial"][l][:, :chunk, :chunk], (1, reps, reps))
        bs = jnp.tile(jnp.repeat(p["b_spatial"][l].T[:chunk], A_GROUP_DIM, axis=1), (reps, 1))
    res = _a_in(x, p["w_in_a"][l], p["ln_v_g"][l][None], p["ln_v_b"][l][None], ws, bs,
                chunk=chunk, tm=tiles["gate"], emit_v=emit_v, mem_kv=None if per_row_sets else mem_kv)
    mixed, om = res[0], res[1]
    if per_row_sets:
        om = _mem_attn_rows(om, mem_kv, chunk)
    x = _mlp_block(x, mixed, om, p["w_out_a"][l], p, l, tm=tiles["mlp"], a_transposed=False)
    return x, (res[2] if emit_v else None)


def _mlp_block(x, a, om, wo, p, l, *, tm, a_transposed):
    return _out_mlp(x, a, om, wo, p["ln1_g"][l][None], p["ln1_b"][l][None], p["w_up"][l], p["w_down"][l],
                    p["ln2_g"][l][None], p["ln2_b"][l][None], tm=tm, a_transposed=a_transposed)


def kernel(x_prompt, x_sample, cache_cmp_kv, cache_slc_kv, cache_win_kv, cache_mem_kv, page_table, mem_prompt,
           w_in_a, ln_v_g, ln_v_b, w_spatial, b_spatial, w_out_a, w_in_b, w_out_b, w_kv_shared, cmp_pe,
           w_phi1, w_phi2, w_mem_kv, ln1_g, ln1_b, ln2_g, ln2_b, w_up, w_down):
    bp, s, d = x_prompt.shape
    bs, t, _ = x_sample.shape
    n_pages = page_table.shape[1]
    past = n_pages * PAGE
    n_buf = cache_win_kv.shape[1]
    assert bs * t == PAGE and PAGE % t == 0 and s % PAGE == 0 and n_buf % PAGE == 0

    p = {"w_in_a": w_in_a.astype(BF), "ln_v_g": ln_v_g, "ln_v_b": ln_v_b, "w_spatial": w_spatial,
         "b_spatial": b_spatial, "w_out_a": w_out_a.astype(BF), "w_out_b": w_out_b.astype(BF),
         "ln1_g": ln1_g, "ln1_b": ln1_b, "ln2_g": ln2_g, "ln2_b": ln2_b,
         "w_up": w_up.astype(BF), "w_down": w_down.astype(BF)}
    w_in_b2 = [_b_in_weight(w_in_b[i]) for i in range(w_in_b.shape[0])]
    w_kv = w_kv_shared.astype(BF)
    cw1, cpe, cw2 = _compress_weights(cmp_pe, w_phi1, w_phi2)

    tl = _row_tiles(s)
    mem_kv_prompt = _mem_kv_proj(mem_prompt, w_mem_kv)
    x = x_prompt
    for l in range(N_A):
        x, _ = _trunk_a(x, mem_kv_prompt[l], p, l, chunk=PAGE, tiles=tl, emit_v=False, per_row_sets=False)
    tabs_p = _rope_tables(jnp.arange(s, dtype=jnp.int32))
    cmp_t, slc_t, win_t, ks, vst, kw, vwt = _kv_shared(x, w_kv, tabs_p, tm=tl["proj"], attn_layouts=True)
    shape_t = (2, KV_HEADS, HEAD_DIM)
    cmp_blk = _compress(cmp_t.reshape(bp, *shape_t, s), None, None, cw1, cpe, cw2)
    n_cmp = s // CMP_STRIDE - 1
    ncp = -(-n_cmp // PAGE) * PAGE
    kc, vct = _rows_layout(_pad_rows(cmp_blk[:, 1:1 + n_cmp], ncp))
    for i in range(DEPTH - N_A):
        l = N_A + i
        q, om, gt = _b_in(x, w_in_b2[i], tabs_p, tm=tl["proj"], mem_kv=mem_kv_prompt[l])
        ot = _nsa(q, gt, kc, vct, ks, vst, kw, vwt, t_base=0, wpos0=0, n_slc=s // SEL_BLOCK)
        x = _mlp_block(x, ot, om, p["w_out_b"][i], p, l, tm=tl["mlp"], a_transposed=True)
    y_prompt = x
    shape5 = (2, KV_HEADS, HEAD_DIM)

    def rows_major(x_t):
        return x_t.reshape(x_t.shape[0], *shape_t, x_t.shape[2]).transpose(0, 4, 1, 2, 3)

    cmp_kv_prompt = rows_major(cmp_t)
    slc_kv_prompt = rows_major(slc_t)
    win_kv_prompt = rows_major(win_t[:, :, s - min(WINDOW, s):])
    mem_kv_prompt_out = mem_kv_prompt.reshape(DEPTH, bp, MEM_TOKENS, 2, MEM_HEADS, HEAD_DIM)

    rows = bs * t
    x = x_sample.reshape(1, rows, d)
    mem_s = cache_mem_kv.reshape(DEPTH, bs, MEM_TOKENS, 2 * MEM_WIDTH)
    v_rows = []
    for l in range(N_A):
        x, v = _trunk_a(x, mem_s[l], p, l, chunk=t, tiles=_row_tiles(rows), emit_v=True, per_row_sets=True)
        v_rows.append(v.reshape(bs, t, A_WIDTH))
    pos_s = past + (jnp.arange(rows, dtype=jnp.int32) % t)
    tabs_s = _rope_tables(pos_s)
    new_cmp, new_slc, new_win = (r[0].T.reshape(bs, t, ROW_WIDTH)
                                 for r in _kv_shared(x, w_kv, tabs_s, tm=rows, attn_layouts=False))
    cmp_pool_t = cache_cmp_kv.transpose(0, 2, 3, 4, 1)
    slc_pool_t = cache_slc_kv.transpose(0, 2, 3, 4, 1)
    cmp_blk = _compress(cmp_pool_t, page_table, _tail_page_t(new_cmp), cw1, cpe, cw2)
    n_cmp = -(-(past + t) // CMP_STRIDE) - 1
    ncp = -(-n_cmp // PAGE) * PAGE
    kct, vct = _rows_t(_pad_rows(cmp_blk[:, 1:1 + n_cmp], ncp))
    win_all = jnp.concatenate([cache_win_kv.reshape(bs, n_buf, ROW_WIDTH), new_win], axis=1)
    kwt, vwt = _rows_t(_pad_rows(win_all, n_buf + PAGE))
    slc_tail_t = _tail_page_t(new_slc)
    n_slc = -(-(past + t) // SEL_BLOCK)
    per_group = GQA * t
    eye = jnp.eye(KV_HEADS, dtype=BF)
    for i in range(DEPTH - N_A):
        l = N_A + i
        q, qm, gt = _b_in(x, w_in_b2[i], tabs_s, tm=rows)
        om = _mem_attn_rows(qm, mem_s[l], t)
        q = q.reshape(KV_HEADS, GQA, bs, t, HEAD_DIM).transpose(2, 0, 1, 3, 4).reshape(bs, KV_HEADS, per_group, HEAD_DIM)
        q = jnp.pad(q, ((0, 0), (0, 0), (0, ROWS_PER_GROUP - per_group), (0, 0)))
        q_bd = jnp.einsum('bgxd,gh->bgxhd', q, eye).reshape(bs, SAMPLE_ROWS, KV_WIDTH)
        gs = gt[0, :NSA_HEADS * 3].reshape(KV_HEADS, GQA, 3, bs, t).transpose(3, 0, 1, 4, 2)
        gs = gs.reshape(bs, KV_HEADS, per_group, 3)
        gs = jnp.pad(gs, ((0, 0), (0, 0), (0, ROWS_PER_GROUP - per_group), (0, LANES - 3)))
        o = _nsa_sample(q_bd, gs.reshape(bs, SAMPLE_ROWS, LANES), slc_pool_t, page_table, slc_tail_t,
                        kct, vct, kwt, vwt, t_base=past, wpos0=past - n_buf, n_slc=n_slc, t_q=t)
        o = o.reshape(bs, KV_HEADS, ROWS_PER_GROUP, KV_HEADS, HEAD_DIM)
        o = jnp.stack([o[:, g, :per_group, g] for g in range(KV_HEADS)], axis=1)
        o = o.reshape(bs, KV_HEADS, GQA, t, HEAD_DIM).transpose(0, 3, 1, 2, 4).reshape(1, rows, NSA_WIDTH)
        x = _mlp_block(x, o, om, p["w_out_b"][i], p, l, tm=rows, a_transposed=False)
    y_sample = x.reshape(bs, t, d)
    cmp_kv_sample = new_cmp.reshape(bs, t, *shape5)
    slc_kv_sample = new_slc.reshape(bs, t, *shape5)
    win_kv_sample = win_all[:, t:].reshape(bs, n_buf, *shape5)
    gmlp_v_sample = jnp.stack(v_rows)

    return (y_prompt, y_sample, cmp_kv_prompt, slc_kv_prompt, win_kv_prompt, mem_kv_prompt_out,
            cmp_kv_sample, slc_kv_sample, win_kv_sample, gmlp_v_sample)
```
